```python
import math
import jax, jax.numpy as jnp
from jax import lax
import numpy as np

D_MODEL = 1024
BATCH = 8
SEQ = 8192
DEPTH = 2

CHUNK = 64
N_META = 16
N_A_LAYERS = DEPTH // 2
N_B_LAYERS = DEPTH - N_A_LAYERS
GDN_HEADS = 8
GDN_HEAD_DIM = 128
GDN_WIDTH = GDN_HEADS * GDN_HEAD_DIM
CONV_WIDTH = 4
SB_HEADS = 16
SB_HEAD_DIM = 64
SB_WIDTH = SB_HEADS * SB_HEAD_DIM
SB_BLOCK = 128
D_FF = -(-(8 * D_MODEL) // (3 * 256)) * 256
GDN_IN_COLS = 4 * GDN_WIDTH + 2 * GDN_HEADS
EPS = 1e-6

kernel_name = "yoco_gdn_stickbreaking_hybrid"


def rms_norm(x, g):
    xf = x.astype(jnp.float32)
    y = xf * lax.rsqrt(jnp.mean(xf * xf, axis=-1, keepdims=True) + EPS)
    return (y * g.astype(jnp.float32)).astype(x.dtype)


def l2_normalize(x):
    return x * lax.rsqrt(jnp.sum(x * x, axis=-1, keepdims=True) + EPS)


def causal_depthwise_conv(x, w):
    L = x.shape[1]
    xp = jnp.pad(x, ((0, 0), (CONV_WIDTH - 1, 0), (0, 0)))
    return sum(xp[:, i:i + L] * w[i] for i in range(CONV_WIDTH))


def gated_delta_chunked(q, k, v, g, beta):
    b, nh, L, dk = q.shape
    dv = v.shape[-1]
    n = L // CHUNK
    q = q.reshape(b, nh, n, CHUNK, dk) * (dk ** -0.5)
    k = k.reshape(b, nh, n, CHUNK, dk)
    v = v.reshape(b, nh, n, CHUNK, dv)
    g = g.reshape(b, nh, n, CHUNK)
    beta = beta.reshape(b, nh, n, CHUNK)
    G = jnp.cumsum(g, axis=-1)
    incl = jnp.tril(jnp.ones((CHUNK, CHUNK), dtype=bool))
    strict = jnp.tril(jnp.ones((CHUNK, CHUNK), dtype=bool), -1)
    diff = G[..., :, None] - G[..., None, :]
    decay = jnp.where(incl, jnp.exp(jnp.where(incl, diff, 0.0)), 0.0)
    kk = jnp.einsum('bhnid,bhnjd->bhnij', k, k)
    lower = jnp.eye(CHUNK, dtype=q.dtype) + jnp.where(strict, beta[..., :, None] * decay * kk, 0.0)
    rhs = jnp.concatenate([beta[..., None] * v, (beta * jnp.exp(G))[..., None] * k], axis=-1)
    sol = lax.linalg.triangular_solve(lower, rhs, left_side=True, lower=True, unit_diagonal=True)
    u_base = sol[..., :dv]
    w_corr = sol[..., dv:]
    attn = jnp.einsum('bhnid,bhnjd->bhnij', q, k) * decay
    q_dec = q * jnp.exp(G)[..., None]
    g_last = G[..., -1]
    k_dec = k * jnp.exp(g_last[..., None] - G)[..., None]

    def step(S, inp):
        u_b, w_c, a_c, qd, kd, gl = inp
        u = u_b - jnp.einsum('bhcd,bhde->bhce', w_c, S)
        o = jnp.einsum('bhcd,bhde->bhce', qd, S) + jnp.einsum('bhij,bhje->bhie', a_c, u)
        S = S * jnp.exp(gl)[..., None, None] + jnp.einsum('bhcd,bhce->bhde', kd, u)
        return S, o

    xs = tuple(jnp.moveaxis(t, 2, 0) for t in (u_base, w_corr, attn, q_dec, k_dec, g_last))
    S0 = jnp.zeros((b, nh, dk, dv), dtype=q.dtype)
    _, o = lax.scan(step, S0, xs)
    return jnp.moveaxis(o, 0, 2).reshape(b, nh, L, dv)


def gdn_mixer(h, w_in, conv_w, a_log, dt_bias, onorm_g, w_out):
    b, L, _ = h.shape
    proj = h @ w_in
    qkv = jax.nn.silu(causal_depthwise_conv(proj[..., :3 * GDN_WIDTH], conv_w)).astype(jnp.float32)
    gate = proj[..., 3 * GDN_WIDTH:4 * GDN_WIDTH].astype(jnp.float32)
    a_in = proj[..., 4 * GDN_WIDTH:4 * GDN_WIDTH + GDN_HEADS].astype(jnp.float32)
    b_in = proj[..., 4 * GDN_WIDTH + GDN_HEADS:].astype(jnp.float32)
    heads = lambda t: t.reshape(b, L, GDN_HEADS, GDN_HEAD_DIM).transpose(0, 2, 1, 3)
    q = l2_normalize(heads(qkv[..., :GDN_WIDTH]))
    k = l2_normalize(heads(qkv[..., GDN_WIDTH:2 * GDN_WIDTH]))
    v = heads(qkv[..., 2 * GDN_WIDTH:])
    g = (-jnp.exp(a_log.astype(jnp.float32)) * jax.nn.softplus(a_in + dt_bias.astype(jnp.float32))).transpose(0, 2, 1)
    beta = jax.nn.sigmoid(b_in).transpose(0, 2, 1)
    front = (-N_META) % CHUNK
    back = (-(front + L)) % CHUNK
    p4 = ((0, 0), (0, 0), (front, back), (0, 0))
    p3 = ((0, 0), (0, 0), (front, back))
    o = gated_delta_chunked(jnp.pad(q, p4), jnp.pad(k, p4), jnp.pad(v, p4), jnp.pad(g, p3), jnp.pad(beta, p3))
    o = o[:, :, front:front + L].transpose(0, 2, 1, 3)
    o = o * lax.rsqrt(jnp.mean(o * o, axis=-1, keepdims=True) + EPS) * onorm_g.astype(jnp.float32)
    o = o * jax.nn.silu(gate.reshape(b, L, GDN_HEADS, GDN_HEAD_DIM))
    return o.reshape(b, L, GDN_WIDTH).astype(h.dtype) @ w_out


def shared_kv(h, norm_g, w_kv):
    b, L, _ = h.shape
    kv = (rms_norm(h, norm_g) @ w_kv).astype(jnp.float32).reshape(b, L, 2, SB_HEADS, SB_HEAD_DIM)
    return kv[:, :, 0].transpose(0, 2, 1, 3), kv[:, :, 1].transpose(0, 2, 1, 3)


def sb_mixer(h, k_sh, v_sh, w_q, w_o):
    b, L, _ = h.shape
    q = (h @ w_q).astype(jnp.float32).reshape(b, L, SB_HEADS, SB_HEAD_DIM).transpose(0, 2, 1, 3)
    pad = (-L) % SB_BLOCK
    Lp = L + pad
    p4 = ((0, 0), (0, 0), (0, pad), (0, 0))
    q, k, v = jnp.pad(q, p4), jnp.pad(k_sh, p4), jnp.pad(v_sh, p4)
    nb = Lp // SB_BLOCK
    q_blocks = q.reshape(b, SB_HEADS, nb, SB_BLOCK, SB_HEAD_DIM).transpose(2, 0, 1, 3, 4)
    key_pos = jnp.arange(Lp)
    scale = SB_HEAD_DIM ** -0.5

    def one_block(args):
        q_blk, blk = args
        q_pos = blk * SB_BLOCK + jnp.arange(SB_BLOCK)
        visible = key_pos[None, :] < q_pos[:, None]
        z = jnp.einsum('bhqd,bhkd->bhqk', q_blk, k) * scale
        log_beta = jax.nn.log_sigmoid(z)
        log_keep = jnp.where(visible, log_beta - z, 0.0)
        log_w = log_beta + lax.cumsum(log_keep, axis=3, reverse=True) - log_keep
        w = jnp.where(visible, jnp.exp(log_w), 0.0)
        return jnp.einsum('bhqk,bhkd->bhqd', w, v)

    o = lax.map(one_block, (q_blocks, jnp.arange(nb)))
    o = o.transpose(1, 2, 0, 3, 4).reshape(b, SB_HEADS, Lp, SB_HEAD_DIM)[:, :, :L]
    o = o.transpose(0, 2, 1, 3).reshape(b, L, SB_WIDTH).astype(h.dtype)
    return o @ w_o


def swiglu(h, w_gate_up, w_down):
    gu = h @ w_gate_up
    return (jax.nn.silu(gu[..., :D_FF]) * gu[..., D_FF:]) @ w_down


def _fwd_setup_inputs(seed: int = 0) -> dict:
    key = jax.random.key(seed)
    ks = jax.random.split(key, 20)
    f32 = jnp.float32
    dense = lambda k, shape, fan_in: jax.random.normal(k, shape, f32) * (fan_in ** -0.5)
    gain = lambda k, shape: 1.0 + 0.02 * jax.random.normal(k, shape, f32)
    dt = jnp.exp(jax.random.uniform(ks[5], (N_A_LAYERS, GDN_HEADS), f32, math.log(1e-3), math.log(1e-1)))
    return {
        "x": jax.random.normal(ks[0], (BATCH, SEQ, D_MODEL), f32),
        "meta_tokens": jax.random.normal(ks[1], (N_META, D_MODEL), f32),
        "gdn_norm_g": gain(ks[2], (N_A_LAYERS, D_MODEL)),
        "gdn_w_in": dense(ks[3], (N_A_LAYERS, D_MODEL, GDN_IN_COLS), D_MODEL),
        "gdn_conv_w": dense(ks[4], (N_A_LAYERS, CONV_WIDTH, 3 * GDN_WIDTH), CONV_WIDTH),
        "gdn_a_log": jnp.log(jax.random.uniform(ks[6], (N_A_LAYERS, GDN_HEADS), f32, 1.0, 16.0)),
        "gdn_dt_bias": dt + jnp.log(-jnp.expm1(-dt)),
        "gdn_onorm_g": gain(ks[7], (N_A_LAYERS, GDN_HEAD_DIM)),
        "gdn_w_out": dense(ks[8], (N_A_LAYERS, GDN_WIDTH, D_MODEL), GDN_WIDTH),
        "kv_norm_g": gain(ks[9], (D_MODEL,)),
        "w_kv": dense(ks[10], (D_MODEL, 2 * SB_WIDTH), D_MODEL),
        "sb_norm_g": gain(ks[11], (N_B_LAYERS, D_MODEL)),
        "sb_w_q": dense(ks[12], (N_B_LAYERS, D_MODEL, SB_WIDTH), D_MODEL),
        "sb_w_o": dense(ks[13], (N_B_LAYERS, SB_WIDTH, D_MODEL), SB_WIDTH),
        "ffn_norm_g": gain(ks[14], (DEPTH, D_MODEL)),
        "ffn_w_gate_up": dense(ks[15], (DEPTH, D_MODEL, 2 * D_FF), D_MODEL),
        "ffn_w_down": dense(ks[16], (DEPTH, D_FF, D_MODEL), D_FF),
        "final_norm_g": gain(ks[17], (D_MODEL,)),
    }


def _fwd_reference(x, meta_tokens, gdn_norm_g, gdn_w_in, gdn_conv_w, gdn_a_log, gdn_dt_bias,
              gdn_onorm_g, gdn_w_out, kv_norm_g, w_kv, sb_norm_g, sb_w_q, sb_w_o,
              ffn_norm_g, ffn_w_gate_up, ffn_w_down, final_norm_g):
    b = x.shape[0]
    meta = jnp.broadcast_to(meta_tokens.astype(x.dtype)[None], (b, N_META, D_MODEL))
    h = jnp.concatenate([meta, x], axis=1)
    k_sh = None
    v_sh = None
    for layer in range(DEPTH):
        if layer < N_A_LAYERS:
            h = h + gdn_mixer(rms_norm(h, gdn_norm_g[layer]), gdn_w_in[layer], gdn_conv_w[layer],
                              gdn_a_log[layer], gdn_dt_bias[layer], gdn_onorm_g[layer], gdn_w_out[layer])
        else:
            j = layer - N_A_LAYERS
            h = h + sb_mixer(rms_norm(h, sb_norm_g[j]), k_sh, v_sh, sb_w_q[j], sb_w_o[j])
        h = h + swiglu(rms_norm(h, ffn_norm_g[layer]), ffn_w_gate_up[layer], ffn_w_down[layer])
        if layer == N_A_LAYERS - 1:
            k_sh, v_sh = shared_kv(h, kv_norm_g, w_kv)
    return rms_norm(h, final_norm_g)[:, N_META:]


import jax as _jax
import jax.numpy as _jnp

TWIN_FORMAT = 'train_step'
FWD_PARAMS = ['x', 'meta_tokens', 'gdn_norm_g', 'gdn_w_in', 'gdn_conv_w', 'gdn_a_log', 'gdn_dt_bias', 'gdn_onorm_g', 'gdn_w_out', 'kv_norm_g', 'w_kv', 'sb_norm_g', 'sb_w_q', 'sb_w_o', 'ffn_norm_g', 'ffn_w_gate_up', 'ffn_w_down', 'final_norm_g']
TWIN_WEIGHTS = ['meta_tokens', 'gdn_norm_g', 'gdn_w_in', 'gdn_conv_w', 'gdn_a_log', 'gdn_dt_bias', 'gdn_onorm_g', 'gdn_w_out', 'kv_norm_g', 'w_kv', 'sb_norm_g', 'sb_w_q', 'sb_w_o', 'ffn_norm_g', 'ffn_w_gate_up', 'ffn_w_down', 'final_norm_g']
TWIN_DIFF_INPUT = 'x'
TWIN_INPUTS = ['x', 'meta_tokens', 'gdn_norm_g', 'gdn_w_in', 'gdn_conv_w', 'gdn_a_log', 'gdn_dt_bias', 'gdn_onorm_g', 'gdn_w_out', 'kv_norm_g', 'w_kv', 'sb_norm_g', 'sb_w_q', 'sb_w_o', 'ffn_norm_g', 'ffn_w_gate_up', 'ffn_w_down', 'final_norm_g', 'loss_target', 'm_meta_tokens', 'm_gdn_norm_g', 'm_gdn_w_in', 'm_gdn_conv_w', 'm_gdn_a_log', 'm_gdn_dt_bias', 'm_gdn_onorm_g', 'm_gdn_w_out', 'm_kv_norm_g', 'm_w_kv', 'm_sb_norm_g', 'm_sb_w_q', 'm_sb_w_o', 'm_ffn_norm_g', 'm_ffn_w_gate_up', 'm_ffn_w_down', 'm_final_norm_g', 'v_meta_tokens', 'v_gdn_norm_g', 'v_gdn_w_in', 'v_gdn_conv_w', 'v_gdn_a_log', 'v_gdn_dt_bias', 'v_gdn_onorm_g', 'v_gdn_w_out', 'v_kv_norm_g', 'v_w_kv', 'v_sb_norm_g', 'v_sb_w_q', 'v_sb_w_o', 'v_ffn_norm_g', 'v_ffn_w_gate_up', 'v_ffn_w_down', 'v_final_norm_g']
TWIN_OUTPUTS = ['loss', 'grad_x', 'grad_meta_tokens', 'grad_gdn_norm_g', 'grad_gdn_w_in', 'grad_gdn_conv_w', 'grad_gdn_a_log', 'grad_gdn_dt_bias', 'grad_gdn_onorm_g', 'grad_gdn_w_out', 'grad_kv_norm_g', 'grad_w_kv', 'grad_sb_norm_g', 'grad_sb_w_q', 'grad_sb_w_o', 'grad_ffn_norm_g', 'grad_ffn_w_gate_up', 'grad_ffn_w_down', 'grad_final_norm_g', 'delta_meta_tokens', 'delta_gdn_norm_g', 'delta_gdn_w_in', 'delta_gdn_conv_w', 'delta_gdn_a_log', 'delta_gdn_dt_bias', 'delta_gdn_onorm_g', 'delta_gdn_w_out', 'delta_kv_norm_g', 'delta_w_kv', 'delta_sb_norm_g', 'delta_sb_w_q', 'delta_sb_w_o', 'delta_ffn_norm_g', 'delta_ffn_w_gate_up', 'delta_ffn_w_down', 'delta_final_norm_g', 'new_m_meta_tokens', 'new_m_gdn_norm_g', 'new_m_gdn_w_in', 'new_m_gdn_conv_w', 'new_m_gdn_a_log', 'new_m_gdn_dt_bias', 'new_m_gdn_onorm_g', 'new_m_gdn_w_out', 'new_m_kv_norm_g', 'new_m_w_kv', 'new_m_sb_norm_g', 'new_m_sb_w_q', 'new_m_sb_w_o', 'new_m_ffn_norm_g', 'new_m_ffn_w_gate_up', 'new_m_ffn_w_down', 'new_m_final_norm_g', 'new_v_meta_tokens', 'new_v_gdn_norm_g', 'new_v_gdn_w_in', 'new_v_gdn_conv_w', 'new_v_gdn_a_log', 'new_v_gdn_dt_bias', 'new_v_gdn_onorm_g', 'new_v_gdn_w_out', 'new_v_kv_norm_g', 'new_v_w_kv', 'new_v_sb_norm_g', 'new_v_sb_w_q', 'new_v_sb_w_o', 'new_v_ffn_norm_g', 'new_v_ffn_w_gate_up', 'new_v_ffn_w_down', 'new_v_final_norm_g']
TWIN_LEAF_KINDS = {'loss': 'loss', 'grad_x': 'grad_x', 'grad_meta_tokens': 'grad_w', 'grad_gdn_norm_g': 'grad_w', 'grad_gdn_w_in': 'grad_w', 'grad_gdn_conv_w': 'grad_w', 'grad_gdn_a_log': 'grad_w', 'grad_gdn_dt_bias': 'grad_w', 'grad_gdn_onorm_g': 'grad_w', 'grad_gdn_w_out': 'grad_w', 'grad_kv_norm_g': 'grad_w', 'grad_w_kv': 'grad_w', 'grad_sb_norm_g': 'grad_w', 'grad_sb_w_q': 'grad_w', 'grad_sb_w_o': 'grad_w', 'grad_ffn_norm_g': 'grad_w', 'grad_ffn_w_gate_up': 'grad_w', 'grad_ffn_w_down': 'grad_w', 'grad_final_norm_g': 'grad_w', 'delta_meta_tokens': 'delta_w', 'delta_gdn_norm_g': 'delta_w', 'delta_gdn_w_in': 'delta_w', 'delta_gdn_conv_w': 'delta_w', 'delta_gdn_a_log': 'delta_w', 'delta_gdn_dt_bias': 'delta_w', 'delta_gdn_onorm_g': 'delta_w', 'delta_gdn_w_out': 'delta_w', 'delta_kv_norm_g': 'delta_w', 'delta_w_kv': 'delta_w', 'delta_sb_norm_g': 'delta_w', 'delta_sb_w_q': 'delta_w', 'delta_sb_w_o': 'delta_w', 'delta_ffn_norm_g': 'delta_w', 'delta_ffn_w_gate_up': 'delta_w', 'delta_ffn_w_down': 'delta_w', 'delta_final_norm_g': 'delta_w', 'new_m_meta_tokens': 'new_m', 'new_m_gdn_norm_g': 'new_m', 'new_m_gdn_w_in': 'new_m', 'new_m_gdn_conv_w': 'new_m', 'new_m_gdn_a_log': 'new_m', 'new_m_gdn_dt_bias': 'new_m', 'new_m_gdn_onorm_g': 'new_m', 'new_m_gdn_w_out': 'new_m', 'new_m_kv_norm_g': 'new_m', 'new_m_w_kv': 'new_m', 'new_m_sb_norm_g': 'new_m', 'new_m_sb_w_q': 'new_m', 'new_m_sb_w_o': 'new_m', 'new_m_ffn_norm_g': 'new_m', 'new_m_ffn_w_gate_up': 'new_m', 'new_m_ffn_w_down': 'new_m', 'new_m_final_norm_g': 'new_m', 'new_v_meta_tokens': 'new_v', 'new_v_gdn_norm_g': 'new_v', 'new_v_gdn_w_in': 'new_v', 'new_v_gdn_conv_w': 'new_v', 'new_v_gdn_a_log': 'new_v', 'new_v_gdn_dt_bias': 'new_v', 'new_v_gdn_onorm_g': 'new_v', 'new_v_gdn_w_out': 'new_v', 'new_v_kv_norm_g': 'new_v', 'new_v_w_kv': 'new_v', 'new_v_sb_norm_g': 'new_v', 'new_v_sb_w_q': 'new_v', 'new_v_sb_w_o': 'new_v', 'new_v_ffn_norm_g': 'new_v', 'new_v_ffn_w_gate_up': 'new_v', 'new_v_ffn_w_down': 'new_v', 'new_v_final_norm_g': 'new_v'}


def _forward(args):
    return _fwd_reference(*[args[k] for k in FWD_PARAMS])


def _output_shape():
    def fwd():
        inp = _fwd_setup_inputs(0)
        return _fwd_reference(*[inp[k] for k in FWD_PARAMS])
    out = _jax.eval_shape(fwd)
    return out.shape, out.dtype

N_MICROBATCH = 1
ADAM_LR = 0.001
ADAM_B1 = 0.9
ADAM_B2 = 0.999
ADAM_EPS = 1e-08
ADAM_WD = 0.01
ADAM_STEP = 10
PER_EXAMPLE_BATCH_AXIS = {'x': 0, 'loss_target': 0}
SHARED_INPUTS = []
_WEIGHT_DTYPES = {'meta_tokens': _jnp.float32, 'gdn_norm_g': _jnp.float32, 'gdn_w_in': _jnp.float32, 'gdn_conv_w': _jnp.float32, 'gdn_a_log': _jnp.float32, 'gdn_dt_bias': _jnp.float32, 'gdn_onorm_g': _jnp.float32, 'gdn_w_out': _jnp.float32, 'kv_norm_g': _jnp.float32, 'w_kv': _jnp.float32, 'sb_norm_g': _jnp.float32, 'sb_w_q': _jnp.float32, 'sb_w_o': _jnp.float32, 'ffn_norm_g': _jnp.float32, 'ffn_w_gate_up': _jnp.float32, 'ffn_w_down': _jnp.float32, 'final_norm_g': _jnp.float32}
MOMENT_SCALE = {'meta_tokens': 6.706792e-03, 'gdn_norm_g': 2.675909e-01, 'gdn_w_in': 1.327901e-01, 'gdn_conv_w': 1.214737e-01, 'gdn_a_log': 1.078307e+00, 'gdn_dt_bias': 1.035420e+00, 'gdn_onorm_g': 4.346403e-01, 'gdn_w_out': 1.548254e-01, 'kv_norm_g': 1.264207e-01, 'w_kv': 9.407426e-02, 'sb_norm_g': 5.548809e-02, 'sb_w_q': 5.533117e-02, 'sb_w_o': 1.215598e-01, 'ffn_norm_g': 1.708369e-01, 'ffn_w_gate_up': 6.952515e-02, 'ffn_w_down': 1.135204e-01, 'final_norm_g': 6.406980e+01}


def _to_microbatches(a, axis):
    t = _jnp.moveaxis(a, axis, 0)
    t = t.reshape((N_MICROBATCH, t.shape[0] // N_MICROBATCH) + t.shape[1:])
    return _jnp.moveaxis(t, 1, axis + 1)


def setup_inputs(seed: int = 0) -> dict:
    inp = _fwd_setup_inputs(seed)
    key = _jax.random.fold_in(_jax.random.key(seed), 7919)
    shape, _ = _output_shape()
    out = dict(inp)
    out["loss_target"] = _jax.random.normal(_jax.random.fold_in(key, 0), shape, _jnp.float32)
    for i, name in enumerate(TWIN_WEIGHTS):
        w = inp[name].astype(_jnp.float32)
        if MOMENT_SCALE is None:
            s = _jnp.sqrt(_jnp.mean(_jnp.square(w)) + 1e-30)
        else:
            s = MOMENT_SCALE[name]
        km, kv = _jax.random.split(_jax.random.fold_in(key, i + 1))
        out[name] = w
        out["m_" + name] = s * _jax.random.normal(km, w.shape, _jnp.float32)
        out["v_" + name] = (s * s) * _jax.random.uniform(kv, w.shape, _jnp.float32, 0.5, 1.5)
    if N_MICROBATCH > 1:
        for name, axis in PER_EXAMPLE_BATCH_AXIS.items():
            out[name] = _to_microbatches(out[name], axis)
    return {'x': out['x'], 'meta_tokens': out['meta_tokens'], 'gdn_norm_g': out['gdn_norm_g'], 'gdn_w_in': out['gdn_w_in'], 'gdn_conv_w': out['gdn_conv_w'], 'gdn_a_log': out['gdn_a_log'], 'gdn_dt_bias': out['gdn_dt_bias'], 'gdn_onorm_g': out['gdn_onorm_g'], 'gdn_w_out': out['gdn_w_out'], 'kv_norm_g': out['kv_norm_g'], 'w_kv': out['w_kv'], 'sb_norm_g': out['sb_norm_g'], 'sb_w_q': out['sb_w_q'], 'sb_w_o': out['sb_w_o'], 'ffn_norm_g': out['ffn_norm_g'], 'ffn_w_gate_up': out['ffn_w_gate_up'], 'ffn_w_down': out['ffn_w_down'], 'final_norm_g': out['final_norm_g'], 'loss_target': out['loss_target'], 'm_meta_tokens': out['m_meta_tokens'], 'm_gdn_norm_g': out['m_gdn_norm_g'], 'm_gdn_w_in': out['m_gdn_w_in'], 'm_gdn_conv_w': out['m_gdn_conv_w'], 'm_gdn_a_log': out['m_gdn_a_log'], 'm_gdn_dt_bias': out['m_gdn_dt_bias'], 'm_gdn_onorm_g': out['m_gdn_onorm_g'], 'm_gdn_w_out': out['m_gdn_w_out'], 'm_kv_norm_g': out['m_kv_norm_g'], 'm_w_kv': out['m_w_kv'], 'm_sb_norm_g': out['m_sb_norm_g'], 'm_sb_w_q': out['m_sb_w_q'], 'm_sb_w_o': out['m_sb_w_o'], 'm_ffn_norm_g': out['m_ffn_norm_g'], 'm_ffn_w_gate_up': out['m_ffn_w_gate_up'], 'm_ffn_w_down': out['m_ffn_w_down'], 'm_final_norm_g': out['m_final_norm_g'], 'v_meta_tokens': out['v_meta_tokens'], 'v_gdn_norm_g': out['v_gdn_norm_g'], 'v_gdn_w_in': out['v_gdn_w_in'], 'v_gdn_conv_w': out['v_gdn_conv_w'], 'v_gdn_a_log': out['v_gdn_a_log'], 'v_gdn_dt_bias': out['v_gdn_dt_bias'], 'v_gdn_onorm_g': out['v_gdn_onorm_g'], 'v_gdn_w_out': out['v_gdn_w_out'], 'v_kv_norm_g': out['v_kv_norm_g'], 'v_w_kv': out['v_w_kv'], 'v_sb_norm_g': out['v_sb_norm_g'], 'v_sb_w_q': out['v_sb_w_q'], 'v_sb_w_o': out['v_sb_w_o'], 'v_ffn_norm_g': out['v_ffn_norm_g'], 'v_ffn_w_gate_up': out['v_ffn_w_gate_up'], 'v_ffn_w_down': out['v_ffn_w_down'], 'v_final_norm_g': out['v_final_norm_g']}


def _loss(weights, diff, rest, loss_target):
    with _jax.named_scope("forward"):
        args = {**rest, TWIN_DIFF_INPUT: diff, **{k: w.astype(_WEIGHT_DTYPES[k]) for k, w in weights.items()}}
        y = _forward(args)
    with _jax.named_scope("loss_head"):
        err = _jnp.square(y.astype(_jnp.float32) - loss_target)
        return 0.5 * _jnp.sum(_jnp.mean(err, axis=-1)) if err.ndim else 0.5 * err


def _adamw(w, g, m, v):
    m = ADAM_B1 * m + (1.0 - ADAM_B1) * g
    v = ADAM_B2 * v + (1.0 - ADAM_B2) * _jnp.square(g)
    m_hat = m / (1.0 - ADAM_B1 ** ADAM_STEP)
    v_hat = v / (1.0 - ADAM_B2 ** ADAM_STEP)
    delta = -ADAM_LR * (m_hat / (_jnp.sqrt(v_hat) + ADAM_EPS) + ADAM_WD * w)
    return delta, m, v


def reference(x, meta_tokens, gdn_norm_g, gdn_w_in, gdn_conv_w, gdn_a_log, gdn_dt_bias, gdn_onorm_g, gdn_w_out, kv_norm_g, w_kv, sb_norm_g, sb_w_q, sb_w_o, ffn_norm_g, ffn_w_gate_up, ffn_w_down, final_norm_g, loss_target, m_meta_tokens, m_gdn_norm_g, m_gdn_w_in, m_gdn_conv_w, m_gdn_a_log, m_gdn_dt_bias, m_gdn_onorm_g, m_gdn_w_out, m_kv_norm_g, m_w_kv, m_sb_norm_g, m_sb_w_q, m_sb_w_o, m_ffn_norm_g, m_ffn_w_gate_up, m_ffn_w_down, m_final_norm_g, v_meta_tokens, v_gdn_norm_g, v_gdn_w_in, v_gdn_conv_w, v_gdn_a_log, v_gdn_dt_bias, v_gdn_onorm_g, v_gdn_w_out, v_kv_norm_g, v_w_kv, v_sb_norm_g, v_sb_w_q, v_sb_w_o, v_ffn_norm_g, v_ffn_w_gate_up, v_ffn_w_down, v_final_norm_g):
    given = dict(x=x, meta_tokens=meta_tokens, gdn_norm_g=gdn_norm_g, gdn_w_in=gdn_w_in, gdn_conv_w=gdn_conv_w, gdn_a_log=gdn_a_log, gdn_dt_bias=gdn_dt_bias, gdn_onorm_g=gdn_onorm_g, gdn_w_out=gdn_w_out, kv_norm_g=kv_norm_g, w_kv=w_kv, sb_norm_g=sb_norm_g, sb_w_q=sb_w_q, sb_w_o=sb_w_o, ffn_norm_g=ffn_norm_g, ffn_w_gate_up=ffn_w_gate_up, ffn_w_down=ffn_w_down, final_norm_g=final_norm_g, loss_target=loss_target, m_meta_tokens=m_meta_tokens, m_gdn_norm_g=m_gdn_norm_g, m_gdn_w_in=m_gdn_w_in, m_gdn_conv_w=m_gdn_conv_w, m_gdn_a_log=m_gdn_a_log, m_gdn_dt_bias=m_gdn_dt_bias, m_gdn_onorm_g=m_gdn_onorm_g, m_gdn_w_out=m_gdn_w_out, m_kv_norm_g=m_kv_norm_g, m_w_kv=m_w_kv, m_sb_norm_g=m_sb_norm_g, m_sb_w_q=m_sb_w_q, m_sb_w_o=m_sb_w_o, m_ffn_norm_g=m_ffn_norm_g, m_ffn_w_gate_up=m_ffn_w_gate_up, m_ffn_w_down=m_ffn_w_down, m_final_norm_g=m_final_norm_g, v_meta_tokens=v_meta_tokens, v_gdn_norm_g=v_gdn_norm_g, v_gdn_w_in=v_gdn_w_in, v_gdn_conv_w=v_gdn_conv_w, v_gdn_a_log=v_gdn_a_log, v_gdn_dt_bias=v_gdn_dt_bias, v_gdn_onorm_g=v_gdn_onorm_g, v_gdn_w_out=v_gdn_w_out, v_kv_norm_g=v_kv_norm_g, v_w_kv=v_w_kv, v_sb_norm_g=v_sb_norm_g, v_sb_w_q=v_sb_w_q, v_sb_w_o=v_sb_w_o, v_ffn_norm_g=v_ffn_norm_g, v_ffn_w_gate_up=v_ffn_w_gate_up, v_ffn_w_down=v_ffn_w_down, v_final_norm_g=v_final_norm_g)
    weights = {n: given[n] for n in TWIN_WEIGHTS}
    shared = {n: given[n] for n in SHARED_INPUTS}
    per_example = {n: given[n] for n in ['x']}
    grad_fn = _jax.value_and_grad(_loss, argnums=(0, 1))

    def one_microbatch(ex, loss_target):
        ex = dict(ex)
        diff = ex.pop(TWIN_DIFF_INPUT)
        return grad_fn(weights, diff, {**shared, **ex}, loss_target)

    if N_MICROBATCH == 1:
        loss, (grad_w, grad_x) = one_microbatch(per_example, given["loss_target"])
    else:
        def body(carry, xs):
            loss_sum, grad_sum = carry
            l_k, (gw_k, gx_k) = one_microbatch(xs[0], xs[1])
            with _jax.named_scope("update"):
                return (loss_sum + l_k, _jax.tree.map(_jnp.add, grad_sum, gw_k)), gx_k

        init = (_jnp.zeros((), _jnp.float32), _jax.tree.map(_jnp.zeros_like, weights))
        (loss, grad_w), grad_x = _jax.lax.scan(body, init, (per_example, given["loss_target"]))
    with _jax.named_scope("update"):
        delta_w, new_m, new_v = {}, {}, {}
        for n in TWIN_WEIGHTS:
            delta_w[n], new_m[n], new_v[n] = _adamw(weights[n], grad_w[n], given["m_" + n], given["v_" + n])
    return (loss, grad_x, *[grad_w[n] for n in TWIN_WEIGHTS], *[delta_w[n] for n in TWIN_WEIGHTS],
            *[new_m[n] for n in TWIN_WEIGHTS], *[new_v[n] for n in TWIN_WEIGHTS])
```

```python
import functools
import math

import jax
import jax.numpy as jnp
from jax import lax
from jax.experimental import pallas as pl
from jax.experimental.pallas import tpu as pltpu

F32 = jnp.float32
BF16 = jnp.bfloat16
HIGHEST = lax.Precision.HIGHEST

N_DEV = 8
LANES = 128
N_META = 16
PAD_ROWS = LANES - N_META
GDN_CHUNK = 64
GDN_HEAD_DIM = 128
SB_HEAD_DIM = 64
CONV_WIDTH = 4
NORM_EPS = 1e-6
FLAT_COLS = 1024
VMEM_LIMIT = 56 * 1024 * 1024

ADAM_LR = 0.001
ADAM_B1 = 0.9
ADAM_B2 = 0.999
ADAM_EPS = 1e-08
ADAM_WD = 0.01
ADAM_STEP = 10

NN = (((1,), (0,)), ((), ()))
NT = (((1,), (1,)), ((), ()))
TN = (((0,), (0,)), ((), ()))


def _params(*sem):
    return pltpu.CompilerParams(dimension_semantics=sem, vmem_limit_bytes=VMEM_LIMIT)


def _pick(n, cands):
    for c in cands:
        if n % c == 0:
            return c
    return n


def _bdot(a, b, dims=NN):
    return lax.dot_general(a.astype(BF16), b.astype(BF16), dims, preferred_element_type=F32)


def _hdot(a, b, dims=NN):
    return lax.dot_general(a, b, dims, preferred_element_type=F32, precision=HIGHEST)


def _split_dot(a, m, parts=2):
    out = None
    for _ in range(parts):
        piece = a.astype(BF16)
        a = a - piece.astype(F32)
        term = lax.dot_general(piece, m, NN, preferred_element_type=F32)
        out = term if out is None else out + term
    return out


def _sigmoid(x):
    return 1.0 / (1.0 + jnp.exp(-x))


def _iota2(shape, axis):
    return lax.broadcasted_iota(jnp.int32, shape, axis)


def _matmul(a, b, mode, name, out_dtype=F32, add=None):
    if mode == "nn":
        (m, k), n = a.shape, b.shape[1]
    elif mode == "nt":
        (m, k), n = a.shape, b.shape[0]
    else:
        (k, m), n = a.shape, b.shape[1]
    row_c = (640, 512, 384, 256, 128)
    col_c = (1024, 1408, 768, 512, 384, 256, 128)
    if mode == "tn":
        bm, bn, bk = _pick(m, col_c), _pick(n, col_c), _pick(k, row_c)
    else:
        bm, bn, bk = _pick(m, row_c), _pick(n, col_c), _pick(k, (1024, 1408, 768, 512, 256, 128))
    nk = k // bk
    dims = {"nn": NN, "nt": NT, "tn": TN}[mode]
    a_spec = {"nn": pl.BlockSpec((bm, bk), lambda i, j, q: (i, q)),
              "nt": pl.BlockSpec((bm, bk), lambda i, j, q: (i, q)),
              "tn": pl.BlockSpec((bk, bm), lambda i, j, q: (q, i))}[mode]
    b_spec = {"nn": pl.BlockSpec((bk, bn), lambda i, j, q: (q, j)),
              "nt": pl.BlockSpec((bn, bk), lambda i, j, q: (j, q)),
              "tn": pl.BlockSpec((bk, bn), lambda i, j, q: (q, j))}[mode]
    o_spec = pl.BlockSpec((bm, bn), lambda i, j, q: (i, j))
    has_add = add is not None

    def body(*refs):
        if has_add:
            a_ref, b_ref, add_ref, o_ref, acc_ref = refs
        else:
            a_ref, b_ref, o_ref, acc_ref = refs
        q = pl.program_id(2)
        part = _bdot(a_ref[...], b_ref[...], dims)

        @pl.when(q == 0)
        def _():
            acc_ref[...] = part

        @pl.when(q > 0)
        def _():
            acc_ref[...] += part

        @pl.when(q == nk - 1)
        def _():
            r = acc_ref[...]
            if has_add:
                r = r + add_ref[...]
            o_ref[...] = r.astype(out_dtype)

    ins = [a, b] + ([add] if has_add else [])
    in_specs = [a_spec, b_spec] + ([o_spec] if has_add else [])
    return pl.pallas_call(
        body, name=name, grid=(m // bm, n // bn, nk),
        in_specs=in_specs, out_specs=o_spec,
        out_shape=jax.ShapeDtypeStruct((m, n), out_dtype),
        scratch_shapes=[pltpu.VMEM((bm, bn), F32)],
        compiler_params=_params("parallel", "parallel", "arbitrary"),
    )(*ins)


def _rmsnorm(h, gains, name):
    m, d = h.shape
    bm = _pick(m, (640, 384, 128))
    ng = len(gains)

    def body(*refs):
        h_ref, g_refs, o_refs = refs[0], refs[1:1 + ng], refs[1 + ng:]
        x = h_ref[...]
        xhat = x * lax.rsqrt(jnp.mean(x * x, axis=-1, keepdims=True) + NORM_EPS)
        for g_ref, o_ref in zip(g_refs, o_refs):
            o_ref[...] = (xhat * g_ref[...]).astype(BF16)

    row = pl.BlockSpec((bm, d), lambda i: (i, 0))
    gain = pl.BlockSpec((1, d), lambda i: (0, 0))
    return pl.pallas_call(
        body, name=name, grid=(m // bm,),
        in_specs=[row] + [gain] * ng, out_specs=[row] * ng,
        out_shape=[jax.ShapeDtypeStruct((m, d), BF16)] * ng,
        compiler_params=_params("parallel"),
    )(h, *gains)


def _rmsnorm_bwd(h, gains, dns, dres, name):
    m, d = h.shape
    bm = _pick(m, (640, 384, 128))
    ng = len(gains)
    nsteps = m // bm

    def body(*refs):
        h_ref, dres_ref = refs[0], refs[1]
        g_refs = refs[2:2 + ng]
        dn_refs = refs[2 + ng:2 + 2 * ng]
        dh_ref = refs[2 + 2 * ng]
        dg_refs = refs[3 + 2 * ng:3 + 3 * ng]
        acc_refs = refs[3 + 3 * ng:]
        i = pl.program_id(0)
        x = h_ref[...]
        r = lax.rsqrt(jnp.mean(x * x, axis=-1, keepdims=True) + NORM_EPS)
        xhat = x * r
        dh = dres_ref[...]
        for g_ref, dn_ref, dg_ref, acc_ref in zip(g_refs, dn_refs, dg_refs, acc_refs):
            dn = dn_ref[...]
            dy = dn * g_ref[...]
            dh = dh + r * (dy - xhat * jnp.mean(dy * xhat, axis=-1, keepdims=True))
            part = jnp.sum((dn * xhat).reshape(bm // 8, 8, d), axis=0)

            @pl.when(i == 0)
            def _():
                acc_ref[...] = part

            @pl.when(i > 0)
            def _():
                acc_ref[...] += part

            @pl.when(i == nsteps - 1)
            def _():
                dg_ref[...] = jnp.sum(acc_ref[...], axis=0, keepdims=True)

        dh_ref[...] = dh

    row = pl.BlockSpec((bm, d), lambda i: (i, 0))
    gain = pl.BlockSpec((1, d), lambda i: (0, 0))
    outs = pl.pallas_call(
        body, name=name, grid=(nsteps,),
        in_specs=[row, row] + [gain] * ng + [row] * ng,
        out_specs=[row] + [gain] * ng,
        out_shape=[jax.ShapeDtypeStruct((m, d), F32)] + [jax.ShapeDtypeStruct((1, d), F32)] * ng,
        scratch_shapes=[pltpu.VMEM((8, d), F32)] * ng,
        compiler_params=_params("arbitrary"),
    )(h, dres, *gains, *dns)
    return outs[0], list(outs[1:])


def _loss_head(h, gain, target):
    m, d = h.shape
    nsteps = m // LANES

    def body(h_ref, g_ref, t_ref, loss_ref, dh_ref, dg_ref, lacc, gacc):
        i = pl.program_id(0)

        @pl.when(i == 0)
        def _():
            lacc[...] = jnp.zeros_like(lacc)
            gacc[...] = jnp.zeros_like(gacc)
            dh_ref[...] = jnp.zeros_like(dh_ref)

        @pl.when(i > 0)
        def _():
            x = h_ref[...]
            r = lax.rsqrt(jnp.mean(x * x, axis=-1, keepdims=True) + NORM_EPS)
            xhat = x * r
            g = g_ref[...]
            err = xhat * g - t_ref[...]
            lacc[...] += jnp.sum((err * err).reshape(LANES // 8, 8, d), axis=0)
            dn = err * (1.0 / d)
            dy = dn * g
            dh_ref[...] = r * (dy - xhat * jnp.mean(dy * xhat, axis=-1, keepdims=True))
            gacc[...] += jnp.sum((dn * xhat).reshape(LANES // 8, 8, d), axis=0)

        @pl.when(i == nsteps - 1)
        def _():
            tot = jnp.sum(jnp.sum(lacc[...], axis=1, keepdims=True), axis=0, keepdims=True)
            loss_ref[...] = jnp.broadcast_to(tot * (0.5 / d), (1, LANES))
            dg_ref[...] = jnp.sum(gacc[...], axis=0, keepdims=True)

    row = pl.BlockSpec((LANES, d), lambda i: (i, 0))
    trow = pl.BlockSpec((LANES, d), lambda i: (jnp.maximum(i - 1, 0), 0))
    gain_spec = pl.BlockSpec((1, d), lambda i: (0, 0))
    return pl.pallas_call(
        body, name="loss_head", grid=(nsteps,),
        in_specs=[row, gain_spec, trow],
        out_specs=[pl.BlockSpec((1, LANES), lambda i: (0, 0)), row, gain_spec],
        out_shape=[jax.ShapeDtypeStruct((1, LANES), F32), jax.ShapeDtypeStruct((m, d), F32),
                   jax.ShapeDtypeStruct((1, d), F32)],
        scratch_shapes=[pltpu.VMEM((8, d), F32), pltpu.VMEM((8, d), F32)],
        compiler_params=_params("arbitrary"),
    )(h, gain, target)


def _swiglu(gu, name):
    m, f2 = gu.shape
    f = f2 // 2

    def body(gu_ref, o_ref):
        g = gu_ref[:, :f]
        u = gu_ref[:, f:]
        o_ref[...] = (g * _sigmoid(g) * u).astype(BF16)

    return pl.pallas_call(
        body, name=name, grid=(m // LANES,),
        in_specs=[pl.BlockSpec((LANES, f2), lambda i: (i, 0))],
        out_specs=pl.BlockSpec((LANES, f), lambda i: (i, 0)),
        out_shape=jax.ShapeDtypeStruct((m, f), BF16),
        compiler_params=_params("parallel"),
    )(gu)


def _swiglu_bwd(gu, da, name):
    m, f2 = gu.shape
    f = f2 // 2

    def body(gu_ref, da_ref, o_ref):
        g = gu_ref[:, :f]
        u = gu_ref[:, f:]
        da_ = da_ref[...]
        s = _sigmoid(g)
        o_ref[:, :f] = (da_ * u * s * (1.0 + g * (1.0 - s))).astype(BF16)
        o_ref[:, f:] = (da_ * g * s).astype(BF16)

    return pl.pallas_call(
        body, name=name, grid=(m // LANES,),
        in_specs=[pl.BlockSpec((LANES, f2), lambda i: (i, 0)), pl.BlockSpec((LANES, f), lambda i: (i, 0))],
        out_specs=pl.BlockSpec((LANES, f2), lambda i: (i, 0)),
        out_shape=jax.ShapeDtypeStruct((m, f2), BF16),
        compiler_params=_params("parallel"),
    )(gu, da)


def _conv_taps(xa, w_ref):
    acc = xa * w_ref[CONV_WIDTH - 1:CONV_WIDTH, :]
    for i in range(CONV_WIDTH - 1):
        acc = acc + pltpu.roll(xa, CONV_WIDTH - 1 - i, 0) * w_ref[i:i + 1, :]
    return acc[8:]


def _gdn_conv(proj, conv_w, n_heads):
    m, w3 = proj.shape
    wd = w3 // 3
    nblk = m // LANES

    def body(x_ref, w_ref, o_ref):
        c = pl.program_id(0)
        is_qk = c < 2 * n_heads

        def finish(cv):
            s = cv * _sigmoid(cv)
            nrm = s * lax.rsqrt(jnp.sum(s * s, axis=-1, keepdims=True) + NORM_EPS)
            return jnp.where(is_qk, nrm, s)

        x0 = jnp.concatenate([jnp.zeros((8, LANES), F32), x_ref[pl.ds(0, LANES), :]], axis=0)
        o_ref[pl.ds(0, LANES), :] = finish(_conv_taps(x0, w_ref))

        def step(b, carry):
            r0 = pl.multiple_of(b * LANES, LANES)
            xa = x_ref[pl.ds(r0 - 8, LANES + 8), :]
            o_ref[pl.ds(r0, LANES), :] = finish(_conv_taps(xa, w_ref))
            return carry

        lax.fori_loop(1, nblk, step, 0)

    return pl.pallas_call(
        body, name="gdn_conv", grid=(w3 // LANES,),
        in_specs=[pl.BlockSpec((m, LANES), lambda c: (0, c)), pl.BlockSpec((CONV_WIDTH, LANES), lambda c: (0, c))],
        out_specs=pl.BlockSpec((None, m, LANES), lambda c: (c // n_heads, 0, c % n_heads)),
        out_shape=jax.ShapeDtypeStruct((3, m, wd), F32),
        compiler_params=_params("parallel"),
    )(proj, conv_w)


def _gdn_conv_bwd(proj, conv_w, dact, n_heads):
    m, w3 = proj.shape
    nblk = m // LANES

    def body(x_ref, w_ref, dy_ref, dx_ref, dw_ref, dc_ref):
        c = pl.program_id(0)
        is_qk = c < 2 * n_heads
        dc_ref[pl.ds(m, 8), :] = jnp.zeros((8, LANES), F32)

        def pre_act(xa):
            cv = _conv_taps(xa, w_ref)
            sg = _sigmoid(cv)
            s = cv * sg
            return cv, sg, s

        def dconv_in(b, xa, dwacc):
            r0 = pl.multiple_of(b * LANES, LANES)
            cv, sg, s = pre_act(xa)
            dy = dy_ref[pl.ds(r0, LANES), :]
            rn = lax.rsqrt(jnp.sum(s * s, axis=-1, keepdims=True) + NORM_EPS)
            yn = s * rn
            ds_qk = rn * (dy - yn * jnp.sum(dy * yn, axis=-1, keepdims=True))
            ds = jnp.where(is_qk, ds_qk, dy)
            dcv = ds * sg * (1.0 + cv * (1.0 - sg))
            dc_ref[pl.ds(r0, LANES), :] = dcv
            new = []
            for i in range(CONV_WIDTH):
                xs = xa[8:] if i == CONV_WIDTH - 1 else pltpu.roll(xa, CONV_WIDTH - 1 - i, 0)[8:]
                new.append(dwacc[i] + jnp.sum((dcv * xs).reshape(LANES // 8, 8, LANES), axis=0))
            return tuple(new)

        x0 = jnp.concatenate([jnp.zeros((8, LANES), F32), x_ref[pl.ds(0, LANES), :]], axis=0)
        zero = jnp.zeros((8, LANES), F32)
        dwacc = dconv_in(0, x0, (zero,) * CONV_WIDTH)

        def step1(b, acc):
            r0 = pl.multiple_of(b * LANES, LANES)
            return dconv_in(b, x_ref[pl.ds(r0 - 8, LANES + 8), :], acc)

        dwacc = lax.fori_loop(1, nblk, step1, dwacc)
        for i in range(CONV_WIDTH):
            dw_ref[i:i + 1, :] = jnp.sum(dwacc[i], axis=0, keepdims=True)

        def step2(b, carry):
            r0 = pl.multiple_of(b * LANES, LANES)
            da = dc_ref[pl.ds(r0, LANES + 8), :]
            acc = da * w_ref[CONV_WIDTH - 1:CONV_WIDTH, :]
            for i in range(CONV_WIDTH - 1):
                sh = CONV_WIDTH - 1 - i
                acc = acc + pltpu.roll(da, LANES + 8 - sh, 0) * w_ref[i:i + 1, :]
            dx_ref[pl.ds(r0, LANES), :] = acc[:LANES]
            return carry

        lax.fori_loop(0, nblk, step2, 0)

    return pl.pallas_call(
        body, name="gdn_conv_bwd", grid=(w3 // LANES,),
        in_specs=[pl.BlockSpec((m, LANES), lambda c: (0, c)),
                  pl.BlockSpec((CONV_WIDTH, LANES), lambda c: (0, c)),
                  pl.BlockSpec((None, m, LANES), lambda c: (c // n_heads, 0, c % n_heads))],
        out_specs=[pl.BlockSpec((m, LANES), lambda c: (0, c)), pl.BlockSpec((CONV_WIDTH, LANES), lambda c: (0, c))],
        out_shape=[jax.ShapeDtypeStruct((m, w3), F32), jax.ShapeDtypeStruct((CONV_WIDTH, w3), F32)],
        scratch_shapes=[pltpu.VMEM((m + 8, LANES), F32)],
        compiler_params=_params("parallel"),
    )(proj, conv_w, dact)


def _softplus(x):
    return jnp.maximum(x, 0.0) + jnp.log(1.0 + jnp.exp(-jnp.abs(x)))


def _gdn_gates(ab, a_log, dt_bias, n_heads):
    m = ab.shape[0]
    bm = _pick(m, (640, 384, 128))

    def body(ab_ref, al_ref, dt_ref, g_ref, b_ref):
        i = pl.program_id(0)
        x = ab_ref[...]
        live = (_iota2((bm, LANES), 0) + i * bm) >= PAD_ROWS
        g = jnp.where(live, -jnp.exp(al_ref[...]) * _softplus(x + dt_ref[...]), 0.0)
        beta = jnp.where(live, _sigmoid(x), 0.0)
        for h in range(n_heads):
            g_ref[:, h * LANES:(h + 1) * LANES] = jnp.broadcast_to(g[:, h:h + 1], (bm, LANES))
            b_ref[:, h * LANES:(h + 1) * LANES] = jnp.broadcast_to(
                beta[:, n_heads + h:n_heads + h + 1], (bm, LANES))

    wide = pl.BlockSpec((bm, n_heads * LANES), lambda i: (i, 0))
    return pl.pallas_call(
        body, name="gdn_gates", grid=(m // bm,),
        in_specs=[pl.BlockSpec((bm, LANES), lambda i: (i, 0))] + [pl.BlockSpec((1, LANES), lambda i: (0, 0))] * 2,
        out_specs=[wide, wide],
        out_shape=[jax.ShapeDtypeStruct((m, n_heads * LANES), F32)] * 2,
        compiler_params=_params("parallel"),
    )(ab, a_log, dt_bias)


def _gdn_gates_bwd(ab, a_log, dt_bias, dg_b, dbeta_b, n_heads):
    m = ab.shape[0]
    bm = _pick(m, (640, 384, 128))
    nsteps = m // bm

    def body(ab_ref, al_ref, dt_ref, dg_ref, db_ref, dab_ref, dal_ref, ddt_ref, acc_al, acc_dt):
        i = pl.program_id(0)
        x = ab_ref[...]
        lane = _iota2((bm, LANES), 1)
        live = (_iota2((bm, LANES), 0) + i * bm) >= PAD_ROWS
        dg = jnp.zeros((bm, LANES), F32)
        dbeta = jnp.zeros((bm, LANES), F32)
        for h in range(n_heads):
            dg = jnp.where(lane == h, dg_ref[:, h * LANES:(h + 1) * LANES], dg)
            dbeta = jnp.where(lane == n_heads + h, db_ref[:, h * LANES:(h + 1) * LANES], dbeta)
        dg = jnp.where(live, dg, 0.0)
        dbeta = jnp.where(live, dbeta, 0.0)
        nea = -jnp.exp(al_ref[...])
        pre = x + dt_ref[...]
        d_pre = dg * nea * _sigmoid(pre)
        beta = _sigmoid(x)
        dab_ref[...] = d_pre + dbeta * beta * (1.0 - beta)
        p_al = jnp.sum((dg * nea * _softplus(pre)).reshape(bm // 8, 8, LANES), axis=0)
        p_dt = jnp.sum(d_pre.reshape(bm // 8, 8, LANES), axis=0)

        @pl.when(i == 0)
        def _():
            acc_al[...] = p_al
            acc_dt[...] = p_dt

        @pl.when(i > 0)
        def _():
            acc_al[...] += p_al
            acc_dt[...] += p_dt

        @pl.when(i == nsteps - 1)
        def _():
            dal_ref[...] = jnp.sum(acc_al[...], axis=0, keepdims=True)
            ddt_ref[...] = jnp.sum(acc_dt[...], axis=0, keepdims=True)

    wide = pl.BlockSpec((bm, n_heads * LANES), lambda i: (i, 0))
    one = pl.BlockSpec((1, LANES), lambda i: (0, 0))
    nar = pl.BlockSpec((bm, LANES), lambda i: (i, 0))
    return pl.pallas_call(
        body, name="gdn_gates_bwd", grid=(nsteps,),
        in_specs=[nar, one, one, wide, wide],
        out_specs=[nar, one, one],
        out_shape=[jax.ShapeDtypeStruct((m, LANES), F32), jax.ShapeDtypeStruct((1, LANES), F32),
                   jax.ShapeDtypeStruct((1, LANES), F32)],
        scratch_shapes=[pltpu.VMEM((8, LANES), F32)] * 2,
        compiler_params=_params("arbitrary"),
    )(ab, a_log, dt_bias, dg_b, dbeta_b)


def _chunk_masks():
    c = GDN_CHUNK
    ri, ci = _iota2((c, c), 0), _iota2((c, c), 1)
    return ri >= ci, ri > ci, ri == ci


def _chunk_decay(gb):
    c = GDN_CHUNK
    incl, _, _ = _chunk_masks()
    g_cum = _hdot(incl.astype(F32), gb)
    g_row = _hdot(jnp.ones((c, LANES), F32), g_cum, NT) * (1.0 / LANES)
    diff = g_cum[:, :c] - g_row
    dmat = jnp.where(incl, jnp.exp(jnp.where(incl, diff, 0.0)), 0.0)
    return g_cum, dmat


def _unit_lower_inverse(a):
    _, _, eye = _chunk_masks()
    x = eye.astype(F32) - a
    p = a
    for _ in range(int(math.log2(GDN_CHUNK)) - 1):
        p = _hdot(p, p)
        x = x + _hdot(x, p)
    return x


def _gdn_prep(qkv, g_b, beta_b, n_heads):
    _, m, wd = qkv.shape
    c = GDN_CHUNK
    nc = m // c
    cb = _pick(nc, (5, 3, 2))
    rows = cb * c
    qscale = GDN_HEAD_DIM ** -0.5

    def body(qkv_ref, g_ref, b_ref, ub_ref, wc_ref, qd_ref, kd_ref, gc_ref, p_ref, t_ref):
        incl, strict, _ = _chunk_masks()
        for s in range(cb):
            sl = pl.ds(s * c, c)
            q = qkv_ref[0, sl, :] * qscale
            k = qkv_ref[1, sl, :]
            v = qkv_ref[2, sl, :]
            bb = b_ref[sl, :]
            g_cum, dmat = _chunk_decay(g_ref[sl, :])
            gam = jnp.exp(g_cum)
            kk = _bdot(k, k, NT)
            a = jnp.where(strict, bb[:, :c] * dmat * kk, 0.0)
            t = _unit_lower_inverse(a)
            ub_ref[sl, :] = _hdot(t, bb * v)
            wc_ref[sl, :] = _hdot(t, bb * gam * k)
            p_ref[s] = _bdot(q, k, NT) * dmat
            qd_ref[sl, :] = q * gam
            kd_ref[sl, :] = k * jnp.exp(g_cum[c - 1:c, :] - g_cum)
            gc_ref[sl, :] = g_cum
            t_ref[s] = t

    blk = pl.BlockSpec((rows, LANES), lambda h, n: (n, h))
    sq = pl.BlockSpec((None, cb, c, c), lambda h, n: (h, n, 0, 0))
    big = jax.ShapeDtypeStruct((m, wd), F32)
    small = jax.ShapeDtypeStruct((n_heads, nc, c, c), F32)
    return pl.pallas_call(
        body, name="gdn_prep", grid=(n_heads, nc // cb),
        in_specs=[pl.BlockSpec((3, rows, LANES), lambda h, n: (0, n, h)), blk, blk],
        out_specs=[blk] * 5 + [sq, sq],
        out_shape=[big] * 5 + [small, small],
        compiler_params=_params("parallel", "parallel"),
    )(qkv, g_b, beta_b)


def _gdn_prep_bwd(qkv, g_b, beta_b, ub, wc, tinv, dub, dwc, dqd, dkd, dp, dgl, n_heads):
    _, m, wd = qkv.shape
    c = GDN_CHUNK
    nc = m // c
    cb = _pick(nc, (5, 3, 2))
    rows = cb * c
    qscale = GDN_HEAD_DIM ** -0.5

    def body(qkv_ref, g_ref, b_ref, ub_ref, wc_ref, t_ref, dub_ref, dwc_ref, dqd_ref, dkd_ref, dp_ref, dgl_ref,
             dqkv_ref, dg_ref, db_ref):
        incl, strict, _ = _chunk_masks()
        ones_c = jnp.ones((c, LANES), F32)
        ones_l = jnp.ones((LANES, LANES), F32)
        last_row = _iota2((c, LANES), 0) == c - 1
        for s in range(cb):
            sl = pl.ds(s * c, c)
            q = qkv_ref[0, sl, :] * qscale
            k = qkv_ref[1, sl, :]
            v = qkv_ref[2, sl, :]
            bb = b_ref[sl, :]
            bcol = bb[:, :c]
            g_cum, dmat = _chunk_decay(g_ref[sl, :])
            gam = jnp.exp(g_cum)
            edec = jnp.exp(g_cum[c - 1:c, :] - g_cum)
            kk = _bdot(k, k, NT)
            qk = _bdot(q, k, NT)
            t = t_ref[s]
            drhs_u = _hdot(t, dub_ref[sl, :], TN)
            drhs_w = _hdot(t, dwc_ref[sl, :], TN)
            da = -jnp.where(strict, _bdot(drhs_u, ub_ref[sl, :], NT) + _bdot(drhs_w, wc_ref[sl, :], NT), 0.0)
            dbeta = _hdot(da * dmat * kk, ones_c)
            dd = da * bcol * kk
            dkk = da * bcol * dmat
            dk = _bdot(dkk, k) + _bdot(dkk, k, TN)
            dv = bb * drhs_u
            dbeta = dbeta + _hdot(drhs_u * v, ones_l)
            tk = _hdot(drhs_w * k, ones_l)
            dk = dk + bb * gam * drhs_w
            dbeta = dbeta + tk * gam
            dgam = bb * tk
            dpm = jnp.where(incl, dp_ref[s], 0.0)
            dpd = dpm * dmat
            dq = _bdot(dpd, k)
            dk = dk + _bdot(dpd, q, TN)
            dd = dd + dpm * qk
            dqd = dqd_ref[sl, :]
            dq = dq + dqd * gam
            dgam = dgam + _hdot(dqd * q, ones_l)
            dkd = dkd_ref[sl, :]
            dk = dk + dkd * edec
            skd = _hdot(dkd * k * edec, ones_l)
            dgc = dgam * gam - skd
            e = dd * dmat
            dgc = dgc + _hdot(e, ones_c) - _hdot(e, ones_c, TN)
            tot = _hdot(jnp.ones((c, c), F32), skd)
            dgc = dgc + jnp.where(last_row, tot + dgl_ref[s, 0:1, :], 0.0)
            dg_ref[sl, :] = _hdot((_iota2((c, c), 0) <= _iota2((c, c), 1)).astype(F32), dgc)
            db_ref[sl, :] = dbeta
            dqkv_ref[0, sl, :] = dq * qscale
            dqkv_ref[1, sl, :] = dk
            dqkv_ref[2, sl, :] = dv

    blk = pl.BlockSpec((rows, LANES), lambda h, n: (n, h))
    blk3 = pl.BlockSpec((3, rows, LANES), lambda h, n: (0, n, h))
    sq = pl.BlockSpec((None, cb, c, c), lambda h, n: (h, n, 0, 0))
    sgl = pl.BlockSpec((None, cb, 8, LANES), lambda h, n: (h, n, 0, 0))
    big = jax.ShapeDtypeStruct((m, wd), F32)
    return pl.pallas_call(
        body, name="gdn_prep_bwd", grid=(n_heads, nc // cb),
        in_specs=[blk3, blk, blk, blk, blk, sq, blk, blk, blk, blk, sq, sgl],
        out_specs=[blk3, blk, blk],
        out_shape=[jax.ShapeDtypeStruct((3, m, wd), F32), big, big],
        compiler_params=_params("parallel", "parallel"),
    )(qkv, g_b, beta_b, ub, wc, tinv, dub, dwc, dqd, dkd, dp, dgl)


def _gdn_scan(ub, wc, qd, kd, gc, p, n_heads):
    m, wd = ub.shape
    c = GDN_CHUNK
    nc = m // c
    dh = GDN_HEAD_DIM

    def body(ub_ref, wc_ref, qd_ref, kd_ref, gc_ref, p_ref, o_ref, ssave_ref, s_ref):
        n = pl.program_id(1)

        @pl.when(n == 0)
        def _():
            s_ref[...] = jnp.zeros_like(s_ref)

        s = s_ref[...]
        ssave_ref[...] = s
        u = ub_ref[...] - _bdot(wc_ref[...], s)
        o_ref[...] = _bdot(qd_ref[...], s) + _bdot(p_ref[...], u)
        egl = jnp.exp(gc_ref[c - 1:c, :])
        s_ref[...] = s * egl + _bdot(kd_ref[...], u, TN)

    blk = pl.BlockSpec((c, LANES), lambda h, n: (n, h))
    return pl.pallas_call(
        body, name="gdn_scan", grid=(n_heads, nc),
        in_specs=[blk] * 5 + [pl.BlockSpec((None, None, c, c), lambda h, n: (h, n, 0, 0))],
        out_specs=[blk, pl.BlockSpec((None, None, dh, dh), lambda h, n: (h, n, 0, 0))],
        out_shape=[jax.ShapeDtypeStruct((m, wd), F32), jax.ShapeDtypeStruct((n_heads, nc, dh, dh), F32)],
        scratch_shapes=[pltpu.VMEM((dh, dh), F32)],
        compiler_params=_params("parallel", "arbitrary"),
    )(ub, wc, qd, kd, gc, p)


def _gdn_scan_bwd(do, ssave, ub, wc, qd, kd, gc, p, n_heads):
    m, wd = ub.shape
    c = GDN_CHUNK
    nc = m // c
    dh = GDN_HEAD_DIM

    def body(do_ref, ss_ref, ub_ref, wc_ref, qd_ref, kd_ref, gc_ref, p_ref,
             dub_ref, dwc_ref, dqd_ref, dkd_ref, dp_ref, dgl_ref, ds_ref):
        n = pl.program_id(1)

        @pl.when(n == 0)
        def _():
            ds_ref[...] = jnp.zeros_like(ds_ref)

        ds_next = ds_ref[...]
        s = ss_ref[...]
        do_ = do_ref[...]
        wcv = wc_ref[...]
        kdv = kd_ref[...]
        u = ub_ref[...] - _bdot(wcv, s)
        egl = jnp.exp(gc_ref[c - 1:c, :])
        du = _bdot(p_ref[...], do_, TN) + _bdot(kdv, ds_next)
        dub_ref[...] = du
        dwc_ref[...] = -_bdot(du, s, NT)
        dp_ref[...] = _bdot(do_, u, NT)
        dqd_ref[...] = _bdot(do_, s, NT)
        dkd_ref[...] = _bdot(u, ds_next, NT)
        col = _hdot(jnp.ones((8, dh), F32), s * ds_next)
        dgl_ref[...] = egl * _hdot(col, jnp.ones((LANES, LANES), F32))
        ds_ref[...] = _bdot(qd_ref[...], do_, TN) + egl * ds_next - _bdot(wcv, du, TN)

    blk = pl.BlockSpec((c, LANES), lambda h, n: (nc - 1 - n, h))
    sq = pl.BlockSpec((None, None, c, c), lambda h, n: (h, nc - 1 - n, 0, 0))
    big = jax.ShapeDtypeStruct((m, wd), F32)
    return pl.pallas_call(
        body, name="gdn_scan_bwd", grid=(n_heads, nc),
        in_specs=[blk, pl.BlockSpec((None, None, dh, dh), lambda h, n: (h, nc - 1 - n, 0, 0))] + [blk] * 5 + [sq],
        out_specs=[blk] * 4 + [sq, pl.BlockSpec((None, None, 8, LANES), lambda h, n: (h, nc - 1 - n, 0, 0))],
        out_shape=[big] * 4 + [jax.ShapeDtypeStruct((n_heads, nc, c, c), F32),
                               jax.ShapeDtypeStruct((n_heads, nc, 8, LANES), F32)],
        scratch_shapes=[pltpu.VMEM((dh, dh), F32)],
        compiler_params=_params("parallel", "arbitrary"),
    )(do, ssave, ub, wc, qd, kd, gc, p)


def _gdn_outnorm(o, gate, gain, n_heads):
    m, wd = o.shape
    bm = _pick(m, (640, 384, 128))

    def body(o_ref, gt_ref, gn_ref, y_ref):
        gn = gn_ref[...]
        for h in range(n_heads):
            sl = slice(h * LANES, (h + 1) * LANES)
            x = o_ref[:, sl]
            gt = gt_ref[:, sl]
            r = lax.rsqrt(jnp.mean(x * x, axis=-1, keepdims=True) + NORM_EPS)
            y_ref[:, sl] = (x * r * gn * gt * _sigmoid(gt)).astype(BF16)

    row = pl.BlockSpec((bm, wd), lambda i: (i, 0))
    return pl.pallas_call(
        body, name="gdn_outnorm", grid=(m // bm,),
        in_specs=[row, row, pl.BlockSpec((1, LANES), lambda i: (0, 0))], out_specs=row,
        out_shape=jax.ShapeDtypeStruct((m, wd), BF16),
        compiler_params=_params("parallel"),
    )(o, gate, gain)


def _gdn_outnorm_bwd(o, gate, gain, dy, n_heads):
    m, wd = o.shape
    bm = _pick(m, (640, 384, 128))
    nsteps = m // bm

    def body(o_ref, gt_ref, gn_ref, dy_ref, do_ref, dgt_ref, dgn_ref, acc_ref):
        i = pl.program_id(0)
        gn = gn_ref[...]
        part = jnp.zeros((8, LANES), F32)
        for h in range(n_heads):
            sl = slice(h * LANES, (h + 1) * LANES)
            x = o_ref[:, sl]
            gt = gt_ref[:, sl]
            d_out = dy_ref[:, sl]
            r = lax.rsqrt(jnp.mean(x * x, axis=-1, keepdims=True) + NORM_EPS)
            xhat = x * r
            sg = _sigmoid(gt)
            dz = d_out * gt * sg
            dgt_ref[:, sl] = d_out * xhat * gn * sg * (1.0 + gt * (1.0 - sg))
            dxh = dz * gn
            do_ref[:, sl] = r * (dxh - xhat * jnp.mean(dxh * xhat, axis=-1, keepdims=True))
            part = part + jnp.sum((dz * xhat).reshape(bm // 8, 8, LANES), axis=0)

        @pl.when(i == 0)
        def _():
            acc_ref[...] = part

        @pl.when(i > 0)
        def _():
            acc_ref[...] += part

        @pl.when(i == nsteps - 1)
        def _():
            dgn_ref[...] = jnp.sum(acc_ref[...], axis=0, keepdims=True)

    row = pl.BlockSpec((bm, wd), lambda i: (i, 0))
    one = pl.BlockSpec((1, LANES), lambda i: (0, 0))
    return pl.pallas_call(
        body, name="gdn_outnorm_bwd", grid=(nsteps,),
        in_specs=[row, row, one, row], out_specs=[row, row, one],
        out_shape=[jax.ShapeDtypeStruct((m, wd), F32)] * 2 + [jax.ShapeDtypeStruct((1, LANES), F32)],
        scratch_shapes=[pltpu.VMEM((8, LANES), F32)],
        compiler_params=_params("arbitrary"),
    )(o, gate, gain, dy)


def _sb_consts(kind):
    ri, ci = _iota2((LANES, LANES), 0), _iota2((LANES, LANES), 1)
    tri = {"suffix_excl": ri > ci, "prefix_incl": ri <= ci, "prefix_excl": ri < ci}[kind]
    return jnp.concatenate([tri.astype(BF16), jnp.ones((LANES, LANES), BF16)], axis=1)


def _sb_scores(qm, kt, qi, kb):
    z = lax.dot_general(qm, kt, NT, preferred_element_type=F32)
    t = jnp.log(1.0 + jnp.exp(-jnp.abs(z)))
    lb = jnp.minimum(z, 0.0) - t
    lk = jnp.minimum(-z, 0.0) - t
    kpos = _iota2((LANES, LANES), 1) + kb * LANES
    qpos = _iota2((LANES, LANES), 0) + qi * LANES
    vis = (kpos < qpos) & (kpos >= PAD_ROWS)
    return lb, jnp.where(vis, lk, 0.0), vis


def _sb_attention(q, kv, n_heads):
    m, wd = q.shape
    npair = wd // LANES
    nq = m // LANES
    scale = SB_HEAD_DIM ** -0.5

    def body(q_ref, k_ref, v_ref, o_ref, lt_ref):
        qi = pl.program_id(1)
        lane = _iota2((LANES, LANES), 1)
        su = _sb_consts("suffix_excl")
        qs = q_ref[...] * scale
        out = jnp.zeros((LANES, LANES), F32)
        ltot = jnp.zeros((LANES, LANES), F32)
        for head in range(2):
            mh = (lane < SB_HEAD_DIM) if head == 0 else (lane >= SB_HEAD_DIM)
            qm = jnp.where(mh, qs, 0.0).astype(BF16)

            def step(j, carry, qm=qm):
                run, acc = carry
                kb = qi - j
                r0 = pl.multiple_of(kb * LANES, LANES)
                kt = k_ref[pl.ds(r0, LANES), :].astype(BF16)
                vt = v_ref[pl.ds(r0, LANES), :].astype(BF16)
                lb, lk, vis = _sb_scores(qm, kt, qi, kb)
                cs = _split_dot(lk, su, parts=3)
                w = jnp.where(vis, jnp.exp(lb + cs[:, :LANES] + run), 0.0)
                acc = acc + lax.dot_general(w.astype(BF16), vt, NN, preferred_element_type=F32)
                return run + cs[:, LANES:], acc

            zero = jnp.zeros((LANES, LANES), F32)
            run, acc = lax.fori_loop(0, qi + 1, step, (zero, zero))
            out = jnp.where(mh, acc, out)
            ltot = jnp.where(mh, run, ltot)
        o_ref[...] = out
        lt_ref[...] = ltot

    qspec = pl.BlockSpec((LANES, LANES), lambda hp, qi: (qi, hp))
    return pl.pallas_call(
        body, name="sb_attention", grid=(npair, nq),
        in_specs=[qspec, pl.BlockSpec((m, LANES), lambda hp, qi: (0, hp)),
                  pl.BlockSpec((m, LANES), lambda hp, qi: (0, npair + hp))],
        out_specs=[qspec, qspec],
        out_shape=[jax.ShapeDtypeStruct((m, wd), F32)] * 2,
        compiler_params=_params("parallel", "arbitrary"),
    )(q, kv, kv)


def _sb_attention_bwd(q, kv, o, ltot, do):
    m, wd = q.shape
    npair = wd // LANES
    nq = m // LANES
    scale = SB_HEAD_DIM ** -0.5

    def body(q_ref, k_ref, v_ref, o_ref, lt_ref, do_ref, dq_ref, dk_ref, dv_ref):
        qi = pl.program_id(1)

        @pl.when(qi == 0)
        def _():
            dk_ref[...] = jnp.zeros_like(dk_ref)
            dv_ref[...] = jnp.zeros_like(dv_ref)

        lane = _iota2((LANES, LANES), 1)
        pi = _sb_consts("prefix_incl")
        px = _sb_consts("prefix_excl")
        qs = q_ref[...] * scale
        do_ = do_ref[...]
        ltot = lt_ref[...]
        dq_out = jnp.zeros((LANES, LANES), F32)
        for head in range(2):
            mh = (lane < SB_HEAD_DIM) if head == 0 else (lane >= SB_HEAD_DIM)
            qm = jnp.where(mh, qs, 0.0).astype(BF16)
            dom = jnp.where(mh, do_, 0.0).astype(BF16)
            lt_h = jnp.broadcast_to(ltot[:, head * SB_HEAD_DIM:head * SB_HEAD_DIM + 1], (LANES, LANES))

            def step(j, carry, qm=qm, dom=dom, lt_h=lt_h):
                run_k, run_p, dq_acc = carry
                r0 = pl.multiple_of(j * LANES, LANES)
                kt = k_ref[pl.ds(r0, LANES), :].astype(BF16)
                vt = v_ref[pl.ds(r0, LANES), :].astype(BF16)
                lb, lk, vis = _sb_scores(qm, kt, qi, j)
                cs = _split_dot(lk, pi, parts=3)
                w = jnp.where(vis, jnp.exp(lb + lt_h - (run_k + cs[:, :LANES])), 0.0)
                da = lax.dot_general(dom, vt, NT, preferred_element_type=F32)
                pw = w * da
                ps = _split_dot(pw, px)
                beta = jnp.exp(lb)
                dz = jnp.where(vis, pw * (1.0 - beta) - beta * (run_p + ps[:, :LANES]), 0.0).astype(BF16)
                dq_acc = dq_acc + lax.dot_general(dz, kt, NN, preferred_element_type=F32)
                dk_ref[pl.ds(r0, LANES), :] += lax.dot_general(dz, qm, TN, preferred_element_type=F32)
                dv_ref[pl.ds(r0, LANES), :] += lax.dot_general(w.astype(BF16), dom, TN, preferred_element_type=F32)
                return run_k + cs[:, LANES:], run_p + ps[:, LANES:], dq_acc

            zero = jnp.zeros((LANES, LANES), F32)
            _, _, dq_acc = lax.fori_loop(0, qi + 1, step, (zero, zero, zero))
            dq_out = jnp.where(mh, dq_acc * scale, dq_out)
        dq_ref[...] = dq_out

    qspec = pl.BlockSpec((LANES, LANES), lambda hp, qi: (qi, hp))
    strip = pl.BlockSpec((m, LANES), lambda hp, qi: (0, hp))
    big = jax.ShapeDtypeStruct((m, wd), F32)
    return pl.pallas_call(
        body, name="sb_attention_bwd", grid=(npair, nq),
        in_specs=[qspec, strip, pl.BlockSpec((m, LANES), lambda hp, qi: (0, npair + hp)), qspec, qspec, qspec],
        out_specs=[qspec, strip, strip],
        out_shape=[big, big, big],
        compiler_params=_params("parallel", "arbitrary"),
    )(q, kv, kv, o, ltot, do)


def _mesh_pos():
    return lax.axis_index("x"), lax.axis_index("y"), lax.axis_index("c")


def _exchange(buf, name, scatter):
    rows = buf.shape[-2]

    def body(src_ref, out_ref, send_sems, recv_sems, local_sem):
        x, y, c = _mesh_pos()
        me = 4 * x + 2 * y + c

        def slab(d):
            return src_ref.at[d] if scatter else src_ref

        local = pltpu.make_async_copy(slab(me), out_ref.at[me], local_sem)
        local.start()
        sends = []
        peers = []
        for k in range(1, N_DEV):
            px = 1 - x if k & 4 else x
            py = 1 - y if k & 2 else y
            pc = 1 - c if k & 1 else c
            pid = 4 * px + 2 * py + pc
            cp = pltpu.make_async_remote_copy(
                src_ref=slab(pid), dst_ref=out_ref.at[me],
                send_sem=send_sems.at[k - 1], recv_sem=recv_sems.at[k - 1],
                device_id=(px, py, pc), device_id_type=pl.DeviceIdType.MESH)
            cp.start()
            sends.append(cp)
            peers.append(pid)
        for k in range(1, N_DEV):
            pltpu.make_async_remote_copy(
                src_ref=slab(me), dst_ref=out_ref.at[peers[k - 1]],
                send_sem=send_sems.at[k - 1], recv_sem=recv_sems.at[k - 1],
                device_id=(x, y, c), device_id_type=pl.DeviceIdType.MESH).wait_recv()
        for cp in sends:
            cp.wait_send()
        local.wait()

    return pl.pallas_call(
        body, name=name,
        in_specs=[pl.BlockSpec(memory_space=pl.ANY)], out_specs=pl.BlockSpec(memory_space=pl.ANY),
        out_shape=jax.ShapeDtypeStruct((N_DEV, rows, FLAT_COLS), buf.dtype),
        scratch_shapes=[pltpu.SemaphoreType.DMA((N_DEV - 1,)), pltpu.SemaphoreType.DMA((N_DEV - 1,)),
                        pltpu.SemaphoreType.DMA],
        compiler_params=pltpu.CompilerParams(has_side_effects=True),
    )(buf)


def _adamw(parts, w, mom, var):
    rows = w.shape[0]
    br = _pick(rows, (80, 64, 40, 32, 16, 8))
    c1 = 1.0 - ADAM_B1 ** ADAM_STEP
    c2 = 1.0 - ADAM_B2 ** ADAM_STEP

    def body(p_ref, w_ref, m_ref, v_ref, g_ref, d_ref, nm_ref, nv_ref):
        g = p_ref[0]
        for s in range(1, N_DEV):
            g = g + p_ref[s]
        m_new = ADAM_B1 * m_ref[...] + (1.0 - ADAM_B1) * g
        v_new = ADAM_B2 * v_ref[...] + (1.0 - ADAM_B2) * (g * g)
        m_hat = m_new / c1
        v_hat = v_new / c2
        g_ref[...] = g
        d_ref[...] = -ADAM_LR * (m_hat / (jnp.sqrt(v_hat) + ADAM_EPS) + ADAM_WD * w_ref[...])
        nm_ref[...] = m_new
        nv_ref[...] = v_new

    row = pl.BlockSpec((br, FLAT_COLS), lambda i: (i, 0))
    flat = jax.ShapeDtypeStruct((rows, FLAT_COLS), F32)
    return pl.pallas_call(
        body, name="adamw", grid=(rows // br,),
        in_specs=[pl.BlockSpec((N_DEV, br, FLAT_COLS), lambda i: (0, i, 0)), row, row, row],
        out_specs=[row] * 4, out_shape=[flat] * 4,
        compiler_params=_params("parallel"),
    )(parts, w, mom, var)


SHARDED = (
    ("meta_tokens", 1), ("gdn_norm_g", 1), ("gdn_w_in", 2), ("gdn_conv_w", 2), ("gdn_w_out", 1),
    ("w_kv", 1), ("sb_w_q", 1), ("sb_w_o", 1), ("ffn_w_gate_up", 2), ("ffn_w_down", 1))
REPLICATED = ("gdn_a_log", "gdn_dt_bias", "gdn_onorm_g", "kv_norm_g", "sb_norm_g", "ffn_norm_g", "final_norm_g")
WEIGHTS = ("meta_tokens", "gdn_norm_g", "gdn_w_in", "gdn_conv_w", "gdn_a_log", "gdn_dt_bias", "gdn_onorm_g",
           "gdn_w_out", "kv_norm_g", "w_kv", "sb_norm_g", "sb_w_q", "sb_w_o", "ffn_norm_g", "ffn_w_gate_up",
           "ffn_w_down", "final_norm_g")


def _seg_rows(size):
    return -(-size // FLAT_COLS)


def _pack(arrays, lead=()):
    nl = len(lead)
    segs = []
    for a in arrays:
        size = math.prod(a.shape[nl:])
        flat = a.reshape(lead + (size,))
        pad = _seg_rows(size) * FLAT_COLS - size
        if pad:
            flat = jnp.pad(flat, [(0, 0)] * nl + [(0, pad)])
        segs.append(flat.reshape(lead + (-1, FLAT_COLS)))
    rows = sum(s.shape[nl] for s in segs)
    tail = (-rows) % 8
    if tail:
        segs.append(jnp.zeros(lead + (tail, FLAT_COLS), F32))
    return jnp.concatenate(segs, axis=nl)


def _unpack(flat, shapes, lead=()):
    nl = len(lead)
    out, r = [], 0
    for shp in shapes:
        size = math.prod(shp)
        nr = _seg_rows(size)
        seg = lax.slice_in_dim(flat, r, r + nr, axis=nl).reshape(lead + (nr * FLAT_COLS,))
        out.append(lax.slice_in_dim(seg, 0, size, axis=nl).reshape(lead + tuple(shp)))
        r += nr
    return out


def _merge_shards(g, axis):
    t = jnp.moveaxis(g, 0, axis)
    shp = t.shape
    return t.reshape(shp[:axis] + (shp[axis] * shp[axis + 1],) + shp[axis + 2:])


def _split_shards(full, axis):
    shp = full.shape
    t = full.reshape(shp[:axis] + (N_DEV, shp[axis] // N_DEV) + shp[axis + 1:])
    return jnp.moveaxis(t, axis, 0)


def _pad_lanes(a, width=LANES):
    return jnp.pad(a, ((0, 0), (0, width - a.shape[1])))


def _local_step(x, target, w):
    d = x.shape[1]
    gh = w["gdn_a_log"].shape[1]
    gw = gh * GDN_HEAD_DIM
    w_in = w["gdn_w_in"][0]
    w_qkv, w_gate = w_in[:, :3 * gw].astype(BF16), w_in[:, 3 * gw:4 * gw].astype(BF16)
    w_ab = _pad_lanes(w_in[:, 4 * gw:]).astype(BF16)
    conv_w = w["gdn_conv_w"][0]
    a_log, dt_bias = _pad_lanes(w["gdn_a_log"]), _pad_lanes(w["gdn_dt_bias"])
    w_out = w["gdn_w_out"][0].astype(BF16)
    sbw = w["sb_w_q"].shape[2]
    sb_heads = sbw // SB_HEAD_DIM
    w_kv = w["w_kv"].astype(BF16)
    w_k, w_v = w_kv[:, :sbw], w_kv[:, sbw:]
    w_q, w_o = w["sb_w_q"][0].astype(BF16), w["sb_w_o"][0].astype(BF16)
    w_gu = [w["ffn_w_gate_up"][l].astype(BF16) for l in range(2)]
    w_dn = [w["ffn_w_down"][l].astype(BF16) for l in range(2)]
    ffn_g = [w["ffn_norm_g"][l:l + 1] for l in range(2)]
    kv_g, fin_g = w["kv_norm_g"][None], w["final_norm_g"][None]

    h0 = jnp.concatenate([jnp.zeros((PAD_ROWS, d), F32), w["meta_tokens"], x], axis=0)

    (n1,) = _rmsnorm(h0, [w["gdn_norm_g"]], "norm_gdn")
    proj = _matmul(n1, w_qkv, "nn", "mm_gdn_qkv")
    gate = _matmul(n1, w_gate, "nn", "mm_gdn_gate")
    ab = _matmul(n1, w_ab, "nn", "mm_gdn_ab")
    qkv = _gdn_conv(proj, conv_w, gh)
    g_b, beta_b = _gdn_gates(ab, a_log, dt_bias, gh)
    ub, wc, qd, kd, gc, pmat, tinv = _gdn_prep(qkv, g_b, beta_b, gh)
    o_gdn, ssave = _gdn_scan(ub, wc, qd, kd, gc, pmat, gh)
    o2 = _gdn_outnorm(o_gdn, gate, w["gdn_onorm_g"], gh)
    h1 = _matmul(o2, w_out, "nn", "mm_gdn_out", add=h0)

    def ffn_fwd(h, l):
        (n,) = _rmsnorm(h, [ffn_g[l]], f"norm_ffn{l}")
        gu = _matmul(n, w_gu[l], "nn", f"mm_ffn{l}_gu")
        act = _swiglu(gu, f"swiglu{l}")
        return n, gu, act, _matmul(act, w_dn[l], "nn", f"mm_ffn{l}_down", add=h)

    n2, gu0, act0, h2 = ffn_fwd(h1, 0)

    nkv, n3 = _rmsnorm(h2, [kv_g, w["sb_norm_g"]], "norm_kv_sb")
    kv = _matmul(nkv, w_kv, "nn", "mm_kv")
    q_sb = _matmul(n3, w_q, "nn", "mm_sb_q")
    o_sb, ltot = _sb_attention(q_sb, kv, sb_heads)
    h3 = _matmul(o_sb, w_o, "nn", "mm_sb_o", add=h2)
    n4, gu1, act1, h4 = ffn_fwd(h3, 1)

    loss, dh4, d_fin = _loss_head(h4, fin_g, target)

    def ffn_bwd(dh, h, n, gu, act, l):
        d_act = _matmul(dh, w_dn[l], "nt", f"mm_ffn{l}_dact")
        dw_dn = _matmul(act, dh, "tn", f"mm_ffn{l}_dwdown")
        dgu = _swiglu_bwd(gu, d_act, f"swiglu_bwd{l}")
        dw_gu = _matmul(n, dgu, "tn", f"mm_ffn{l}_dwgu")
        dn = _matmul(dgu, w_gu[l], "nt", f"mm_ffn{l}_dn")
        dh_in, (dg,) = _rmsnorm_bwd(h, [ffn_g[l]], [dn], dh, f"norm_ffn{l}_bwd")
        return dh_in, dw_gu, dw_dn, dg

    dh3, dw_gu1, dw_dn1, dg_ffn1 = ffn_bwd(dh4, h3, n4, gu1, act1, 1)

    do_sb = _matmul(dh3, w_o, "nt", "mm_sb_do")
    dw_o = _matmul(o_sb, dh3, "tn", "mm_sb_dwo")
    dq_sb, dk_sb, dv_sb = _sb_attention_bwd(q_sb, kv, o_sb, ltot, do_sb)
    dw_q = _matmul(n3, dq_sb, "tn", "mm_sb_dwq")
    dn3 = _matmul(dq_sb, w_q, "nt", "mm_sb_dn")
    dw_k = _matmul(nkv, dk_sb, "tn", "mm_dwk")
    dw_v = _matmul(nkv, dv_sb, "tn", "mm_dwv")
    dnkv = _matmul(dk_sb, w_k, "nt", "mm_dnk")
    dnkv = _matmul(dv_sb, w_v, "nt", "mm_dnv", add=dnkv)
    dh2, (dg_kv, dg_sb) = _rmsnorm_bwd(h2, [kv_g, w["sb_norm_g"]], [dnkv, dn3], dh3, "norm_kv_sb_bwd")

    dh1, dw_gu0, dw_dn0, dg_ffn0 = ffn_bwd(dh2, h1, n2, gu0, act0, 0)

    do2 = _matmul(dh1, w_out, "nt", "mm_gdn_do")
    dw_out = _matmul(o2, dh1, "tn", "mm_gdn_dwout")
    do_gdn, dgate, d_onorm = _gdn_outnorm_bwd(o_gdn, gate, w["gdn_onorm_g"], do2, gh)
    dub, dwc, dqd, dkd, dpm, dgl = _gdn_scan_bwd(do_gdn, ssave, ub, wc, qd, kd, gc, pmat, gh)
    dqkv, dg_b, dbeta_b = _gdn_prep_bwd(qkv, g_b, beta_b, ub, wc, tinv, dub, dwc, dqd, dkd, dpm, dgl, gh)
    dab, d_alog, d_dtb = _gdn_gates_bwd(ab, a_log, dt_bias, dg_b, dbeta_b, gh)
    dproj, d_conv = _gdn_conv_bwd(proj, conv_w, dqkv, gh)
    dw_qkv = _matmul(n1, dproj, "tn", "mm_gdn_dwqkv")
    dw_gate = _matmul(n1, dgate, "tn", "mm_gdn_dwgate")
    dw_ab = _matmul(n1, dab, "tn", "mm_gdn_dwab")
    dn1 = _matmul(dproj, w_qkv, "nt", "mm_gdn_dn_qkv")
    dn1 = _matmul(dgate, w_gate, "nt", "mm_gdn_dn_gate", add=dn1)
    dn1 = _matmul(dab, w_ab, "nt", "mm_gdn_dn_ab", add=dn1)
    dh0, (dg_gdn,) = _rmsnorm_bwd(h0, [w["gdn_norm_g"]], [dn1], dh1, "norm_gdn_bwd")

    grads = {
        "meta_tokens": dh0[PAD_ROWS:LANES],
        "gdn_norm_g": dg_gdn,
        "gdn_w_in": jnp.concatenate([dw_qkv, dw_gate, dw_ab[:, :2 * gh]], axis=1)[None],
        "gdn_conv_w": d_conv[None],
        "gdn_a_log": d_alog[:, :gh],
        "gdn_dt_bias": d_dtb[:, :gh],
        "gdn_onorm_g": d_onorm,
        "gdn_w_out": dw_out[None],
        "kv_norm_g": dg_kv[0],
        "w_kv": jnp.concatenate([dw_k, dw_v], axis=1),
        "sb_norm_g": dg_sb,
        "sb_w_q": dw_q[None],
        "sb_w_o": dw_o[None],
        "ffn_norm_g": jnp.concatenate([dg_ffn0, dg_ffn1], axis=0),
        "ffn_w_gate_up": jnp.stack([dw_gu0, dw_gu1]),
        "ffn_w_down": jnp.stack([dw_dn0, dw_dn1]),
        "final_norm_g": d_fin[0],
    }
    return loss, dh0[LANES:], grads


def kernel(x, meta_tokens, gdn_norm_g, gdn_w_in, gdn_conv_w, gdn_a_log, gdn_dt_bias, gdn_onorm_g, gdn_w_out, kv_norm_g, w_kv, sb_norm_g, sb_w_q, sb_w_o, ffn_norm_g, ffn_w_gate_up, ffn_w_down, final_norm_g, loss_target, m_meta_tokens, m_gdn_norm_g, m_gdn_w_in, m_gdn_conv_w, m_gdn_a_log, m_gdn_dt_bias, m_gdn_onorm_g, m_gdn_w_out, m_kv_norm_g, m_w_kv, m_sb_norm_g, m_sb_w_q, m_sb_w_o, m_ffn_norm_g, m_ffn_w_gate_up, m_ffn_w_down, m_final_norm_g, v_meta_tokens, v_gdn_norm_g, v_gdn_w_in, v_gdn_conv_w, v_gdn_a_log, v_gdn_dt_bias, v_gdn_onorm_g, v_gdn_w_out, v_kv_norm_g, v_w_kv, v_sb_norm_g, v_sb_w_q, v_sb_w_o, v_ffn_norm_g, v_ffn_w_gate_up, v_ffn_w_down, v_final_norm_g):
    given = dict(locals())
    shard_names = [n for n, _ in SHARDED]
    order = shard_names + list(REPLICATED)
    local_shapes = [given[n].shape for n in order]

    gathered = _exchange(_pack([given[n] for n in shard_names]), "all_gather_weights", scatter=False)
    parts = _unpack(gathered, [given[n].shape for n in shard_names], lead=(N_DEV,))
    full = {n: _merge_shards(p, ax) for (n, ax), p in zip(SHARDED, parts)}
    full.update({n: given[n] for n in REPLICATED})

    loss_b, grad_x, grads = _local_step(x[0], loss_target[0], full)

    send = _pack([_split_shards(grads[n], ax) for n, ax in SHARDED]
                 + [jnp.broadcast_to(grads[n][None], (N_DEV,) + grads[n].shape) for n in REPLICATED], lead=(N_DEV,))
    recv = _exchange(send, "scatter_gradients", scatter=True)

    flat = lambda prefix: _pack([given[prefix + n] for n in order])
    g_f, d_f, m_f, v_f = _adamw(recv, flat(""), flat("m_"), flat("v_"))
    outs = {}
    for tag, f in (("grad", g_f), ("delta", d_f), ("new_m", m_f), ("new_v", v_f)):
        outs[tag] = dict(zip(order, _unpack(f, local_shapes)))

    loss = lax.psum(loss_b[0, 0], ("x", "y", "c"))
    result = [loss, grad_x[None]]
    for tag in ("grad", "delta", "new_m", "new_v"):
        result += [outs[tag][n] for n in WEIGHTS]
    return tuple(result)
```

```python
import functools
import math

import jax
import jax.numpy as jnp
from jax import lax
from jax.experimental import pallas as pl
from jax.experimental.pallas import tpu as pltpu

F32 = jnp.float32
BF16 = jnp.bfloat16
HIGHEST = lax.Precision.HIGHEST

N_DEV = 8
LANES = 128
N_META = 16
PAD_ROWS = LANES - N_META
GDN_CHUNK = 64
GDN_HEAD_DIM = 128
SB_HEAD_DIM = 64
SB_UNROLL = 8
SB_UNROLL_BWD = 4
CONV_WIDTH = 4
NORM_EPS = 1e-6
FLAT_COLS = 1024
FLAT_ROW_BLOCK = 64
VMEM_LIMIT = 56 * 1024 * 1024

ADAM_LR = 0.001
ADAM_B1 = 0.9
ADAM_B2 = 0.999
ADAM_EPS = 1e-08
ADAM_WD = 0.01
ADAM_STEP = 10

NN = (((1,), (0,)), ((), ()))
NT = (((1,), (1,)), ((), ()))
TN = (((0,), (0,)), ((), ()))


def _params(*sem):
    return pltpu.CompilerParams(dimension_semantics=sem, vmem_limit_bytes=VMEM_LIMIT)


def _pick(n, cands):
    for c in cands:
        if n % c == 0:
            return c
    return n


def _bdot(a, b, dims=NN):
    return lax.dot_general(a.astype(BF16), b.astype(BF16), dims, preferred_element_type=F32)


def _hdot(a, b, dims=NN):
    return lax.dot_general(a, b, dims, preferred_element_type=F32, precision=HIGHEST)


def _split_dot(a, m, parts=2):
    out = None
    for _ in range(parts):
        piece = a.astype(BF16)
        a = a - piece.astype(F32)
        term = lax.dot_general(piece, m, NN, preferred_element_type=F32)
        out = term if out is None else out + term
    return out


def _sigmoid(x):
    return 1.0 / (1.0 + jnp.exp(-x))


def _iota2(shape, axis):
    return lax.broadcasted_iota(jnp.int32, shape, axis)


def _matmul(a, b, mode, name, out_dtype=F32, add=None):
    if mode == "nn":
        (m, k), n = a.shape, b.shape[1]
    elif mode == "nt":
        (m, k), n = a.shape, b.shape[0]
    else:
        (k, m), n = a.shape, b.shape[1]
    row_c = (640, 512, 384, 256, 128)
    col_c = (1024, 1408, 768, 512, 384, 256, 128)
    if mode == "tn":
        bm, bn, bk = _pick(m, col_c), _pick(n, col_c), _pick(k, row_c)
    else:
        bm, bn, bk = _pick(m, row_c), _pick(n, col_c), _pick(k, (1024, 1408, 768, 512, 256, 128))
    nk = k // bk
    dims = {"nn": NN, "nt": NT, "tn": TN}[mode]
    a_spec = {"nn": pl.BlockSpec((bm, bk), lambda i, j, q: (i, q)),
              "nt": pl.BlockSpec((bm, bk), lambda i, j, q: (i, q)),
              "tn": pl.BlockSpec((bk, bm), lambda i, j, q: (q, i))}[mode]
    b_spec = {"nn": pl.BlockSpec((bk, bn), lambda i, j, q: (q, j)),
              "nt": pl.BlockSpec((bn, bk), lambda i, j, q: (j, q)),
              "tn": pl.BlockSpec((bk, bn), lambda i, j, q: (q, j))}[mode]
    o_spec = pl.BlockSpec((bm, bn), lambda i, j, q: (i, j))
    has_add = add is not None

    def body(*refs):
        if has_add:
            a_ref, b_ref, add_ref, o_ref, acc_ref = refs
        else:
            a_ref, b_ref, o_ref, acc_ref = refs
        q = pl.program_id(2)
        part = _bdot(a_ref[...], b_ref[...], dims)

        @pl.when(q == 0)
        def _():
            acc_ref[...] = part

        @pl.when(q > 0)
        def _():
            acc_ref[...] += part

        @pl.when(q == nk - 1)
        def _():
            r = acc_ref[...]
            if has_add:
                r = r + add_ref[...]
            o_ref[...] = r.astype(out_dtype)

    ins = [a, b] + ([add] if has_add else [])
    in_specs = [a_spec, b_spec] + ([o_spec] if has_add else [])
    return pl.pallas_call(
        body, name=name, grid=(m // bm, n // bn, nk),
        in_specs=in_specs, out_specs=o_spec,
        out_shape=jax.ShapeDtypeStruct((m, n), out_dtype),
        scratch_shapes=[pltpu.VMEM((bm, bn), F32)],
        compiler_params=_params("parallel", "parallel", "arbitrary"),
    )(*ins)


def _rmsnorm(h, gains, name):
    m, d = h.shape
    bm = _pick(m, (640, 384, 128))
    ng = len(gains)

    def body(*refs):
        h_ref, g_refs, o_refs = refs[0], refs[1:1 + ng], refs[1 + ng:]
        x = h_ref[...]
        xhat = x * lax.rsqrt(jnp.mean(x * x, axis=-1, keepdims=True) + NORM_EPS)
        for g_ref, o_ref in zip(g_refs, o_refs):
            o_ref[...] = (xhat * g_ref[...]).astype(BF16)

    row = pl.BlockSpec((bm, d), lambda i: (i, 0))
    gain = pl.BlockSpec((1, d), lambda i: (0, 0))
    return pl.pallas_call(
        body, name=name, grid=(m // bm,),
        in_specs=[row] + [gain] * ng, out_specs=[row] * ng,
        out_shape=[jax.ShapeDtypeStruct((m, d), BF16)] * ng,
        compiler_params=_params("parallel"),
    )(h, *gains)


def _rmsnorm_bwd(h, gains, dns, dres, name):
    m, d = h.shape
    bm = _pick(m, (640, 384, 128))
    ng = len(gains)
    nsteps = m // bm

    def body(*refs):
        h_ref, dres_ref = refs[0], refs[1]
        g_refs = refs[2:2 + ng]
        dn_refs = refs[2 + ng:2 + 2 * ng]
        dh_ref = refs[2 + 2 * ng]
        dg_refs = refs[3 + 2 * ng:3 + 3 * ng]
        acc_refs = refs[3 + 3 * ng:]
        i = pl.program_id(0)
        x = h_ref[...]
        r = lax.rsqrt(jnp.mean(x * x, axis=-1, keepdims=True) + NORM_EPS)
        xhat = x * r
        dh = dres_ref[...]
        for g_ref, dn_ref, dg_ref, acc_ref in zip(g_refs, dn_refs, dg_refs, acc_refs):
            dn = dn_ref[...]
            dy = dn * g_ref[...]
            dh = dh + r * (dy - xhat * jnp.mean(dy * xhat, axis=-1, keepdims=True))
            part = jnp.sum((dn * xhat).reshape(bm // 8, 8, d), axis=0)

            @pl.when(i == 0)
            def _():
                acc_ref[...] = part

            @pl.when(i > 0)
            def _():
                acc_ref[...] += part

            @pl.when(i == nsteps - 1)
            def _():
                dg_ref[...] = jnp.sum(acc_ref[...], axis=0, keepdims=True)

        dh_ref[...] = dh

    row = pl.BlockSpec((bm, d), lambda i: (i, 0))
    gain = pl.BlockSpec((1, d), lambda i: (0, 0))
    outs = pl.pallas_call(
        body, name=name, grid=(nsteps,),
        in_specs=[row, row] + [gain] * ng + [row] * ng,
        out_specs=[row] + [gain] * ng,
        out_shape=[jax.ShapeDtypeStruct((m, d), F32)] + [jax.ShapeDtypeStruct((1, d), F32)] * ng,
        scratch_shapes=[pltpu.VMEM((8, d), F32)] * ng,
        compiler_params=_params("arbitrary"),
    )(h, dres, *gains, *dns)
    return outs[0], list(outs[1:])


def _loss_head(h, gain, target):
    m, d = h.shape
    nsteps = m // LANES

    def body(h_ref, g_ref, t_ref, loss_ref, dh_ref, dg_ref, lacc, gacc):
        i = pl.program_id(0)

        @pl.when(i == 0)
        def _():
            lacc[...] = jnp.zeros_like(lacc)
            gacc[...] = jnp.zeros_like(gacc)
            dh_ref[...] = jnp.zeros_like(dh_ref)

        @pl.when(i > 0)
        def _():
            x = h_ref[...]
            r = lax.rsqrt(jnp.mean(x * x, axis=-1, keepdims=True) + NORM_EPS)
            xhat = x * r
            g = g_ref[...]
            err = xhat * g - t_ref[...]
            lacc[...] += jnp.sum((err * err).reshape(LANES // 8, 8, d), axis=0)
            dn = err * (1.0 / d)
            dy = dn * g
            dh_ref[...] = r * (dy - xhat * jnp.mean(dy * xhat, axis=-1, keepdims=True))
            gacc[...] += jnp.sum((dn * xhat).reshape(LANES // 8, 8, d), axis=0)

        @pl.when(i == nsteps - 1)
        def _():
            tot = jnp.sum(jnp.sum(lacc[...], axis=1, keepdims=True), axis=0, keepdims=True)
            loss_ref[...] = jnp.broadcast_to(tot * (0.5 / d), (1, LANES))
            dg_ref[...] = jnp.sum(gacc[...], axis=0, keepdims=True)

    row = pl.BlockSpec((LANES, d), lambda i: (i, 0))
    trow = pl.BlockSpec((LANES, d), lambda i: (jnp.maximum(i - 1, 0), 0))
    gain_spec = pl.BlockSpec((1, d), lambda i: (0, 0))
    return pl.pallas_call(
        body, name="loss_head", grid=(nsteps,),
        in_specs=[row, gain_spec, trow],
        out_specs=[pl.BlockSpec((1, LANES), lambda i: (0, 0)), row, gain_spec],
        out_shape=[jax.ShapeDtypeStruct((1, LANES), F32), jax.ShapeDtypeStruct((m, d), F32),
                   jax.ShapeDtypeStruct((1, d), F32)],
        scratch_shapes=[pltpu.VMEM((8, d), F32), pltpu.VMEM((8, d), F32)],
        compiler_params=_params("arbitrary"),
    )(h, gain, target)


def _swiglu(gu, name):
    m, f2 = gu.shape
    f = f2 // 2

    def body(gu_ref, o_ref):
        g = gu_ref[:, :f]
        u = gu_ref[:, f:]
        o_ref[...] = (g * _sigmoid(g) * u).astype(BF16)

    return pl.pallas_call(
        body, name=name, grid=(m // LANES,),
        in_specs=[pl.BlockSpec((LANES, f2), lambda i: (i, 0))],
        out_specs=pl.BlockSpec((LANES, f), lambda i: (i, 0)),
        out_shape=jax.ShapeDtypeStruct((m, f), BF16),
        compiler_params=_params("parallel"),
    )(gu)


def _swiglu_bwd(gu, da, name):
    m, f2 = gu.shape
    f = f2 // 2

    def body(gu_ref, da_ref, o_ref):
        g = gu_ref[:, :f]
        u = gu_ref[:, f:]
        da_ = da_ref[...]
        s = _sigmoid(g)
        o_ref[:, :f] = (da_ * u * s * (1.0 + g * (1.0 - s))).astype(BF16)
        o_ref[:, f:] = (da_ * g * s).astype(BF16)

    return pl.pallas_call(
        body, name=name, grid=(m // LANES,),
        in_specs=[pl.BlockSpec((LANES, f2), lambda i: (i, 0)), pl.BlockSpec((LANES, f), lambda i: (i, 0))],
        out_specs=pl.BlockSpec((LANES, f2), lambda i: (i, 0)),
        out_shape=jax.ShapeDtypeStruct((m, f2), BF16),
        compiler_params=_params("parallel"),
    )(gu, da)


def _conv_taps(xa, w_ref):
    acc = xa * w_ref[CONV_WIDTH - 1:CONV_WIDTH, :]
    for i in range(CONV_WIDTH - 1):
        acc = acc + pltpu.roll(xa, CONV_WIDTH - 1 - i, 0) * w_ref[i:i + 1, :]
    return acc[8:]


def _gdn_conv(proj, conv_w, n_heads):
    m, w3 = proj.shape
    wd = w3 // 3
    nblk = m // LANES

    def body(x_ref, w_ref, o_ref):
        c = pl.program_id(0)
        is_qk = c < 2 * n_heads

        def finish(cv):
            s = cv * _sigmoid(cv)
            nrm = s * lax.rsqrt(jnp.sum(s * s, axis=-1, keepdims=True) + NORM_EPS)
            return jnp.where(is_qk, nrm, s)

        x0 = jnp.concatenate([jnp.zeros((8, LANES), F32), x_ref[pl.ds(0, LANES), :]], axis=0)
        o_ref[pl.ds(0, LANES), :] = finish(_conv_taps(x0, w_ref))

        def step(b, carry):
            r0 = pl.multiple_of(b * LANES, LANES)
            xa = x_ref[pl.ds(r0 - 8, LANES + 8), :]
            o_ref[pl.ds(r0, LANES), :] = finish(_conv_taps(xa, w_ref))
            return carry

        lax.fori_loop(1, nblk, step, 0)

    return pl.pallas_call(
        body, name="gdn_conv", grid=(w3 // LANES,),
        in_specs=[pl.BlockSpec((m, LANES), lambda c: (0, c)), pl.BlockSpec((CONV_WIDTH, LANES), lambda c: (0, c))],
        out_specs=pl.BlockSpec((None, m, LANES), lambda c: (c // n_heads, 0, c % n_heads)),
        out_shape=jax.ShapeDtypeStruct((3, m, wd), F32),
        compiler_params=_params("parallel"),
    )(proj, conv_w)


def _gdn_conv_bwd(proj, conv_w, dact, n_heads):
    m, w3 = proj.shape
    nblk = m // LANES

    def body(x_ref, w_ref, dy_ref, dx_ref, dw_ref, dc_ref):
        c = pl.program_id(0)
        is_qk = c < 2 * n_heads
        dc_ref[pl.ds(m, 8), :] = jnp.zeros((8, LANES), F32)

        def pre_act(xa):
            cv = _conv_taps(xa, w_ref)
            sg = _sigmoid(cv)
            s = cv * sg
            return cv, sg, s

        def dconv_in(b, xa, dwacc):
            r0 = pl.multiple_of(b * LANES, LANES)
            cv, sg, s = pre_act(xa)
            dy = dy_ref[pl.ds(r0, LANES), :]
            rn = lax.rsqrt(jnp.sum(s * s, axis=-1, keepdims=True) + NORM_EPS)
            yn = s * rn
            ds_qk = rn * (dy - yn * jnp.sum(dy * yn, axis=-1, keepdims=True))
            ds = jnp.where(is_qk, ds_qk, dy)
            dcv = ds * sg * (1.0 + cv * (1.0 - sg))
            dc_ref[pl.ds(r0, LANES), :] = dcv
            new = []
            for i in range(CONV_WIDTH):
                xs = xa[8:] if i == CONV_WIDTH - 1 else pltpu.roll(xa, CONV_WIDTH - 1 - i, 0)[8:]
                new.append(dwacc[i] + jnp.sum((dcv * xs).reshape(LANES // 8, 8, LANES), axis=0))
            return tuple(new)

        x0 = jnp.concatenate([jnp.zeros((8, LANES), F32), x_ref[pl.ds(0, LANES), :]], axis=0)
        zero = jnp.zeros((8, LANES), F32)
        dwacc = dconv_in(0, x0, (zero,) * CONV_WIDTH)

        def step1(b, acc):
            r0 = pl.multiple_of(b * LANES, LANES)
            return dconv_in(b, x_ref[pl.ds(r0 - 8, LANES + 8), :], acc)

        dwacc = lax.fori_loop(1, nblk, step1, dwacc)
        for i in range(CONV_WIDTH):
            dw_ref[i:i + 1, :] = jnp.sum(dwacc[i], axis=0, keepdims=True)

        def step2(b, carry):
            r0 = pl.multiple_of(b * LANES, LANES)
            da = dc_ref[pl.ds(r0, LANES + 8), :]
            acc = da * w_ref[CONV_WIDTH - 1:CONV_WIDTH, :]
            for i in range(CONV_WIDTH - 1):
                sh = CONV_WIDTH - 1 - i
                acc = acc + pltpu.roll(da, LANES + 8 - sh, 0) * w_ref[i:i + 1, :]
            dx_ref[pl.ds(r0, LANES), :] = acc[:LANES]
            return carry

        lax.fori_loop(0, nblk, step2, 0)

    return pl.pallas_call(
        body, name="gdn_conv_bwd", grid=(w3 // LANES,),
        in_specs=[pl.BlockSpec((m, LANES), lambda c: (0, c)),
                  pl.BlockSpec((CONV_WIDTH, LANES), lambda c: (0, c)),
                  pl.BlockSpec((None, m, LANES), lambda c: (c // n_heads, 0, c % n_heads))],
        out_specs=[pl.BlockSpec((m, LANES), lambda c: (0, c)), pl.BlockSpec((CONV_WIDTH, LANES), lambda c: (0, c))],
        out_shape=[jax.ShapeDtypeStruct((m, w3), F32), jax.ShapeDtypeStruct((CONV_WIDTH, w3), F32)],
        scratch_shapes=[pltpu.VMEM((m + 8, LANES), F32)],
        compiler_params=_params("parallel"),
    )(proj, conv_w, dact)


def _softplus(x):
    return jnp.maximum(x, 0.0) + jnp.log(1.0 + jnp.exp(-jnp.abs(x)))


def _gdn_gates(ab, a_log, dt_bias, n_heads):
    m = ab.shape[0]
    bm = _pick(m, (640, 384, 128))

    def body(ab_ref, al_ref, dt_ref, g_ref, b_ref):
        i = pl.program_id(0)
        x = ab_ref[...]
        live = (_iota2((bm, LANES), 0) + i * bm) >= PAD_ROWS
        g = jnp.where(live, -jnp.exp(al_ref[...]) * _softplus(x + dt_ref[...]), 0.0)
        beta = jnp.where(live, _sigmoid(x), 0.0)
        for h in range(n_heads):
            g_ref[:, h * LANES:(h + 1) * LANES] = jnp.broadcast_to(g[:, h:h + 1], (bm, LANES))
            b_ref[:, h * LANES:(h + 1) * LANES] = jnp.broadcast_to(
                beta[:, n_heads + h:n_heads + h + 1], (bm, LANES))

    wide = pl.BlockSpec((bm, n_heads * LANES), lambda i: (i, 0))
    return pl.pallas_call(
        body, name="gdn_gates", grid=(m // bm,),
        in_specs=[pl.BlockSpec((bm, LANES), lambda i: (i, 0))] + [pl.BlockSpec((1, LANES), lambda i: (0, 0))] * 2,
        out_specs=[wide, wide],
        out_shape=[jax.ShapeDtypeStruct((m, n_heads * LANES), F32)] * 2,
        compiler_params=_params("parallel"),
    )(ab, a_log, dt_bias)


def _gdn_gates_bwd(ab, a_log, dt_bias, dg_b, dbeta_b, n_heads):
    m = ab.shape[0]
    bm = _pick(m, (640, 384, 128))
    nsteps = m // bm

    def body(ab_ref, al_ref, dt_ref, dg_ref, db_ref, dab_ref, dal_ref, ddt_ref, acc_al, acc_dt):
        i = pl.program_id(0)
        x = ab_ref[...]
        lane = _iota2((bm, LANES), 1)
        live = (_iota2((bm, LANES), 0) + i * bm) >= PAD_ROWS
        dg = jnp.zeros((bm, LANES), F32)
        dbeta = jnp.zeros((bm, LANES), F32)
        for h in range(n_heads):
            dg = jnp.where(lane == h, dg_ref[:, h * LANES:(h + 1) * LANES], dg)
            dbeta = jnp.where(lane == n_heads + h, db_ref[:, h * LANES:(h + 1) * LANES], dbeta)
        dg = jnp.where(live, dg, 0.0)
        dbeta = jnp.where(live, dbeta, 0.0)
        nea = -jnp.exp(al_ref[...])
        pre = x + dt_ref[...]
        d_pre = dg * nea * _sigmoid(pre)
        beta = _sigmoid(x)
        dab_ref[...] = d_pre + dbeta * beta * (1.0 - beta)
        p_al = jnp.sum((dg * nea * _softplus(pre)).reshape(bm // 8, 8, LANES), axis=0)
        p_dt = jnp.sum(d_pre.reshape(bm // 8, 8, LANES), axis=0)

        @pl.when(i == 0)
        def _():
            acc_al[...] = p_al
            acc_dt[...] = p_dt

        @pl.when(i > 0)
        def _():
            acc_al[...] += p_al
            acc_dt[...] += p_dt

        @pl.when(i == nsteps - 1)
        def _():
            dal_ref[...] = jnp.sum(acc_al[...], axis=0, keepdims=True)
            ddt_ref[...] = jnp.sum(acc_dt[...], axis=0, keepdims=True)

    wide = pl.BlockSpec((bm, n_heads * LANES), lambda i: (i, 0))
    one = pl.BlockSpec((1, LANES), lambda i: (0, 0))
    nar = pl.BlockSpec((bm, LANES), lambda i: (i, 0))
    return pl.pallas_call(
        body, name="gdn_gates_bwd", grid=(nsteps,),
        in_specs=[nar, one, one, wide, wide],
        out_specs=[nar, one, one],
        out_shape=[jax.ShapeDtypeStruct((m, LANES), F32), jax.ShapeDtypeStruct((1, LANES), F32),
                   jax.ShapeDtypeStruct((1, LANES), F32)],
        scratch_shapes=[pltpu.VMEM((8, LANES), F32)] * 2,
        compiler_params=_params("arbitrary"),
    )(ab, a_log, dt_bias, dg_b, dbeta_b)


def _chunk_masks():
    c = GDN_CHUNK
    ri, ci = _iota2((c, c), 0), _iota2((c, c), 1)
    return ri >= ci, ri > ci, ri == ci


def _chunk_decay(gb):
    c = GDN_CHUNK
    incl, _, _ = _chunk_masks()
    g_cum = _hdot(incl.astype(F32), gb)
    g_row = _hdot(jnp.ones((c, LANES), F32), g_cum, NT) * (1.0 / LANES)
    diff = g_cum[:, :c] - g_row
    dmat = jnp.where(incl, jnp.exp(jnp.where(incl, diff, 0.0)), 0.0)
    return g_cum, dmat


def _unit_lower_inverse(a):
    _, _, eye = _chunk_masks()
    x = eye.astype(F32) - a
    p = a
    for _ in range(int(math.log2(GDN_CHUNK)) - 1):
        p = _hdot(p, p)
        x = x + _hdot(x, p)
    return x


def _gdn_prep(qkv, g_b, beta_b, n_heads):
    _, m, wd = qkv.shape
    c = GDN_CHUNK
    nc = m // c
    cb = _pick(nc, (5, 3, 2))
    rows = cb * c
    qscale = GDN_HEAD_DIM ** -0.5

    def body(qkv_ref, g_ref, b_ref, ub_ref, wc_ref, qd_ref, kd_ref, gc_ref, p_ref, t_ref):
        incl, strict, _ = _chunk_masks()
        for s in range(cb):
            sl = pl.ds(s * c, c)
            q = qkv_ref[0, sl, :] * qscale
            k = qkv_ref[1, sl, :]
            v = qkv_ref[2, sl, :]
            bb = b_ref[sl, :]
            g_cum, dmat = _chunk_decay(g_ref[sl, :])
            gam = jnp.exp(g_cum)
            kk = _bdot(k, k, NT)
            a = jnp.where(strict, bb[:, :c] * dmat * kk, 0.0)
            t = _unit_lower_inverse(a)
            ub_ref[sl, :] = _hdot(t, bb * v)
            wc_ref[sl, :] = _hdot(t, bb * gam * k)
            p_ref[s] = _bdot(q, k, NT) * dmat
            qd_ref[sl, :] = q * gam
            kd_ref[sl, :] = k * jnp.exp(g_cum[c - 1:c, :] - g_cum)
            gc_ref[sl, :] = g_cum
            t_ref[s] = t

    blk = pl.BlockSpec((rows, LANES), lambda h, n: (n, h))
    sq = pl.BlockSpec((None, cb, c, c), lambda h, n: (h, n, 0, 0))
    big = jax.ShapeDtypeStruct((m, wd), F32)
    small = jax.ShapeDtypeStruct((n_heads, nc, c, c), F32)
    return pl.pallas_call(
        body, name="gdn_prep", grid=(n_heads, nc // cb),
        in_specs=[pl.BlockSpec((3, rows, LANES), lambda h, n: (0, n, h)), blk, blk],
        out_specs=[blk] * 5 + [sq, sq],
        out_shape=[big] * 5 + [small, small],
        compiler_params=_params("parallel", "parallel"),
    )(qkv, g_b, beta_b)


def _gdn_prep_bwd(qkv, g_b, beta_b, ub, wc, tinv, dub, dwc, dqd, dkd, dp, dgl, n_heads):
    _, m, wd = qkv.shape
    c = GDN_CHUNK
    nc = m // c
    cb = _pick(nc, (5, 3, 2))
    rows = cb * c
    qscale = GDN_HEAD_DIM ** -0.5

    def body(qkv_ref, g_ref, b_ref, ub_ref, wc_ref, t_ref, dub_ref, dwc_ref, dqd_ref, dkd_ref, dp_ref, dgl_ref,
             dqkv_ref, dg_ref, db_ref):
        incl, strict, _ = _chunk_masks()
        ones_c = jnp.ones((c, LANES), F32)
        ones_l = jnp.ones((LANES, LANES), F32)
        last_row = _iota2((c, LANES), 0) == c - 1
        for s in range(cb):
            sl = pl.ds(s * c, c)
            q = qkv_ref[0, sl, :] * qscale
            k = qkv_ref[1, sl, :]
            v = qkv_ref[2, sl, :]
            bb = b_ref[sl, :]
            bcol = bb[:, :c]
            g_cum, dmat = _chunk_decay(g_ref[sl, :])
            gam = jnp.exp(g_cum)
            edec = jnp.exp(g_cum[c - 1:c, :] - g_cum)
            kk = _bdot(k, k, NT)
            qk = _bdot(q, k, NT)
            t = t_ref[s]
            drhs_u = _hdot(t, dub_ref[sl, :], TN)
            drhs_w = _hdot(t, dwc_ref[sl, :], TN)
            da = -jnp.where(strict, _bdot(drhs_u, ub_ref[sl, :], NT) + _bdot(drhs_w, wc_ref[sl, :], NT), 0.0)
            dbeta = _hdot(da * dmat * kk, ones_c)
            dd = da * bcol * kk
            dkk = da * bcol * dmat
            dk = _bdot(dkk, k) + _bdot(dkk, k, TN)
            dv = bb * drhs_u
            dbeta = dbeta + _hdot(drhs_u * v, ones_l)
            tk = _hdot(drhs_w * k, ones_l)
            dk = dk + bb * gam * drhs_w
            dbeta = dbeta + tk * gam
            dgam = bb * tk
            dpm = jnp.where(incl, dp_ref[s], 0.0)
            dpd = dpm * dmat
            dq = _bdot(dpd, k)
            dk = dk + _bdot(dpd, q, TN)
            dd = dd + dpm * qk
            dqd = dqd_ref[sl, :]
            dq = dq + dqd * gam
            dgam = dgam + _hdot(dqd * q, ones_l)
            dkd = dkd_ref[sl, :]
            dk = dk + dkd * edec
            skd = _hdot(dkd * k * edec, ones_l)
            dgc = dgam * gam - skd
            e = dd * dmat
            dgc = dgc + _hdot(e, ones_c) - _hdot(e, ones_c, TN)
            tot = _hdot(jnp.ones((c, c), F32), skd)
            dgc = dgc + jnp.where(last_row, tot + dgl_ref[s, 0:1, :], 0.0)
            dg_ref[sl, :] = _hdot((_iota2((c, c), 0) <= _iota2((c, c), 1)).astype(F32), dgc)
            db_ref[sl, :] = dbeta
            dqkv_ref[0, sl, :] = dq * qscale
            dqkv_ref[1, sl, :] = dk
            dqkv_ref[2, sl, :] = dv

    blk = pl.BlockSpec((rows, LANES), lambda h, n: (n, h))
    blk3 = pl.BlockSpec((3, rows, LANES), lambda h, n: (0, n, h))
    sq = pl.BlockSpec((None, cb, c, c), lambda h, n: (h, n, 0, 0))
    sgl = pl.BlockSpec((None, cb, 8, LANES), lambda h, n: (h, n, 0, 0))
    big = jax.ShapeDtypeStruct((m, wd), F32)
    return pl.pallas_call(
        body, name="gdn_prep_bwd", grid=(n_heads, nc // cb),
        in_specs=[blk3, blk, blk, blk, blk, sq, blk, blk, blk, blk, sq, sgl],
        out_specs=[blk3, blk, blk],
        out_shape=[jax.ShapeDtypeStruct((3, m, wd), F32), big, big],
        compiler_params=_params("parallel", "parallel"),
    )(qkv, g_b, beta_b, ub, wc, tinv, dub, dwc, dqd, dkd, dp, dgl)


def _gdn_scan(ub, wc, qd, kd, gc, p, n_heads):
    m, wd = ub.shape
    c = GDN_CHUNK
    nc = m // c
    dh = GDN_HEAD_DIM

    def body(ub_ref, wc_ref, qd_ref, kd_ref, gc_ref, p_ref, o_ref, ssave_ref, s_ref):
        n = pl.program_id(1)

        @pl.when(n == 0)
        def _():
            s_ref[...] = jnp.zeros_like(s_ref)

        s = s_ref[...]
        ssave_ref[...] = s
        u = ub_ref[...] - _bdot(wc_ref[...], s)
        o_ref[...] = _bdot(qd_ref[...], s) + _bdot(p_ref[...], u)
        egl = jnp.exp(gc_ref[c - 1:c, :])
        s_ref[...] = s * egl + _bdot(kd_ref[...], u, TN)

    blk = pl.BlockSpec((c, LANES), lambda h, n: (n, h))
    return pl.pallas_call(
        body, name="gdn_scan", grid=(n_heads, nc),
        in_specs=[blk] * 5 + [pl.BlockSpec((None, None, c, c), lambda h, n: (h, n, 0, 0))],
        out_specs=[blk, pl.BlockSpec((None, None, dh, dh), lambda h, n: (h, n, 0, 0))],
        out_shape=[jax.ShapeDtypeStruct((m, wd), F32), jax.ShapeDtypeStruct((n_heads, nc, dh, dh), F32)],
        scratch_shapes=[pltpu.VMEM((dh, dh), F32)],
        compiler_params=_params("parallel", "arbitrary"),
    )(ub, wc, qd, kd, gc, p)


def _gdn_scan_bwd(do, ssave, ub, wc, qd, kd, gc, p, n_heads):
    m, wd = ub.shape
    c = GDN_CHUNK
    nc = m // c
    dh = GDN_HEAD_DIM

    def body(do_ref, ss_ref, ub_ref, wc_ref, qd_ref, kd_ref, gc_ref, p_ref,
             dub_ref, dwc_ref, dqd_ref, dkd_ref, dp_ref, dgl_ref, ds_ref):
        n = pl.program_id(1)

        @pl.when(n == 0)
        def _():
            ds_ref[...] = jnp.zeros_like(ds_ref)

        ds_next = ds_ref[...]
        s = ss_ref[...]
        do_ = do_ref[...]
        wcv = wc_ref[...]
        kdv = kd_ref[...]
        u = ub_ref[...] - _bdot(wcv, s)
        egl = jnp.exp(gc_ref[c - 1:c, :])
        du = _bdot(p_ref[...], do_, TN) + _bdot(kdv, ds_next)
        dub_ref[...] = du
        dwc_ref[...] = -_bdot(du, s, NT)
        dp_ref[...] = _bdot(do_, u, NT)
        dqd_ref[...] = _bdot(do_, s, NT)
        dkd_ref[...] = _bdot(u, ds_next, NT)
        col = _hdot(jnp.ones((8, dh), F32), s * ds_next)
        dgl_ref[...] = egl * _hdot(col, jnp.ones((LANES, LANES), F32))
        ds_ref[...] = _bdot(qd_ref[...], do_, TN) + egl * ds_next - _bdot(wcv, du, TN)

    blk = pl.BlockSpec((c, LANES), lambda h, n: (nc - 1 - n, h))
    sq = pl.BlockSpec((None, None, c, c), lambda h, n: (h, nc - 1 - n, 0, 0))
    big = jax.ShapeDtypeStruct((m, wd), F32)
    return pl.pallas_call(
        body, name="gdn_scan_bwd", grid=(n_heads, nc),
        in_specs=[blk, pl.BlockSpec((None, None, dh, dh), lambda h, n: (h, nc - 1 - n, 0, 0))] + [blk] * 5 + [sq],
        out_specs=[blk] * 4 + [sq, pl.BlockSpec((None, None, 8, LANES), lambda h, n: (h, nc - 1 - n, 0, 0))],
        out_shape=[big] * 4 + [jax.ShapeDtypeStruct((n_heads, nc, c, c), F32),
                               jax.ShapeDtypeStruct((n_heads, nc, 8, LANES), F32)],
        scratch_shapes=[pltpu.VMEM((dh, dh), F32)],
        compiler_params=_params("parallel", "arbitrary"),
    )(do, ssave, ub, wc, qd, kd, gc, p)


def _gdn_outnorm(o, gate, gain, n_heads):
    m, wd = o.shape
    bm = _pick(m, (640, 384, 128))

    def body(o_ref, gt_ref, gn_ref, y_ref):
        gn = gn_ref[...]
        for h in range(n_heads):
            sl = slice(h * LANES, (h + 1) * LANES)
            x = o_ref[:, sl]
            gt = gt_ref[:, sl]
            r = lax.rsqrt(jnp.mean(x * x, axis=-1, keepdims=True) + NORM_EPS)
            y_ref[:, sl] = (x * r * gn * gt * _sigmoid(gt)).astype(BF16)

    row = pl.BlockSpec((bm, wd), lambda i: (i, 0))
    return pl.pallas_call(
        body, name="gdn_outnorm", grid=(m // bm,),
        in_specs=[row, row, pl.BlockSpec((1, LANES), lambda i: (0, 0))], out_specs=row,
        out_shape=jax.ShapeDtypeStruct((m, wd), BF16),
        compiler_params=_params("parallel"),
    )(o, gate, gain)


def _gdn_outnorm_bwd(o, gate, gain, dy, n_heads):
    m, wd = o.shape
    bm = _pick(m, (640, 384, 128))
    nsteps = m // bm

    def body(o_ref, gt_ref, gn_ref, dy_ref, do_ref, dgt_ref, dgn_ref, acc_ref):
        i = pl.program_id(0)
        gn = gn_ref[...]
        part = jnp.zeros((8, LANES), F32)
        for h in range(n_heads):
            sl = slice(h * LANES, (h + 1) * LANES)
            x = o_ref[:, sl]
            gt = gt_ref[:, sl]
            d_out = dy_ref[:, sl]
            r = lax.rsqrt(jnp.mean(x * x, axis=-1, keepdims=True) + NORM_EPS)
            xhat = x * r
            sg = _sigmoid(gt)
            dz = d_out * gt * sg
            dgt_ref[:, sl] = d_out * xhat * gn * sg * (1.0 + gt * (1.0 - sg))
            dxh = dz * gn
            do_ref[:, sl] = r * (dxh - xhat * jnp.mean(dxh * xhat, axis=-1, keepdims=True))
            part = part + jnp.sum((dz * xhat).reshape(bm // 8, 8, LANES), axis=0)

        @pl.when(i == 0)
        def _():
            acc_ref[...] = part

        @pl.when(i > 0)
        def _():
            acc_ref[...] += part

        @pl.when(i == nsteps - 1)
        def _():
            dgn_ref[...] = jnp.sum(acc_ref[...], axis=0, keepdims=True)

    row = pl.BlockSpec((bm, wd), lambda i: (i, 0))
    one = pl.BlockSpec((1, LANES), lambda i: (0, 0))
    return pl.pallas_call(
        body, name="gdn_outnorm_bwd", grid=(nsteps,),
        in_specs=[row, row, one, row], out_specs=[row, row, one],
        out_shape=[jax.ShapeDtypeStruct((m, wd), F32)] * 2 + [jax.ShapeDtypeStruct((1, LANES), F32)],
        scratch_shapes=[pltpu.VMEM((8, LANES), F32)],
        compiler_params=_params("arbitrary"),
    )(o, gate, gain, dy)


def _sb_consts(kind):
    ri, ci = _iota2((LANES, LANES), 0), _iota2((LANES, LANES), 1)
    tri = {"suffix_excl": ri > ci, "prefix_incl": ri <= ci, "prefix_excl": ri < ci}[kind]
    return jnp.concatenate([tri.astype(BF16), jnp.ones((LANES, LANES), BF16)], axis=1)


def _sb_stack(a):
    first = _iota2((LANES, LANES), 1) < SB_HEAD_DIM
    return jnp.concatenate([jnp.where(first, a, 0.0), jnp.where(first, 0.0, a)], axis=0)


def _sb_unstack(a2):
    first = _iota2((LANES, LANES), 1) < SB_HEAD_DIM
    return jnp.where(first, a2[:LANES], a2[LANES:])


def _sb_visible(qi, kb, live):
    kpos = _iota2((2 * LANES, LANES), 1) + kb * LANES
    qpos = (_iota2((2 * LANES, LANES), 0) & (LANES - 1)) + qi * LANES
    return (kpos < qpos) & (kpos >= PAD_ROWS) & live


def _sb_scores(q2, kt, vis):
    z = lax.dot_general(q2, kt, NT, preferred_element_type=F32)
    t = jnp.log(1.0 + jnp.exp(-jnp.abs(z)))
    lb = jnp.minimum(z, 0.0) - t
    lk = lb - z
    if vis is not None:
        lk = jnp.where(vis, lk, 0.0)
    return lb, lk


def _sb_attention(q, kv):
    m, wd = q.shape
    npair = wd // LANES
    nq = m // LANES
    scale = SB_HEAD_DIM ** -0.5

    def body(q_ref, k_ref, v_ref, o_ref, lt_ref):
        qi = pl.program_id(1)
        su = _sb_consts("suffix_excl")
        q2 = _sb_stack(q_ref[...] * scale).astype(BF16)

        def kv_tile(kb):
            r0 = pl.multiple_of(kb * LANES, LANES)
            return k_ref[pl.ds(r0, LANES), :], v_ref[pl.ds(r0, LANES), :]

        def tiles(kbs, run, acc, vis):
            parts = []
            for kb in kbs:
                kt, vt = kv_tile(kb)
                lb, lk = _sb_scores(q2, kt, vis)
                parts.append((lb, _split_dot(lk, su), vt))
            for lb, cs, vt in parts:
                w = jnp.exp(lb + cs[:, :LANES] + run)
                if vis is not None:
                    w = jnp.where(vis, w, 0.0)
                run = run + cs[:, LANES:]
                acc = acc + lax.dot_general(w.astype(BF16), vt, NN, preferred_element_type=F32)
            return run, acc

        def group(j, carry):
            kb = qi - 1 - SB_UNROLL * j
            return tiles([kb - u for u in range(SB_UNROLL)], carry[0], carry[1], None)

        zero = jnp.zeros((2 * LANES, LANES), F32)
        run, acc = tiles([qi], zero, zero, _sb_visible(qi, qi, True))
        n_mid = jnp.maximum(qi - 1, 0)
        run, acc = lax.fori_loop(0, n_mid // SB_UNROLL, group, (run, acc))
        run, acc = lax.fori_loop(0, n_mid % SB_UNROLL,
                                 lambda j, c: tiles([n_mid % SB_UNROLL - j], c[0], c[1], None), (run, acc))
        run, acc = tiles([0], run, acc, _sb_visible(qi, 0, qi >= 1))
        o_ref[...] = _sb_unstack(acc)
        lt_ref[...] = _sb_unstack(run)

    qspec = pl.BlockSpec((LANES, LANES), lambda hp, qi: (qi, hp))
    return pl.pallas_call(
        body, name="sb_attention", grid=(npair, nq),
        in_specs=[qspec, pl.BlockSpec((m, LANES), lambda hp, qi: (0, hp)),
                  pl.BlockSpec((m, LANES), lambda hp, qi: (0, npair + hp))],
        out_specs=[qspec, qspec],
        out_shape=[jax.ShapeDtypeStruct((m, wd), F32)] * 2,
        compiler_params=_params("parallel", "arbitrary"),
    )(q, kv, kv)


def _sb_attention_bwd(q, kv, ltot, do):
    m, wd = q.shape
    npair = wd // LANES
    nq = m // LANES
    scale = SB_HEAD_DIM ** -0.5

    def body(q_ref, k_ref, v_ref, lt_ref, do_ref, dq_ref, dk_ref, dv_ref):
        qi = pl.program_id(1)

        @pl.when(qi == 0)
        def _():
            dk_ref[...] = jnp.zeros_like(dk_ref)
            dv_ref[...] = jnp.zeros_like(dv_ref)

        pi = _sb_consts("prefix_incl")
        px = _sb_consts("prefix_excl")
        q2 = _sb_stack(q_ref[...] * scale).astype(BF16)
        do2 = _sb_stack(do_ref[...]).astype(BF16)
        ltot = lt_ref[...]
        lt2 = jnp.concatenate([jnp.broadcast_to(ltot[:, 0:1], (LANES, LANES)),
                               jnp.broadcast_to(ltot[:, SB_HEAD_DIM:SB_HEAD_DIM + 1], (LANES, LANES))], axis=0)

        def tiles(kbs, carry, vis):
            run_k, run_p, dq_acc = carry
            parts = []
            for kb in kbs:
                r0 = pl.multiple_of(kb * LANES, LANES)
                kt = k_ref[pl.ds(r0, LANES), :]
                vt = v_ref[pl.ds(r0, LANES), :]
                lb, lk = _sb_scores(q2, kt, vis)
                cs = _split_dot(lk, pi, parts=3)
                da = lax.dot_general(do2, vt, NT, preferred_element_type=F32)
                parts.append((r0, kt, lb, cs, da))
            for r0, kt, lb, cs, da in parts:
                w = jnp.exp(lb + lt2 - (run_k + cs[:, :LANES]))
                if vis is not None:
                    w = jnp.where(vis, w, 0.0)
                run_k = run_k + cs[:, LANES:]
                pw = w * da
                ps = _split_dot(pw, px)
                beta = jnp.exp(lb)
                dz = pw * (1.0 - beta) - beta * (run_p + ps[:, :LANES])
                if vis is not None:
                    dz = jnp.where(vis, dz, 0.0)
                run_p = run_p + ps[:, LANES:]
                dz = dz.astype(BF16)
                dq_acc = dq_acc + lax.dot_general(dz, kt, NN, preferred_element_type=F32)
                dk_ref[pl.ds(r0, LANES), :] += lax.dot_general(dz, q2, TN, preferred_element_type=F32)
                dv_ref[pl.ds(r0, LANES), :] += lax.dot_general(w.astype(BF16), do2, TN, preferred_element_type=F32)
            return run_k, run_p, dq_acc

        def group(j, carry):
            kb = 1 + SB_UNROLL_BWD * j
            return tiles([kb + u for u in range(SB_UNROLL_BWD)], carry, None)

        zero = jnp.zeros((2 * LANES, LANES), F32)
        carry = tiles([0], (zero, zero, zero), _sb_visible(qi, 0, True))
        n_mid = jnp.maximum(qi - 1, 0)
        n_grp = n_mid // SB_UNROLL_BWD
        carry = lax.fori_loop(0, n_grp, group, carry)
        carry = lax.fori_loop(1 + n_grp * SB_UNROLL_BWD, jnp.maximum(qi, 1), lambda kb, c: tiles([kb], c, None), carry)
        carry = tiles([qi], carry, _sb_visible(qi, qi, qi >= 1))
        dq_ref[...] = _sb_unstack(carry[2]) * scale

    qspec = pl.BlockSpec((LANES, LANES), lambda hp, qi: (qi, hp))
    strip = pl.BlockSpec((m, LANES), lambda hp, qi: (0, hp))
    big = jax.ShapeDtypeStruct((m, wd), F32)
    return pl.pallas_call(
        body, name="sb_attention_bwd", grid=(npair, nq),
        in_specs=[qspec, strip, pl.BlockSpec((m, LANES), lambda hp, qi: (0, npair + hp)), qspec, qspec],
        out_specs=[qspec, strip, strip],
        out_shape=[big, big, big],
        compiler_params=_params("parallel", "arbitrary"),
    )(q, kv, kv, ltot, do)


def _mesh_pos():
    return lax.axis_index("x"), lax.axis_index("y"), lax.axis_index("c")


def _exchange(buf, name, scatter):
    rows = buf.shape[-2]

    def body(src_ref, out_ref, send_sems, recv_sems, local_sem):
        x, y, c = _mesh_pos()
        me = 4 * x + 2 * y + c

        def slab(d):
            return src_ref.at[d] if scatter else src_ref

        local = pltpu.make_async_copy(slab(me), out_ref.at[me], local_sem)
        local.start()
        sends = []
        peers = []
        for k in range(1, N_DEV):
            px = 1 - x if k & 4 else x
            py = 1 - y if k & 2 else y
            pc = 1 - c if k & 1 else c
            pid = 4 * px + 2 * py + pc
            cp = pltpu.make_async_remote_copy(
                src_ref=slab(pid), dst_ref=out_ref.at[me],
                send_sem=send_sems.at[k - 1], recv_sem=recv_sems.at[k - 1],
                device_id=(px, py, pc), device_id_type=pl.DeviceIdType.MESH)
            cp.start()
            sends.append(cp)
            peers.append(pid)
        for k in range(1, N_DEV):
            pltpu.make_async_remote_copy(
                src_ref=slab(me), dst_ref=out_ref.at[peers[k - 1]],
                send_sem=send_sems.at[k - 1], recv_sem=recv_sems.at[k - 1],
                device_id=(x, y, c), device_id_type=pl.DeviceIdType.MESH).wait_recv()
        for cp in sends:
            cp.wait_send()
        local.wait()

    return pl.pallas_call(
        body, name=name,
        in_specs=[pl.BlockSpec(memory_space=pl.ANY)], out_specs=pl.BlockSpec(memory_space=pl.ANY),
        out_shape=jax.ShapeDtypeStruct((N_DEV, rows, FLAT_COLS), buf.dtype),
        scratch_shapes=[pltpu.SemaphoreType.DMA((N_DEV - 1,)), pltpu.SemaphoreType.DMA((N_DEV - 1,)),
                        pltpu.SemaphoreType.DMA],
        compiler_params=pltpu.CompilerParams(has_side_effects=True),
    )(buf)


def _adamw(parts, w, mom, var):
    rows = w.shape[0]
    br = FLAT_ROW_BLOCK
    c1 = 1.0 - ADAM_B1 ** ADAM_STEP
    c2 = 1.0 - ADAM_B2 ** ADAM_STEP

    def body(p_ref, w_ref, m_ref, v_ref, g_ref, d_ref, nm_ref, nv_ref):
        g = p_ref[0]
        for s in range(1, N_DEV):
            g = g + p_ref[s]
        m_new = ADAM_B1 * m_ref[...] + (1.0 - ADAM_B1) * g
        v_new = ADAM_B2 * v_ref[...] + (1.0 - ADAM_B2) * (g * g)
        m_hat = m_new / c1
        v_hat = v_new / c2
        g_ref[...] = g
        d_ref[...] = -ADAM_LR * (m_hat / (jnp.sqrt(v_hat) + ADAM_EPS) + ADAM_WD * w_ref[...])
        nm_ref[...] = m_new
        nv_ref[...] = v_new

    row = pl.BlockSpec((br, FLAT_COLS), lambda i: (i, 0))
    flat = jax.ShapeDtypeStruct((rows, FLAT_COLS), F32)
    return pl.pallas_call(
        body, name="adamw", grid=(rows // br,),
        in_specs=[pl.BlockSpec((N_DEV, br, FLAT_COLS), lambda i: (0, i, 0)), row, row, row],
        out_specs=[row] * 4, out_shape=[flat] * 4,
        compiler_params=_params("parallel"),
    )(parts, w, mom, var)


SHARDED = (
    ("meta_tokens", 1), ("gdn_norm_g", 1), ("gdn_w_in", 2), ("gdn_conv_w", 2), ("gdn_w_out", 1),
    ("w_kv", 1), ("sb_w_q", 1), ("sb_w_o", 1), ("ffn_w_gate_up", 2), ("ffn_w_down", 1))
REPLICATED = ("gdn_a_log", "gdn_dt_bias", "gdn_onorm_g", "kv_norm_g", "sb_norm_g", "ffn_norm_g", "final_norm_g")
WEIGHTS = ("meta_tokens", "gdn_norm_g", "gdn_w_in", "gdn_conv_w", "gdn_a_log", "gdn_dt_bias", "gdn_onorm_g",
           "gdn_w_out", "kv_norm_g", "w_kv", "sb_norm_g", "sb_w_q", "sb_w_o", "ffn_norm_g", "ffn_w_gate_up",
           "ffn_w_down", "final_norm_g")


def _seg_rows(size):
    return -(-size // (8 * FLAT_COLS)) * 8


def _pack(arrays, lead=()):
    nl = len(lead)
    segs = []
    for a in arrays:
        size = math.prod(a.shape[nl:])
        flat = a.reshape(lead + (size,))
        pad = _seg_rows(size) * FLAT_COLS - size
        if pad:
            flat = jnp.pad(flat, [(0, 0)] * nl + [(0, pad)])
        segs.append(flat.reshape(lead + (-1, FLAT_COLS)))
    rows = sum(s.shape[nl] for s in segs)
    tail = (-rows) % FLAT_ROW_BLOCK
    if tail:
        segs.append(jnp.zeros(lead + (tail, FLAT_COLS), F32))
    return jnp.concatenate(segs, axis=nl)


def _unpack(flat, shapes, lead=()):
    nl = len(lead)
    out, r = [], 0
    for shp in shapes:
        size = math.prod(shp)
        nr = _seg_rows(size)
        seg = lax.slice_in_dim(flat, r, r + nr, axis=nl).reshape(lead + (nr * FLAT_COLS,))
        out.append(lax.slice_in_dim(seg, 0, size, axis=nl).reshape(lead + tuple(shp)))
        r += nr
    return out


def _merge_shards(g, axis):
    t = jnp.moveaxis(g, 0, axis)
    shp = t.shape
    return t.reshape(shp[:axis] + (shp[axis] * shp[axis + 1],) + shp[axis + 2:])


def _split_shards(full, axis):
    shp = full.shape
    t = full.reshape(shp[:axis] + (N_DEV, shp[axis] // N_DEV) + shp[axis + 1:])
    return jnp.moveaxis(t, axis, 0)


def _pad_lanes(a, width=LANES):
    return jnp.pad(a, ((0, 0), (0, width - a.shape[1])))


def _local_step(x, target, w):
    d = x.shape[1]
    gh = w["gdn_a_log"].shape[1]
    gw = gh * GDN_HEAD_DIM
    w_in = w["gdn_w_in"][0]
    w_qkv, w_gate = w_in[:, :3 * gw].astype(BF16), w_in[:, 3 * gw:4 * gw].astype(BF16)
    w_ab = _pad_lanes(w_in[:, 4 * gw:]).astype(BF16)
    conv_w = w["gdn_conv_w"][0]
    a_log, dt_bias = _pad_lanes(w["gdn_a_log"]), _pad_lanes(w["gdn_dt_bias"])
    w_out = w["gdn_w_out"][0].astype(BF16)
    sbw = w["sb_w_q"].shape[2]
    sb_heads = sbw // SB_HEAD_DIM
    w_kv = w["w_kv"].astype(BF16)
    w_k, w_v = w_kv[:, :sbw], w_kv[:, sbw:]
    w_q, w_o = w["sb_w_q"][0].astype(BF16), w["sb_w_o"][0].astype(BF16)
    w_gu = [w["ffn_w_gate_up"][l].astype(BF16) for l in range(2)]
    w_dn = [w["ffn_w_down"][l].astype(BF16) for l in range(2)]
    ffn_g = [w["ffn_norm_g"][l:l + 1] for l in range(2)]
    kv_g, fin_g = w["kv_norm_g"][None], w["final_norm_g"][None]

    h0 = jnp.concatenate([jnp.zeros((PAD_ROWS, d), F32), w["meta_tokens"], x], axis=0)

    (n1,) = _rmsnorm(h0, [w["gdn_norm_g"]], "norm_gdn")
    proj = _matmul(n1, w_qkv, "nn", "mm_gdn_qkv")
    gate = _matmul(n1, w_gate, "nn", "mm_gdn_gate")
    ab = _matmul(n1, w_ab, "nn", "mm_gdn_ab")
    qkv = _gdn_conv(proj, conv_w, gh)
    g_b, beta_b = _gdn_gates(ab, a_log, dt_bias, gh)
    ub, wc, qd, kd, gc, pmat, tinv = _gdn_prep(qkv, g_b, beta_b, gh)
    o_gdn, ssave = _gdn_scan(ub, wc, qd, kd, gc, pmat, gh)
    o2 = _gdn_outnorm(o_gdn, gate, w["gdn_onorm_g"], gh)
    h1 = _matmul(o2, w_out, "nn", "mm_gdn_out", add=h0)

    def ffn_fwd(h, l):
        (n,) = _rmsnorm(h, [ffn_g[l]], f"norm_ffn{l}")
        gu = _matmul(n, w_gu[l], "nn", f"mm_ffn{l}_gu")
        act = _swiglu(gu, f"swiglu{l}")
        return n, gu, act, _matmul(act, w_dn[l], "nn", f"mm_ffn{l}_down", add=h)

    n2, gu0, act0, h2 = ffn_fwd(h1, 0)

    nkv, n3 = _rmsnorm(h2, [kv_g, w["sb_norm_g"]], "norm_kv_sb")
    kv = _matmul(nkv, w_kv, "nn", "mm_kv", out_dtype=BF16)
    q_sb = _matmul(n3, w_q, "nn", "mm_sb_q")
    o_sb, ltot = _sb_attention(q_sb, kv)
    h3 = _matmul(o_sb, w_o, "nn", "mm_sb_o", add=h2)
    n4, gu1, act1, h4 = ffn_fwd(h3, 1)

    loss, dh4, d_fin = _loss_head(h4, fin_g, target)

    def ffn_bwd(dh, h, n, gu, act, l):
        d_act = _matmul(dh, w_dn[l], "nt", f"mm_ffn{l}_dact")
        dw_dn = _matmul(act, dh, "tn", f"mm_ffn{l}_dwdown")
        dgu = _swiglu_bwd(gu, d_act, f"swiglu_bwd{l}")
        dw_gu = _matmul(n, dgu, "tn", f"mm_ffn{l}_dwgu")
        dn = _matmul(dgu, w_gu[l], "nt", f"mm_ffn{l}_dn")
        dh_in, (dg,) = _rmsnorm_bwd(h, [ffn_g[l]], [dn], dh, f"norm_ffn{l}_bwd")
        return dh_in, dw_gu, dw_dn, dg

    dh3, dw_gu1, dw_dn1, dg_ffn1 = ffn_bwd(dh4, h3, n4, gu1, act1, 1)

    do_sb = _matmul(dh3, w_o, "nt", "mm_sb_do")
    dw_o = _matmul(o_sb, dh3, "tn", "mm_sb_dwo")
    dq_sb, dk_sb, dv_sb = _sb_attention_bwd(q_sb, kv, ltot, do_sb)
    dw_q = _matmul(n3, dq_sb, "tn", "mm_sb_dwq")
    dn3 = _matmul(dq_sb, w_q, "nt", "mm_sb_dn")
    dw_k = _matmul(nkv, dk_sb, "tn", "mm_dwk")
    dw_v = _matmul(nkv, dv_sb, "tn", "mm_dwv")
    dnkv = _matmul(dk_sb, w_k, "nt", "mm_dnk")
    dnkv = _matmul(dv_sb, w_v, "nt", "mm_dnv", add=dnkv)
    dh2, (dg_kv, dg_sb) = _rmsnorm_bwd(h2, [kv_g, w["sb_norm_g"]], [dnkv, dn3], dh3, "norm_kv_sb_bwd")

    dh1, dw_gu0, dw_dn0, dg_ffn0 = ffn_bwd(dh2, h1, n2, gu0, act0, 0)

    do2 = _matmul(dh1, w_out, "nt", "mm_gdn_do")
    dw_out = _matmul(o2, dh1, "tn", "mm_gdn_dwout")
    do_gdn, dgate, d_onorm = _gdn_outnorm_bwd(o_gdn, gate, w["gdn_onorm_g"], do2, gh)
    dub, dwc, dqd, dkd, dpm, dgl = _gdn_scan_bwd(do_gdn, ssave, ub, wc, qd, kd, gc, pmat, gh)
    dqkv, dg_b, dbeta_b = _gdn_prep_bwd(qkv, g_b, beta_b, ub, wc, tinv, dub, dwc, dqd, dkd, dpm, dgl, gh)
    dab, d_alog, d_dtb = _gdn_gates_bwd(ab, a_log, dt_bias, dg_b, dbeta_b, gh)
    dproj, d_conv = _gdn_conv_bwd(proj, conv_w, dqkv, gh)
    dw_qkv = _matmul(n1, dproj, "tn", "mm_gdn_dwqkv")
    dw_gate = _matmul(n1, dgate, "tn", "mm_gdn_dwgate")
    dw_ab = _matmul(n1, dab, "tn", "mm_gdn_dwab")
    dn1 = _matmul(dproj, w_qkv, "nt", "mm_gdn_dn_qkv")
    dn1 = _matmul(dgate, w_gate, "nt", "mm_gdn_dn_gate", add=dn1)
    dn1 = _matmul(dab, w_ab, "nt", "mm_gdn_dn_ab", add=dn1)
    dh0, (dg_gdn,) = _rmsnorm_bwd(h0, [w["gdn_norm_g"]], [dn1], dh1, "norm_gdn_bwd")

    grads = {
        "meta_tokens": dh0[PAD_ROWS:LANES],
        "gdn_norm_g": dg_gdn,
        "gdn_w_in": jnp.concatenate([dw_qkv, dw_gate, dw_ab[:, :2 * gh]], axis=1)[None],
        "gdn_conv_w": d_conv[None],
        "gdn_a_log": d_alog[:, :gh],
        "gdn_dt_bias": d_dtb[:, :gh],
        "gdn_onorm_g": d_onorm,
        "gdn_w_out": dw_out[None],
        "kv_norm_g": dg_kv[0],
        "w_kv": jnp.concatenate([dw_k, dw_v], axis=1),
        "sb_norm_g": dg_sb,
        "sb_w_q": dw_q[None],
        "sb_w_o": dw_o[None],
        "ffn_norm_g": jnp.concatenate([dg_ffn0, dg_ffn1], axis=0),
        "ffn_w_gate_up": jnp.stack([dw_gu0, dw_gu1]),
        "ffn_w_down": jnp.stack([dw_dn0, dw_dn1]),
        "final_norm_g": d_fin[0],
    }
    return loss, dh0[LANES:], grads


def kernel(x, meta_tokens, gdn_norm_g, gdn_w_in, gdn_conv_w, gdn_a_log, gdn_dt_bias, gdn_onorm_g, gdn_w_out, kv_norm_g, w_kv, sb_norm_g, sb_w_q, sb_w_o, ffn_norm_g, ffn_w_gate_up, ffn_w_down, final_norm_g, loss_target, m_meta_tokens, m_gdn_norm_g, m_gdn_w_in, m_gdn_conv_w, m_gdn_a_log, m_gdn_dt_bias, m_gdn_onorm_g, m_gdn_w_out, m_kv_norm_g, m_w_kv, m_sb_norm_g, m_sb_w_q, m_sb_w_o, m_ffn_norm_g, m_ffn_w_gate_up, m_ffn_w_down, m_final_norm_g, v_meta_tokens, v_gdn_norm_g, v_gdn_w_in, v_gdn_conv_w, v_gdn_a_log, v_gdn_dt_bias, v_gdn_onorm_g, v_gdn_w_out, v_kv_norm_g, v_w_kv, v_sb_norm_g, v_sb_w_q, v_sb_w_o, v_ffn_norm_g, v_ffn_w_gate_up, v_ffn_w_down, v_final_norm_g):
    given = dict(locals())
    shard_names = [n for n, _ in SHARDED]
    order = shard_names + list(REPLICATED)
    local_shapes = [given[n].shape for n in order]

    gathered = _exchange(_pack([given[n] for n in shard_names]), "all_gather_weights", scatter=False)
    parts = _unpack(gathered, [given[n].shape for n in shard_names], lead=(N_DEV,))
    full = {n: _merge_shards(p, ax) for (n, ax), p in zip(SHARDED, parts)}
    full.update({n: given[n] for n in REPLICATED})

    loss_b, grad_x, grads = _local_step(x[0], loss_target[0], full)

    send = _pack([_split_shards(grads[n], ax) for n, ax in SHARDED]
                 + [jnp.broadcast_to(grads[n][None], (N_DEV,) + grads[n].shape) for n in REPLICATED], lead=(N_DEV,))
    recv = _exchange(send, "scatter_gradients", scatter=True)

    flat = lambda prefix: _pack([given[prefix + n] for n in order])
    g_f, d_f, m_f, v_f = _adamw(recv, flat(""), flat("m_"), flat("v_"))
    outs = {}
    for tag, f in (("grad", g_f), ("delta", d_f), ("new_m", m_f), ("new_v", v_f)):
        outs[tag] = dict(zip(order, _unpack(f, local_shapes)))

    loss = lax.psum(loss_b[0, 0], ("x", "y", "c"))
    result = [loss, grad_x[None]]
    for tag in ("grad", "delta", "new_m", "new_v"):
        result += [outs[tag][n] for n in WEIGHTS]
    return tuple(result)
```

```python
import functools
import math

import jax
import jax.numpy as jnp
from jax import lax
from jax.experimental import pallas as pl
from jax.experimental.pallas import tpu as pltpu

F32 = jnp.float32
BF16 = jnp.bfloat16
HIGHEST = lax.Precision.HIGHEST

N_DEV = 8
LANES = 128
N_META = 16
PAD_ROWS = LANES - N_META
GDN_CHUNK = 64
GDN_HEAD_DIM = 128
SB_HEAD_DIM = 64
SB_UNROLL = 8
SB_UNROLL_BWD = 4
CONV_WIDTH = 4
NORM_EPS = 1e-6
FLAT_COLS = 1024
FLAT_ROW_BLOCK = 64
VMEM_LIMIT = 56 * 1024 * 1024

ADAM_LR = 0.001
ADAM_B1 = 0.9
ADAM_B2 = 0.999
ADAM_EPS = 1e-08
ADAM_WD = 0.01
ADAM_STEP = 10

NN = (((1,), (0,)), ((), ()))
NT = (((1,), (1,)), ((), ()))
TN = (((0,), (0,)), ((), ()))


def _params(*sem):
    return pltpu.CompilerParams(dimension_semantics=sem, vmem_limit_bytes=VMEM_LIMIT)


def _pick(n, cands):
    for c in cands:
        if n % c == 0:
            return c
    return n


def _bdot(a, b, dims=NN):
    return lax.dot_general(a.astype(BF16), b.astype(BF16), dims, preferred_element_type=F32)


def _hdot(a, b, dims=NN):
    return lax.dot_general(a, b, dims, preferred_element_type=F32, precision=HIGHEST)


def _split_dot(a, m, parts=2):
    out = None
    for _ in range(parts):
        piece = a.astype(BF16)
        a = a - piece.astype(F32)
        term = lax.dot_general(piece, m, NN, preferred_element_type=F32)
        out = term if out is None else out + term
    return out


def _sigmoid(x):
    return 1.0 / (1.0 + jnp.exp(-x))


def _iota2(shape, axis):
    return lax.broadcasted_iota(jnp.int32, shape, axis)


def _matmul(a, b, mode, name, out_dtype=F32, add=None):
    if mode == "nn":
        (m, k), n = a.shape, b.shape[1]
    elif mode == "nt":
        (m, k), n = a.shape, b.shape[0]
    else:
        (k, m), n = a.shape, b.shape[1]
    row_c = (640, 512, 384, 256, 128)
    col_c = (1024, 1408, 768, 512, 384, 256, 128)
    if mode == "tn":
        bm, bn, bk = _pick(m, col_c), _pick(n, col_c), _pick(k, row_c)
    else:
        bm, bn, bk = _pick(m, row_c), _pick(n, col_c), _pick(k, (1024, 1408, 768, 512, 256, 128))
    nk = k // bk
    dims = {"nn": NN, "nt": NT, "tn": TN}[mode]
    a_spec = {"nn": pl.BlockSpec((bm, bk), lambda i, j, q: (i, q)),
              "nt": pl.BlockSpec((bm, bk), lambda i, j, q: (i, q)),
              "tn": pl.BlockSpec((bk, bm), lambda i, j, q: (q, i))}[mode]
    b_spec = {"nn": pl.BlockSpec((bk, bn), lambda i, j, q: (q, j)),
              "nt": pl.BlockSpec((bn, bk), lambda i, j, q: (j, q)),
              "tn": pl.BlockSpec((bk, bn), lambda i, j, q: (q, j))}[mode]
    o_spec = pl.BlockSpec((bm, bn), lambda i, j, q: (i, j))
    has_add = add is not None

    def body(*refs):
        if has_add:
            a_ref, b_ref, add_ref, o_ref, acc_ref = refs
        else:
            a_ref, b_ref, o_ref, acc_ref = refs
        q = pl.program_id(2)
        part = _bdot(a_ref[...], b_ref[...], dims)

        @pl.when(q == 0)
        def _():
            acc_ref[...] = part

        @pl.when(q > 0)
        def _():
            acc_ref[...] += part

        @pl.when(q == nk - 1)
        def _():
            r = acc_ref[...]
            if has_add:
                r = r + add_ref[...]
            o_ref[...] = r.astype(out_dtype)

    ins = [a, b] + ([add] if has_add else [])
    in_specs = [a_spec, b_spec] + ([o_spec] if has_add else [])
    return pl.pallas_call(
        body, name=name, grid=(m // bm, n // bn, nk),
        in_specs=in_specs, out_specs=o_spec,
        out_shape=jax.ShapeDtypeStruct((m, n), out_dtype),
        scratch_shapes=[pltpu.VMEM((bm, bn), F32)],
        compiler_params=_params("parallel", "parallel", "arbitrary"),
    )(*ins)


def _rmsnorm(h, gains, name):
    m, d = h.shape
    bm = _pick(m, (640, 384, 128))
    ng = len(gains)

    def body(*refs):
        h_ref, g_refs, o_refs = refs[0], refs[1:1 + ng], refs[1 + ng:]
        x = h_ref[...]
        xhat = x * lax.rsqrt(jnp.mean(x * x, axis=-1, keepdims=True) + NORM_EPS)
        for g_ref, o_ref in zip(g_refs, o_refs):
            o_ref[...] = (xhat * g_ref[...]).astype(BF16)

    row = pl.BlockSpec((bm, d), lambda i: (i, 0))
    gain = pl.BlockSpec((1, d), lambda i: (0, 0))
    return pl.pallas_call(
        body, name=name, grid=(m // bm,),
        in_specs=[row] + [gain] * ng, out_specs=[row] * ng,
        out_shape=[jax.ShapeDtypeStruct((m, d), BF16)] * ng,
        compiler_params=_params("parallel"),
    )(h, *gains)


def _rmsnorm_bwd(h, gains, dns, dres, name):
    m, d = h.shape
    bm = _pick(m, (640, 384, 128))
    ng = len(gains)
    nsteps = m // bm

    def body(*refs):
        h_ref, dres_ref = refs[0], refs[1]
        g_refs = refs[2:2 + ng]
        dn_refs = refs[2 + ng:2 + 2 * ng]
        dh_ref = refs[2 + 2 * ng]
        dg_refs = refs[3 + 2 * ng:3 + 3 * ng]
        acc_refs = refs[3 + 3 * ng:]
        i = pl.program_id(0)
        x = h_ref[...]
        r = lax.rsqrt(jnp.mean(x * x, axis=-1, keepdims=True) + NORM_EPS)
        xhat = x * r
        dh = dres_ref[...]
        for g_ref, dn_ref, dg_ref, acc_ref in zip(g_refs, dn_refs, dg_refs, acc_refs):
            dn = dn_ref[...]
            dy = dn * g_ref[...]
            dh = dh + r * (dy - xhat * jnp.mean(dy * xhat, axis=-1, keepdims=True))
            part = jnp.sum((dn * xhat).reshape(bm // 8, 8, d), axis=0)

            @pl.when(i == 0)
            def _():
                acc_ref[...] = part

            @pl.when(i > 0)
            def _():
                acc_ref[...] += part

            @pl.when(i == nsteps - 1)
            def _():
                dg_ref[...] = jnp.sum(acc_ref[...], axis=0, keepdims=True)

        dh_ref[...] = dh

    row = pl.BlockSpec((bm, d), lambda i: (i, 0))
    gain = pl.BlockSpec((1, d), lambda i: (0, 0))
    outs = pl.pallas_call(
        body, name=name, grid=(nsteps,),
        in_specs=[row, row] + [gain] * ng + [row] * ng,
        out_specs=[row] + [gain] * ng,
        out_shape=[jax.ShapeDtypeStruct((m, d), F32)] + [jax.ShapeDtypeStruct((1, d), F32)] * ng,
        scratch_shapes=[pltpu.VMEM((8, d), F32)] * ng,
        compiler_params=_params("arbitrary"),
    )(h, dres, *gains, *dns)
    return outs[0], list(outs[1:])


def _loss_head(h, gain, target):
    m, d = h.shape
    nsteps = m // LANES

    def body(h_ref, g_ref, t_ref, loss_ref, dh_ref, dg_ref, lacc, gacc):
        i = pl.program_id(0)

        @pl.when(i == 0)
        def _():
            lacc[...] = jnp.zeros_like(lacc)
            gacc[...] = jnp.zeros_like(gacc)
            dh_ref[...] = jnp.zeros_like(dh_ref)

        @pl.when(i > 0)
        def _():
            x = h_ref[...]
            r = lax.rsqrt(jnp.mean(x * x, axis=-1, keepdims=True) + NORM_EPS)
            xhat = x * r
            g = g_ref[...]
            err = xhat * g - t_ref[...]
            lacc[...] += jnp.sum((err * err).reshape(LANES // 8, 8, d), axis=0)
            dn = err * (1.0 / d)
            dy = dn * g
            dh_ref[...] = r * (dy - xhat * jnp.mean(dy * xhat, axis=-1, keepdims=True))
            gacc[...] += jnp.sum((dn * xhat).reshape(LANES // 8, 8, d), axis=0)

        @pl.when(i == nsteps - 1)
        def _():
            tot = jnp.sum(jnp.sum(lacc[...], axis=1, keepdims=True), axis=0, keepdims=True)
            loss_ref[...] = jnp.broadcast_to(tot * (0.5 / d), (1, LANES))
            dg_ref[...] = jnp.sum(gacc[...], axis=0, keepdims=True)

    row = pl.BlockSpec((LANES, d), lambda i: (i, 0))
    trow = pl.BlockSpec((LANES, d), lambda i: (jnp.maximum(i - 1, 0), 0))
    gain_spec = pl.BlockSpec((1, d), lambda i: (0, 0))
    return pl.pallas_call(
        body, name="loss_head", grid=(nsteps,),
        in_specs=[row, gain_spec, trow],
        out_specs=[pl.BlockSpec((1, LANES), lambda i: (0, 0)), row, gain_spec],
        out_shape=[jax.ShapeDtypeStruct((1, LANES), F32), jax.ShapeDtypeStruct((m, d), F32),
                   jax.ShapeDtypeStruct((1, d), F32)],
        scratch_shapes=[pltpu.VMEM((8, d), F32), pltpu.VMEM((8, d), F32)],
        compiler_params=_params("arbitrary"),
    )(h, gain, target)


def _swiglu(gu, name):
    m, f2 = gu.shape
    f = f2 // 2

    def body(gu_ref, o_ref):
        g = gu_ref[:, :f]
        u = gu_ref[:, f:]
        o_ref[...] = (g * _sigmoid(g) * u).astype(BF16)

    return pl.pallas_call(
        body, name=name, grid=(m // LANES,),
        in_specs=[pl.BlockSpec((LANES, f2), lambda i: (i, 0))],
        out_specs=pl.BlockSpec((LANES, f), lambda i: (i, 0)),
        out_shape=jax.ShapeDtypeStruct((m, f), BF16),
        compiler_params=_params("parallel"),
    )(gu)


def _swiglu_bwd(gu, da, name):
    m, f2 = gu.shape
    f = f2 // 2

    def body(gu_ref, da_ref, o_ref):
        g = gu_ref[:, :f]
        u = gu_ref[:, f:]
        da_ = da_ref[...]
        s = _sigmoid(g)
        o_ref[:, :f] = (da_ * u * s * (1.0 + g * (1.0 - s))).astype(BF16)
        o_ref[:, f:] = (da_ * g * s).astype(BF16)

    return pl.pallas_call(
        body, name=name, grid=(m // LANES,),
        in_specs=[pl.BlockSpec((LANES, f2), lambda i: (i, 0)), pl.BlockSpec((LANES, f), lambda i: (i, 0))],
        out_specs=pl.BlockSpec((LANES, f2), lambda i: (i, 0)),
        out_shape=jax.ShapeDtypeStruct((m, f2), BF16),
        compiler_params=_params("parallel"),
    )(gu, da)


def _conv_taps(xa, w_ref):
    acc = xa * w_ref[CONV_WIDTH - 1:CONV_WIDTH, :]
    for i in range(CONV_WIDTH - 1):
        acc = acc + pltpu.roll(xa, CONV_WIDTH - 1 - i, 0) * w_ref[i:i + 1, :]
    return acc[8:]


def _gdn_conv(proj, conv_w, n_heads):
    m, w3 = proj.shape
    wd = w3 // 3
    nblk = m // LANES

    def body(x_ref, w_ref, o_ref):
        c = pl.program_id(0)
        is_qk = c < 2 * n_heads

        def finish(cv):
            s = cv * _sigmoid(cv)
            nrm = s * lax.rsqrt(jnp.sum(s * s, axis=-1, keepdims=True) + NORM_EPS)
            return jnp.where(is_qk, nrm, s)

        x0 = jnp.concatenate([jnp.zeros((8, LANES), F32), x_ref[pl.ds(0, LANES), :]], axis=0)
        o_ref[pl.ds(0, LANES), :] = finish(_conv_taps(x0, w_ref))

        def step(b, carry):
            r0 = pl.multiple_of(b * LANES, LANES)
            xa = x_ref[pl.ds(r0 - 8, LANES + 8), :]
            o_ref[pl.ds(r0, LANES), :] = finish(_conv_taps(xa, w_ref))
            return carry

        lax.fori_loop(1, nblk, step, 0)

    return pl.pallas_call(
        body, name="gdn_conv", grid=(w3 // LANES,),
        in_specs=[pl.BlockSpec((m, LANES), lambda c: (0, c)), pl.BlockSpec((CONV_WIDTH, LANES), lambda c: (0, c))],
        out_specs=pl.BlockSpec((None, m, LANES), lambda c: (c // n_heads, 0, c % n_heads)),
        out_shape=jax.ShapeDtypeStruct((3, m, wd), F32),
        compiler_params=_params("parallel"),
    )(proj, conv_w)


def _gdn_conv_bwd(proj, conv_w, dact, n_heads):
    m, w3 = proj.shape
    nblk = m // LANES

    def body(x_ref, w_ref, dy_ref, dx_ref, dw_ref, dc_ref):
        c = pl.program_id(0)
        is_qk = c < 2 * n_heads
        dc_ref[pl.ds(m, 8), :] = jnp.zeros((8, LANES), F32)

        def pre_act(xa):
            cv = _conv_taps(xa, w_ref)
            sg = _sigmoid(cv)
            s = cv * sg
            return cv, sg, s

        def dconv_in(b, xa, dwacc):
            r0 = pl.multiple_of(b * LANES, LANES)
            cv, sg, s = pre_act(xa)
            dy = dy_ref[pl.ds(r0, LANES), :]
            rn = lax.rsqrt(jnp.sum(s * s, axis=-1, keepdims=True) + NORM_EPS)
            yn = s * rn
            ds_qk = rn * (dy - yn * jnp.sum(dy * yn, axis=-1, keepdims=True))
            ds = jnp.where(is_qk, ds_qk, dy)
            dcv = ds * sg * (1.0 + cv * (1.0 - sg))
            dc_ref[pl.ds(r0, LANES), :] = dcv
            new = []
            for i in range(CONV_WIDTH):
                xs = xa[8:] if i == CONV_WIDTH - 1 else pltpu.roll(xa, CONV_WIDTH - 1 - i, 0)[8:]
                new.append(dwacc[i] + jnp.sum((dcv * xs).reshape(LANES // 8, 8, LANES), axis=0))
            return tuple(new)

        x0 = jnp.concatenate([jnp.zeros((8, LANES), F32), x_ref[pl.ds(0, LANES), :]], axis=0)
        zero = jnp.zeros((8, LANES), F32)
        dwacc = dconv_in(0, x0, (zero,) * CONV_WIDTH)

        def step1(b, acc):
            r0 = pl.multiple_of(b * LANES, LANES)
            return dconv_in(b, x_ref[pl.ds(r0 - 8, LANES + 8), :], acc)

        dwacc = lax.fori_loop(1, nblk, step1, dwacc)
        for i in range(CONV_WIDTH):
            dw_ref[i:i + 1, :] = jnp.sum(dwacc[i], axis=0, keepdims=True)

        def step2(b, carry):
            r0 = pl.multiple_of(b * LANES, LANES)
            da = dc_ref[pl.ds(r0, LANES + 8), :]
            acc = da * w_ref[CONV_WIDTH - 1:CONV_WIDTH, :]
            for i in range(CONV_WIDTH - 1):
                sh = CONV_WIDTH - 1 - i
                acc = acc + pltpu.roll(da, LANES + 8 - sh, 0) * w_ref[i:i + 1, :]
            dx_ref[pl.ds(r0, LANES), :] = acc[:LANES]
            return carry

        lax.fori_loop(0, nblk, step2, 0)

    return pl.pallas_call(
        body, name="gdn_conv_bwd", grid=(w3 // LANES,),
        in_specs=[pl.BlockSpec((m, LANES), lambda c: (0, c)),
                  pl.BlockSpec((CONV_WIDTH, LANES), lambda c: (0, c)),
                  pl.BlockSpec((None, m, LANES), lambda c: (c // n_heads, 0, c % n_heads))],
        out_specs=[pl.BlockSpec((m, LANES), lambda c: (0, c)), pl.BlockSpec((CONV_WIDTH, LANES), lambda c: (0, c))],
        out_shape=[jax.ShapeDtypeStruct((m, w3), F32), jax.ShapeDtypeStruct((CONV_WIDTH, w3), F32)],
        scratch_shapes=[pltpu.VMEM((m + 8, LANES), F32)],
        compiler_params=_params("parallel"),
    )(proj, conv_w, dact)


def _softplus(x):
    return jnp.maximum(x, 0.0) + jnp.log(1.0 + jnp.exp(-jnp.abs(x)))


def _gdn_gates(ab, a_log, dt_bias, n_heads):
    m = ab.shape[0]
    bm = _pick(m, (640, 384, 128))

    def body(ab_ref, al_ref, dt_ref, g_ref, b_ref):
        i = pl.program_id(0)
        x = ab_ref[...]
        live = (_iota2((bm, LANES), 0) + i * bm) >= PAD_ROWS
        g = jnp.where(live, -jnp.exp(al_ref[...]) * _softplus(x + dt_ref[...]), 0.0)
        beta = jnp.where(live, _sigmoid(x), 0.0)
        for h in range(n_heads):
            g_ref[:, h * LANES:(h + 1) * LANES] = jnp.broadcast_to(g[:, h:h + 1], (bm, LANES))
            b_ref[:, h * LANES:(h + 1) * LANES] = jnp.broadcast_to(
                beta[:, n_heads + h:n_heads + h + 1], (bm, LANES))

    wide = pl.BlockSpec((bm, n_heads * LANES), lambda i: (i, 0))
    return pl.pallas_call(
        body, name="gdn_gates", grid=(m // bm,),
        in_specs=[pl.BlockSpec((bm, LANES), lambda i: (i, 0))] + [pl.BlockSpec((1, LANES), lambda i: (0, 0))] * 2,
        out_specs=[wide, wide],
        out_shape=[jax.ShapeDtypeStruct((m, n_heads * LANES), F32)] * 2,
        compiler_params=_params("parallel"),
    )(ab, a_log, dt_bias)


def _gdn_gates_bwd(ab, a_log, dt_bias, dg_b, dbeta_b, n_heads):
    m = ab.shape[0]
    bm = _pick(m, (640, 384, 128))
    nsteps = m // bm

    def body(ab_ref, al_ref, dt_ref, dg_ref, db_ref, dab_ref, dal_ref, ddt_ref, acc_al, acc_dt):
        i = pl.program_id(0)
        x = ab_ref[...]
        lane = _iota2((bm, LANES), 1)
        live = (_iota2((bm, LANES), 0) + i * bm) >= PAD_ROWS
        dg = jnp.zeros((bm, LANES), F32)
        dbeta = jnp.zeros((bm, LANES), F32)
        for h in range(n_heads):
            dg = jnp.where(lane == h, dg_ref[:, h * LANES:(h + 1) * LANES], dg)
            dbeta = jnp.where(lane == n_heads + h, db_ref[:, h * LANES:(h + 1) * LANES], dbeta)
        dg = jnp.where(live, dg, 0.0)
        dbeta = jnp.where(live, dbeta, 0.0)
        nea = -jnp.exp(al_ref[...])
        pre = x + dt_ref[...]
        d_pre = dg * nea * _sigmoid(pre)
        beta = _sigmoid(x)
        dab_ref[...] = d_pre + dbeta * beta * (1.0 - beta)
        p_al = jnp.sum((dg * nea * _softplus(pre)).reshape(bm // 8, 8, LANES), axis=0)
        p_dt = jnp.sum(d_pre.reshape(bm // 8, 8, LANES), axis=0)

        @pl.when(i == 0)
        def _():
            acc_al[...] = p_al
            acc_dt[...] = p_dt

        @pl.when(i > 0)
        def _():
            acc_al[...] += p_al
            acc_dt[...] += p_dt

        @pl.when(i == nsteps - 1)
        def _():
            dal_ref[...] = jnp.sum(acc_al[...], axis=0, keepdims=True)
            ddt_ref[...] = jnp.sum(acc_dt[...], axis=0, keepdims=True)

    wide = pl.BlockSpec((bm, n_heads * LANES), lambda i: (i, 0))
    one = pl.BlockSpec((1, LANES), lambda i: (0, 0))
    nar = pl.BlockSpec((bm, LANES), lambda i: (i, 0))
    return pl.pallas_call(
        body, name="gdn_gates_bwd", grid=(nsteps,),
        in_specs=[nar, one, one, wide, wide],
        out_specs=[nar, one, one],
        out_shape=[jax.ShapeDtypeStruct((m, LANES), F32), jax.ShapeDtypeStruct((1, LANES), F32),
                   jax.ShapeDtypeStruct((1, LANES), F32)],
        scratch_shapes=[pltpu.VMEM((8, LANES), F32)] * 2,
        compiler_params=_params("arbitrary"),
    )(ab, a_log, dt_bias, dg_b, dbeta_b)


def _chunk_masks():
    c = GDN_CHUNK
    ri, ci = _iota2((c, c), 0), _iota2((c, c), 1)
    return ri >= ci, ri > ci, ri == ci


def _chunk_decay(gb):
    c = GDN_CHUNK
    incl, _, _ = _chunk_masks()
    g_cum = _hdot(incl.astype(F32), gb)
    g_row = _hdot(jnp.ones((c, LANES), F32), g_cum, NT) * (1.0 / LANES)
    diff = g_cum[:, :c] - g_row
    dmat = jnp.where(incl, jnp.exp(jnp.where(incl, diff, 0.0)), 0.0)
    return g_cum, dmat


def _unit_lower_inverse(a):
    _, _, eye = _chunk_masks()
    x = eye.astype(F32) - a
    p = a
    for _ in range(int(math.log2(GDN_CHUNK)) - 1):
        p = _hdot(p, p)
        x = x + _hdot(x, p)
    return x


def _gdn_prep(qkv, g_b, beta_b, n_heads):
    _, m, wd = qkv.shape
    c = GDN_CHUNK
    nc = m // c
    cb = _pick(nc, (5, 3, 2))
    rows = cb * c
    qscale = GDN_HEAD_DIM ** -0.5

    def body(qkv_ref, g_ref, b_ref, ub_ref, wc_ref, qd_ref, kd_ref, gc_ref, p_ref, t_ref):
        incl, strict, _ = _chunk_masks()
        for s in range(cb):
            sl = pl.ds(s * c, c)
            q = qkv_ref[0, sl, :] * qscale
            k = qkv_ref[1, sl, :]
            v = qkv_ref[2, sl, :]
            bb = b_ref[sl, :]
            g_cum, dmat = _chunk_decay(g_ref[sl, :])
            gam = jnp.exp(g_cum)
            kk = _bdot(k, k, NT)
            a = jnp.where(strict, bb[:, :c] * dmat * kk, 0.0)
            t = _unit_lower_inverse(a)
            ub_ref[sl, :] = _hdot(t, bb * v)
            wc_ref[sl, :] = _hdot(t, bb * gam * k)
            p_ref[s] = _bdot(q, k, NT) * dmat
            qd_ref[sl, :] = q * gam
            kd_ref[sl, :] = k * jnp.exp(g_cum[c - 1:c, :] - g_cum)
            gc_ref[sl, :] = g_cum
            t_ref[s] = t

    blk = pl.BlockSpec((rows, LANES), lambda h, n: (n, h))
    sq = pl.BlockSpec((None, cb, c, c), lambda h, n: (h, n, 0, 0))
    big = jax.ShapeDtypeStruct((m, wd), F32)
    small = jax.ShapeDtypeStruct((n_heads, nc, c, c), F32)
    return pl.pallas_call(
        body, name="gdn_prep", grid=(n_heads, nc // cb),
        in_specs=[pl.BlockSpec((3, rows, LANES), lambda h, n: (0, n, h)), blk, blk],
        out_specs=[blk] * 5 + [sq, sq],
        out_shape=[big] * 5 + [small, small],
        compiler_params=_params("parallel", "parallel"),
    )(qkv, g_b, beta_b)


def _gdn_prep_bwd(qkv, g_b, beta_b, ub, wc, tinv, dub, dwc, dqd, dkd, dp, dgl, n_heads):
    _, m, wd = qkv.shape
    c = GDN_CHUNK
    nc = m // c
    cb = _pick(nc, (5, 3, 2))
    rows = cb * c
    qscale = GDN_HEAD_DIM ** -0.5

    def body(qkv_ref, g_ref, b_ref, ub_ref, wc_ref, t_ref, dub_ref, dwc_ref, dqd_ref, dkd_ref, dp_ref, dgl_ref,
             dqkv_ref, dg_ref, db_ref):
        incl, strict, _ = _chunk_masks()
        ones_c = jnp.ones((c, LANES), F32)
        ones_l = jnp.ones((LANES, LANES), F32)
        last_row = _iota2((c, LANES), 0) == c - 1
        for s in range(cb):
            sl = pl.ds(s * c, c)
            q = qkv_ref[0, sl, :] * qscale
            k = qkv_ref[1, sl, :]
            v = qkv_ref[2, sl, :]
            bb = b_ref[sl, :]
            bcol = bb[:, :c]
            g_cum, dmat = _chunk_decay(g_ref[sl, :])
            gam = jnp.exp(g_cum)
            edec = jnp.exp(g_cum[c - 1:c, :] - g_cum)
            kk = _bdot(k, k, NT)
            qk = _bdot(q, k, NT)
            t = t_ref[s]
            drhs_u = _hdot(t, dub_ref[sl, :], TN)
            drhs_w = _hdot(t, dwc_ref[sl, :], TN)
            da = -jnp.where(strict, _bdot(drhs_u, ub_ref[sl, :], NT) + _bdot(drhs_w, wc_ref[sl, :], NT), 0.0)
            dbeta = _hdot(da * dmat * kk, ones_c)
            dd = da * bcol * kk
            dkk = da * bcol * dmat
            dk = _bdot(dkk, k) + _bdot(dkk, k, TN)
            dv = bb * drhs_u
            dbeta = dbeta + _hdot(drhs_u * v, ones_l)
            tk = _hdot(drhs_w * k, ones_l)
            dk = dk + bb * gam * drhs_w
            dbeta = dbeta + tk * gam
            dgam = bb * tk
            dpm = jnp.where(incl, dp_ref[s], 0.0)
            dpd = dpm * dmat
            dq = _bdot(dpd, k)
            dk = dk + _bdot(dpd, q, TN)
            dd = dd + dpm * qk
            dqd = dqd_ref[sl, :]
            dq = dq + dqd * gam
            dgam = dgam + _hdot(dqd * q, ones_l)
            dkd = dkd_ref[sl, :]
            dk = dk + dkd * edec
            skd = _hdot(dkd * k * edec, ones_l)
            dgc = dgam * gam - skd
            e = dd * dmat
            dgc = dgc + _hdot(e, ones_c) - _hdot(e, ones_c, TN)
            tot = _hdot(jnp.ones((c, c), F32), skd)
            dgc = dgc + jnp.where(last_row, tot + dgl_ref[s, 0:1, :], 0.0)
            dg_ref[sl, :] = _hdot((_iota2((c, c), 0) <= _iota2((c, c), 1)).astype(F32), dgc)
            db_ref[sl, :] = dbeta
            dqkv_ref[0, sl, :] = dq * qscale
            dqkv_ref[1, sl, :] = dk
            dqkv_ref[2, sl, :] = dv

    blk = pl.BlockSpec((rows, LANES), lambda h, n: (n, h))
    blk3 = pl.BlockSpec((3, rows, LANES), lambda h, n: (0, n, h))
    sq = pl.BlockSpec((None, cb, c, c), lambda h, n: (h, n, 0, 0))
    sgl = pl.BlockSpec((None, cb, 8, LANES), lambda h, n: (h, n, 0, 0))
    big = jax.ShapeDtypeStruct((m, wd), F32)
    return pl.pallas_call(
        body, name="gdn_prep_bwd", grid=(n_heads, nc // cb),
        in_specs=[blk3, blk, blk, blk, blk, sq, blk, blk, blk, blk, sq, sgl],
        out_specs=[blk3, blk, blk],
        out_shape=[jax.ShapeDtypeStruct((3, m, wd), F32), big, big],
        compiler_params=_params("parallel", "parallel"),
    )(qkv, g_b, beta_b, ub, wc, tinv, dub, dwc, dqd, dkd, dp, dgl)


def _gdn_scan(ub, wc, qd, kd, gc, p, n_heads):
    m, wd = ub.shape
    c = GDN_CHUNK
    nc = m // c
    dh = GDN_HEAD_DIM

    def body(ub_ref, wc_ref, qd_ref, kd_ref, gc_ref, p_ref, o_ref, ssave_ref, s_ref):
        n = pl.program_id(1)

        @pl.when(n == 0)
        def _():
            s_ref[...] = jnp.zeros_like(s_ref)

        s = s_ref[...]
        ssave_ref[...] = s
        u = ub_ref[...] - _bdot(wc_ref[...], s)
        o_ref[...] = _bdot(qd_ref[...], s) + _bdot(p_ref[...], u)
        egl = jnp.exp(gc_ref[c - 1:c, :])
        s_ref[...] = s * egl + _bdot(kd_ref[...], u, TN)

    blk = pl.BlockSpec((c, LANES), lambda h, n: (n, h))
    return pl.pallas_call(
        body, name="gdn_scan", grid=(n_heads, nc),
        in_specs=[blk] * 5 + [pl.BlockSpec((None, None, c, c), lambda h, n: (h, n, 0, 0))],
        out_specs=[blk, pl.BlockSpec((None, None, dh, dh), lambda h, n: (h, n, 0, 0))],
        out_shape=[jax.ShapeDtypeStruct((m, wd), F32), jax.ShapeDtypeStruct((n_heads, nc, dh, dh), F32)],
        scratch_shapes=[pltpu.VMEM((dh, dh), F32)],
        compiler_params=_params("parallel", "arbitrary"),
    )(ub, wc, qd, kd, gc, p)


def _gdn_scan_bwd(do, ssave, ub, wc, qd, kd, gc, p, n_heads):
    m, wd = ub.shape
    c = GDN_CHUNK
    nc = m // c
    dh = GDN_HEAD_DIM

    def body(do_ref, ss_ref, ub_ref, wc_ref, qd_ref, kd_ref, gc_ref, p_ref,
             dub_ref, dwc_ref, dqd_ref, dkd_ref, dp_ref, dgl_ref, ds_ref):
        n = pl.program_id(1)

        @pl.when(n == 0)
        def _():
            ds_ref[...] = jnp.zeros_like(ds_ref)

        ds_next = ds_ref[...]
        s = ss_ref[...]
        do_ = do_ref[...]
        wcv = wc_ref[...]
        kdv = kd_ref[...]
        u = ub_ref[...] - _bdot(wcv, s)
        egl = jnp.exp(gc_ref[c - 1:c, :])
        du = _bdot(p_ref[...], do_, TN) + _bdot(kdv, ds_next)
        dub_ref[...] = du
        dwc_ref[...] = -_bdot(du, s, NT)
        dp_ref[...] = _bdot(do_, u, NT)
        dqd_ref[...] = _bdot(do_, s, NT)
        dkd_ref[...] = _bdot(u, ds_next, NT)
        col = _hdot(jnp.ones((8, dh), F32), s * ds_next)
        dgl_ref[...] = egl * _hdot(col, jnp.ones((LANES, LANES), F32))
        ds_ref[...] = _bdot(qd_ref[...], do_, TN) + egl * ds_next - _bdot(wcv, du, TN)

    blk = pl.BlockSpec((c, LANES), lambda h, n: (nc - 1 - n, h))
    sq = pl.BlockSpec((None, None, c, c), lambda h, n: (h, nc - 1 - n, 0, 0))
    big = jax.ShapeDtypeStruct((m, wd), F32)
    return pl.pallas_call(
        body, name="gdn_scan_bwd", grid=(n_heads, nc),
        in_specs=[blk, pl.BlockSpec((None, None, dh, dh), lambda h, n: (h, nc - 1 - n, 0, 0))] + [blk] * 5 + [sq],
        out_specs=[blk] * 4 + [sq, pl.BlockSpec((None, None, 8, LANES), lambda h, n: (h, nc - 1 - n, 0, 0))],
        out_shape=[big] * 4 + [jax.ShapeDtypeStruct((n_heads, nc, c, c), F32),
                               jax.ShapeDtypeStruct((n_heads, nc, 8, LANES), F32)],
        scratch_shapes=[pltpu.VMEM((dh, dh), F32)],
        compiler_params=_params("parallel", "arbitrary"),
    )(do, ssave, ub, wc, qd, kd, gc, p)


def _gdn_outnorm(o, gate, gain, n_heads):
    m, wd = o.shape
    bm = _pick(m, (640, 384, 128))

    def body(o_ref, gt_ref, gn_ref, y_ref):
        gn = gn_ref[...]
        for h in range(n_heads):
            sl = slice(h * LANES, (h + 1) * LANES)
            x = o_ref[:, sl]
            gt = gt_ref[:, sl]
            r = lax.rsqrt(jnp.mean(x * x, axis=-1, keepdims=True) + NORM_EPS)
            y_ref[:, sl] = (x * r * gn * gt * _sigmoid(gt)).astype(BF16)

    row = pl.BlockSpec((bm, wd), lambda i: (i, 0))
    return pl.pallas_call(
        body, name="gdn_outnorm", grid=(m // bm,),
        in_specs=[row, row, pl.BlockSpec((1, LANES), lambda i: (0, 0))], out_specs=row,
        out_shape=jax.ShapeDtypeStruct((m, wd), BF16),
        compiler_params=_params("parallel"),
    )(o, gate, gain)


def _gdn_outnorm_bwd(o, gate, gain, dy, n_heads):
    m, wd = o.shape
    bm = _pick(m, (640, 384, 128))
    nsteps = m // bm

    def body(o_ref, gt_ref, gn_ref, dy_ref, do_ref, dgt_ref, dgn_ref, acc_ref):
        i = pl.program_id(0)
        gn = gn_ref[...]
        part = jnp.zeros((8, LANES), F32)
        for h in range(n_heads):
            sl = slice(h * LANES, (h + 1) * LANES)
            x = o_ref[:, sl]
            gt = gt_ref[:, sl]
            d_out = dy_ref[:, sl]
            r = lax.rsqrt(jnp.mean(x * x, axis=-1, keepdims=True) + NORM_EPS)
            xhat = x * r
            sg = _sigmoid(gt)
            dz = d_out * gt * sg
            dgt_ref[:, sl] = d_out * xhat * gn * sg * (1.0 + gt * (1.0 - sg))
            dxh = dz * gn
            do_ref[:, sl] = r * (dxh - xhat * jnp.mean(dxh * xhat, axis=-1, keepdims=True))
            part = part + jnp.sum((dz * xhat).reshape(bm // 8, 8, LANES), axis=0)

        @pl.when(i == 0)
        def _():
            acc_ref[...] = part

        @pl.when(i > 0)
        def _():
            acc_ref[...] += part

        @pl.when(i == nsteps - 1)
        def _():
            dgn_ref[...] = jnp.sum(acc_ref[...], axis=0, keepdims=True)

    row = pl.BlockSpec((bm, wd), lambda i: (i, 0))
    one = pl.BlockSpec((1, LANES), lambda i: (0, 0))
    return pl.pallas_call(
        body, name="gdn_outnorm_bwd", grid=(nsteps,),
        in_specs=[row, row, one, row], out_specs=[row, row, one],
        out_shape=[jax.ShapeDtypeStruct((m, wd), F32)] * 2 + [jax.ShapeDtypeStruct((1, LANES), F32)],
        scratch_shapes=[pltpu.VMEM((8, LANES), F32)],
        compiler_params=_params("arbitrary"),
    )(o, gate, gain, dy)


def _sb_consts(kind):
    ri, ci = _iota2((LANES, LANES), 0), _iota2((LANES, LANES), 1)
    tri = {"suffix_excl": ri > ci, "prefix_incl": ri <= ci, "prefix_excl": ri < ci}[kind]
    return jnp.concatenate([tri.astype(BF16), jnp.ones((LANES, LANES), BF16)], axis=1)


def _sb_stack(a):
    first = _iota2((LANES, LANES), 1) < SB_HEAD_DIM
    return jnp.concatenate([jnp.where(first, a, 0.0), jnp.where(first, 0.0, a)], axis=0)


def _sb_unstack(a2):
    first = _iota2((LANES, LANES), 1) < SB_HEAD_DIM
    return jnp.where(first, a2[:LANES], a2[LANES:])


def _sb_visible(qi, kb, live):
    kpos = _iota2((2 * LANES, LANES), 1) + kb * LANES
    qpos = (_iota2((2 * LANES, LANES), 0) & (LANES - 1)) + qi * LANES
    return (kpos < qpos) & (kpos >= PAD_ROWS) & live


def _sb_scores(q2, kt, vis):
    z = lax.dot_general(q2, kt, NT, preferred_element_type=F32)
    t = jnp.log(1.0 + jnp.exp(-jnp.abs(z)))
    lb = jnp.minimum(z, 0.0) - t
    lk = lb - z
    if vis is not None:
        lk = jnp.where(vis, lk, 0.0)
    return lb, lk


def _sb_attention(q, kv):
    m, wd = q.shape
    npair = wd // LANES
    nq = m // LANES
    scale = SB_HEAD_DIM ** -0.5

    def body(q_ref, k_ref, v_ref, o_ref, w_ref, b_ref):
        qi = pl.program_id(1)
        su = _sb_consts("suffix_excl")
        q2 = _sb_stack(q_ref[...] * scale).astype(BF16)

        def tiles(kbs, carry, live):
            run, acc = carry
            parts = []
            for kb in kbs:
                r0 = pl.multiple_of(kb * LANES, LANES)
                vis = None if live is None else _sb_visible(qi, kb, live)
                lb, lk = _sb_scores(q2, k_ref[pl.ds(r0, LANES), :], vis)
                parts.append((kb, r0, vis, lb, _split_dot(lk, su)))
            for kb, r0, vis, lb, cs in parts:
                w = jnp.exp(lb + cs[:, :LANES] + run)
                if vis is not None:
                    w = jnp.where(vis, w, 0.0)
                run = run + cs[:, LANES:]
                wb = w.astype(BF16)
                w_ref[kb] = wb
                b_ref[kb] = jnp.exp(lb).astype(BF16)
                acc = acc + lax.dot_general(wb, v_ref[pl.ds(r0, LANES), :], NN, preferred_element_type=F32)
            return run, acc

        def group(j, carry):
            kb = qi - 1 - SB_UNROLL * j
            return tiles([kb - u for u in range(SB_UNROLL)], carry, None)

        zero = jnp.zeros((2 * LANES, LANES), F32)
        carry = tiles([qi], (zero, zero), True)
        n_mid = jnp.maximum(qi - 1, 0)
        carry = lax.fori_loop(0, n_mid // SB_UNROLL, group, carry)
        carry = lax.fori_loop(0, n_mid % SB_UNROLL, lambda j, c: tiles([n_mid % SB_UNROLL - j], c, None), carry)

        @pl.when(qi >= 1)
        def _():
            _, acc = tiles([0], carry, True)
            o_ref[...] = _sb_unstack(acc)

        @pl.when(qi == 0)
        def _():
            o_ref[...] = _sb_unstack(carry[1])

    qspec = pl.BlockSpec((LANES, LANES), lambda hp, qi: (qi, hp))
    saved = pl.BlockSpec((None, None, nq, 2 * LANES, LANES), lambda hp, qi: (hp, qi, 0, 0, 0))
    saved_shape = jax.ShapeDtypeStruct((npair, nq, nq, 2 * LANES, LANES), BF16)
    return pl.pallas_call(
        body, name="sb_attention", grid=(npair, nq),
        in_specs=[qspec, pl.BlockSpec((m, LANES), lambda hp, qi: (0, hp)),
                  pl.BlockSpec((m, LANES), lambda hp, qi: (0, npair + hp))],
        out_specs=[qspec, saved, saved],
        out_shape=[jax.ShapeDtypeStruct((m, wd), F32), saved_shape, saved_shape],
        compiler_params=_params("parallel", "arbitrary"),
    )(q, kv, kv)


def _sb_attention_bwd(q, kv, w_all, b_all, do):
    m, wd = q.shape
    npair = wd // LANES
    nq = m // LANES
    scale = SB_HEAD_DIM ** -0.5

    def body(q_ref, k_ref, v_ref, w_ref, b_ref, do_ref, dq_ref, dk_ref, dv_ref):
        qi = pl.program_id(1)

        @pl.when(qi == 0)
        def _():
            dk_ref[...] = jnp.zeros_like(dk_ref)
            dv_ref[...] = jnp.zeros_like(dv_ref)

        px = _sb_consts("prefix_excl")
        q2 = _sb_stack(q_ref[...] * scale).astype(BF16)
        do2 = _sb_stack(do_ref[...]).astype(BF16)

        def tiles(kbs, carry, live):
            run_p, dq_acc = carry
            parts = []
            for kb in kbs:
                r0 = pl.multiple_of(kb * LANES, LANES)
                wb = w_ref[kb]
                da = lax.dot_general(do2, v_ref[pl.ds(r0, LANES), :], NT, preferred_element_type=F32)
                pw = wb.astype(F32) * da
                parts.append((kb, r0, wb, pw, _split_dot(pw, px)))
            for kb, r0, wb, pw, ps in parts:
                beta = b_ref[kb].astype(F32)
                dz = pw * (1.0 - beta) - beta * (run_p + ps[:, :LANES])
                if live is not None:
                    dz = jnp.where(_sb_visible(qi, kb, live), dz, 0.0)
                run_p = run_p + ps[:, LANES:]
                dz = dz.astype(BF16)
                dq_acc = dq_acc + lax.dot_general(dz, k_ref[pl.ds(r0, LANES), :], NN, preferred_element_type=F32)
                dk_ref[pl.ds(r0, LANES), :] += lax.dot_general(dz, q2, TN, preferred_element_type=F32)
                dv_ref[pl.ds(r0, LANES), :] += lax.dot_general(wb, do2, TN, preferred_element_type=F32)
            return run_p, dq_acc

        def group(j, carry):
            kb = 1 + SB_UNROLL_BWD * j
            return tiles([kb + u for u in range(SB_UNROLL_BWD)], carry, None)

        zero = jnp.zeros((2 * LANES, LANES), F32)
        carry = tiles([0], (zero, zero), True)
        n_mid = jnp.maximum(qi - 1, 0)
        n_grp = n_mid // SB_UNROLL_BWD
        carry = lax.fori_loop(0, n_grp, group, carry)
        carry = lax.fori_loop(1 + n_grp * SB_UNROLL_BWD, jnp.maximum(qi, 1), lambda kb, c: tiles([kb], c, None), carry)

        @pl.when(qi >= 1)
        def _():
            _, dq_acc = tiles([qi], carry, True)
            dq_ref[...] = _sb_unstack(dq_acc) * scale

        @pl.when(qi == 0)
        def _():
            dq_ref[...] = _sb_unstack(carry[1]) * scale

    qspec = pl.BlockSpec((LANES, LANES), lambda hp, qi: (qi, hp))
    strip = pl.BlockSpec((m, LANES), lambda hp, qi: (0, hp))
    saved = pl.BlockSpec((None, None, nq, 2 * LANES, LANES), lambda hp, qi: (hp, qi, 0, 0, 0))
    big = jax.ShapeDtypeStruct((m, wd), F32)
    return pl.pallas_call(
        body, name="sb_attention_bwd", grid=(npair, nq),
        in_specs=[qspec, strip, pl.BlockSpec((m, LANES), lambda hp, qi: (0, npair + hp)), saved, saved, qspec],
        out_specs=[qspec, strip, strip],
        out_shape=[big, big, big],
        compiler_params=_params("parallel", "arbitrary"),
    )(q, kv, kv, w_all, b_all, do)


def _mesh_pos():
    return lax.axis_index("x"), lax.axis_index("y"), lax.axis_index("c")


def _exchange(srcs, name, scatter):
    n = len(srcs)
    rows = [s.shape[-2] for s in srcs]
    offs = [sum(rows[:i]) for i in range(n)]
    total = sum(rows)

    def body(*refs):
        src_refs, out_ref = refs[:n], refs[n]
        send_sems, recv_sems, local_sems = refs[n + 1:]
        x, y, c = _mesh_pos()
        me = 4 * x + 2 * y + c

        def piece(i, d):
            return src_refs[i].at[d] if scatter else src_refs[i]

        def window(slot, i):
            return out_ref.at[slot, pl.ds(offs[i], rows[i])]

        local = [pltpu.make_async_copy(piece(i, me), window(me, i), local_sems.at[i]) for i in range(n)]
        for cp in local:
            cp.start()
        peers = []
        for k in range(1, N_DEV):
            px = 1 - x if k & 4 else x
            py = 1 - y if k & 2 else y
            pc = 1 - c if k & 1 else c
            pid = 4 * px + 2 * py + pc
            for i in range(n):
                pltpu.make_async_remote_copy(
                    src_ref=piece(i, pid), dst_ref=window(me, i),
                    send_sem=send_sems.at[k - 1], recv_sem=recv_sems.at[k - 1],
                    device_id=(px, py, pc), device_id_type=pl.DeviceIdType.MESH).start()
            peers.append(pid)
        slabs = [pltpu.make_async_remote_copy(
            src_ref=out_ref.at[me], dst_ref=out_ref.at[peers[k - 1]],
            send_sem=send_sems.at[k - 1], recv_sem=recv_sems.at[k - 1],
            device_id=(x, y, c), device_id_type=pl.DeviceIdType.MESH) for k in range(1, N_DEV)]
        for cp in slabs:
            cp.wait_recv()
        for cp in slabs:
            cp.wait_send()
        for cp in local:
            cp.wait()

    return pl.pallas_call(
        body, name=name,
        in_specs=[pl.BlockSpec(memory_space=pl.ANY)] * n, out_specs=pl.BlockSpec(memory_space=pl.ANY),
        out_shape=jax.ShapeDtypeStruct((N_DEV, total, FLAT_COLS), srcs[0].dtype),
        scratch_shapes=[pltpu.SemaphoreType.DMA((N_DEV - 1,)), pltpu.SemaphoreType.DMA((N_DEV - 1,)),
                        pltpu.SemaphoreType.DMA((n,))],
        compiler_params=pltpu.CompilerParams(has_side_effects=True),
    )(*srcs)


def _adamw(parts, w, mom, var):
    rows = w.shape[0]
    br = FLAT_ROW_BLOCK
    c1 = 1.0 - ADAM_B1 ** ADAM_STEP
    c2 = 1.0 - ADAM_B2 ** ADAM_STEP

    def body(p_ref, w_ref, m_ref, v_ref, g_ref, d_ref, nm_ref, nv_ref):
        g = p_ref[0]
        for s in range(1, N_DEV):
            g = g + p_ref[s]
        m_new = ADAM_B1 * m_ref[...] + (1.0 - ADAM_B1) * g
        v_new = ADAM_B2 * v_ref[...] + (1.0 - ADAM_B2) * (g * g)
        m_hat = m_new / c1
        v_hat = v_new / c2
        g_ref[...] = g
        d_ref[...] = -ADAM_LR * (m_hat / (jnp.sqrt(v_hat) + ADAM_EPS) + ADAM_WD * w_ref[...])
        nm_ref[...] = m_new
        nv_ref[...] = v_new

    row = pl.BlockSpec((br, FLAT_COLS), lambda i: (i, 0))
    flat = jax.ShapeDtypeStruct((rows, FLAT_COLS), F32)
    return pl.pallas_call(
        body, name="adamw", grid=(rows // br,),
        in_specs=[pl.BlockSpec((N_DEV, br, FLAT_COLS), lambda i: (0, i, 0)), row, row, row],
        out_specs=[row] * 4, out_shape=[flat] * 4,
        compiler_params=_params("parallel"),
    )(parts, w, mom, var)


SMALL_SHARDED = (("meta_tokens", 1), ("gdn_norm_g", 1), ("gdn_conv_w", 2))
BIG_SHARDED = (("gdn_w_in", 2), ("gdn_w_out", 1), ("w_kv", 1), ("sb_w_q", 1), ("sb_w_o", 1),
               ("ffn_w_gate_up", 2), ("ffn_w_down", 1))
REPLICATED = ("gdn_a_log", "gdn_dt_bias", "gdn_onorm_g", "kv_norm_g", "sb_norm_g", "ffn_norm_g", "final_norm_g")
WEIGHTS = ("meta_tokens", "gdn_norm_g", "gdn_w_in", "gdn_conv_w", "gdn_a_log", "gdn_dt_bias", "gdn_onorm_g",
           "gdn_w_out", "kv_norm_g", "w_kv", "sb_norm_g", "sb_w_q", "sb_w_o", "ffn_norm_g", "ffn_w_gate_up",
           "ffn_w_down", "final_norm_g")
F32_ROWS = 8
BF16_ROWS = 16


def _n_rows(shape, mult):
    return -(-math.prod(shape) // (mult * FLAT_COLS)) * mult


def _as_rows(a, nl, mult):
    lead = a.shape[:nl]
    size = math.prod(a.shape[nl:])
    nr = _n_rows(a.shape[nl:], mult)
    flat = a.reshape(lead + (size,))
    if nr * FLAT_COLS != size:
        flat = jnp.pad(flat, [(0, 0)] * nl + [(0, nr * FLAT_COLS - size)])
    return flat.reshape(lead + (nr, FLAT_COLS))


def _from_rows(flat, r0, shape, nl, mult):
    lead = flat.shape[:nl]
    nr = _n_rows(shape, mult)
    seg = lax.slice_in_dim(flat, r0, r0 + nr, axis=nl).reshape(lead + (nr * FLAT_COLS,))
    return lax.slice_in_dim(seg, 0, math.prod(shape), axis=nl).reshape(lead + tuple(shape)), r0 + nr


def _merge_shards(g, axis):
    t = jnp.moveaxis(g, 0, axis)
    shp = t.shape
    return t.reshape(shp[:axis] + (shp[axis] * shp[axis + 1],) + shp[axis + 2:])


def _split_shards(full, axis):
    shp = full.shape
    t = full.reshape(shp[:axis] + (N_DEV, shp[axis] // N_DEV) + shp[axis + 1:])
    return jnp.moveaxis(t, axis, 0)


def _pad_lanes(a, width=LANES):
    return jnp.pad(a, ((0, 0), (0, width - a.shape[1])))


def _local_step(x, target, w):
    d = x.shape[1]
    gh = w["gdn_a_log"].shape[1]
    gw = gh * GDN_HEAD_DIM
    w_in = w["gdn_w_in"][0]
    w_qkv, w_gate = w_in[:, :3 * gw].astype(BF16), w_in[:, 3 * gw:4 * gw].astype(BF16)
    w_ab = _pad_lanes(w_in[:, 4 * gw:]).astype(BF16)
    conv_w = w["gdn_conv_w"][0]
    a_log, dt_bias = _pad_lanes(w["gdn_a_log"]), _pad_lanes(w["gdn_dt_bias"])
    w_out = w["gdn_w_out"][0].astype(BF16)
    sbw = w["sb_w_q"].shape[2]
    sb_heads = sbw // SB_HEAD_DIM
    w_kv = w["w_kv"].astype(BF16)
    w_k, w_v = w_kv[:, :sbw], w_kv[:, sbw:]
    w_q, w_o = w["sb_w_q"][0].astype(BF16), w["sb_w_o"][0].astype(BF16)
    w_gu = [w["ffn_w_gate_up"][l].astype(BF16) for l in range(2)]
    w_dn = [w["ffn_w_down"][l].astype(BF16) for l in range(2)]
    ffn_g = [w["ffn_norm_g"][l:l + 1] for l in range(2)]
    kv_g, fin_g = w["kv_norm_g"][None], w["final_norm_g"][None]

    h0 = jnp.concatenate([jnp.zeros((PAD_ROWS, d), F32), w["meta_tokens"], x], axis=0)

    (n1,) = _rmsnorm(h0, [w["gdn_norm_g"]], "norm_gdn")
    proj = _matmul(n1, w_qkv, "nn", "mm_gdn_qkv")
    gate = _matmul(n1, w_gate, "nn", "mm_gdn_gate")
    ab = _matmul(n1, w_ab, "nn", "mm_gdn_ab")
    qkv = _gdn_conv(proj, conv_w, gh)
    g_b, beta_b = _gdn_gates(ab, a_log, dt_bias, gh)
    ub, wc, qd, kd, gc, pmat, tinv = _gdn_prep(qkv, g_b, beta_b, gh)
    o_gdn, ssave = _gdn_scan(ub, wc, qd, kd, gc, pmat, gh)
    o2 = _gdn_outnorm(o_gdn, gate, w["gdn_onorm_g"], gh)
    h1 = _matmul(o2, w_out, "nn", "mm_gdn_out", add=h0)

    def ffn_fwd(h, l):
        (n,) = _rmsnorm(h, [ffn_g[l]], f"norm_ffn{l}")
        gu = _matmul(n, w_gu[l], "nn", f"mm_ffn{l}_gu")
        act = _swiglu(gu, f"swiglu{l}")
        return n, gu, act, _matmul(act, w_dn[l], "nn", f"mm_ffn{l}_down", add=h)

    n2, gu0, act0, h2 = ffn_fwd(h1, 0)

    nkv, n3 = _rmsnorm(h2, [kv_g, w["sb_norm_g"]], "norm_kv_sb")
    kv = _matmul(nkv, w_kv, "nn", "mm_kv", out_dtype=BF16)
    q_sb = _matmul(n3, w_q, "nn", "mm_sb_q")
    o_sb, w_sb, beta_sb = _sb_attention(q_sb, kv)
    h3 = _matmul(o_sb, w_o, "nn", "mm_sb_o", add=h2)
    n4, gu1, act1, h4 = ffn_fwd(h3, 1)

    loss, dh4, d_fin = _loss_head(h4, fin_g, target)

    def ffn_bwd(dh, h, n, gu, act, l):
        d_act = _matmul(dh, w_dn[l], "nt", f"mm_ffn{l}_dact")
        dw_dn = _matmul(act, dh, "tn", f"mm_ffn{l}_dwdown")
        dgu = _swiglu_bwd(gu, d_act, f"swiglu_bwd{l}")
        dw_gu = _matmul(n, dgu, "tn", f"mm_ffn{l}_dwgu")
        dn = _matmul(dgu, w_gu[l], "nt", f"mm_ffn{l}_dn")
        dh_in, (dg,) = _rmsnorm_bwd(h, [ffn_g[l]], [dn], dh, f"norm_ffn{l}_bwd")
        return dh_in, dw_gu, dw_dn, dg

    dh3, dw_gu1, dw_dn1, dg_ffn1 = ffn_bwd(dh4, h3, n4, gu1, act1, 1)

    do_sb = _matmul(dh3, w_o, "nt", "mm_sb_do")
    dw_o = _matmul(o_sb, dh3, "tn", "mm_sb_dwo")
    dq_sb, dk_sb, dv_sb = _sb_attention_bwd(q_sb, kv, w_sb, beta_sb, do_sb)
    dw_q = _matmul(n3, dq_sb, "tn", "mm_sb_dwq")
    dn3 = _matmul(dq_sb, w_q, "nt", "mm_sb_dn")
    dw_k = _matmul(nkv, dk_sb, "tn", "mm_dwk")
    dw_v = _matmul(nkv, dv_sb, "tn", "mm_dwv")
    dnkv = _matmul(dk_sb, w_k, "nt", "mm_dnk")
    dnkv = _matmul(dv_sb, w_v, "nt", "mm_dnv", add=dnkv)
    dh2, (dg_kv, dg_sb) = _rmsnorm_bwd(h2, [kv_g, w["sb_norm_g"]], [dnkv, dn3], dh3, "norm_kv_sb_bwd")

    dh1, dw_gu0, dw_dn0, dg_ffn0 = ffn_bwd(dh2, h1, n2, gu0, act0, 0)

    do2 = _matmul(dh1, w_out, "nt", "mm_gdn_do")
    dw_out = _matmul(o2, dh1, "tn", "mm_gdn_dwout")
    do_gdn, dgate, d_onorm = _gdn_outnorm_bwd(o_gdn, gate, w["gdn_onorm_g"], do2, gh)
    dub, dwc, dqd, dkd, dpm, dgl = _gdn_scan_bwd(do_gdn, ssave, ub, wc, qd, kd, gc, pmat, gh)
    dqkv, dg_b, dbeta_b = _gdn_prep_bwd(qkv, g_b, beta_b, ub, wc, tinv, dub, dwc, dqd, dkd, dpm, dgl, gh)
    dab, d_alog, d_dtb = _gdn_gates_bwd(ab, a_log, dt_bias, dg_b, dbeta_b, gh)
    dproj, d_conv = _gdn_conv_bwd(proj, conv_w, dqkv, gh)
    dw_qkv = _matmul(n1, dproj, "tn", "mm_gdn_dwqkv")
    dw_gate = _matmul(n1, dgate, "tn", "mm_gdn_dwgate")
    dw_ab = _matmul(n1, dab, "tn", "mm_gdn_dwab")
    dn1 = _matmul(dproj, w_qkv, "nt", "mm_gdn_dn_qkv")
    dn1 = _matmul(dgate, w_gate, "nt", "mm_gdn_dn_gate", add=dn1)
    dn1 = _matmul(dab, w_ab, "nt", "mm_gdn_dn_ab", add=dn1)
    dh0, (dg_gdn,) = _rmsnorm_bwd(h0, [w["gdn_norm_g"]], [dn1], dh1, "norm_gdn_bwd")

    grads = {
        "meta_tokens": dh0[PAD_ROWS:LANES],
        "gdn_norm_g": dg_gdn,
        "gdn_w_in": jnp.concatenate([dw_qkv, dw_gate, dw_ab[:, :2 * gh]], axis=1)[None],
        "gdn_conv_w": d_conv[None],
        "gdn_a_log": d_alog[:, :gh],
        "gdn_dt_bias": d_dtb[:, :gh],
        "gdn_onorm_g": d_onorm,
        "gdn_w_out": dw_out[None],
        "kv_norm_g": dg_kv[0],
        "w_kv": jnp.concatenate([dw_k, dw_v], axis=1),
        "sb_norm_g": dg_sb,
        "sb_w_q": dw_q[None],
        "sb_w_o": dw_o[None],
        "ffn_norm_g": jnp.concatenate([dg_ffn0, dg_ffn1], axis=0),
        "ffn_w_gate_up": jnp.stack([dw_gu0, dw_gu1]),
        "ffn_w_down": jnp.stack([dw_dn0, dw_dn1]),
        "final_norm_g": d_fin[0],
    }
    return loss, dh0[LANES:], grads


def kernel(x, meta_tokens, gdn_norm_g, gdn_w_in, gdn_conv_w, gdn_a_log, gdn_dt_bias, gdn_onorm_g, gdn_w_out, kv_norm_g, w_kv, sb_norm_g, sb_w_q, sb_w_o, ffn_norm_g, ffn_w_gate_up, ffn_w_down, final_norm_g, loss_target, m_meta_tokens, m_gdn_norm_g, m_gdn_w_in, m_gdn_conv_w, m_gdn_a_log, m_gdn_dt_bias, m_gdn_onorm_g, m_gdn_w_out, m_kv_norm_g, m_w_kv, m_sb_norm_g, m_sb_w_q, m_sb_w_o, m_ffn_norm_g, m_ffn_w_gate_up, m_ffn_w_down, m_final_norm_g, v_meta_tokens, v_gdn_norm_g, v_gdn_w_in, v_gdn_conv_w, v_gdn_a_log, v_gdn_dt_bias, v_gdn_onorm_g, v_gdn_w_out, v_kv_norm_g, v_w_kv, v_sb_norm_g, v_sb_w_q, v_sb_w_o, v_ffn_norm_g, v_ffn_w_gate_up, v_ffn_w_down, v_final_norm_g):
    given = dict(locals())
    small = [n for n, _ in SMALL_SHARDED]
    big = [n for n, _ in BIG_SHARDED]

    g_small = _exchange([jnp.concatenate([_as_rows(given[n], 0, F32_ROWS) for n in small], axis=0)],
                        "all_gather_small", scatter=False)
    g_big = _exchange([_as_rows(given[n].astype(BF16), 0, BF16_ROWS) for n in big],
                      "all_gather_weights", scatter=False)
    full = {n: given[n] for n in REPLICATED}
    for table, gathered, mult in ((SMALL_SHARDED, g_small, F32_ROWS), (BIG_SHARDED, g_big, BF16_ROWS)):
        r = 0
        for n, ax in table:
            shards, r = _from_rows(gathered, r, given[n].shape, 1, mult)
            full[n] = _merge_shards(shards, ax)

    loss_b, grad_x, grads = _local_step(x[0], loss_target[0], full)

    order = small + list(REPLICATED) + big
    used = sum(_n_rows(given[n].shape, F32_ROWS) for n in order)
    filler = (-used) % FLAT_ROW_BLOCK
    small_send = [_as_rows(_split_shards(grads[n], ax), 1, F32_ROWS) for n, ax in SMALL_SHARDED]
    small_send += [_as_rows(jnp.broadcast_to(grads[n][None], (N_DEV,) + grads[n].shape), 1, F32_ROWS)
                   for n in REPLICATED]
    send = [jnp.concatenate(small_send, axis=1)]
    send += [_as_rows(_split_shards(grads[n], ax), 1, F32_ROWS) for n, ax in BIG_SHARDED]
    if filler:
        send.append(jnp.zeros((N_DEV, filler, FLAT_COLS), F32))
    recv = _exchange(send, "scatter_gradients", scatter=True)

    def flat(prefix):
        segs = [_as_rows(given[prefix + n], 0, F32_ROWS) for n in order]
        if filler:
            segs.append(jnp.zeros((filler, FLAT_COLS), F32))
        return jnp.concatenate(segs, axis=0)

    g_f, d_f, m_f, v_f = _adamw(recv, flat(""), flat("m_"), flat("v_"))
    outs = {}
    for tag, f in (("grad", g_f), ("delta", d_f), ("new_m", m_f), ("new_v", v_f)):
        outs[tag], r = {}, 0
        for n in order:
            outs[tag][n], r = _from_rows(f, r, given[n].shape, 0, F32_ROWS)

    loss = lax.psum(loss_b[0, 0], ("x", "y", "c"))
    result = [loss, grad_x[None]]
    for tag in ("grad", "delta", "new_m", "new_v"):
        result += [outs[tag][n] for n in WEIGHTS]
    return tuple(result)
```

```python
import functools
import math

import jax
import jax.numpy as jnp
from jax import lax
from jax.experimental import pallas as pl
from jax.experimental.pallas import tpu as pltpu

F32 = jnp.float32
BF16 = jnp.bfloat16
HIGHEST = lax.Precision.HIGHEST

N_DEV = 8
LANES = 128
N_META = 16
PAD_ROWS = LANES - N_META
GDN_CHUNK = 64
GDN_HEAD_DIM = 128
SB_HEAD_DIM = 64
SB_UNROLL = 8
SB_UNROLL_BWD = 4
CONV_WIDTH = 4
NORM_EPS = 1e-6
FLAT_COLS = 1024
FLAT_ROW_BLOCK = 64
VMEM_LIMIT = 56 * 1024 * 1024

ADAM_LR = 0.001
ADAM_B1 = 0.9
ADAM_B2 = 0.999
ADAM_EPS = 1e-08
ADAM_WD = 0.01
ADAM_STEP = 10

NN = (((1,), (0,)), ((), ()))
NT = (((1,), (1,)), ((), ()))
TN = (((0,), (0,)), ((), ()))


def _params(*sem):
    return pltpu.CompilerParams(dimension_semantics=sem, vmem_limit_bytes=VMEM_LIMIT)


def _pick(n, cands):
    for c in cands:
        if n % c == 0:
            return c
    return n


def _bdot(a, b, dims=NN):
    return lax.dot_general(a.astype(BF16), b.astype(BF16), dims, preferred_element_type=F32)


def _hdot(a, b, dims=NN):
    return lax.dot_general(a, b, dims, preferred_element_type=F32, precision=HIGHEST)


def _split_dot(a, m, parts=2):
    out = None
    for _ in range(parts):
        piece = a.astype(BF16)
        a = a - piece.astype(F32)
        term = lax.dot_general(piece, m, NN, preferred_element_type=F32)
        out = term if out is None else out + term
    return out


def _lockstep(gens):
    while gens:
        alive = []
        for g in gens:
            try:
                next(g)
                alive.append(g)
            except StopIteration:
                pass
        gens = alive


def _sigmoid(x):
    return 1.0 / (1.0 + jnp.exp(-x))


def _iota2(shape, axis):
    return lax.broadcasted_iota(jnp.int32, shape, axis)


def _matmul(a, b, mode, name, out_dtype=F32, add=None):
    if mode == "nn":
        (m, k), n = a.shape, b.shape[1]
    elif mode == "nt":
        (m, k), n = a.shape, b.shape[0]
    else:
        (k, m), n = a.shape, b.shape[1]
    row_c = (640, 512, 384, 256, 128)
    col_c = (1024, 1408, 768, 512, 384, 256, 128)
    if mode == "tn":
        bm, bn, bk = _pick(m, col_c), _pick(n, col_c), _pick(k, row_c)
    else:
        bm, bn, bk = _pick(m, row_c), _pick(n, col_c), _pick(k, (1024, 1408, 768, 512, 256, 128))
    nk = k // bk
    dims = {"nn": NN, "nt": NT, "tn": TN}[mode]
    a_spec = {"nn": pl.BlockSpec((bm, bk), lambda i, j, q: (i, q)),
              "nt": pl.BlockSpec((bm, bk), lambda i, j, q: (i, q)),
              "tn": pl.BlockSpec((bk, bm), lambda i, j, q: (q, i))}[mode]
    b_spec = {"nn": pl.BlockSpec((bk, bn), lambda i, j, q: (q, j)),
              "nt": pl.BlockSpec((bn, bk), lambda i, j, q: (j, q)),
              "tn": pl.BlockSpec((bk, bn), lambda i, j, q: (q, j))}[mode]
    o_spec = pl.BlockSpec((bm, bn), lambda i, j, q: (i, j))
    has_add = add is not None

    def body(*refs):
        if has_add:
            a_ref, b_ref, add_ref, o_ref, acc_ref = refs
        else:
            a_ref, b_ref, o_ref, acc_ref = refs
        q = pl.program_id(2)
        part = _bdot(a_ref[...], b_ref[...], dims)

        @pl.when(q == 0)
        def _():
            acc_ref[...] = part

        @pl.when(q > 0)
        def _():
            acc_ref[...] += part

        @pl.when(q == nk - 1)
        def _():
            r = acc_ref[...]
            if has_add:
                r = r + add_ref[...]
            o_ref[...] = r.astype(out_dtype)

    ins = [a, b] + ([add] if has_add else [])
    in_specs = [a_spec, b_spec] + ([o_spec] if has_add else [])
    return pl.pallas_call(
        body, name=name, grid=(m // bm, n // bn, nk),
        in_specs=in_specs, out_specs=o_spec,
        out_shape=jax.ShapeDtypeStruct((m, n), out_dtype),
        scratch_shapes=[pltpu.VMEM((bm, bn), F32)],
        compiler_params=_params("parallel", "parallel", "arbitrary"),
    )(*ins)


def _rmsnorm(h, gains, name):
    m, d = h.shape
    bm = _pick(m, (640, 384, 128))
    ng = len(gains)

    def body(*refs):
        h_ref, g_refs, o_refs = refs[0], refs[1:1 + ng], refs[1 + ng:]
        x = h_ref[...]
        xhat = x * lax.rsqrt(jnp.mean(x * x, axis=-1, keepdims=True) + NORM_EPS)
        for g_ref, o_ref in zip(g_refs, o_refs):
            o_ref[...] = (xhat * g_ref[...]).astype(BF16)

    row = pl.BlockSpec((bm, d), lambda i: (i, 0))
    gain = pl.BlockSpec((1, d), lambda i: (0, 0))
    return pl.pallas_call(
        body, name=name, grid=(m // bm,),
        in_specs=[row] + [gain] * ng, out_specs=[row] * ng,
        out_shape=[jax.ShapeDtypeStruct((m, d), BF16)] * ng,
        compiler_params=_params("parallel"),
    )(h, *gains)


def _rmsnorm_bwd(h, gains, dns, dres, name):
    m, d = h.shape
    bm = _pick(m, (640, 384, 128))
    ng = len(gains)
    nsteps = m // bm

    def body(*refs):
        h_ref, dres_ref = refs[0], refs[1]
        g_refs = refs[2:2 + ng]
        dn_refs = refs[2 + ng:2 + 2 * ng]
        dh_ref = refs[2 + 2 * ng]
        dg_refs = refs[3 + 2 * ng:3 + 3 * ng]
        acc_refs = refs[3 + 3 * ng:]
        i = pl.program_id(0)
        x = h_ref[...]
        r = lax.rsqrt(jnp.mean(x * x, axis=-1, keepdims=True) + NORM_EPS)
        xhat = x * r
        dh = dres_ref[...]
        for g_ref, dn_ref, dg_ref, acc_ref in zip(g_refs, dn_refs, dg_refs, acc_refs):
            dn = dn_ref[...]
            dy = dn * g_ref[...]
            dh = dh + r * (dy - xhat * jnp.mean(dy * xhat, axis=-1, keepdims=True))
            part = jnp.sum((dn * xhat).reshape(bm // 8, 8, d), axis=0)

            @pl.when(i == 0)
            def _():
                acc_ref[...] = part

            @pl.when(i > 0)
            def _():
                acc_ref[...] += part

            @pl.when(i == nsteps - 1)
            def _():
                dg_ref[...] = jnp.sum(acc_ref[...], axis=0, keepdims=True)

        dh_ref[...] = dh

    row = pl.BlockSpec((bm, d), lambda i: (i, 0))
    gain = pl.BlockSpec((1, d), lambda i: (0, 0))
    outs = pl.pallas_call(
        body, name=name, grid=(nsteps,),
        in_specs=[row, row] + [gain] * ng + [row] * ng,
        out_specs=[row] + [gain] * ng,
        out_shape=[jax.ShapeDtypeStruct((m, d), F32)] + [jax.ShapeDtypeStruct((1, d), F32)] * ng,
        scratch_shapes=[pltpu.VMEM((8, d), F32)] * ng,
        compiler_params=_params("arbitrary"),
    )(h, dres, *gains, *dns)
    return outs[0], list(outs[1:])


def _loss_head(h, gain, target):
    m, d = h.shape
    nsteps = m // LANES

    def body(h_ref, g_ref, t_ref, loss_ref, dh_ref, dg_ref, lacc, gacc):
        i = pl.program_id(0)

        @pl.when(i == 0)
        def _():
            lacc[...] = jnp.zeros_like(lacc)
            gacc[...] = jnp.zeros_like(gacc)
            dh_ref[...] = jnp.zeros_like(dh_ref)

        @pl.when(i > 0)
        def _():
            x = h_ref[...]
            r = lax.rsqrt(jnp.mean(x * x, axis=-1, keepdims=True) + NORM_EPS)
            xhat = x * r
            g = g_ref[...]
            err = xhat * g - t_ref[...]
            lacc[...] += jnp.sum((err * err).reshape(LANES // 8, 8, d), axis=0)
            dn = err * (1.0 / d)
            dy = dn * g
            dh_ref[...] = r * (dy - xhat * jnp.mean(dy * xhat, axis=-1, keepdims=True))
            gacc[...] += jnp.sum((dn * xhat).reshape(LANES // 8, 8, d), axis=0)

        @pl.when(i == nsteps - 1)
        def _():
            tot = jnp.sum(jnp.sum(lacc[...], axis=1, keepdims=True), axis=0, keepdims=True)
            loss_ref[...] = jnp.broadcast_to(tot * (0.5 / d), (1, LANES))
            dg_ref[...] = jnp.sum(gacc[...], axis=0, keepdims=True)

    row = pl.BlockSpec((LANES, d), lambda i: (i, 0))
    trow = pl.BlockSpec((LANES, d), lambda i: (jnp.maximum(i - 1, 0), 0))
    gain_spec = pl.BlockSpec((1, d), lambda i: (0, 0))
    return pl.pallas_call(
        body, name="loss_head", grid=(nsteps,),
        in_specs=[row, gain_spec, trow],
        out_specs=[pl.BlockSpec((1, LANES), lambda i: (0, 0)), row, gain_spec],
        out_shape=[jax.ShapeDtypeStruct((1, LANES), F32), jax.ShapeDtypeStruct((m, d), F32),
                   jax.ShapeDtypeStruct((1, d), F32)],
        scratch_shapes=[pltpu.VMEM((8, d), F32), pltpu.VMEM((8, d), F32)],
        compiler_params=_params("arbitrary"),
    )(h, gain, target)


def _swiglu(gu, name):
    m, f2 = gu.shape
    f = f2 // 2

    def body(gu_ref, o_ref):
        g = gu_ref[:, :f]
        u = gu_ref[:, f:]
        o_ref[...] = (g * _sigmoid(g) * u).astype(BF16)

    return pl.pallas_call(
        body, name=name, grid=(m // LANES,),
        in_specs=[pl.BlockSpec((LANES, f2), lambda i: (i, 0))],
        out_specs=pl.BlockSpec((LANES, f), lambda i: (i, 0)),
        out_shape=jax.ShapeDtypeStruct((m, f), BF16),
        compiler_params=_params("parallel"),
    )(gu)


def _swiglu_bwd(gu, da, name):
    m, f2 = gu.shape
    f = f2 // 2

    def body(gu_ref, da_ref, o_ref):
        g = gu_ref[:, :f]
        u = gu_ref[:, f:]
        da_ = da_ref[...]
        s = _sigmoid(g)
        o_ref[:, :f] = (da_ * u * s * (1.0 + g * (1.0 - s))).astype(BF16)
        o_ref[:, f:] = (da_ * g * s).astype(BF16)

    return pl.pallas_call(
        body, name=name, grid=(m // LANES,),
        in_specs=[pl.BlockSpec((LANES, f2), lambda i: (i, 0)), pl.BlockSpec((LANES, f), lambda i: (i, 0))],
        out_specs=pl.BlockSpec((LANES, f2), lambda i: (i, 0)),
        out_shape=jax.ShapeDtypeStruct((m, f2), BF16),
        compiler_params=_params("parallel"),
    )(gu, da)


def _conv_taps(xa, w_ref):
    acc = xa * w_ref[CONV_WIDTH - 1:CONV_WIDTH, :]
    for i in range(CONV_WIDTH - 1):
        acc = acc + pltpu.roll(xa, CONV_WIDTH - 1 - i, 0) * w_ref[i:i + 1, :]
    return acc[8:]


def _gdn_conv(proj, conv_w, n_heads):
    m, w3 = proj.shape
    wd = w3 // 3
    nblk = m // LANES

    def body(x_ref, w_ref, o_ref):
        c = pl.program_id(0)
        is_qk = c < 2 * n_heads

        def finish(cv):
            s = cv * _sigmoid(cv)
            nrm = s * lax.rsqrt(jnp.sum(s * s, axis=-1, keepdims=True) + NORM_EPS)
            return jnp.where(is_qk, nrm, s)

        x0 = jnp.concatenate([jnp.zeros((8, LANES), F32), x_ref[pl.ds(0, LANES), :]], axis=0)
        o_ref[pl.ds(0, LANES), :] = finish(_conv_taps(x0, w_ref))

        def step(b, carry):
            r0 = pl.multiple_of(b * LANES, LANES)
            xa = x_ref[pl.ds(r0 - 8, LANES + 8), :]
            o_ref[pl.ds(r0, LANES), :] = finish(_conv_taps(xa, w_ref))
            return carry

        lax.fori_loop(1, nblk, step, 0)

    return pl.pallas_call(
        body, name="gdn_conv", grid=(w3 // LANES,),
        in_specs=[pl.BlockSpec((m, LANES), lambda c: (0, c)), pl.BlockSpec((CONV_WIDTH, LANES), lambda c: (0, c))],
        out_specs=pl.BlockSpec((None, m, LANES), lambda c: (c // n_heads, 0, c % n_heads)),
        out_shape=jax.ShapeDtypeStruct((3, m, wd), F32),
        compiler_params=_params("parallel"),
    )(proj, conv_w)


def _gdn_conv_bwd(proj, conv_w, dact, n_heads):
    m, w3 = proj.shape
    nblk = m // LANES

    def body(x_ref, w_ref, dy_ref, dx_ref, dw_ref, dc_ref):
        c = pl.program_id(0)
        is_qk = c < 2 * n_heads
        dc_ref[pl.ds(m, 8), :] = jnp.zeros((8, LANES), F32)

        def pre_act(xa):
            cv = _conv_taps(xa, w_ref)
            sg = _sigmoid(cv)
            s = cv * sg
            return cv, sg, s

        def dconv_in(b, xa, dwacc):
            r0 = pl.multiple_of(b * LANES, LANES)
            cv, sg, s = pre_act(xa)
            dy = dy_ref[pl.ds(r0, LANES), :]
            rn = lax.rsqrt(jnp.sum(s * s, axis=-1, keepdims=True) + NORM_EPS)
            yn = s * rn
            ds_qk = rn * (dy - yn * jnp.sum(dy * yn, axis=-1, keepdims=True))
            ds = jnp.where(is_qk, ds_qk, dy)
            dcv = ds * sg * (1.0 + cv * (1.0 - sg))
            dc_ref[pl.ds(r0, LANES), :] = dcv
            new = []
            for i in range(CONV_WIDTH):
                xs = xa[8:] if i == CONV_WIDTH - 1 else pltpu.roll(xa, CONV_WIDTH - 1 - i, 0)[8:]
                new.append(dwacc[i] + jnp.sum((dcv * xs).reshape(LANES // 8, 8, LANES), axis=0))
            return tuple(new)

        x0 = jnp.concatenate([jnp.zeros((8, LANES), F32), x_ref[pl.ds(0, LANES), :]], axis=0)
        zero = jnp.zeros((8, LANES), F32)
        dwacc = dconv_in(0, x0, (zero,) * CONV_WIDTH)

        def step1(b, acc):
            r0 = pl.multiple_of(b * LANES, LANES)
            return dconv_in(b, x_ref[pl.ds(r0 - 8, LANES + 8), :], acc)

        dwacc = lax.fori_loop(1, nblk, step1, dwacc)
        for i in range(CONV_WIDTH):
            dw_ref[i:i + 1, :] = jnp.sum(dwacc[i], axis=0, keepdims=True)

        def step2(b, carry):
            r0 = pl.multiple_of(b * LANES, LANES)
            da = dc_ref[pl.ds(r0, LANES + 8), :]
            acc = da * w_ref[CONV_WIDTH - 1:CONV_WIDTH, :]
            for i in range(CONV_WIDTH - 1):
                sh = CONV_WIDTH - 1 - i
                acc = acc + pltpu.roll(da, LANES + 8 - sh, 0) * w_ref[i:i + 1, :]
            dx_ref[pl.ds(r0, LANES), :] = acc[:LANES]
            return carry

        lax.fori_loop(0, nblk, step2, 0)

    return pl.pallas_call(
        body, name="gdn_conv_bwd", grid=(w3 // LANES,),
        in_specs=[pl.BlockSpec((m, LANES), lambda c: (0, c)),
                  pl.BlockSpec((CONV_WIDTH, LANES), lambda c: (0, c)),
                  pl.BlockSpec((None, m, LANES), lambda c: (c // n_heads, 0, c % n_heads))],
        out_specs=[pl.BlockSpec((m, LANES), lambda c: (0, c)), pl.BlockSpec((CONV_WIDTH, LANES), lambda c: (0, c))],
        out_shape=[jax.ShapeDtypeStruct((m, w3), F32), jax.ShapeDtypeStruct((CONV_WIDTH, w3), F32)],
        scratch_shapes=[pltpu.VMEM((m + 8, LANES), F32)],
        compiler_params=_params("parallel"),
    )(proj, conv_w, dact)


def _softplus(x):
    return jnp.maximum(x, 0.0) + jnp.log(1.0 + jnp.exp(-jnp.abs(x)))


def _gdn_gates(ab, a_log, dt_bias, n_heads):
    m = ab.shape[0]
    bm = _pick(m, (640, 384, 128))

    def body(ab_ref, al_ref, dt_ref, g_ref, b_ref):
        i = pl.program_id(0)
        x = ab_ref[...]
        live = (_iota2((bm, LANES), 0) + i * bm) >= PAD_ROWS
        g = jnp.where(live, -jnp.exp(al_ref[...]) * _softplus(x + dt_ref[...]), 0.0)
        beta = jnp.where(live, _sigmoid(x), 0.0)
        for h in range(n_heads):
            g_ref[:, h * LANES:(h + 1) * LANES] = jnp.broadcast_to(g[:, h:h + 1], (bm, LANES))
            b_ref[:, h * LANES:(h + 1) * LANES] = jnp.broadcast_to(
                beta[:, n_heads + h:n_heads + h + 1], (bm, LANES))

    wide = pl.BlockSpec((bm, n_heads * LANES), lambda i: (i, 0))
    return pl.pallas_call(
        body, name="gdn_gates", grid=(m // bm,),
        in_specs=[pl.BlockSpec((bm, LANES), lambda i: (i, 0))] + [pl.BlockSpec((1, LANES), lambda i: (0, 0))] * 2,
        out_specs=[wide, wide],
        out_shape=[jax.ShapeDtypeStruct((m, n_heads * LANES), F32)] * 2,
        compiler_params=_params("parallel"),
    )(ab, a_log, dt_bias)


def _gdn_gates_bwd(ab, a_log, dt_bias, dg_b, dbeta_b, n_heads):
    m = ab.shape[0]
    bm = _pick(m, (640, 384, 128))
    nsteps = m // bm

    def body(ab_ref, al_ref, dt_ref, dg_ref, db_ref, dab_ref, dal_ref, ddt_ref, acc_al, acc_dt):
        i = pl.program_id(0)
        x = ab_ref[...]
        lane = _iota2((bm, LANES), 1)
        live = (_iota2((bm, LANES), 0) + i * bm) >= PAD_ROWS
        dg = jnp.zeros((bm, LANES), F32)
        dbeta = jnp.zeros((bm, LANES), F32)
        for h in range(n_heads):
            dg = jnp.where(lane == h, dg_ref[:, h * LANES:(h + 1) * LANES], dg)
            dbeta = jnp.where(lane == n_heads + h, db_ref[:, h * LANES:(h + 1) * LANES], dbeta)
        dg = jnp.where(live, dg, 0.0)
        dbeta = jnp.where(live, dbeta, 0.0)
        nea = -jnp.exp(al_ref[...])
        pre = x + dt_ref[...]
        d_pre = dg * nea * _sigmoid(pre)
        beta = _sigmoid(x)
        dab_ref[...] = d_pre + dbeta * beta * (1.0 - beta)
        p_al = jnp.sum((dg * nea * _softplus(pre)).reshape(bm // 8, 8, LANES), axis=0)
        p_dt = jnp.sum(d_pre.reshape(bm // 8, 8, LANES), axis=0)

        @pl.when(i == 0)
        def _():
            acc_al[...] = p_al
            acc_dt[...] = p_dt

        @pl.when(i > 0)
        def _():
            acc_al[...] += p_al
            acc_dt[...] += p_dt

        @pl.when(i == nsteps - 1)
        def _():
            dal_ref[...] = jnp.sum(acc_al[...], axis=0, keepdims=True)
            ddt_ref[...] = jnp.sum(acc_dt[...], axis=0, keepdims=True)

    wide = pl.BlockSpec((bm, n_heads * LANES), lambda i: (i, 0))
    one = pl.BlockSpec((1, LANES), lambda i: (0, 0))
    nar = pl.BlockSpec((bm, LANES), lambda i: (i, 0))
    return pl.pallas_call(
        body, name="gdn_gates_bwd", grid=(nsteps,),
        in_specs=[nar, one, one, wide, wide],
        out_specs=[nar, one, one],
        out_shape=[jax.ShapeDtypeStruct((m, LANES), F32), jax.ShapeDtypeStruct((1, LANES), F32),
                   jax.ShapeDtypeStruct((1, LANES), F32)],
        scratch_shapes=[pltpu.VMEM((8, LANES), F32)] * 2,
        compiler_params=_params("arbitrary"),
    )(ab, a_log, dt_bias, dg_b, dbeta_b)


def _chunk_masks():
    c = GDN_CHUNK
    ri, ci = _iota2((c, c), 0), _iota2((c, c), 1)
    return ri >= ci, ri > ci, ri == ci


def _chunk_decay(gbs):
    c = GDN_CHUNK
    incl, _, _ = _chunk_masks()
    inclf = incl.astype(F32)
    ones = jnp.ones((c, LANES), F32)
    g_cums = [_hdot(inclf, gb) for gb in gbs]
    g_rows = [_hdot(ones, g_cum, NT) * (1.0 / LANES) for g_cum in g_cums]
    dmats = [jnp.where(incl, jnp.exp(jnp.where(incl, g_cum[:, :c] - g_row, 0.0)), 0.0)
             for g_cum, g_row in zip(g_cums, g_rows)]
    return g_cums, dmats


def _unit_lower_inverse(mats):
    _, _, eye = _chunk_masks()
    xs = [eye.astype(F32) - a for a in mats]
    ps = list(mats)
    for _ in range(int(math.log2(GDN_CHUNK)) - 1):
        ps = [_hdot(p, p) for p in ps]
        xs = [x + _hdot(x, p) for x, p in zip(xs, ps)]
    return xs


def _gdn_prep(qkv, g_b, beta_b, n_heads):
    _, m, wd = qkv.shape
    c = GDN_CHUNK
    nc = m // c
    cb = _pick(nc, (10, 5, 3, 2))
    rows = cb * c
    qscale = GDN_HEAD_DIM ** -0.5

    def body(qkv_ref, g_ref, b_ref, ub_ref, wc_ref, qd_ref, kd_ref, gc_ref, p_ref, t_ref):
        _, strict, _ = _chunk_masks()
        sls = [pl.ds(s * c, c) for s in range(cb)]
        g_cums, dmats = _chunk_decay([g_ref[sl, :] for sl in sls])
        kks = [_bdot(qkv_ref[1, sl, :], qkv_ref[1, sl, :], NT) for sl in sls]
        ts = _unit_lower_inverse([jnp.where(strict, b_ref[sl, :][:, :c] * dmat * kk, 0.0)
                                  for sl, dmat, kk in zip(sls, dmats, kks)])
        for s, (sl, g_cum, dmat, t) in enumerate(zip(sls, g_cums, dmats, ts)):
            q = qkv_ref[0, sl, :] * qscale
            k = qkv_ref[1, sl, :]
            bb = b_ref[sl, :]
            gam = jnp.exp(g_cum)
            ub_ref[sl, :] = _hdot(t, bb * qkv_ref[2, sl, :])
            wc_ref[sl, :] = _hdot(t, bb * gam * k)
            p_ref[s] = _bdot(q, k, NT) * dmat
            qd_ref[sl, :] = q * gam
            kd_ref[sl, :] = k * jnp.exp(g_cum[c - 1:c, :] - g_cum)
            gc_ref[sl, :] = g_cum
            t_ref[s] = t

    blk = pl.BlockSpec((rows, LANES), lambda h, n: (n, h))
    sq = pl.BlockSpec((None, cb, c, c), lambda h, n: (h, n, 0, 0))
    big = jax.ShapeDtypeStruct((m, wd), F32)
    small = jax.ShapeDtypeStruct((n_heads, nc, c, c), F32)
    return pl.pallas_call(
        body, name="gdn_prep", grid=(n_heads, nc // cb),
        in_specs=[pl.BlockSpec((3, rows, LANES), lambda h, n: (0, n, h)), blk, blk],
        out_specs=[blk] * 5 + [sq, sq],
        out_shape=[big] * 5 + [small, small],
        compiler_params=_params("parallel", "parallel"),
    )(qkv, g_b, beta_b)


def _gdn_prep_bwd(qkv, g_b, beta_b, ub, wc, tinv, dub, dwc, dqd, dkd, dp, dgl, n_heads):
    _, m, wd = qkv.shape
    c = GDN_CHUNK
    nc = m // c
    cb = _pick(nc, (5, 3, 2))
    rows = cb * c
    qscale = GDN_HEAD_DIM ** -0.5

    def body(qkv_ref, g_ref, b_ref, ub_ref, wc_ref, t_ref, dub_ref, dwc_ref, dqd_ref, dkd_ref, dp_ref, dgl_ref,
             dqkv_ref, dg_ref, db_ref):
        incl, strict, _ = _chunk_masks()
        ones_c = jnp.ones((c, LANES), F32)
        ones_l = jnp.ones((LANES, LANES), F32)
        last_row = _iota2((c, LANES), 0) == c - 1

        def chunk(s):
            sl = pl.ds(s * c, c)
            q = qkv_ref[0, sl, :] * qscale
            k = qkv_ref[1, sl, :]
            v = qkv_ref[2, sl, :]
            bb = b_ref[sl, :]
            bcol = bb[:, :c]
            t = t_ref[s]
            g_cum = _hdot(incl.astype(F32), g_ref[sl, :])
            kk = _bdot(k, k, NT)
            qk = _bdot(q, k, NT)
            drhs_u = _hdot(t, dub_ref[sl, :], TN)
            drhs_w = _hdot(t, dwc_ref[sl, :], TN)
            yield
            g_row = _hdot(ones_c, g_cum, NT) * (1.0 / LANES)
            da = -jnp.where(strict, _bdot(drhs_u, ub_ref[sl, :], NT) + _bdot(drhs_w, wc_ref[sl, :], NT), 0.0)
            dbeta = _hdot(drhs_u * v, ones_l)
            tk = _hdot(drhs_w * k, ones_l)
            dqd = dqd_ref[sl, :]
            dgam = _hdot(dqd * q, ones_l)
            yield
            dmat = jnp.where(incl, jnp.exp(jnp.where(incl, g_cum[:, :c] - g_row, 0.0)), 0.0)
            gam = jnp.exp(g_cum)
            edec = jnp.exp(g_cum[c - 1:c, :] - g_cum)
            dkd = dkd_ref[sl, :]
            skd = _hdot(dkd * k * edec, ones_l)
            dbeta = dbeta + _hdot(da * dmat * kk, ones_c)
            dkk = da * bcol * dmat
            dpm = jnp.where(incl, dp_ref[s], 0.0)
            dpd = dpm * dmat
            dk = _bdot(dkk, k) + _bdot(dkk, k, TN) + _bdot(dpd, q, TN)
            dq = _bdot(dpd, k)
            e = (da * bcol * kk + dpm * qk) * dmat
            yield
            dk = dk + bb * gam * drhs_w + dkd * edec
            dq = dq + dqd * gam
            dbeta = dbeta + tk * gam
            dgam = dgam + bb * tk
            dgc = dgam * gam - skd + _hdot(e, ones_c) - _hdot(e, ones_c, TN)
            tot = _hdot(jnp.ones((c, c), F32), skd)
            yield
            dgc = dgc + jnp.where(last_row, tot + dgl_ref[s, 0:1, :], 0.0)
            dg_ref[sl, :] = _hdot((_iota2((c, c), 0) <= _iota2((c, c), 1)).astype(F32), dgc)
            db_ref[sl, :] = dbeta
            dqkv_ref[0, sl, :] = dq * qscale
            dqkv_ref[1, sl, :] = dk
            dqkv_ref[2, sl, :] = bb * drhs_u

        _lockstep([chunk(s) for s in range(cb)])

    blk = pl.BlockSpec((rows, LANES), lambda h, n: (n, h))
    blk3 = pl.BlockSpec((3, rows, LANES), lambda h, n: (0, n, h))
    sq = pl.BlockSpec((None, cb, c, c), lambda h, n: (h, n, 0, 0))
    sgl = pl.BlockSpec((None, cb, 8, LANES), lambda h, n: (h, n, 0, 0))
    big = jax.ShapeDtypeStruct((m, wd), F32)
    return pl.pallas_call(
        body, name="gdn_prep_bwd", grid=(n_heads, nc // cb),
        in_specs=[blk3, blk, blk, blk, blk, sq, blk, blk, blk, blk, sq, sgl],
        out_specs=[blk3, blk, blk],
        out_shape=[jax.ShapeDtypeStruct((3, m, wd), F32), big, big],
        compiler_params=_params("parallel", "parallel"),
    )(qkv, g_b, beta_b, ub, wc, tinv, dub, dwc, dqd, dkd, dp, dgl)


def _gdn_scan(ub, wc, qd, kd, gc, p, n_heads):
    m, wd = ub.shape
    c = GDN_CHUNK
    nc = m // c
    dh = GDN_HEAD_DIM

    def body(ub_ref, wc_ref, qd_ref, kd_ref, gc_ref, p_ref, o_ref, ssave_ref, s_ref):
        n = pl.program_id(0)

        @pl.when(n == 0)
        def _():
            s_ref[...] = jnp.zeros_like(s_ref)

        def head(h):
            sl = slice(h * LANES, (h + 1) * LANES)
            s = s_ref[h]
            ssave_ref[h] = s
            ws = _bdot(wc_ref[:, sl], s)
            qs = _bdot(qd_ref[:, sl], s)
            yield
            u = ub_ref[:, sl] - ws
            pu = _bdot(p_ref[h], u)
            ku = _bdot(kd_ref[:, sl], u, TN)
            yield
            o_ref[:, sl] = qs + pu
            s_ref[h] = s * jnp.exp(gc_ref[c - 1:c, sl]) + ku

        _lockstep([head(h) for h in range(n_heads)])

    blk = pl.BlockSpec((c, wd), lambda n: (n, 0))
    return pl.pallas_call(
        body, name="gdn_scan", grid=(nc,),
        in_specs=[blk] * 5 + [pl.BlockSpec((n_heads, None, c, c), lambda n: (0, n, 0, 0))],
        out_specs=[blk, pl.BlockSpec((n_heads, None, dh, dh), lambda n: (0, n, 0, 0))],
        out_shape=[jax.ShapeDtypeStruct((m, wd), F32), jax.ShapeDtypeStruct((n_heads, nc, dh, dh), F32)],
        scratch_shapes=[pltpu.VMEM((n_heads, dh, dh), F32)],
        compiler_params=_params("arbitrary"),
    )(ub, wc, qd, kd, gc, p)


def _gdn_scan_bwd(do, ssave, ub, wc, qd, kd, gc, p, n_heads):
    m, wd = ub.shape
    c = GDN_CHUNK
    nc = m // c
    dh = GDN_HEAD_DIM

    def body(do_ref, ss_ref, ub_ref, wc_ref, qd_ref, kd_ref, gc_ref, p_ref,
             dub_ref, dwc_ref, dqd_ref, dkd_ref, dp_ref, dgl_ref, ds_ref):
        n = pl.program_id(0)

        @pl.when(n == 0)
        def _():
            ds_ref[...] = jnp.zeros_like(ds_ref)

        def head(h):
            sl = slice(h * LANES, (h + 1) * LANES)
            ds_next = ds_ref[h]
            s = ss_ref[h]
            do_ = do_ref[:, sl]
            wcv = wc_ref[:, sl]
            kdv = kd_ref[:, sl]
            ws = _bdot(wcv, s)
            du = _bdot(p_ref[h], do_, TN) + _bdot(kdv, ds_next)
            dqd_ref[:, sl] = _bdot(do_, s, NT)
            col = _hdot(jnp.ones((8, dh), F32), s * ds_next)
            qdo = _bdot(qd_ref[:, sl], do_, TN)
            yield
            u = ub_ref[:, sl] - ws
            egl = jnp.exp(gc_ref[c - 1:c, sl])
            dub_ref[:, sl] = du
            dwc_ref[:, sl] = -_bdot(du, s, NT)
            dp_ref[h] = _bdot(do_, u, NT)
            dkd_ref[:, sl] = _bdot(u, ds_next, NT)
            dgl_ref[h] = egl * _hdot(col, jnp.ones((LANES, LANES), F32))
            ds_ref[h] = qdo + egl * ds_next - _bdot(wcv, du, TN)

        _lockstep([head(h) for h in range(n_heads)])

    blk = pl.BlockSpec((c, wd), lambda n: (nc - 1 - n, 0))
    sq = pl.BlockSpec((n_heads, None, c, c), lambda n: (0, nc - 1 - n, 0, 0))
    big = jax.ShapeDtypeStruct((m, wd), F32)
    return pl.pallas_call(
        body, name="gdn_scan_bwd", grid=(nc,),
        in_specs=[blk, pl.BlockSpec((n_heads, None, dh, dh), lambda n: (0, nc - 1 - n, 0, 0))] + [blk] * 5 + [sq],
        out_specs=[blk] * 4 + [sq, pl.BlockSpec((n_heads, None, 8, LANES), lambda n: (0, nc - 1 - n, 0, 0))],
        out_shape=[big] * 4 + [jax.ShapeDtypeStruct((n_heads, nc, c, c), F32),
                               jax.ShapeDtypeStruct((n_heads, nc, 8, LANES), F32)],
        scratch_shapes=[pltpu.VMEM((n_heads, dh, dh), F32)],
        compiler_params=_params("arbitrary"),
    )(do, ssave, ub, wc, qd, kd, gc, p)


def _gdn_outnorm(o, gate, gain, n_heads):
    m, wd = o.shape
    bm = _pick(m, (640, 384, 128))

    def body(o_ref, gt_ref, gn_ref, y_ref):
        gn = gn_ref[...]
        for h in range(n_heads):
            sl = slice(h * LANES, (h + 1) * LANES)
            x = o_ref[:, sl]
            gt = gt_ref[:, sl]
            r = lax.rsqrt(jnp.mean(x * x, axis=-1, keepdims=True) + NORM_EPS)
            y_ref[:, sl] = (x * r * gn * gt * _sigmoid(gt)).astype(BF16)

    row = pl.BlockSpec((bm, wd), lambda i: (i, 0))
    return pl.pallas_call(
        body, name="gdn_outnorm", grid=(m // bm,),
        in_specs=[row, row, pl.BlockSpec((1, LANES), lambda i: (0, 0))], out_specs=row,
        out_shape=jax.ShapeDtypeStruct((m, wd), BF16),
        compiler_params=_params("parallel"),
    )(o, gate, gain)


def _gdn_outnorm_bwd(o, gate, gain, dy, n_heads):
    m, wd = o.shape
    bm = _pick(m, (640, 384, 128))
    nsteps = m // bm

    def body(o_ref, gt_ref, gn_ref, dy_ref, do_ref, dgt_ref, dgn_ref, acc_ref):
        i = pl.program_id(0)
        gn = gn_ref[...]
        part = jnp.zeros((8, LANES), F32)
        for h in range(n_heads):
            sl = slice(h * LANES, (h + 1) * LANES)
            x = o_ref[:, sl]
            gt = gt_ref[:, sl]
            d_out = dy_ref[:, sl]
            r = lax.rsqrt(jnp.mean(x * x, axis=-1, keepdims=True) + NORM_EPS)
            xhat = x * r
            sg = _sigmoid(gt)
            dz = d_out * gt * sg
            dgt_ref[:, sl] = d_out * xhat * gn * sg * (1.0 + gt * (1.0 - sg))
            dxh = dz * gn
            do_ref[:, sl] = r * (dxh - xhat * jnp.mean(dxh * xhat, axis=-1, keepdims=True))
            part = part + jnp.sum((dz * xhat).reshape(bm // 8, 8, LANES), axis=0)

        @pl.when(i == 0)
        def _():
            acc_ref[...] = part

        @pl.when(i > 0)
        def _():
            acc_ref[...] += part

        @pl.when(i == nsteps - 1)
        def _():
            dgn_ref[...] = jnp.sum(acc_ref[...], axis=0, keepdims=True)

    row = pl.BlockSpec((bm, wd), lambda i: (i, 0))
    one = pl.BlockSpec((1, LANES), lambda i: (0, 0))
    return pl.pallas_call(
        body, name="gdn_outnorm_bwd", grid=(nsteps,),
        in_specs=[row, row, one, row], out_specs=[row, row, one],
        out_shape=[jax.ShapeDtypeStruct((m, wd), F32)] * 2 + [jax.ShapeDtypeStruct((1, LANES), F32)],
        scratch_shapes=[pltpu.VMEM((8, LANES), F32)],
        compiler_params=_params("arbitrary"),
    )(o, gate, gain, dy)


def _sb_consts(kind):
    ri, ci = _iota2((LANES, LANES), 0), _iota2((LANES, LANES), 1)
    tri = {"suffix_excl": ri > ci, "prefix_incl": ri <= ci, "prefix_excl": ri < ci}[kind]
    return jnp.concatenate([tri.astype(BF16), jnp.ones((LANES, LANES), BF16)], axis=1)


def _sb_stack(a):
    first = _iota2((LANES, LANES), 1) < SB_HEAD_DIM
    return jnp.concatenate([jnp.where(first, a, 0.0), jnp.where(first, 0.0, a)], axis=0)


def _sb_unstack(a2):
    first = _iota2((LANES, LANES), 1) < SB_HEAD_DIM
    return jnp.where(first, a2[:LANES], a2[LANES:])


def _sb_visible(qi, kb, live):
    kpos = _iota2((2 * LANES, LANES), 1) + kb * LANES
    qpos = (_iota2((2 * LANES, LANES), 0) & (LANES - 1)) + qi * LANES
    return (kpos < qpos) & (kpos >= PAD_ROWS) & live


def _sb_log_gates(z, vis):
    t = jnp.log(1.0 + jnp.exp(-jnp.abs(z)))
    lb = jnp.minimum(z, 0.0) - t
    lk = lb - z
    if vis is not None:
        lk = jnp.where(vis, lk, 0.0)
    return lb, lk


def _sb_attention(q, kv):
    m, wd = q.shape
    npair = wd // LANES
    nq = m // LANES
    scale = SB_HEAD_DIM ** -0.5

    def body(q_ref, k_ref, v_ref, o_ref, w_ref, b_ref):
        qi = pl.program_id(1)
        su = _sb_consts("suffix_excl")
        q2 = _sb_stack(q_ref[...] * scale).astype(BF16)

        def tiles(kbs, carry, live):
            run, acc = carry
            r0s = [pl.multiple_of(kb * LANES, LANES) for kb in kbs]
            zs = [lax.dot_general(q2, k_ref[pl.ds(r0, LANES), :], NT, preferred_element_type=F32) for r0 in r0s]
            parts = []
            for kb, r0, z in zip(kbs, r0s, zs):
                vis = None if live is None else _sb_visible(qi, kb, live)
                lb, lk = _sb_log_gates(z, vis)
                parts.append((kb, r0, vis, lb, lk))
            parts = [(kb, r0, vis, lb, _split_dot(lk, su)) for kb, r0, vis, lb, lk in parts]
            for kb, r0, vis, lb, cs in parts:
                w = jnp.exp(lb + cs[:, :LANES] + run)
                if vis is not None:
                    w = jnp.where(vis, w, 0.0)
                run = run + cs[:, LANES:]
                wb = w.astype(BF16)
                w_ref[kb] = wb
                b_ref[kb] = jnp.exp(lb).astype(BF16)
                acc = acc + lax.dot_general(wb, v_ref[pl.ds(r0, LANES), :], NN, preferred_element_type=F32)
            return run, acc

        def group(j, carry):
            kb = qi - 1 - SB_UNROLL * j
            return tiles([kb - u for u in range(SB_UNROLL)], carry, None)

        zero = jnp.zeros((2 * LANES, LANES), F32)
        carry = tiles([qi], (zero, zero), True)
        n_mid = jnp.maximum(qi - 1, 0)
        carry = lax.fori_loop(0, n_mid // SB_UNROLL, group, carry)
        carry = lax.fori_loop(0, n_mid % SB_UNROLL, lambda j, c: tiles([n_mid % SB_UNROLL - j], c, None), carry)

        @pl.when(qi >= 1)
        def _():
            _, acc = tiles([0], carry, True)
            o_ref[...] = _sb_unstack(acc)

        @pl.when(qi == 0)
        def _():
            o_ref[...] = _sb_unstack(carry[1])

    qspec = pl.BlockSpec((LANES, LANES), lambda hp, qi: (qi, hp))
    saved = pl.BlockSpec((None, None, nq, 2 * LANES, LANES), lambda hp, qi: (hp, qi, 0, 0, 0))
    saved_shape = jax.ShapeDtypeStruct((npair, nq, nq, 2 * LANES, LANES), BF16)
    return pl.pallas_call(
        body, name="sb_attention", grid=(npair, nq),
        in_specs=[qspec, pl.BlockSpec((m, LANES), lambda hp, qi: (0, hp)),
                  pl.BlockSpec((m, LANES), lambda hp, qi: (0, npair + hp))],
        out_specs=[qspec, saved, saved],
        out_shape=[jax.ShapeDtypeStruct((m, wd), F32), saved_shape, saved_shape],
        compiler_params=_params("parallel", "arbitrary"),
    )(q, kv, kv)


def _sb_attention_bwd(q, kv, w_all, b_all, do):
    m, wd = q.shape
    npair = wd // LANES
    nq = m // LANES
    scale = SB_HEAD_DIM ** -0.5

    def body(q_ref, k_ref, v_ref, w_ref, b_ref, do_ref, dq_ref, dk_ref, dv_ref):
        qi = pl.program_id(1)

        @pl.when(qi == 0)
        def _():
            dk_ref[...] = jnp.zeros_like(dk_ref)
            dv_ref[...] = jnp.zeros_like(dv_ref)

        px = _sb_consts("prefix_excl")
        q2 = _sb_stack(q_ref[...] * scale).astype(BF16)
        do2 = _sb_stack(do_ref[...]).astype(BF16)

        def tiles(kbs, carry, live):
            run_p, dq_acc = carry
            r0s = [pl.multiple_of(kb * LANES, LANES) for kb in kbs]
            das = [lax.dot_general(do2, v_ref[pl.ds(r0, LANES), :], NT, preferred_element_type=F32) for r0 in r0s]
            wbs = [w_ref[kb] for kb in kbs]
            for wb, r0 in zip(wbs, r0s):
                dv_ref[pl.ds(r0, LANES), :] += lax.dot_general(wb, do2, TN, preferred_element_type=F32)
            pws = [wb.astype(F32) * da for wb, da in zip(wbs, das)]
            pss = [_split_dot(pw, px) for pw in pws]
            dzs = []
            for kb, pw, ps in zip(kbs, pws, pss):
                beta = b_ref[kb].astype(F32)
                dz = pw * (1.0 - beta) - beta * (run_p + ps[:, :LANES])
                if live is not None:
                    dz = jnp.where(_sb_visible(qi, kb, live), dz, 0.0)
                run_p = run_p + ps[:, LANES:]
                dzs.append(dz.astype(BF16))
            for r0, dz in zip(r0s, dzs):
                dq_acc = dq_acc + lax.dot_general(dz, k_ref[pl.ds(r0, LANES), :], NN, preferred_element_type=F32)
                dk_ref[pl.ds(r0, LANES), :] += lax.dot_general(dz, q2, TN, preferred_element_type=F32)
            return run_p, dq_acc

        def group(j, carry):
            kb = 1 + SB_UNROLL_BWD * j
            return tiles([kb + u for u in range(SB_UNROLL_BWD)], carry, None)

        zero = jnp.zeros((2 * LANES, LANES), F32)
        carry = tiles([0], (zero, zero), True)
        n_mid = jnp.maximum(qi - 1, 0)
        n_grp = n_mid // SB_UNROLL_BWD
        carry = lax.fori_loop(0, n_grp, group, carry)
        carry = lax.fori_loop(1 + n_grp * SB_UNROLL_BWD, jnp.maximum(qi, 1), lambda kb, c: tiles([kb], c, None), carry)

        @pl.when(qi >= 1)
        def _():
            _, dq_acc = tiles([qi], carry, True)
            dq_ref[...] = _sb_unstack(dq_acc) * scale

        @pl.when(qi == 0)
        def _():
            dq_ref[...] = _sb_unstack(carry[1]) * scale

    qspec = pl.BlockSpec((LANES, LANES), lambda hp, qi: (qi, hp))
    strip = pl.BlockSpec((m, LANES), lambda hp, qi: (0, hp))
    saved = pl.BlockSpec((None, None, nq, 2 * LANES, LANES), lambda hp, qi: (hp, qi, 0, 0, 0))
    big = jax.ShapeDtypeStruct((m, wd), F32)
    return pl.pallas_call(
        body, name="sb_attention_bwd", grid=(npair, nq),
        in_specs=[qspec, strip, pl.BlockSpec((m, LANES), lambda hp, qi: (0, npair + hp)), saved, saved, qspec],
        out_specs=[qspec, strip, strip],
        out_shape=[big, big, big],
        compiler_params=_params("parallel", "arbitrary"),
    )(q, kv, kv, w_all, b_all, do)


def _mesh_pos():
    return lax.axis_index("x"), lax.axis_index("y"), lax.axis_index("c")


def _exchange(srcs, name, scatter):
    n = len(srcs)
    rows = [s.shape[-2] for s in srcs]
    offs = [sum(rows[:i]) for i in range(n)]
    total = sum(rows)

    def body(*refs):
        src_refs, out_ref = refs[:n], refs[n]
        send_sems, recv_sems, local_sems = refs[n + 1:]
        x, y, c = _mesh_pos()
        me = 4 * x + 2 * y + c

        def piece(i, d):
            return src_refs[i].at[d] if scatter else src_refs[i]

        def window(slot, i):
            return out_ref.at[slot, pl.ds(offs[i], rows[i])]

        local = [pltpu.make_async_copy(piece(i, me), window(me, i), local_sems.at[i]) for i in range(n)]
        for cp in local:
            cp.start()
        peers = []
        for k in range(1, N_DEV):
            px = 1 - x if k & 4 else x
            py = 1 - y if k & 2 else y
            pc = 1 - c if k & 1 else c
            pid = 4 * px + 2 * py + pc
            for i in range(n):
                pltpu.make_async_remote_copy(
                    src_ref=piece(i, pid), dst_ref=window(me, i),
                    send_sem=send_sems.at[k - 1], recv_sem=recv_sems.at[k - 1],
                    device_id=(px, py, pc), device_id_type=pl.DeviceIdType.MESH).start()
            peers.append(pid)
        slabs = [pltpu.make_async_remote_copy(
            src_ref=out_ref.at[me], dst_ref=out_ref.at[peers[k - 1]],
            send_sem=send_sems.at[k - 1], recv_sem=recv_sems.at[k - 1],
            device_id=(x, y, c), device_id_type=pl.DeviceIdType.MESH) for k in range(1, N_DEV)]
        for cp in slabs:
            cp.wait_recv()
        for cp in slabs:
            cp.wait_send()
        for cp in local:
            cp.wait()

    return pl.pallas_call(
        body, name=name,
        in_specs=[pl.BlockSpec(memory_space=pl.ANY)] * n, out_specs=pl.BlockSpec(memory_space=pl.ANY),
        out_shape=jax.ShapeDtypeStruct((N_DEV, total, FLAT_COLS), srcs[0].dtype),
        scratch_shapes=[pltpu.SemaphoreType.DMA((N_DEV - 1,)), pltpu.SemaphoreType.DMA((N_DEV - 1,)),
                        pltpu.SemaphoreType.DMA((n,))],
        compiler_params=pltpu.CompilerParams(has_side_effects=True),
    )(*srcs)


def _adamw(parts, w, mom, var):
    rows = w.shape[0]
    br = FLAT_ROW_BLOCK
    c1 = 1.0 - ADAM_B1 ** ADAM_STEP
    c2 = 1.0 - ADAM_B2 ** ADAM_STEP

    def body(p_ref, w_ref, m_ref, v_ref, g_ref, d_ref, nm_ref, nv_ref):
        g = p_ref[0]
        for s in range(1, N_DEV):
            g = g + p_ref[s]
        m_new = ADAM_B1 * m_ref[...] + (1.0 - ADAM_B1) * g
        v_new = ADAM_B2 * v_ref[...] + (1.0 - ADAM_B2) * (g * g)
        m_hat = m_new / c1
        v_hat = v_new / c2
        g_ref[...] = g
        d_ref[...] = -ADAM_LR * (m_hat / (jnp.sqrt(v_hat) + ADAM_EPS) + ADAM_WD * w_ref[...])
        nm_ref[...] = m_new
        nv_ref[...] = v_new

    row = pl.BlockSpec((br, FLAT_COLS), lambda i: (i, 0))
    flat = jax.ShapeDtypeStruct((rows, FLAT_COLS), F32)
    return pl.pallas_call(
        body, name="adamw", grid=(rows // br,),
        in_specs=[pl.BlockSpec((N_DEV, br, FLAT_COLS), lambda i: (0, i, 0)), row, row, row],
        out_specs=[row] * 4, out_shape=[flat] * 4,
        compiler_params=_params("parallel"),
    )(parts, w, mom, var)


SMALL_SHARDED = (("meta_tokens", 1), ("gdn_norm_g", 1), ("gdn_conv_w", 2))
BIG_SHARDED = (("gdn_w_in", 2), ("gdn_w_out", 1), ("w_kv", 1), ("sb_w_q", 1), ("sb_w_o", 1),
               ("ffn_w_gate_up", 2), ("ffn_w_down", 1))
REPLICATED = ("gdn_a_log", "gdn_dt_bias", "gdn_onorm_g", "kv_norm_g", "sb_norm_g", "ffn_norm_g", "final_norm_g")
WEIGHTS = ("meta_tokens", "gdn_norm_g", "gdn_w_in", "gdn_conv_w", "gdn_a_log", "gdn_dt_bias", "gdn_onorm_g",
           "gdn_w_out", "kv_norm_g", "w_kv", "sb_norm_g", "sb_w_q", "sb_w_o", "ffn_norm_g", "ffn_w_gate_up",
           "ffn_w_down", "final_norm_g")
F32_ROWS = 8
BF16_ROWS = 16


def _n_rows(shape, mult):
    return -(-math.prod(shape) // (mult * FLAT_COLS)) * mult


def _as_rows(a, nl, mult):
    lead = a.shape[:nl]
    size = math.prod(a.shape[nl:])
    nr = _n_rows(a.shape[nl:], mult)
    flat = a.reshape(lead + (size,))
    if nr * FLAT_COLS != size:
        flat = jnp.pad(flat, [(0, 0)] * nl + [(0, nr * FLAT_COLS - size)])
    return flat.reshape(lead + (nr, FLAT_COLS))


def _from_rows(flat, r0, shape, nl, mult):
    lead = flat.shape[:nl]
    nr = _n_rows(shape, mult)
    seg = lax.slice_in_dim(flat, r0, r0 + nr, axis=nl).reshape(lead + (nr * FLAT_COLS,))
    return lax.slice_in_dim(seg, 0, math.prod(shape), axis=nl).reshape(lead + tuple(shape)), r0 + nr


def _merge_shards(g, axis):
    t = jnp.moveaxis(g, 0, axis)
    shp = t.shape
    return t.reshape(shp[:axis] + (shp[axis] * shp[axis + 1],) + shp[axis + 2:])


def _split_shards(full, axis):
    shp = full.shape
    t = full.reshape(shp[:axis] + (N_DEV, shp[axis] // N_DEV) + shp[axis + 1:])
    return jnp.moveaxis(t, axis, 0)


def _pad_lanes(a, width=LANES):
    return jnp.pad(a, ((0, 0), (0, width - a.shape[1])))


def _local_step(x, target, w):
    d = x.shape[1]
    gh = w["gdn_a_log"].shape[1]
    gw = gh * GDN_HEAD_DIM
    w_in = w["gdn_w_in"][0]
    w_qkv, w_gate = w_in[:, :3 * gw].astype(BF16), w_in[:, 3 * gw:4 * gw].astype(BF16)
    w_ab = _pad_lanes(w_in[:, 4 * gw:]).astype(BF16)
    conv_w = w["gdn_conv_w"][0]
    a_log, dt_bias = _pad_lanes(w["gdn_a_log"]), _pad_lanes(w["gdn_dt_bias"])
    w_out = w["gdn_w_out"][0].astype(BF16)
    sbw = w["sb_w_q"].shape[2]
    sb_heads = sbw // SB_HEAD_DIM
    w_kv = w["w_kv"].astype(BF16)
    w_k, w_v = w_kv[:, :sbw], w_kv[:, sbw:]
    w_q, w_o = w["sb_w_q"][0].astype(BF16), w["sb_w_o"][0].astype(BF16)
    w_gu = [w["ffn_w_gate_up"][l].astype(BF16) for l in range(2)]
    w_dn = [w["ffn_w_down"][l].astype(BF16) for l in range(2)]
    ffn_g = [w["ffn_norm_g"][l:l + 1] for l in range(2)]
    kv_g, fin_g = w["kv_norm_g"][None], w["final_norm_g"][None]

    h0 = jnp.concatenate([jnp.zeros((PAD_ROWS, d), F32), w["meta_tokens"], x], axis=0)

    (n1,) = _rmsnorm(h0, [w["gdn_norm_g"]], "norm_gdn")
    proj = _matmul(n1, w_qkv, "nn", "mm_gdn_qkv")
    gate = _matmul(n1, w_gate, "nn", "mm_gdn_gate")
    ab = _matmul(n1, w_ab, "nn", "mm_gdn_ab")
    qkv = _gdn_conv(proj, conv_w, gh)
    g_b, beta_b = _gdn_gates(ab, a_log, dt_bias, gh)
    ub, wc, qd, kd, gc, pmat, tinv = _gdn_prep(qkv, g_b, beta_b, gh)
    o_gdn, ssave = _gdn_scan(ub, wc, qd, kd, gc, pmat, gh)
    o2 = _gdn_outnorm(o_gdn, gate, w["gdn_onorm_g"], gh)
    h1 = _matmul(o2, w_out, "nn", "mm_gdn_out", add=h0)

    def ffn_fwd(h, l):
        (n,) = _rmsnorm(h, [ffn_g[l]], f"norm_ffn{l}")
        gu = _matmul(n, w_gu[l], "nn", f"mm_ffn{l}_gu")
        act = _swiglu(gu, f"swiglu{l}")
        return n, gu, act, _matmul(act, w_dn[l], "nn", f"mm_ffn{l}_down", add=h)

    n2, gu0, act0, h2 = ffn_fwd(h1, 0)

    nkv, n3 = _rmsnorm(h2, [kv_g, w["sb_norm_g"]], "norm_kv_sb")
    kv = _matmul(nkv, w_kv, "nn", "mm_kv", out_dtype=BF16)
    q_sb = _matmul(n3, w_q, "nn", "mm_sb_q")
    o_sb, w_sb, beta_sb = _sb_attention(q_sb, kv)
    h3 = _matmul(o_sb, w_o, "nn", "mm_sb_o", add=h2)
    n4, gu1, act1, h4 = ffn_fwd(h3, 1)

    loss, dh4, d_fin = _loss_head(h4, fin_g, target)

    def ffn_bwd(dh, h, n, gu, act, l):
        d_act = _matmul(dh, w_dn[l], "nt", f"mm_ffn{l}_dact")
        dw_dn = _matmul(act, dh, "tn", f"mm_ffn{l}_dwdown")
        dgu = _swiglu_bwd(gu, d_act, f"swiglu_bwd{l}")
        dw_gu = _matmul(n, dgu, "tn", f"mm_ffn{l}_dwgu")
        dn = _matmul(dgu, w_gu[l], "nt", f"mm_ffn{l}_dn")
        dh_in, (dg,) = _rmsnorm_bwd(h, [ffn_g[l]], [dn], dh, f"norm_ffn{l}_bwd")
        return dh_in, dw_gu, dw_dn, dg

    dh3, dw_gu1, dw_dn1, dg_ffn1 = ffn_bwd(dh4, h3, n4, gu1, act1, 1)

    do_sb = _matmul(dh3, w_o, "nt", "mm_sb_do")
    dw_o = _matmul(o_sb, dh3, "tn", "mm_sb_dwo")
    dq_sb, dk_sb, dv_sb = _sb_attention_bwd(q_sb, kv, w_sb, beta_sb, do_sb)
    dw_q = _matmul(n3, dq_sb, "tn", "mm_sb_dwq")
    dn3 = _matmul(dq_sb, w_q, "nt", "mm_sb_dn")
    dw_k = _matmul(nkv, dk_sb, "tn", "mm_dwk")
    dw_v = _matmul(nkv, dv_sb, "tn", "mm_dwv")
    dnkv = _matmul(dk_sb, w_k, "nt", "mm_dnk")
    dnkv = _matmul(dv_sb, w_v, "nt", "mm_dnv", add=dnkv)
    dh2, (dg_kv, dg_sb) = _rmsnorm_bwd(h2, [kv_g, w["sb_norm_g"]], [dnkv, dn3], dh3, "norm_kv_sb_bwd")

    dh1, dw_gu0, dw_dn0, dg_ffn0 = ffn_bwd(dh2, h1, n2, gu0, act0, 0)

    do2 = _matmul(dh1, w_out, "nt", "mm_gdn_do")
    dw_out = _matmul(o2, dh1, "tn", "mm_gdn_dwout")
    do_gdn, dgate, d_onorm = _gdn_outnorm_bwd(o_gdn, gate, w["gdn_onorm_g"], do2, gh)
    dub, dwc, dqd, dkd, dpm, dgl = _gdn_scan_bwd(do_gdn, ssave, ub, wc, qd, kd, gc, pmat, gh)
    dqkv, dg_b, dbeta_b = _gdn_prep_bwd(qkv, g_b, beta_b, ub, wc, tinv, dub, dwc, dqd, dkd, dpm, dgl, gh)
    dab, d_alog, d_dtb = _gdn_gates_bwd(ab, a_log, dt_bias, dg_b, dbeta_b, gh)
    dproj, d_conv = _gdn_conv_bwd(proj, conv_w, dqkv, gh)
    dw_qkv = _matmul(n1, dproj, "tn", "mm_gdn_dwqkv")
    dw_gate = _matmul(n1, dgate, "tn", "mm_gdn_dwgate")
    dw_ab = _matmul(n1, dab, "tn", "mm_gdn_dwab")
    dn1 = _matmul(dproj, w_qkv, "nt", "mm_gdn_dn_qkv")
    dn1 = _matmul(dgate, w_gate, "nt", "mm_gdn_dn_gate", add=dn1)
    dn1 = _matmul(dab, w_ab, "nt", "mm_gdn_dn_ab", add=dn1)
    dh0, (dg_gdn,) = _rmsnorm_bwd(h0, [w["gdn_norm_g"]], [dn1], dh1, "norm_gdn_bwd")

    grads = {
        "meta_tokens": dh0[PAD_ROWS:LANES],
        "gdn_norm_g": dg_gdn,
        "gdn_w_in": jnp.concatenate([dw_qkv, dw_gate, dw_ab[:, :2 * gh]], axis=1)[None],
        "gdn_conv_w": d_conv[None],
        "gdn_a_log": d_alog[:, :gh],
        "gdn_dt_bias": d_dtb[:, :gh],
        "gdn_onorm_g": d_onorm,
        "gdn_w_out": dw_out[None],
        "kv_norm_g": dg_kv[0],
        "w_kv": jnp.concatenate([dw_k, dw_v], axis=1),
        "sb_norm_g": dg_sb,
        "sb_w_q": dw_q[None],
        "sb_w_o": dw_o[None],
        "ffn_norm_g": jnp.concatenate([dg_ffn0, dg_ffn1], axis=0),
        "ffn_w_gate_up": jnp.stack([dw_gu0, dw_gu1]),
        "ffn_w_down": jnp.stack([dw_dn0, dw_dn1]),
        "final_norm_g": d_fin[0],
    }
    return loss, dh0[LANES:], grads


def kernel(x, meta_tokens, gdn_norm_g, gdn_w_in, gdn_conv_w, gdn_a_log, gdn_dt_bias, gdn_onorm_g, gdn_w_out, kv_norm_g, w_kv, sb_norm_g, sb_w_q, sb_w_o, ffn_norm_g, ffn_w_gate_up, ffn_w_down, final_norm_g, loss_target, m_meta_tokens, m_gdn_norm_g, m_gdn_w_in, m_gdn_conv_w, m_gdn_a_log, m_gdn_dt_bias, m_gdn_onorm_g, m_gdn_w_out, m_kv_norm_g, m_w_kv, m_sb_norm_g, m_sb_w_q, m_sb_w_o, m_ffn_norm_g, m_ffn_w_gate_up, m_ffn_w_down, m_final_norm_g, v_meta_tokens, v_gdn_norm_g, v_gdn_w_in, v_gdn_conv_w, v_gdn_a_log, v_gdn_dt_bias, v_gdn_onorm_g, v_gdn_w_out, v_kv_norm_g, v_w_kv, v_sb_norm_g, v_sb_w_q, v_sb_w_o, v_ffn_norm_g, v_ffn_w_gate_up, v_ffn_w_down, v_final_norm_g):
    given = dict(locals())
    small = [n for n, _ in SMALL_SHARDED]
    big = [n for n, _ in BIG_SHARDED]

    g_small = _exchange([jnp.concatenate([_as_rows(given[n], 0, F32_ROWS) for n in small], axis=0)],
                        "all_gather_small", scatter=False)
    g_big = _exchange([_as_rows(given[n].astype(BF16), 0, BF16_ROWS) for n in big],
                      "all_gather_weights", scatter=False)
    full = {n: given[n] for n in REPLICATED}
    for table, gathered, mult in ((SMALL_SHARDED, g_small, F32_ROWS), (BIG_SHARDED, g_big, BF16_ROWS)):
        r = 0
        for n, ax in table:
            shards, r = _from_rows(gathered, r, given[n].shape, 1, mult)
            full[n] = _merge_shards(shards, ax)

    loss_b, grad_x, grads = _local_step(x[0], loss_target[0], full)

    order = small + list(REPLICATED) + big
    used = sum(_n_rows(given[n].shape, F32_ROWS) for n in order)
    filler = (-used) % FLAT_ROW_BLOCK
    small_send = [_as_rows(_split_shards(grads[n], ax), 1, F32_ROWS) for n, ax in SMALL_SHARDED]
    small_send += [_as_rows(jnp.broadcast_to(grads[n][None], (N_DEV,) + grads[n].shape), 1, F32_ROWS)
                   for n in REPLICATED]
    send = [jnp.concatenate(small_send, axis=1)]
    send += [_as_rows(_split_shards(grads[n], ax), 1, F32_ROWS) for n, ax in BIG_SHARDED]
    if filler:
        send.append(jnp.zeros((N_DEV, filler, FLAT_COLS), F32))
    recv = _exchange(send, "scatter_gradients", scatter=True)

    def flat(prefix):
        segs = [_as_rows(given[prefix + n], 0, F32_ROWS) for n in order]
        if filler:
            segs.append(jnp.zeros((filler, FLAT_COLS), F32))
        return jnp.concatenate(segs, axis=0)

    g_f, d_f, m_f, v_f = _adamw(recv, flat(""), flat("m_"), flat("v_"))
    outs = {}
    for tag, f in (("grad", g_f), ("delta", d_f), ("new_m", m_f), ("new_v", v_f)):
        outs[tag], r = {}, 0
        for n in order:
            outs[tag][n], r = _from_rows(f, r, given[n].shape, 0, F32_ROWS)

    loss = lax.psum(loss_b[0, 0], ("x", "y", "c"))
    result = [loss, grad_x[None]]
    for tag in ("grad", "delta", "new_m", "new_v"):
        result += [outs[tag][n] for n in WEIGHTS]
    return tuple(result)
```

```python
import functools
import math

import jax
import jax.numpy as jnp
from jax import lax
from jax.experimental import pallas as pl
from jax.experimental.pallas import tpu as pltpu

F32 = jnp.float32
BF16 = jnp.bfloat16
HIGHEST = lax.Precision.HIGHEST

N_DEV = 8
LANES = 128
N_META = 16
PAD_ROWS = LANES - N_META
GDN_CHUNK = 64
GDN_HEAD_DIM = 128
SB_HEAD_DIM = 64
SB_UNROLL = 8
SB_UNROLL_BWD = 4
CONV_WIDTH = 4
NORM_EPS = 1e-6
FLAT_COLS = 1024
FLAT_ROW_BLOCK = 64
VMEM_LIMIT = 56 * 1024 * 1024

ADAM_LR = 0.001
ADAM_B1 = 0.9
ADAM_B2 = 0.999
ADAM_EPS = 1e-08
ADAM_WD = 0.01
ADAM_STEP = 10

NN = (((1,), (0,)), ((), ()))
NT = (((1,), (1,)), ((), ()))
TN = (((0,), (0,)), ((), ()))


def _params(*sem):
    return pltpu.CompilerParams(dimension_semantics=sem, vmem_limit_bytes=VMEM_LIMIT)


def _pick(n, cands):
    for c in cands:
        if n % c == 0:
            return c
    return n


def _bdot(a, b, dims=NN):
    return lax.dot_general(a.astype(BF16), b.astype(BF16), dims, preferred_element_type=F32)


def _hdot(a, b, dims=NN):
    return lax.dot_general(a, b, dims, preferred_element_type=F32, precision=HIGHEST)


def _split_dot(a, m, parts=2):
    out = None
    for _ in range(parts):
        piece = a.astype(BF16)
        a = a - piece.astype(F32)
        term = lax.dot_general(piece, m, NN, preferred_element_type=F32)
        out = term if out is None else out + term
    return out


def _lockstep(gens):
    while gens:
        alive = []
        for g in gens:
            try:
                next(g)
                alive.append(g)
            except StopIteration:
                pass
        gens = alive


def _sigmoid(x):
    return 1.0 / (1.0 + jnp.exp(-x))


def _iota2(shape, axis):
    return lax.broadcasted_iota(jnp.int32, shape, axis)


def _matmul(a, b, mode, name, out_dtype=None, add=None):
    if out_dtype is None:
        out_dtype = BF16 if mode == "tn" else F32
    if mode == "nn":
        (m, k), n = a.shape, b.shape[1]
    elif mode == "nt":
        (m, k), n = a.shape, b.shape[0]
    else:
        (k, m), n = a.shape, b.shape[1]
    row_c = (640, 512, 384, 256, 128)
    col_c = (1024, 1408, 768, 512, 384, 256, 128)
    if mode == "tn":
        bm, bn, bk = _pick(m, col_c), _pick(n, col_c), _pick(k, row_c)
    else:
        bm, bn, bk = _pick(m, row_c), _pick(n, col_c), _pick(k, (1024, 1408, 768, 512, 256, 128))
    nk = k // bk
    dims = {"nn": NN, "nt": NT, "tn": TN}[mode]
    a_spec = {"nn": pl.BlockSpec((bm, bk), lambda i, j, q: (i, q)),
              "nt": pl.BlockSpec((bm, bk), lambda i, j, q: (i, q)),
              "tn": pl.BlockSpec((bk, bm), lambda i, j, q: (q, i))}[mode]
    b_spec = {"nn": pl.BlockSpec((bk, bn), lambda i, j, q: (q, j)),
              "nt": pl.BlockSpec((bn, bk), lambda i, j, q: (j, q)),
              "tn": pl.BlockSpec((bk, bn), lambda i, j, q: (q, j))}[mode]
    o_spec = pl.BlockSpec((bm, bn), lambda i, j, q: (i, j))
    has_add = add is not None

    def body(*refs):
        if has_add:
            a_ref, b_ref, add_ref, o_ref, acc_ref = refs
        else:
            a_ref, b_ref, o_ref, acc_ref = refs
        q = pl.program_id(2)
        part = _bdot(a_ref[...], b_ref[...], dims)

        @pl.when(q == 0)
        def _():
            acc_ref[...] = part

        @pl.when(q > 0)
        def _():
            acc_ref[...] += part

        @pl.when(q == nk - 1)
        def _():
            r = acc_ref[...]
            if has_add:
                r = r + add_ref[...]
            o_ref[...] = r.astype(out_dtype)

    ins = [a, b] + ([add] if has_add else [])
    in_specs = [a_spec, b_spec] + ([o_spec] if has_add else [])
    return pl.pallas_call(
        body, name=name, grid=(m // bm, n // bn, nk),
        in_specs=in_specs, out_specs=o_spec,
        out_shape=jax.ShapeDtypeStruct((m, n), out_dtype),
        scratch_shapes=[pltpu.VMEM((bm, bn), F32)],
        compiler_params=_params("parallel", "parallel", "arbitrary"),
    )(*ins)


def _rmsnorm(h, gains, name):
    m, d = h.shape
    bm = _pick(m, (640, 384, 128))
    ng = len(gains)

    def body(*refs):
        h_ref, g_refs, o_refs = refs[0], refs[1:1 + ng], refs[1 + ng:]
        x = h_ref[...]
        xhat = x * lax.rsqrt(jnp.mean(x * x, axis=-1, keepdims=True) + NORM_EPS)
        for g_ref, o_ref in zip(g_refs, o_refs):
            o_ref[...] = (xhat * g_ref[...]).astype(BF16)

    row = pl.BlockSpec((bm, d), lambda i: (i, 0))
    gain = pl.BlockSpec((1, d), lambda i: (0, 0))
    return pl.pallas_call(
        body, name=name, grid=(m // bm,),
        in_specs=[row] + [gain] * ng, out_specs=[row] * ng,
        out_shape=[jax.ShapeDtypeStruct((m, d), BF16)] * ng,
        compiler_params=_params("parallel"),
    )(h, *gains)


def _rmsnorm_bwd(h, gains, dns, dres, name):
    m, d = h.shape
    bm = _pick(m, (640, 384, 128))
    ng = len(gains)
    nsteps = m // bm

    def body(*refs):
        h_ref, dres_ref = refs[0], refs[1]
        g_refs = refs[2:2 + ng]
        dn_refs = refs[2 + ng:2 + 2 * ng]
        dh_ref = refs[2 + 2 * ng]
        dg_refs = refs[3 + 2 * ng:3 + 3 * ng]
        acc_refs = refs[3 + 3 * ng:]
        i = pl.program_id(0)
        x = h_ref[...]
        r = lax.rsqrt(jnp.mean(x * x, axis=-1, keepdims=True) + NORM_EPS)
        xhat = x * r
        dh = dres_ref[...]
        for g_ref, dn_ref, dg_ref, acc_ref in zip(g_refs, dn_refs, dg_refs, acc_refs):
            dn = dn_ref[...]
            dy = dn * g_ref[...]
            dh = dh + r * (dy - xhat * jnp.mean(dy * xhat, axis=-1, keepdims=True))
            part = jnp.sum((dn * xhat).reshape(bm // 8, 8, d), axis=0)

            @pl.when(i == 0)
            def _():
                acc_ref[...] = part

            @pl.when(i > 0)
            def _():
                acc_ref[...] += part

            @pl.when(i == nsteps - 1)
            def _():
                dg_ref[...] = jnp.sum(acc_ref[...], axis=0, keepdims=True)

        dh_ref[...] = dh

    row = pl.BlockSpec((bm, d), lambda i: (i, 0))
    gain = pl.BlockSpec((1, d), lambda i: (0, 0))
    outs = pl.pallas_call(
        body, name=name, grid=(nsteps,),
        in_specs=[row, row] + [gain] * ng + [row] * ng,
        out_specs=[row] + [gain] * ng,
        out_shape=[jax.ShapeDtypeStruct((m, d), F32)] + [jax.ShapeDtypeStruct((1, d), F32)] * ng,
        scratch_shapes=[pltpu.VMEM((8, d), F32)] * ng,
        compiler_params=_params("arbitrary"),
    )(h, dres, *gains, *dns)
    return outs[0], list(outs[1:])


def _loss_head(h, gain, target):
    m, d = h.shape
    nsteps = m // LANES

    def body(h_ref, g_ref, t_ref, loss_ref, dh_ref, dg_ref, lacc, gacc):
        i = pl.program_id(0)

        @pl.when(i == 0)
        def _():
            lacc[...] = jnp.zeros_like(lacc)
            gacc[...] = jnp.zeros_like(gacc)
            dh_ref[...] = jnp.zeros_like(dh_ref)

        @pl.when(i > 0)
        def _():
            x = h_ref[...]
            r = lax.rsqrt(jnp.mean(x * x, axis=-1, keepdims=True) + NORM_EPS)
            xhat = x * r
            g = g_ref[...]
            err = xhat * g - t_ref[...]
            lacc[...] += jnp.sum((err * err).reshape(LANES // 8, 8, d), axis=0)
            dn = err * (1.0 / d)
            dy = dn * g
            dh_ref[...] = r * (dy - xhat * jnp.mean(dy * xhat, axis=-1, keepdims=True))
            gacc[...] += jnp.sum((dn * xhat).reshape(LANES // 8, 8, d), axis=0)

        @pl.when(i == nsteps - 1)
        def _():
            tot = jnp.sum(jnp.sum(lacc[...], axis=1, keepdims=True), axis=0, keepdims=True)
            loss_ref[...] = jnp.broadcast_to(tot * (0.5 / d), (1, LANES))
            dg_ref[...] = jnp.sum(gacc[...], axis=0, keepdims=True)

    row = pl.BlockSpec((LANES, d), lambda i: (i, 0))
    trow = pl.BlockSpec((LANES, d), lambda i: (jnp.maximum(i - 1, 0), 0))
    gain_spec = pl.BlockSpec((1, d), lambda i: (0, 0))
    return pl.pallas_call(
        body, name="loss_head", grid=(nsteps,),
        in_specs=[row, gain_spec, trow],
        out_specs=[pl.BlockSpec((1, LANES), lambda i: (0, 0)), row, gain_spec],
        out_shape=[jax.ShapeDtypeStruct((1, LANES), F32), jax.ShapeDtypeStruct((m, d), F32),
                   jax.ShapeDtypeStruct((1, d), F32)],
        scratch_shapes=[pltpu.VMEM((8, d), F32), pltpu.VMEM((8, d), F32)],
        compiler_params=_params("arbitrary"),
    )(h, gain, target)


def _swiglu(gu, name):
    m, f2 = gu.shape
    f = f2 // 2

    def body(gu_ref, o_ref):
        g = gu_ref[:, :f]
        u = gu_ref[:, f:]
        o_ref[...] = (g * _sigmoid(g) * u).astype(BF16)

    return pl.pallas_call(
        body, name=name, grid=(m // LANES,),
        in_specs=[pl.BlockSpec((LANES, f2), lambda i: (i, 0))],
        out_specs=pl.BlockSpec((LANES, f), lambda i: (i, 0)),
        out_shape=jax.ShapeDtypeStruct((m, f), BF16),
        compiler_params=_params("parallel"),
    )(gu)


def _swiglu_bwd(gu, da, name):
    m, f2 = gu.shape
    f = f2 // 2

    def body(gu_ref, da_ref, o_ref):
        g = gu_ref[:, :f]
        u = gu_ref[:, f:]
        da_ = da_ref[...]
        s = _sigmoid(g)
        o_ref[:, :f] = (da_ * u * s * (1.0 + g * (1.0 - s))).astype(BF16)
        o_ref[:, f:] = (da_ * g * s).astype(BF16)

    return pl.pallas_call(
        body, name=name, grid=(m // LANES,),
        in_specs=[pl.BlockSpec((LANES, f2), lambda i: (i, 0)), pl.BlockSpec((LANES, f), lambda i: (i, 0))],
        out_specs=pl.BlockSpec((LANES, f2), lambda i: (i, 0)),
        out_shape=jax.ShapeDtypeStruct((m, f2), BF16),
        compiler_params=_params("parallel"),
    )(gu, da)


def _conv_taps(xa, w_ref):
    acc = xa * w_ref[CONV_WIDTH - 1:CONV_WIDTH, :]
    for i in range(CONV_WIDTH - 1):
        acc = acc + pltpu.roll(xa, CONV_WIDTH - 1 - i, 0) * w_ref[i:i + 1, :]
    return acc[8:]


def _gdn_conv(proj, conv_w, n_heads):
    m, w3 = proj.shape
    wd = w3 // 3
    nblk = m // LANES

    def body(x_ref, w_ref, o_ref):
        c = pl.program_id(0)
        is_qk = c < 2 * n_heads

        def finish(cv):
            s = cv * _sigmoid(cv)
            nrm = s * lax.rsqrt(jnp.sum(s * s, axis=-1, keepdims=True) + NORM_EPS)
            return jnp.where(is_qk, nrm, s)

        x0 = jnp.concatenate([jnp.zeros((8, LANES), F32), x_ref[pl.ds(0, LANES), :]], axis=0)
        o_ref[pl.ds(0, LANES), :] = finish(_conv_taps(x0, w_ref))

        def step(b, carry):
            r0 = pl.multiple_of(b * LANES, LANES)
            xa = x_ref[pl.ds(r0 - 8, LANES + 8), :]
            o_ref[pl.ds(r0, LANES), :] = finish(_conv_taps(xa, w_ref))
            return carry

        lax.fori_loop(1, nblk, step, 0)

    return pl.pallas_call(
        body, name="gdn_conv", grid=(w3 // LANES,),
        in_specs=[pl.BlockSpec((m, LANES), lambda c: (0, c)), pl.BlockSpec((CONV_WIDTH, LANES), lambda c: (0, c))],
        out_specs=pl.BlockSpec((None, m, LANES), lambda c: (c // n_heads, 0, c % n_heads)),
        out_shape=jax.ShapeDtypeStruct((3, m, wd), F32),
        compiler_params=_params("parallel"),
    )(proj, conv_w)


def _gdn_conv_bwd(proj, conv_w, dact, n_heads):
    m, w3 = proj.shape
    nblk = m // LANES

    def body(x_ref, w_ref, dy_ref, dx_ref, dw_ref, dc_ref):
        c = pl.program_id(0)
        is_qk = c < 2 * n_heads
        dc_ref[pl.ds(m, 8), :] = jnp.zeros((8, LANES), F32)

        def pre_act(xa):
            cv = _conv_taps(xa, w_ref)
            sg = _sigmoid(cv)
            s = cv * sg
            return cv, sg, s

        def dconv_in(b, xa, dwacc):
            r0 = pl.multiple_of(b * LANES, LANES)
            cv, sg, s = pre_act(xa)
            dy = dy_ref[pl.ds(r0, LANES), :]
            rn = lax.rsqrt(jnp.sum(s * s, axis=-1, keepdims=True) + NORM_EPS)
            yn = s * rn
            ds_qk = rn * (dy - yn * jnp.sum(dy * yn, axis=-1, keepdims=True))
            ds = jnp.where(is_qk, ds_qk, dy)
            dcv = ds * sg * (1.0 + cv * (1.0 - sg))
            dc_ref[pl.ds(r0, LANES), :] = dcv
            new = []
            for i in range(CONV_WIDTH):
                xs = xa[8:] if i == CONV_WIDTH - 1 else pltpu.roll(xa, CONV_WIDTH - 1 - i, 0)[8:]
                new.append(dwacc[i] + jnp.sum((dcv * xs).reshape(LANES // 8, 8, LANES), axis=0))
            return tuple(new)

        x0 = jnp.concatenate([jnp.zeros((8, LANES), F32), x_ref[pl.ds(0, LANES), :]], axis=0)
        zero = jnp.zeros((8, LANES), F32)
        dwacc = dconv_in(0, x0, (zero,) * CONV_WIDTH)

        def step1(b, acc):
            r0 = pl.multiple_of(b * LANES, LANES)
            return dconv_in(b, x_ref[pl.ds(r0 - 8, LANES + 8), :], acc)

        dwacc = lax.fori_loop(1, nblk, step1, dwacc)
        for i in range(CONV_WIDTH):
            dw_ref[i:i + 1, :] = jnp.sum(dwacc[i], axis=0, keepdims=True)

        def step2(b, carry):
            r0 = pl.multiple_of(b * LANES, LANES)
            da = dc_ref[pl.ds(r0, LANES + 8), :]
            acc = da * w_ref[CONV_WIDTH - 1:CONV_WIDTH, :]
            for i in range(CONV_WIDTH - 1):
                sh = CONV_WIDTH - 1 - i
                acc = acc + pltpu.roll(da, LANES + 8 - sh, 0) * w_ref[i:i + 1, :]
            dx_ref[pl.ds(r0, LANES), :] = acc[:LANES]
            return carry

        lax.fori_loop(0, nblk, step2, 0)

    return pl.pallas_call(
        body, name="gdn_conv_bwd", grid=(w3 // LANES,),
        in_specs=[pl.BlockSpec((m, LANES), lambda c: (0, c)),
                  pl.BlockSpec((CONV_WIDTH, LANES), lambda c: (0, c)),
                  pl.BlockSpec((None, m, LANES), lambda c: (c // n_heads, 0, c % n_heads))],
        out_specs=[pl.BlockSpec((m, LANES), lambda c: (0, c)), pl.BlockSpec((CONV_WIDTH, LANES), lambda c: (0, c))],
        out_shape=[jax.ShapeDtypeStruct((m, w3), F32), jax.ShapeDtypeStruct((CONV_WIDTH, w3), F32)],
        scratch_shapes=[pltpu.VMEM((m + 8, LANES), F32)],
        compiler_params=_params("parallel"),
    )(proj, conv_w, dact)


def _softplus(x):
    return jnp.maximum(x, 0.0) + jnp.log(1.0 + jnp.exp(-jnp.abs(x)))


def _gdn_gates(ab, a_log, dt_bias, n_heads):
    m = ab.shape[0]
    bm = _pick(m, (640, 384, 128))

    def body(ab_ref, al_ref, dt_ref, g_ref, b_ref):
        i = pl.program_id(0)
        x = ab_ref[...]
        live = (_iota2((bm, LANES), 0) + i * bm) >= PAD_ROWS
        g = jnp.where(live, -jnp.exp(al_ref[...]) * _softplus(x + dt_ref[...]), 0.0)
        beta = jnp.where(live, _sigmoid(x), 0.0)
        for h in range(n_heads):
            g_ref[:, h * LANES:(h + 1) * LANES] = jnp.broadcast_to(g[:, h:h + 1], (bm, LANES))
            b_ref[:, h * LANES:(h + 1) * LANES] = jnp.broadcast_to(
                beta[:, n_heads + h:n_heads + h + 1], (bm, LANES))

    wide = pl.BlockSpec((bm, n_heads * LANES), lambda i: (i, 0))
    return pl.pallas_call(
        body, name="gdn_gates", grid=(m // bm,),
        in_specs=[pl.BlockSpec((bm, LANES), lambda i: (i, 0))] + [pl.BlockSpec((1, LANES), lambda i: (0, 0))] * 2,
        out_specs=[wide, wide],
        out_shape=[jax.ShapeDtypeStruct((m, n_heads * LANES), F32)] * 2,
        compiler_params=_params("parallel"),
    )(ab, a_log, dt_bias)


def _gdn_gates_bwd(ab, a_log, dt_bias, dg_b, dbeta_b, n_heads):
    m = ab.shape[0]
    bm = _pick(m, (640, 384, 128))
    nsteps = m // bm

    def body(ab_ref, al_ref, dt_ref, dg_ref, db_ref, dab_ref, dal_ref, ddt_ref, acc_al, acc_dt):
        i = pl.program_id(0)
        x = ab_ref[...]
        lane = _iota2((bm, LANES), 1)
        live = (_iota2((bm, LANES), 0) + i * bm) >= PAD_ROWS
        dg = jnp.zeros((bm, LANES), F32)
        dbeta = jnp.zeros((bm, LANES), F32)
        for h in range(n_heads):
            dg = jnp.where(lane == h, dg_ref[:, h * LANES:(h + 1) * LANES], dg)
            dbeta = jnp.where(lane == n_heads + h, db_ref[:, h * LANES:(h + 1) * LANES], dbeta)
        dg = jnp.where(live, dg, 0.0)
        dbeta = jnp.where(live, dbeta, 0.0)
        nea = -jnp.exp(al_ref[...])
        pre = x + dt_ref[...]
        d_pre = dg * nea * _sigmoid(pre)
        beta = _sigmoid(x)
        dab_ref[...] = d_pre + dbeta * beta * (1.0 - beta)
        p_al = jnp.sum((dg * nea * _softplus(pre)).reshape(bm // 8, 8, LANES), axis=0)
        p_dt = jnp.sum(d_pre.reshape(bm // 8, 8, LANES), axis=0)

        @pl.when(i == 0)
        def _():
            acc_al[...] = p_al
            acc_dt[...] = p_dt

        @pl.when(i > 0)
        def _():
            acc_al[...] += p_al
            acc_dt[...] += p_dt

        @pl.when(i == nsteps - 1)
        def _():
            dal_ref[...] = jnp.sum(acc_al[...], axis=0, keepdims=True)
            ddt_ref[...] = jnp.sum(acc_dt[...], axis=0, keepdims=True)

    wide = pl.BlockSpec((bm, n_heads * LANES), lambda i: (i, 0))
    one = pl.BlockSpec((1, LANES), lambda i: (0, 0))
    nar = pl.BlockSpec((bm, LANES), lambda i: (i, 0))
    return pl.pallas_call(
        body, name="gdn_gates_bwd", grid=(nsteps,),
        in_specs=[nar, one, one, wide, wide],
        out_specs=[nar, one, one],
        out_shape=[jax.ShapeDtypeStruct((m, LANES), F32), jax.ShapeDtypeStruct((1, LANES), F32),
                   jax.ShapeDtypeStruct((1, LANES), F32)],
        scratch_shapes=[pltpu.VMEM((8, LANES), F32)] * 2,
        compiler_params=_params("arbitrary"),
    )(ab, a_log, dt_bias, dg_b, dbeta_b)


def _chunk_masks():
    c = GDN_CHUNK
    ri, ci = _iota2((c, c), 0), _iota2((c, c), 1)
    return ri >= ci, ri > ci, ri == ci


def _chunk_decay(gbs):
    c = GDN_CHUNK
    incl, _, _ = _chunk_masks()
    inclf = incl.astype(F32)
    ones = jnp.ones((c, LANES), F32)
    g_cums = [_hdot(inclf, gb) for gb in gbs]
    g_rows = [_hdot(ones, g_cum, NT) * (1.0 / LANES) for g_cum in g_cums]
    dmats = [jnp.where(incl, jnp.exp(jnp.where(incl, g_cum[:, :c] - g_row, 0.0)), 0.0)
             for g_cum, g_row in zip(g_cums, g_rows)]
    return g_cums, dmats


def _unit_lower_inverse(mats):
    _, _, eye = _chunk_masks()
    xs = [eye.astype(F32) - a for a in mats]
    ps = list(mats)
    for _ in range(int(math.log2(GDN_CHUNK)) - 1):
        ps = [_hdot(p, p) for p in ps]
        xs = [x + _hdot(x, p) for x, p in zip(xs, ps)]
    return xs


def _gdn_prep(qkv, g_b, beta_b, n_heads):
    _, m, wd = qkv.shape
    c = GDN_CHUNK
    nc = m // c
    cb = _pick(nc, (10, 5, 3, 2))
    rows = cb * c
    qscale = GDN_HEAD_DIM ** -0.5

    def body(qkv_ref, g_ref, b_ref, ub_ref, wc_ref, qd_ref, kd_ref, gc_ref, p_ref, t_ref):
        _, strict, _ = _chunk_masks()
        sls = [pl.ds(s * c, c) for s in range(cb)]
        g_cums, dmats = _chunk_decay([g_ref[sl, :] for sl in sls])
        kks = [_bdot(qkv_ref[1, sl, :], qkv_ref[1, sl, :], NT) for sl in sls]
        ts = _unit_lower_inverse([jnp.where(strict, b_ref[sl, :][:, :c] * dmat * kk, 0.0)
                                  for sl, dmat, kk in zip(sls, dmats, kks)])
        for s, (sl, g_cum, dmat, t) in enumerate(zip(sls, g_cums, dmats, ts)):
            q = qkv_ref[0, sl, :] * qscale
            k = qkv_ref[1, sl, :]
            bb = b_ref[sl, :]
            gam = jnp.exp(g_cum)
            ub_ref[sl, :] = _hdot(t, bb * qkv_ref[2, sl, :])
            wc_ref[sl, :] = _hdot(t, bb * gam * k)
            p_ref[s] = _bdot(q, k, NT) * dmat
            qd_ref[sl, :] = q * gam
            kd_ref[sl, :] = k * jnp.exp(g_cum[c - 1:c, :] - g_cum)
            gc_ref[sl, :] = g_cum
            t_ref[s] = t

    blk = pl.BlockSpec((rows, LANES), lambda h, n: (n, h))
    sq = pl.BlockSpec((None, cb, c, c), lambda h, n: (h, n, 0, 0))
    big = jax.ShapeDtypeStruct((m, wd), F32)
    small = jax.ShapeDtypeStruct((n_heads, nc, c, c), F32)
    return pl.pallas_call(
        body, name="gdn_prep", grid=(n_heads, nc // cb),
        in_specs=[pl.BlockSpec((3, rows, LANES), lambda h, n: (0, n, h)), blk, blk],
        out_specs=[blk] * 5 + [sq, sq],
        out_shape=[big] * 5 + [small, small],
        compiler_params=_params("parallel", "parallel"),
    )(qkv, g_b, beta_b)


def _gdn_prep_bwd(qkv, g_b, beta_b, ub, wc, tinv, dub, dwc, dqd, dkd, dp, dgl, n_heads):
    _, m, wd = qkv.shape
    c = GDN_CHUNK
    nc = m // c
    cb = _pick(nc, (5, 3, 2))
    rows = cb * c
    qscale = GDN_HEAD_DIM ** -0.5

    def body(qkv_ref, g_ref, b_ref, ub_ref, wc_ref, t_ref, dub_ref, dwc_ref, dqd_ref, dkd_ref, dp_ref, dgl_ref,
             dqkv_ref, dg_ref, db_ref):
        incl, strict, _ = _chunk_masks()
        ones_c = jnp.ones((c, LANES), F32)
        ones_l = jnp.ones((LANES, LANES), F32)
        last_row = _iota2((c, LANES), 0) == c - 1

        def chunk(s):
            sl = pl.ds(s * c, c)
            q = qkv_ref[0, sl, :] * qscale
            k = qkv_ref[1, sl, :]
            v = qkv_ref[2, sl, :]
            bb = b_ref[sl, :]
            bcol = bb[:, :c]
            t = t_ref[s]
            g_cum = _hdot(incl.astype(F32), g_ref[sl, :])
            kk = _bdot(k, k, NT)
            qk = _bdot(q, k, NT)
            drhs_u = _hdot(t, dub_ref[sl, :], TN)
            drhs_w = _hdot(t, dwc_ref[sl, :], TN)
            yield
            g_row = _hdot(ones_c, g_cum, NT) * (1.0 / LANES)
            da = -jnp.where(strict, _bdot(drhs_u, ub_ref[sl, :], NT) + _bdot(drhs_w, wc_ref[sl, :], NT), 0.0)
            dbeta = _hdot(drhs_u * v, ones_l)
            tk = _hdot(drhs_w * k, ones_l)
            dqd = dqd_ref[sl, :]
            dgam = _hdot(dqd * q, ones_l)
            yield
            dmat = jnp.where(incl, jnp.exp(jnp.where(incl, g_cum[:, :c] - g_row, 0.0)), 0.0)
            gam = jnp.exp(g_cum)
            edec = jnp.exp(g_cum[c - 1:c, :] - g_cum)
            dkd = dkd_ref[sl, :]
            skd = _hdot(dkd * k * edec, ones_l)
            dbeta = dbeta + _hdot(da * dmat * kk, ones_c)
            dkk = da * bcol * dmat
            dpm = jnp.where(incl, dp_ref[s], 0.0)
            dpd = dpm * dmat
            dk = _bdot(dkk, k) + _bdot(dkk, k, TN) + _bdot(dpd, q, TN)
            dq = _bdot(dpd, k)
            e = (da * bcol * kk + dpm * qk) * dmat
            yield
            dk = dk + bb * gam * drhs_w + dkd * edec
            dq = dq + dqd * gam
            dbeta = dbeta + tk * gam
            dgam = dgam + bb * tk
            dgc = dgam * gam - skd + _hdot(e, ones_c) - _hdot(e, ones_c, TN)
            tot = _hdot(jnp.ones((c, c), F32), skd)
            yield
            dgc = dgc + jnp.where(last_row, tot + dgl_ref[s, 0:1, :], 0.0)
            dg_ref[sl, :] = _hdot((_iota2((c, c), 0) <= _iota2((c, c), 1)).astype(F32), dgc)
            db_ref[sl, :] = dbeta
            dqkv_ref[0, sl, :] = dq * qscale
            dqkv_ref[1, sl, :] = dk
            dqkv_ref[2, sl, :] = bb * drhs_u

        _lockstep([chunk(s) for s in range(cb)])

    blk = pl.BlockSpec((rows, LANES), lambda h, n: (n, h))
    blk3 = pl.BlockSpec((3, rows, LANES), lambda h, n: (0, n, h))
    sq = pl.BlockSpec((None, cb, c, c), lambda h, n: (h, n, 0, 0))
    sgl = pl.BlockSpec((None, cb, 8, LANES), lambda h, n: (h, n, 0, 0))
    big = jax.ShapeDtypeStruct((m, wd), F32)
    return pl.pallas_call(
        body, name="gdn_prep_bwd", grid=(n_heads, nc // cb),
        in_specs=[blk3, blk, blk, blk, blk, sq, blk, blk, blk, blk, sq, sgl],
        out_specs=[blk3, blk, blk],
        out_shape=[jax.ShapeDtypeStruct((3, m, wd), F32), big, big],
        compiler_params=_params("parallel", "parallel"),
    )(qkv, g_b, beta_b, ub, wc, tinv, dub, dwc, dqd, dkd, dp, dgl)


def _gdn_scan(ub, wc, qd, kd, gc, p, n_heads):
    m, wd = ub.shape
    c = GDN_CHUNK
    nc = m // c
    dh = GDN_HEAD_DIM

    def body(ub_ref, wc_ref, qd_ref, kd_ref, gc_ref, p_ref, o_ref, ssave_ref, s_ref):
        n = pl.program_id(0)

        @pl.when(n == 0)
        def _():
            s_ref[...] = jnp.zeros_like(s_ref)

        def head(h):
            sl = slice(h * LANES, (h + 1) * LANES)
            s = s_ref[h]
            ssave_ref[h] = s
            ws = _bdot(wc_ref[:, sl], s)
            qs = _bdot(qd_ref[:, sl], s)
            yield
            u = ub_ref[:, sl] - ws
            pu = _bdot(p_ref[h], u)
            ku = _bdot(kd_ref[:, sl], u, TN)
            yield
            o_ref[:, sl] = qs + pu
            s_ref[h] = s * jnp.exp(gc_ref[c - 1:c, sl]) + ku

        _lockstep([head(h) for h in range(n_heads)])

    blk = pl.BlockSpec((c, wd), lambda n: (n, 0))
    return pl.pallas_call(
        body, name="gdn_scan", grid=(nc,),
        in_specs=[blk] * 5 + [pl.BlockSpec((n_heads, None, c, c), lambda n: (0, n, 0, 0))],
        out_specs=[blk, pl.BlockSpec((n_heads, None, dh, dh), lambda n: (0, n, 0, 0))],
        out_shape=[jax.ShapeDtypeStruct((m, wd), F32), jax.ShapeDtypeStruct((n_heads, nc, dh, dh), F32)],
        scratch_shapes=[pltpu.VMEM((n_heads, dh, dh), F32)],
        compiler_params=_params("arbitrary"),
    )(ub, wc, qd, kd, gc, p)


def _gdn_scan_bwd(do, ssave, ub, wc, qd, kd, gc, p, n_heads):
    m, wd = ub.shape
    c = GDN_CHUNK
    nc = m // c
    dh = GDN_HEAD_DIM

    def body(do_ref, ss_ref, ub_ref, wc_ref, qd_ref, kd_ref, gc_ref, p_ref,
             dub_ref, dwc_ref, dqd_ref, dkd_ref, dp_ref, dgl_ref, ds_ref):
        n = pl.program_id(0)

        @pl.when(n == 0)
        def _():
            ds_ref[...] = jnp.zeros_like(ds_ref)

        def head(h):
            sl = slice(h * LANES, (h + 1) * LANES)
            ds_next = ds_ref[h]
            s = ss_ref[h]
            do_ = do_ref[:, sl]
            wcv = wc_ref[:, sl]
            kdv = kd_ref[:, sl]
            ws = _bdot(wcv, s)
            du = _bdot(p_ref[h], do_, TN) + _bdot(kdv, ds_next)
            dqd_ref[:, sl] = _bdot(do_, s, NT)
            col = _hdot(jnp.ones((8, dh), F32), s * ds_next)
            qdo = _bdot(qd_ref[:, sl], do_, TN)
            yield
            u = ub_ref[:, sl] - ws
            egl = jnp.exp(gc_ref[c - 1:c, sl])
            dub_ref[:, sl] = du
            dwc_ref[:, sl] = -_bdot(du, s, NT)
            dp_ref[h] = _bdot(do_, u, NT)
            dkd_ref[:, sl] = _bdot(u, ds_next, NT)
            dgl_ref[h] = egl * _hdot(col, jnp.ones((LANES, LANES), F32))
            ds_ref[h] = qdo + egl * ds_next - _bdot(wcv, du, TN)

        _lockstep([head(h) for h in range(n_heads)])

    blk = pl.BlockSpec((c, wd), lambda n: (nc - 1 - n, 0))
    sq = pl.BlockSpec((n_heads, None, c, c), lambda n: (0, nc - 1 - n, 0, 0))
    big = jax.ShapeDtypeStruct((m, wd), F32)
    return pl.pallas_call(
        body, name="gdn_scan_bwd", grid=(nc,),
        in_specs=[blk, pl.BlockSpec((n_heads, None, dh, dh), lambda n: (0, nc - 1 - n, 0, 0))] + [blk] * 5 + [sq],
        out_specs=[blk] * 4 + [sq, pl.BlockSpec((n_heads, None, 8, LANES), lambda n: (0, nc - 1 - n, 0, 0))],
        out_shape=[big] * 4 + [jax.ShapeDtypeStruct((n_heads, nc, c, c), F32),
                               jax.ShapeDtypeStruct((n_heads, nc, 8, LANES), F32)],
        scratch_shapes=[pltpu.VMEM((n_heads, dh, dh), F32)],
        compiler_params=_params("arbitrary"),
    )(do, ssave, ub, wc, qd, kd, gc, p)


def _gdn_outnorm(o, gate, gain, n_heads):
    m, wd = o.shape
    bm = _pick(m, (640, 384, 128))

    def body(o_ref, gt_ref, gn_ref, y_ref):
        gn = gn_ref[...]
        for h in range(n_heads):
            sl = slice(h * LANES, (h + 1) * LANES)
            x = o_ref[:, sl]
            gt = gt_ref[:, sl]
            r = lax.rsqrt(jnp.mean(x * x, axis=-1, keepdims=True) + NORM_EPS)
            y_ref[:, sl] = (x * r * gn * gt * _sigmoid(gt)).astype(BF16)

    row = pl.BlockSpec((bm, wd), lambda i: (i, 0))
    return pl.pallas_call(
        body, name="gdn_outnorm", grid=(m // bm,),
        in_specs=[row, row, pl.BlockSpec((1, LANES), lambda i: (0, 0))], out_specs=row,
        out_shape=jax.ShapeDtypeStruct((m, wd), BF16),
        compiler_params=_params("parallel"),
    )(o, gate, gain)


def _gdn_outnorm_bwd(o, gate, gain, dy, n_heads):
    m, wd = o.shape
    bm = _pick(m, (640, 384, 128))
    nsteps = m // bm

    def body(o_ref, gt_ref, gn_ref, dy_ref, do_ref, dgt_ref, dgn_ref, acc_ref):
        i = pl.program_id(0)
        gn = gn_ref[...]
        part = jnp.zeros((8, LANES), F32)
        for h in range(n_heads):
            sl = slice(h * LANES, (h + 1) * LANES)
            x = o_ref[:, sl]
            gt = gt_ref[:, sl]
            d_out = dy_ref[:, sl]
            r = lax.rsqrt(jnp.mean(x * x, axis=-1, keepdims=True) + NORM_EPS)
            xhat = x * r
            sg = _sigmoid(gt)
            dz = d_out * gt * sg
            dgt_ref[:, sl] = d_out * xhat * gn * sg * (1.0 + gt * (1.0 - sg))
            dxh = dz * gn
            do_ref[:, sl] = r * (dxh - xhat * jnp.mean(dxh * xhat, axis=-1, keepdims=True))
            part = part + jnp.sum((dz * xhat).reshape(bm // 8, 8, LANES), axis=0)

        @pl.when(i == 0)
        def _():
            acc_ref[...] = part

        @pl.when(i > 0)
        def _():
            acc_ref[...] += part

        @pl.when(i == nsteps - 1)
        def _():
            dgn_ref[...] = jnp.sum(acc_ref[...], axis=0, keepdims=True)

    row = pl.BlockSpec((bm, wd), lambda i: (i, 0))
    one = pl.BlockSpec((1, LANES), lambda i: (0, 0))
    return pl.pallas_call(
        body, name="gdn_outnorm_bwd", grid=(nsteps,),
        in_specs=[row, row, one, row], out_specs=[row, row, one],
        out_shape=[jax.ShapeDtypeStruct((m, wd), F32)] * 2 + [jax.ShapeDtypeStruct((1, LANES), F32)],
        scratch_shapes=[pltpu.VMEM((8, LANES), F32)],
        compiler_params=_params("arbitrary"),
    )(o, gate, gain, dy)


def _sb_consts(kind):
    ri, ci = _iota2((LANES, LANES), 0), _iota2((LANES, LANES), 1)
    tri = {"suffix_excl": ri > ci, "prefix_incl": ri <= ci, "prefix_excl": ri < ci}[kind]
    return jnp.concatenate([tri.astype(BF16), jnp.ones((LANES, LANES), BF16)], axis=1)


def _sb_stack(a):
    first = _iota2((LANES, LANES), 1) < SB_HEAD_DIM
    return jnp.concatenate([jnp.where(first, a, 0.0), jnp.where(first, 0.0, a)], axis=0)


def _sb_unstack(a2):
    first = _iota2((LANES, LANES), 1) < SB_HEAD_DIM
    return jnp.where(first, a2[:LANES], a2[LANES:])


def _sb_visible(qi, kb, live):
    kpos = _iota2((2 * LANES, LANES), 1) + kb * LANES
    qpos = (_iota2((2 * LANES, LANES), 0) & (LANES - 1)) + qi * LANES
    return (kpos < qpos) & (kpos >= PAD_ROWS) & live


def _sb_log_gates(z, vis):
    t = jnp.log(1.0 + jnp.exp(-jnp.abs(z)))
    lb = jnp.minimum(z, 0.0) - t
    lk = lb - z
    if vis is not None:
        lk = jnp.where(vis, lk, 0.0)
    return lb, lk


def _sb_attention(q, kv):
    m, wd = q.shape
    npair = wd // LANES
    nq = m // LANES
    scale = SB_HEAD_DIM ** -0.5

    def body(q_ref, k_ref, v_ref, o_ref, w_ref, b_ref):
        qi = pl.program_id(1)
        su = _sb_consts("suffix_excl")
        q2 = _sb_stack(q_ref[...] * scale).astype(BF16)

        def tiles(kbs, carry, live):
            run, acc = carry
            r0s = [pl.multiple_of(kb * LANES, LANES) for kb in kbs]
            zs = [lax.dot_general(q2, k_ref[pl.ds(r0, LANES), :], NT, preferred_element_type=F32) for r0 in r0s]
            parts = []
            for kb, r0, z in zip(kbs, r0s, zs):
                vis = None if live is None else _sb_visible(qi, kb, live)
                lb, lk = _sb_log_gates(z, vis)
                parts.append((kb, r0, vis, lb, lk))
            parts = [(kb, r0, vis, lb, _split_dot(lk, su)) for kb, r0, vis, lb, lk in parts]
            for kb, r0, vis, lb, cs in parts:
                w = jnp.exp(lb + cs[:, :LANES] + run)
                if vis is not None:
                    w = jnp.where(vis, w, 0.0)
                run = run + cs[:, LANES:]
                wb = w.astype(BF16)
                w_ref[kb] = wb
                b_ref[kb] = jnp.exp(lb).astype(BF16)
                acc = acc + lax.dot_general(wb, v_ref[pl.ds(r0, LANES), :], NN, preferred_element_type=F32)
            return run, acc

        def group(j, carry):
            kb = qi - 1 - SB_UNROLL * j
            return tiles([kb - u for u in range(SB_UNROLL)], carry, None)

        zero = jnp.zeros((2 * LANES, LANES), F32)
        carry = tiles([qi], (zero, zero), True)
        n_mid = jnp.maximum(qi - 1, 0)
        carry = lax.fori_loop(0, n_mid // SB_UNROLL, group, carry)
        carry = lax.fori_loop(0, n_mid % SB_UNROLL, lambda j, c: tiles([n_mid % SB_UNROLL - j], c, None), carry)

        @pl.when(qi >= 1)
        def _():
            _, acc = tiles([0], carry, True)
            o_ref[...] = _sb_unstack(acc)

        @pl.when(qi == 0)
        def _():
            o_ref[...] = _sb_unstack(carry[1])

    qspec = pl.BlockSpec((LANES, LANES), lambda hp, qi: (qi, hp))
    saved = pl.BlockSpec((None, None, nq, 2 * LANES, LANES), lambda hp, qi: (hp, qi, 0, 0, 0))
    saved_shape = jax.ShapeDtypeStruct((npair, nq, nq, 2 * LANES, LANES), BF16)
    return pl.pallas_call(
        body, name="sb_attention", grid=(npair, nq),
        in_specs=[qspec, pl.BlockSpec((m, LANES), lambda hp, qi: (0, hp)),
                  pl.BlockSpec((m, LANES), lambda hp, qi: (0, npair + hp))],
        out_specs=[qspec, saved, saved],
        out_shape=[jax.ShapeDtypeStruct((m, wd), F32), saved_shape, saved_shape],
        compiler_params=_params("parallel", "arbitrary"),
    )(q, kv, kv)


def _sb_attention_bwd(q, kv, w_all, b_all, do):
    m, wd = q.shape
    npair = wd // LANES
    nq = m // LANES
    scale = SB_HEAD_DIM ** -0.5

    def body(q_ref, k_ref, v_ref, w_ref, b_ref, do_ref, dq_ref, dk_ref, dv_ref):
        qi = pl.program_id(1)

        @pl.when(qi == 0)
        def _():
            dk_ref[...] = jnp.zeros_like(dk_ref)
            dv_ref[...] = jnp.zeros_like(dv_ref)

        px = _sb_consts("prefix_excl")
        q2 = _sb_stack(q_ref[...] * scale).astype(BF16)
        do2 = _sb_stack(do_ref[...]).astype(BF16)

        def tiles(kbs, carry, live):
            run_p, dq_acc = carry
            r0s = [pl.multiple_of(kb * LANES, LANES) for kb in kbs]
            das = [lax.dot_general(do2, v_ref[pl.ds(r0, LANES), :], NT, preferred_element_type=F32) for r0 in r0s]
            wbs = [w_ref[kb] for kb in kbs]
            for wb, r0 in zip(wbs, r0s):
                dv_ref[pl.ds(r0, LANES), :] += lax.dot_general(wb, do2, TN, preferred_element_type=F32)
            pws = [wb.astype(F32) * da for wb, da in zip(wbs, das)]
            pss = [_split_dot(pw, px) for pw in pws]
            dzs = []
            for kb, pw, ps in zip(kbs, pws, pss):
                beta = b_ref[kb].astype(F32)
                dz = pw * (1.0 - beta) - beta * (run_p + ps[:, :LANES])
                if live is not None:
                    dz = jnp.where(_sb_visible(qi, kb, live), dz, 0.0)
                run_p = run_p + ps[:, LANES:]
                dzs.append(dz.astype(BF16))
            for r0, dz in zip(r0s, dzs):
                dq_acc = dq_acc + lax.dot_general(dz, k_ref[pl.ds(r0, LANES), :], NN, preferred_element_type=F32)
                dk_ref[pl.ds(r0, LANES), :] += lax.dot_general(dz, q2, TN, preferred_element_type=F32)
            return run_p, dq_acc

        def group(j, carry):
            kb = 1 + SB_UNROLL_BWD * j
            return tiles([kb + u for u in range(SB_UNROLL_BWD)], carry, None)

        zero = jnp.zeros((2 * LANES, LANES), F32)
        carry = tiles([0], (zero, zero), True)
        n_mid = jnp.maximum(qi - 1, 0)
        n_grp = n_mid // SB_UNROLL_BWD
        carry = lax.fori_loop(0, n_grp, group, carry)
        carry = lax.fori_loop(1 + n_grp * SB_UNROLL_BWD, jnp.maximum(qi, 1), lambda kb, c: tiles([kb], c, None), carry)

        @pl.when(qi >= 1)
        def _():
            _, dq_acc = tiles([qi], carry, True)
            dq_ref[...] = _sb_unstack(dq_acc) * scale

        @pl.when(qi == 0)
        def _():
            dq_ref[...] = _sb_unstack(carry[1]) * scale

    qspec = pl.BlockSpec((LANES, LANES), lambda hp, qi: (qi, hp))
    strip = pl.BlockSpec((m, LANES), lambda hp, qi: (0, hp))
    saved = pl.BlockSpec((None, None, nq, 2 * LANES, LANES), lambda hp, qi: (hp, qi, 0, 0, 0))
    big = jax.ShapeDtypeStruct((m, wd), F32)
    return pl.pallas_call(
        body, name="sb_attention_bwd", grid=(npair, nq),
        in_specs=[qspec, strip, pl.BlockSpec((m, LANES), lambda hp, qi: (0, npair + hp)), saved, saved, qspec],
        out_specs=[qspec, strip, strip],
        out_shape=[big, big, big],
        compiler_params=_params("parallel", "arbitrary"),
    )(q, kv, kv, w_all, b_all, do)


def _mesh_pos():
    return lax.axis_index("x"), lax.axis_index("y"), lax.axis_index("c")


def _exchange(srcs, name, scatter):
    n = len(srcs)
    rows = [s.shape[-2] for s in srcs]
    offs = [sum(rows[:i]) for i in range(n)]
    total = sum(rows)

    def body(*refs):
        src_refs, out_ref = refs[:n], refs[n]
        send_sems, recv_sems, local_sems = refs[n + 1:]
        x, y, c = _mesh_pos()
        me = 4 * x + 2 * y + c

        def piece(i, d):
            return src_refs[i].at[d] if scatter else src_refs[i]

        def window(slot, i):
            return out_ref.at[slot, pl.ds(offs[i], rows[i])]

        local = [pltpu.make_async_copy(piece(i, me), window(me, i), local_sems.at[i]) for i in range(n)]
        for cp in local:
            cp.start()
        peers = []
        for k in range(1, N_DEV):
            px = 1 - x if k & 4 else x
            py = 1 - y if k & 2 else y
            pc = 1 - c if k & 1 else c
            pid = 4 * px + 2 * py + pc
            for i in range(n):
                pltpu.make_async_remote_copy(
                    src_ref=piece(i, pid), dst_ref=window(me, i),
                    send_sem=send_sems.at[k - 1], recv_sem=recv_sems.at[k - 1],
                    device_id=(px, py, pc), device_id_type=pl.DeviceIdType.MESH).start()
            peers.append(pid)
        slabs = [pltpu.make_async_remote_copy(
            src_ref=out_ref.at[me], dst_ref=out_ref.at[peers[k - 1]],
            send_sem=send_sems.at[k - 1], recv_sem=recv_sems.at[k - 1],
            device_id=(x, y, c), device_id_type=pl.DeviceIdType.MESH) for k in range(1, N_DEV)]
        for cp in slabs:
            cp.wait_recv()
        for cp in slabs:
            cp.wait_send()
        for cp in local:
            cp.wait()

    return pl.pallas_call(
        body, name=name,
        in_specs=[pl.BlockSpec(memory_space=pl.ANY)] * n, out_specs=pl.BlockSpec(memory_space=pl.ANY),
        out_shape=jax.ShapeDtypeStruct((N_DEV, total, FLAT_COLS), srcs[0].dtype),
        scratch_shapes=[pltpu.SemaphoreType.DMA((N_DEV - 1,)), pltpu.SemaphoreType.DMA((N_DEV - 1,)),
                        pltpu.SemaphoreType.DMA((n,))],
        compiler_params=pltpu.CompilerParams(has_side_effects=True),
    )(*srcs)


def _adamw(parts, w, mom, var, name):
    rows = w.shape[0]
    br = FLAT_ROW_BLOCK
    c1 = 1.0 - ADAM_B1 ** ADAM_STEP
    c2 = 1.0 - ADAM_B2 ** ADAM_STEP

    def body(p_ref, w_ref, m_ref, v_ref, g_ref, d_ref, nm_ref, nv_ref):
        g = p_ref[0].astype(F32)
        for s in range(1, N_DEV):
            g = g + p_ref[s].astype(F32)
        m_new = ADAM_B1 * m_ref[...] + (1.0 - ADAM_B1) * g
        v_new = ADAM_B2 * v_ref[...] + (1.0 - ADAM_B2) * (g * g)
        m_hat = m_new / c1
        v_hat = v_new / c2
        g_ref[...] = g
        d_ref[...] = -ADAM_LR * (m_hat / (jnp.sqrt(v_hat) + ADAM_EPS) + ADAM_WD * w_ref[...])
        nm_ref[...] = m_new
        nv_ref[...] = v_new

    row = pl.BlockSpec((br, FLAT_COLS), lambda i: (i, 0))
    flat = jax.ShapeDtypeStruct((rows, FLAT_COLS), F32)
    return pl.pallas_call(
        body, name=name, grid=(rows // br,),
        in_specs=[pl.BlockSpec((N_DEV, br, FLAT_COLS), lambda i: (0, i, 0)), row, row, row],
        out_specs=[row] * 4, out_shape=[flat] * 4,
        compiler_params=_params("parallel"),
    )(parts, w, mom, var)


SMALL_SHARDED = (("meta_tokens", 1), ("gdn_norm_g", 1), ("gdn_conv_w", 2))
BIG_PIECES = (("gdn_w_in", 0, True), ("gdn_w_out", 0, False), ("w_kv", None, True), ("sb_w_q", 0, False),
              ("sb_w_o", 0, False), ("ffn_w_gate_up", 0, True), ("ffn_w_gate_up", 1, True),
              ("ffn_w_down", 0, False), ("ffn_w_down", 1, False))
REPLICATED = ("gdn_a_log", "gdn_dt_bias", "gdn_onorm_g", "kv_norm_g", "sb_norm_g", "ffn_norm_g", "final_norm_g")
WEIGHTS = ("meta_tokens", "gdn_norm_g", "gdn_w_in", "gdn_conv_w", "gdn_a_log", "gdn_dt_bias", "gdn_onorm_g",
           "gdn_w_out", "kv_norm_g", "w_kv", "sb_norm_g", "sb_w_q", "sb_w_o", "ffn_norm_g", "ffn_w_gate_up",
           "ffn_w_down", "final_norm_g")
F32_ROWS = 8
BF16_ROWS = 16


def _n_rows(shape, mult):
    return -(-math.prod(shape) // (mult * FLAT_COLS)) * mult


def _as_rows(a, nl, mult):
    lead = a.shape[:nl]
    size = math.prod(a.shape[nl:])
    nr = _n_rows(a.shape[nl:], mult)
    flat = a.reshape(lead + (size,))
    if nr * FLAT_COLS != size:
        flat = jnp.pad(flat, [(0, 0)] * nl + [(0, nr * FLAT_COLS - size)])
    return flat.reshape(lead + (nr, FLAT_COLS))


def _from_rows(flat, r0, shape, nl, mult):
    lead = flat.shape[:nl]
    nr = _n_rows(shape, mult)
    seg = lax.slice_in_dim(flat, r0, r0 + nr, axis=nl).reshape(lead + (nr * FLAT_COLS,))
    return lax.slice_in_dim(seg, 0, math.prod(shape), axis=nl).reshape(lead + tuple(shape)), r0 + nr


def _to_piece(shard, layer, transposed):
    mat = shard if layer is None else shard[layer]
    return mat.T if transposed else mat


def _from_pieces(pieces, name):
    mats = [p.T if tr else p for (n, layer, tr), p in zip(BIG_PIECES, pieces) if n == name]
    layers = [layer for n, layer, _ in BIG_PIECES if n == name]
    return mats[0] if layers[0] is None else jnp.stack(mats)


def _merge_shards(g, axis):
    t = jnp.moveaxis(g, 0, axis)
    shp = t.shape
    return t.reshape(shp[:axis] + (shp[axis] * shp[axis + 1],) + shp[axis + 2:])


def _split_shards(full, axis):
    shp = full.shape
    t = full.reshape(shp[:axis] + (N_DEV, shp[axis] // N_DEV) + shp[axis + 1:])
    return jnp.moveaxis(t, axis, 0)


def _pad_lanes(a, width=LANES):
    return jnp.pad(a, ((0, 0), (0, width - a.shape[1])))


def _local_step(x, target, w, mats):
    d = x.shape[1]
    gh = w["gdn_a_log"].shape[1]
    gw = gh * GDN_HEAD_DIM
    w_int = mats["gdn_w_in", 0]
    w_qkvt, w_gatet = w_int[:3 * gw], w_int[3 * gw:4 * gw]
    w_abt = jnp.pad(w_int[4 * gw:], ((0, LANES - 2 * gh), (0, 0)))
    conv_w = w["gdn_conv_w"][0]
    a_log, dt_bias = _pad_lanes(w["gdn_a_log"]), _pad_lanes(w["gdn_dt_bias"])
    w_out = mats["gdn_w_out", 0]
    w_kvt = mats["w_kv", None]
    sbw = w_kvt.shape[0] // 2
    w_kt, w_vt = w_kvt[:sbw], w_kvt[sbw:]
    w_q, w_o = mats["sb_w_q", 0], mats["sb_w_o", 0]
    w_gut = [mats["ffn_w_gate_up", l] for l in range(2)]
    w_dn = [mats["ffn_w_down", l] for l in range(2)]
    ffn_g = [w["ffn_norm_g"][l:l + 1] for l in range(2)]
    kv_g, fin_g = w["kv_norm_g"][None], w["final_norm_g"][None]

    h0 = jnp.concatenate([jnp.zeros((PAD_ROWS, d), F32), w["meta_tokens"], x], axis=0)

    (n1,) = _rmsnorm(h0, [w["gdn_norm_g"]], "norm_gdn")
    proj = _matmul(n1, w_qkvt, "nt", "mm_gdn_qkv")
    gate = _matmul(n1, w_gatet, "nt", "mm_gdn_gate")
    ab = _matmul(n1, w_abt, "nt", "mm_gdn_ab")
    qkv = _gdn_conv(proj, conv_w, gh)
    g_b, beta_b = _gdn_gates(ab, a_log, dt_bias, gh)
    ub, wc, qd, kd, gc, pmat, tinv = _gdn_prep(qkv, g_b, beta_b, gh)
    o_gdn, ssave = _gdn_scan(ub, wc, qd, kd, gc, pmat, gh)
    o2 = _gdn_outnorm(o_gdn, gate, w["gdn_onorm_g"], gh)
    h1 = _matmul(o2, w_out, "nn", "mm_gdn_out", add=h0)

    def ffn_fwd(h, l):
        (n,) = _rmsnorm(h, [ffn_g[l]], f"norm_ffn{l}")
        gu = _matmul(n, w_gut[l], "nt", f"mm_ffn{l}_gu")
        act = _swiglu(gu, f"swiglu{l}")
        return n, gu, act, _matmul(act, w_dn[l], "nn", f"mm_ffn{l}_down", add=h)

    n2, gu0, act0, h2 = ffn_fwd(h1, 0)

    nkv, n3 = _rmsnorm(h2, [kv_g, w["sb_norm_g"]], "norm_kv_sb")
    kv = _matmul(nkv, w_kvt, "nt", "mm_kv", out_dtype=BF16)
    q_sb = _matmul(n3, w_q, "nn", "mm_sb_q")
    o_sb, w_sb, beta_sb = _sb_attention(q_sb, kv)
    h3 = _matmul(o_sb, w_o, "nn", "mm_sb_o", add=h2)
    n4, gu1, act1, h4 = ffn_fwd(h3, 1)

    loss, dh4, d_fin = _loss_head(h4, fin_g, target)

    def ffn_bwd(dh, h, n, gu, act, l):
        d_act = _matmul(dh, w_dn[l], "nt", f"mm_ffn{l}_dact")
        dw_dn = _matmul(act, dh, "tn", f"mm_ffn{l}_dwdown")
        dgu = _swiglu_bwd(gu, d_act, f"swiglu_bwd{l}")
        dw_gu = _matmul(dgu, n, "tn", f"mm_ffn{l}_dwgu")
        dn = _matmul(dgu, w_gut[l], "nn", f"mm_ffn{l}_dn")
        dh_in, (dg,) = _rmsnorm_bwd(h, [ffn_g[l]], [dn], dh, f"norm_ffn{l}_bwd")
        return dh_in, dw_gu, dw_dn, dg

    dh3, dw_gu1, dw_dn1, dg_ffn1 = ffn_bwd(dh4, h3, n4, gu1, act1, 1)

    do_sb = _matmul(dh3, w_o, "nt", "mm_sb_do")
    dw_o = _matmul(o_sb, dh3, "tn", "mm_sb_dwo")
    dq_sb, dk_sb, dv_sb = _sb_attention_bwd(q_sb, kv, w_sb, beta_sb, do_sb)
    dw_q = _matmul(n3, dq_sb, "tn", "mm_sb_dwq")
    dn3 = _matmul(dq_sb, w_q, "nt", "mm_sb_dn")
    dw_k = _matmul(dk_sb, nkv, "tn", "mm_dwk")
    dw_v = _matmul(dv_sb, nkv, "tn", "mm_dwv")
    dnkv = _matmul(dk_sb, w_kt, "nn", "mm_dnk")
    dnkv = _matmul(dv_sb, w_vt, "nn", "mm_dnv", add=dnkv)
    dh2, (dg_kv, dg_sb) = _rmsnorm_bwd(h2, [kv_g, w["sb_norm_g"]], [dnkv, dn3], dh3, "norm_kv_sb_bwd")

    dh1, dw_gu0, dw_dn0, dg_ffn0 = ffn_bwd(dh2, h1, n2, gu0, act0, 0)

    do2 = _matmul(dh1, w_out, "nt", "mm_gdn_do")
    dw_out = _matmul(o2, dh1, "tn", "mm_gdn_dwout")
    do_gdn, dgate, d_onorm = _gdn_outnorm_bwd(o_gdn, gate, w["gdn_onorm_g"], do2, gh)
    dub, dwc, dqd, dkd, dpm, dgl = _gdn_scan_bwd(do_gdn, ssave, ub, wc, qd, kd, gc, pmat, gh)
    dqkv, dg_b, dbeta_b = _gdn_prep_bwd(qkv, g_b, beta_b, ub, wc, tinv, dub, dwc, dqd, dkd, dpm, dgl, gh)
    dab, d_alog, d_dtb = _gdn_gates_bwd(ab, a_log, dt_bias, dg_b, dbeta_b, gh)
    dproj, d_conv = _gdn_conv_bwd(proj, conv_w, dqkv, gh)
    dw_qkv = _matmul(dproj, n1, "tn", "mm_gdn_dwqkv")
    dw_gate = _matmul(dgate, n1, "tn", "mm_gdn_dwgate")
    dw_ab = _matmul(dab, n1, "tn", "mm_gdn_dwab")
    dn1 = _matmul(dproj, w_qkvt, "nn", "mm_gdn_dn_qkv")
    dn1 = _matmul(dgate, w_gatet, "nn", "mm_gdn_dn_gate", add=dn1)
    dn1 = _matmul(dab, w_abt, "nn", "mm_gdn_dn_ab", add=dn1)
    dh0, (dg_gdn,) = _rmsnorm_bwd(h0, [w["gdn_norm_g"]], [dn1], dh1, "norm_gdn_bwd")

    grads = {
        "meta_tokens": dh0[PAD_ROWS:LANES],
        "gdn_norm_g": dg_gdn,
        "gdn_conv_w": d_conv[None],
        "gdn_a_log": d_alog[:, :gh],
        "gdn_dt_bias": d_dtb[:, :gh],
        "gdn_onorm_g": d_onorm,
        "kv_norm_g": dg_kv[0],
        "sb_norm_g": dg_sb,
        "ffn_norm_g": jnp.concatenate([dg_ffn0, dg_ffn1], axis=0),
        "final_norm_g": d_fin[0],
    }
    gmats = {
        ("gdn_w_in", 0): jnp.concatenate([dw_qkv, dw_gate, dw_ab[:2 * gh]], axis=0),
        ("gdn_w_out", 0): dw_out,
        ("w_kv", None): jnp.concatenate([dw_k, dw_v], axis=0),
        ("sb_w_q", 0): dw_q,
        ("sb_w_o", 0): dw_o,
        ("ffn_w_gate_up", 0): dw_gu0,
        ("ffn_w_gate_up", 1): dw_gu1,
        ("ffn_w_down", 0): dw_dn0,
        ("ffn_w_down", 1): dw_dn1,
    }
    return loss, dh0[LANES:], grads, gmats


def kernel(x, meta_tokens, gdn_norm_g, gdn_w_in, gdn_conv_w, gdn_a_log, gdn_dt_bias, gdn_onorm_g, gdn_w_out, kv_norm_g, w_kv, sb_norm_g, sb_w_q, sb_w_o, ffn_norm_g, ffn_w_gate_up, ffn_w_down, final_norm_g, loss_target, m_meta_tokens, m_gdn_norm_g, m_gdn_w_in, m_gdn_conv_w, m_gdn_a_log, m_gdn_dt_bias, m_gdn_onorm_g, m_gdn_w_out, m_kv_norm_g, m_w_kv, m_sb_norm_g, m_sb_w_q, m_sb_w_o, m_ffn_norm_g, m_ffn_w_gate_up, m_ffn_w_down, m_final_norm_g, v_meta_tokens, v_gdn_norm_g, v_gdn_w_in, v_gdn_conv_w, v_gdn_a_log, v_gdn_dt_bias, v_gdn_onorm_g, v_gdn_w_out, v_kv_norm_g, v_w_kv, v_sb_norm_g, v_sb_w_q, v_sb_w_o, v_ffn_norm_g, v_ffn_w_gate_up, v_ffn_w_down, v_final_norm_g):
    given = dict(locals())
    small = [n for n, _ in SMALL_SHARDED]
    big_names = list(dict.fromkeys(n for n, _, _ in BIG_PIECES))

    def local_pieces(prefix):
        return [_to_piece(given[prefix + n], layer, tr) for n, layer, tr in BIG_PIECES]

    piece_shapes = [p.shape for p in local_pieces("")]

    g_small = _exchange([jnp.concatenate([_as_rows(given[n], 0, F32_ROWS) for n in small], axis=0)],
                        "all_gather_small", scatter=False)
    g_big = _exchange([_as_rows(p.astype(BF16), 0, BF16_ROWS) for p in local_pieces("")],
                      "all_gather_weights", scatter=False)
    full = {n: given[n] for n in REPLICATED}
    r = 0
    for n, ax in SMALL_SHARDED:
        shards, r = _from_rows(g_small, r, given[n].shape, 1, F32_ROWS)
        full[n] = _merge_shards(shards, ax)
    mats, r = {}, 0
    for (n, layer, _), shp in zip(BIG_PIECES, piece_shapes):
        rows8, r = _from_rows(g_big, r, shp, 1, BF16_ROWS)
        mats[n, layer] = rows8.reshape((N_DEV * shp[0],) + shp[1:])

    loss_b, grad_x, grads, gmats = _local_step(x[0], loss_target[0], full, mats)

    order = small + list(REPLICATED)
    fill_small = (-sum(_n_rows(given[n].shape, F32_ROWS) for n in order)) % FLAT_ROW_BLOCK
    fill_big = (-sum(_n_rows(shp, BF16_ROWS) for shp in piece_shapes)) % FLAT_ROW_BLOCK
    small_send = [_as_rows(_split_shards(grads[n], ax), 1, F32_ROWS) for n, ax in SMALL_SHARDED]
    small_send += [_as_rows(jnp.broadcast_to(grads[n][None], (N_DEV,) + grads[n].shape), 1, F32_ROWS)
                   for n in REPLICATED]
    if fill_small:
        small_send.append(jnp.zeros((N_DEV, fill_small, FLAT_COLS), F32))
    recv_small = _exchange([jnp.concatenate(small_send, axis=1)], "scatter_small_gradients", scatter=True)
    big_send = [_as_rows(gmats[n, layer].reshape((N_DEV,) + shp), 1, BF16_ROWS)
                for (n, layer, _), shp in zip(BIG_PIECES, piece_shapes)]
    if fill_big:
        big_send.append(jnp.zeros((N_DEV, fill_big, FLAT_COLS), BF16))
    recv_big = _exchange(big_send, "scatter_gradients", scatter=True)

    def flat_small(prefix):
        segs = [_as_rows(given[prefix + n], 0, F32_ROWS) for n in order]
        return jnp.concatenate(segs + ([jnp.zeros((fill_small, FLAT_COLS), F32)] if fill_small else []), axis=0)

    def flat_big(prefix):
        segs = [_as_rows(p, 0, BF16_ROWS) for p in local_pieces(prefix)]
        return jnp.concatenate(segs + ([jnp.zeros((fill_big, FLAT_COLS), F32)] if fill_big else []), axis=0)

    upd_small = _adamw(recv_small, flat_small(""), flat_small("m_"), flat_small("v_"), "adamw_small")
    upd_big = _adamw(recv_big, flat_big(""), flat_big("m_"), flat_big("v_"), "adamw")
    outs = {}
    for tag, f_small, f_big in zip(("grad", "delta", "new_m", "new_v"), upd_small, upd_big):
        outs[tag], r = {}, 0
        for n in order:
            outs[tag][n], r = _from_rows(f_small, r, given[n].shape, 0, F32_ROWS)
        pieces, r = [], 0
        for shp in piece_shapes:
            p, r = _from_rows(f_big, r, shp, 0, BF16_ROWS)
            pieces.append(p)
        for n in big_names:
            outs[tag][n] = _from_pieces(pieces, n)

    loss = lax.psum(loss_b[0, 0], ("x", "y", "c"))
    result = [loss, grad_x[None]]
    for tag in ("grad", "delta", "new_m", "new_v"):
        result += [outs[tag][n] for n in WEIGHTS]
    return tuple(result)
```

```python
import functools
import math

import jax
import jax.numpy as jnp
from jax import lax
from jax.experimental import pallas as pl
from jax.experimental.pallas import tpu as pltpu

F32 = jnp.float32
BF16 = jnp.bfloat16
HIGHEST = lax.Precision.HIGHEST

N_DEV = 8
LANES = 128
N_META = 16
PAD_ROWS = LANES - N_META
GDN_CHUNK = 64
GDN_HEAD_DIM = 128
SB_HEAD_DIM = 64
SB_UNROLL = 8
SB_UNROLL_BWD = 4
CONV_WIDTH = 4
NORM_EPS = 1e-6
FLAT_COLS = 1024
FLAT_ROW_BLOCK = 64
VMEM_LIMIT = 56 * 1024 * 1024

ADAM_LR = 0.001
ADAM_B1 = 0.9
ADAM_B2 = 0.999
ADAM_EPS = 1e-08
ADAM_WD = 0.01
ADAM_STEP = 10

NN = (((1,), (0,)), ((), ()))
NT = (((1,), (1,)), ((), ()))
TN = (((0,), (0,)), ((), ()))


def _params(*sem):
    return pltpu.CompilerParams(dimension_semantics=sem, vmem_limit_bytes=VMEM_LIMIT)


def _pick(n, cands):
    for c in cands:
        if n % c == 0:
            return c
    return n


def _bdot(a, b, dims=NN):
    return lax.dot_general(a.astype(BF16), b.astype(BF16), dims, preferred_element_type=F32)


def _hdot(a, b, dims=NN):
    return lax.dot_general(a, b, dims, preferred_element_type=F32, precision=HIGHEST)


def _split_dot(a, m, parts=2):
    out = None
    for _ in range(parts):
        piece = a.astype(BF16)
        a = a - piece.astype(F32)
        term = lax.dot_general(piece, m, NN, preferred_element_type=F32)
        out = term if out is None else out + term
    return out


def _lockstep(gens):
    while gens:
        alive = []
        for g in gens:
            try:
                next(g)
                alive.append(g)
            except StopIteration:
                pass
        gens = alive


def _sigmoid(x):
    return 1.0 / (1.0 + jnp.exp(-x))


def _iota2(shape, axis):
    return lax.broadcasted_iota(jnp.int32, shape, axis)


def _matmul(a, b, mode, name, out_dtype=None, add=None):
    if out_dtype is None:
        out_dtype = BF16 if mode == "tn" else F32
    if mode == "nn":
        (m, k), n = a.shape, b.shape[1]
    elif mode == "nt":
        (m, k), n = a.shape, b.shape[0]
    else:
        (k, m), n = a.shape, b.shape[1]
    row_c = (640, 512, 384, 256, 128)
    col_c = (1024, 1408, 768, 512, 384, 256, 128)
    if mode == "tn":
        bm, bn, bk = _pick(m, col_c), _pick(n, col_c), _pick(k, row_c)
    else:
        bm, bn, bk = _pick(m, row_c), _pick(n, col_c), _pick(k, (1024, 1408, 768, 512, 256, 128))
    nk = k // bk
    dims = {"nn": NN, "nt": NT, "tn": TN}[mode]
    a_spec = {"nn": pl.BlockSpec((bm, bk), lambda i, j, q: (i, q)),
              "nt": pl.BlockSpec((bm, bk), lambda i, j, q: (i, q)),
              "tn": pl.BlockSpec((bk, bm), lambda i, j, q: (q, i))}[mode]
    b_spec = {"nn": pl.BlockSpec((bk, bn), lambda i, j, q: (q, j)),
              "nt": pl.BlockSpec((bn, bk), lambda i, j, q: (j, q)),
              "tn": pl.BlockSpec((bk, bn), lambda i, j, q: (q, j))}[mode]
    o_spec = pl.BlockSpec((bm, bn), lambda i, j, q: (i, j))
    has_add = add is not None

    def body(*refs):
        if has_add:
            a_ref, b_ref, add_ref, o_ref, acc_ref = refs
        else:
            a_ref, b_ref, o_ref, acc_ref = refs
        q = pl.program_id(2)
        part = _bdot(a_ref[...], b_ref[...], dims)

        @pl.when(q == 0)
        def _():
            acc_ref[...] = part

        @pl.when(q > 0)
        def _():
            acc_ref[...] += part

        @pl.when(q == nk - 1)
        def _():
            r = acc_ref[...]
            if has_add:
                r = r + add_ref[...]
            o_ref[...] = r.astype(out_dtype)

    ins = [a, b] + ([add] if has_add else [])
    in_specs = [a_spec, b_spec] + ([o_spec] if has_add else [])
    return pl.pallas_call(
        body, name=name, grid=(m // bm, n // bn, nk),
        in_specs=in_specs, out_specs=o_spec,
        out_shape=jax.ShapeDtypeStruct((m, n), out_dtype),
        scratch_shapes=[pltpu.VMEM((bm, bn), F32)],
        compiler_params=_params("parallel", "parallel", "arbitrary"),
    )(*ins)


def _rmsnorm(h, gains, name):
    m, d = h.shape
    bm = _pick(m, (640, 384, 128))
    ng = len(gains)

    def body(*refs):
        h_ref, g_refs, o_refs = refs[0], refs[1:1 + ng], refs[1 + ng:]
        x = h_ref[...]
        xhat = x * lax.rsqrt(jnp.mean(x * x, axis=-1, keepdims=True) + NORM_EPS)
        for g_ref, o_ref in zip(g_refs, o_refs):
            o_ref[...] = (xhat * g_ref[...]).astype(BF16)

    row = pl.BlockSpec((bm, d), lambda i: (i, 0))
    gain = pl.BlockSpec((1, d), lambda i: (0, 0))
    return pl.pallas_call(
        body, name=name, grid=(m // bm,),
        in_specs=[row] + [gain] * ng, out_specs=[row] * ng,
        out_shape=[jax.ShapeDtypeStruct((m, d), BF16)] * ng,
        compiler_params=_params("parallel"),
    )(h, *gains)


def _rmsnorm_bwd(h, gains, dns, dres, name):
    m, d = h.shape
    bm = _pick(m, (640, 384, 128))
    ng = len(gains)
    nsteps = m // bm

    def body(*refs):
        h_ref, dres_ref = refs[0], refs[1]
        g_refs = refs[2:2 + ng]
        dn_refs = refs[2 + ng:2 + 2 * ng]
        dh_ref = refs[2 + 2 * ng]
        dg_refs = refs[3 + 2 * ng:3 + 3 * ng]
        acc_refs = refs[3 + 3 * ng:]
        i = pl.program_id(0)
        x = h_ref[...]
        r = lax.rsqrt(jnp.mean(x * x, axis=-1, keepdims=True) + NORM_EPS)
        xhat = x * r
        dh = dres_ref[...]
        for g_ref, dn_ref, dg_ref, acc_ref in zip(g_refs, dn_refs, dg_refs, acc_refs):
            dn = dn_ref[...]
            dy = dn * g_ref[...]
            dh = dh + r * (dy - xhat * jnp.mean(dy * xhat, axis=-1, keepdims=True))
            part = jnp.sum((dn * xhat).reshape(bm // 8, 8, d), axis=0)

            @pl.when(i == 0)
            def _():
                acc_ref[...] = part

            @pl.when(i > 0)
            def _():
                acc_ref[...] += part

            @pl.when(i == nsteps - 1)
            def _():
                dg_ref[...] = jnp.sum(acc_ref[...], axis=0, keepdims=True)

        dh_ref[...] = dh

    row = pl.BlockSpec((bm, d), lambda i: (i, 0))
    gain = pl.BlockSpec((1, d), lambda i: (0, 0))
    outs = pl.pallas_call(
        body, name=name, grid=(nsteps,),
        in_specs=[row, row] + [gain] * ng + [row] * ng,
        out_specs=[row] + [gain] * ng,
        out_shape=[jax.ShapeDtypeStruct((m, d), F32)] + [jax.ShapeDtypeStruct((1, d), F32)] * ng,
        scratch_shapes=[pltpu.VMEM((8, d), F32)] * ng,
        compiler_params=_params("arbitrary"),
    )(h, dres, *gains, *dns)
    return outs[0], list(outs[1:])


def _loss_head(h, gain, target):
    m, d = h.shape
    nsteps = m // LANES

    def body(h_ref, g_ref, t_ref, loss_ref, dh_ref, dg_ref, lacc, gacc):
        i = pl.program_id(0)

        @pl.when(i == 0)
        def _():
            lacc[...] = jnp.zeros_like(lacc)
            gacc[...] = jnp.zeros_like(gacc)
            dh_ref[...] = jnp.zeros_like(dh_ref)

        @pl.when(i > 0)
        def _():
            x = h_ref[...]
            r = lax.rsqrt(jnp.mean(x * x, axis=-1, keepdims=True) + NORM_EPS)
            xhat = x * r
            g = g_ref[...]
            err = xhat * g - t_ref[...]
            lacc[...] += jnp.sum((err * err).reshape(LANES // 8, 8, d), axis=0)
            dn = err * (1.0 / d)
            dy = dn * g
            dh_ref[...] = r * (dy - xhat * jnp.mean(dy * xhat, axis=-1, keepdims=True))
            gacc[...] += jnp.sum((dn * xhat).reshape(LANES // 8, 8, d), axis=0)

        @pl.when(i == nsteps - 1)
        def _():
            tot = jnp.sum(jnp.sum(lacc[...], axis=1, keepdims=True), axis=0, keepdims=True)
            loss_ref[...] = jnp.broadcast_to(tot * (0.5 / d), (1, LANES))
            dg_ref[...] = jnp.sum(gacc[...], axis=0, keepdims=True)

    row = pl.BlockSpec((LANES, d), lambda i: (i, 0))
    trow = pl.BlockSpec((LANES, d), lambda i: (jnp.maximum(i - 1, 0), 0))
    gain_spec = pl.BlockSpec((1, d), lambda i: (0, 0))
    return pl.pallas_call(
        body, name="loss_head", grid=(nsteps,),
        in_specs=[row, gain_spec, trow],
        out_specs=[pl.BlockSpec((1, LANES), lambda i: (0, 0)), row, gain_spec],
        out_shape=[jax.ShapeDtypeStruct((1, LANES), F32), jax.ShapeDtypeStruct((m, d), F32),
                   jax.ShapeDtypeStruct((1, d), F32)],
        scratch_shapes=[pltpu.VMEM((8, d), F32), pltpu.VMEM((8, d), F32)],
        compiler_params=_params("arbitrary"),
    )(h, gain, target)


def _swiglu(gu, name):
    m, f2 = gu.shape
    f = f2 // 2

    def body(gu_ref, o_ref):
        g = gu_ref[:, :f]
        u = gu_ref[:, f:]
        o_ref[...] = (g * _sigmoid(g) * u).astype(BF16)

    return pl.pallas_call(
        body, name=name, grid=(m // LANES,),
        in_specs=[pl.BlockSpec((LANES, f2), lambda i: (i, 0))],
        out_specs=pl.BlockSpec((LANES, f), lambda i: (i, 0)),
        out_shape=jax.ShapeDtypeStruct((m, f), BF16),
        compiler_params=_params("parallel"),
    )(gu)


def _swiglu_bwd(gu, da, name):
    m, f2 = gu.shape
    f = f2 // 2

    def body(gu_ref, da_ref, o_ref):
        g = gu_ref[:, :f]
        u = gu_ref[:, f:]
        da_ = da_ref[...]
        s = _sigmoid(g)
        o_ref[:, :f] = (da_ * u * s * (1.0 + g * (1.0 - s))).astype(BF16)
        o_ref[:, f:] = (da_ * g * s).astype(BF16)

    return pl.pallas_call(
        body, name=name, grid=(m // LANES,),
        in_specs=[pl.BlockSpec((LANES, f2), lambda i: (i, 0)), pl.BlockSpec((LANES, f), lambda i: (i, 0))],
        out_specs=pl.BlockSpec((LANES, f2), lambda i: (i, 0)),
        out_shape=jax.ShapeDtypeStruct((m, f2), BF16),
        compiler_params=_params("parallel"),
    )(gu, da)


def _conv_taps(xa, w_ref):
    acc = xa * w_ref[CONV_WIDTH - 1:CONV_WIDTH, :]
    for i in range(CONV_WIDTH - 1):
        acc = acc + pltpu.roll(xa, CONV_WIDTH - 1 - i, 0) * w_ref[i:i + 1, :]
    return acc[8:]


def _gdn_conv(proj, conv_w, n_heads):
    m, w3 = proj.shape
    wd = w3 // 3
    nblk = m // LANES

    def body(x_ref, w_ref, o_ref):
        c = pl.program_id(0)
        is_qk = c < 2 * n_heads

        def finish(cv):
            s = cv * _sigmoid(cv)
            nrm = s * lax.rsqrt(jnp.sum(s * s, axis=-1, keepdims=True) + NORM_EPS)
            return jnp.where(is_qk, nrm, s)

        x0 = jnp.concatenate([jnp.zeros((8, LANES), F32), x_ref[pl.ds(0, LANES), :]], axis=0)
        o_ref[pl.ds(0, LANES), :] = finish(_conv_taps(x0, w_ref))

        def step(b, carry):
            r0 = pl.multiple_of(b * LANES, LANES)
            xa = x_ref[pl.ds(r0 - 8, LANES + 8), :]
            o_ref[pl.ds(r0, LANES), :] = finish(_conv_taps(xa, w_ref))
            return carry

        lax.fori_loop(1, nblk, step, 0)

    return pl.pallas_call(
        body, name="gdn_conv", grid=(w3 // LANES,),
        in_specs=[pl.BlockSpec((m, LANES), lambda c: (0, c)), pl.BlockSpec((CONV_WIDTH, LANES), lambda c: (0, c))],
        out_specs=pl.BlockSpec((None, m, LANES), lambda c: (c // n_heads, 0, c % n_heads)),
        out_shape=jax.ShapeDtypeStruct((3, m, wd), F32),
        compiler_params=_params("parallel"),
    )(proj, conv_w)


def _gdn_conv_bwd(proj, conv_w, dact, n_heads):
    m, w3 = proj.shape
    nblk = m // LANES

    def body(x_ref, w_ref, dy_ref, dx_ref, dw_ref, dc_ref):
        c = pl.program_id(0)
        is_qk = c < 2 * n_heads
        dc_ref[pl.ds(m, 8), :] = jnp.zeros((8, LANES), F32)

        def pre_act(xa):
            cv = _conv_taps(xa, w_ref)
            sg = _sigmoid(cv)
            s = cv * sg
            return cv, sg, s

        def dconv_in(b, xa, dwacc):
            r0 = pl.multiple_of(b * LANES, LANES)
            cv, sg, s = pre_act(xa)
            dy = dy_ref[pl.ds(r0, LANES), :]
            rn = lax.rsqrt(jnp.sum(s * s, axis=-1, keepdims=True) + NORM_EPS)
            yn = s * rn
            ds_qk = rn * (dy - yn * jnp.sum(dy * yn, axis=-1, keepdims=True))
            ds = jnp.where(is_qk, ds_qk, dy)
            dcv = ds * sg * (1.0 + cv * (1.0 - sg))
            dc_ref[pl.ds(r0, LANES), :] = dcv
            new = []
            for i in range(CONV_WIDTH):
                xs = xa[8:] if i == CONV_WIDTH - 1 else pltpu.roll(xa, CONV_WIDTH - 1 - i, 0)[8:]
                new.append(dwacc[i] + jnp.sum((dcv * xs).reshape(LANES // 8, 8, LANES), axis=0))
            return tuple(new)

        x0 = jnp.concatenate([jnp.zeros((8, LANES), F32), x_ref[pl.ds(0, LANES), :]], axis=0)
        zero = jnp.zeros((8, LANES), F32)
        dwacc = dconv_in(0, x0, (zero,) * CONV_WIDTH)

        def step1(b, acc):
            r0 = pl.multiple_of(b * LANES, LANES)
            return dconv_in(b, x_ref[pl.ds(r0 - 8, LANES + 8), :], acc)

        dwacc = lax.fori_loop(1, nblk, step1, dwacc)
        for i in range(CONV_WIDTH):
            dw_ref[i:i + 1, :] = jnp.sum(dwacc[i], axis=0, keepdims=True)

        def step2(b, carry):
            r0 = pl.multiple_of(b * LANES, LANES)
            da = dc_ref[pl.ds(r0, LANES + 8), :]
            acc = da * w_ref[CONV_WIDTH - 1:CONV_WIDTH, :]
            for i in range(CONV_WIDTH - 1):
                sh = CONV_WIDTH - 1 - i
                acc = acc + pltpu.roll(da, LANES + 8 - sh, 0) * w_ref[i:i + 1, :]
            dx_ref[pl.ds(r0, LANES), :] = acc[:LANES]
            return carry

        lax.fori_loop(0, nblk, step2, 0)

    return pl.pallas_call(
        body, name="gdn_conv_bwd", grid=(w3 // LANES,),
        in_specs=[pl.BlockSpec((m, LANES), lambda c: (0, c)),
                  pl.BlockSpec((CONV_WIDTH, LANES), lambda c: (0, c)),
                  pl.BlockSpec((None, m, LANES), lambda c: (c // n_heads, 0, c % n_heads))],
        out_specs=[pl.BlockSpec((m, LANES), lambda c: (0, c)), pl.BlockSpec((CONV_WIDTH, LANES), lambda c: (0, c))],
        out_shape=[jax.ShapeDtypeStruct((m, w3), F32), jax.ShapeDtypeStruct((CONV_WIDTH, w3), F32)],
        scratch_shapes=[pltpu.VMEM((m + 8, LANES), F32)],
        compiler_params=_params("parallel"),
    )(proj, conv_w, dact)


def _softplus(x):
    return jnp.maximum(x, 0.0) + jnp.log(1.0 + jnp.exp(-jnp.abs(x)))


def _gdn_gates(ab, a_log, dt_bias, n_heads):
    m = ab.shape[0]
    bm = _pick(m, (640, 384, 128))

    def body(ab_ref, al_ref, dt_ref, g_ref, b_ref):
        i = pl.program_id(0)
        x = ab_ref[...]
        live = (_iota2((bm, LANES), 0) + i * bm) >= PAD_ROWS
        g = jnp.where(live, -jnp.exp(al_ref[...]) * _softplus(x + dt_ref[...]), 0.0)
        beta = jnp.where(live, _sigmoid(x), 0.0)
        for h in range(n_heads):
            g_ref[:, h * LANES:(h + 1) * LANES] = jnp.broadcast_to(g[:, h:h + 1], (bm, LANES))
            b_ref[:, h * LANES:(h + 1) * LANES] = jnp.broadcast_to(
                beta[:, n_heads + h:n_heads + h + 1], (bm, LANES))

    wide = pl.BlockSpec((bm, n_heads * LANES), lambda i: (i, 0))
    return pl.pallas_call(
        body, name="gdn_gates", grid=(m // bm,),
        in_specs=[pl.BlockSpec((bm, LANES), lambda i: (i, 0))] + [pl.BlockSpec((1, LANES), lambda i: (0, 0))] * 2,
        out_specs=[wide, wide],
        out_shape=[jax.ShapeDtypeStruct((m, n_heads * LANES), F32)] * 2,
        compiler_params=_params("parallel"),
    )(ab, a_log, dt_bias)


def _gdn_gates_bwd(ab, a_log, dt_bias, dg_b, dbeta_b, n_heads):
    m = ab.shape[0]
    bm = _pick(m, (640, 384, 128))
    nsteps = m // bm

    def body(ab_ref, al_ref, dt_ref, dg_ref, db_ref, dab_ref, dal_ref, ddt_ref, acc_al, acc_dt):
        i = pl.program_id(0)
        x = ab_ref[...]
        lane = _iota2((bm, LANES), 1)
        live = (_iota2((bm, LANES), 0) + i * bm) >= PAD_ROWS
        dg = jnp.zeros((bm, LANES), F32)
        dbeta = jnp.zeros((bm, LANES), F32)
        for h in range(n_heads):
            dg = jnp.where(lane == h, dg_ref[:, h * LANES:(h + 1) * LANES], dg)
            dbeta = jnp.where(lane == n_heads + h, db_ref[:, h * LANES:(h + 1) * LANES], dbeta)
        dg = jnp.where(live, dg, 0.0)
        dbeta = jnp.where(live, dbeta, 0.0)
        nea = -jnp.exp(al_ref[...])
        pre = x + dt_ref[...]
        d_pre = dg * nea * _sigmoid(pre)
        beta = _sigmoid(x)
        dab_ref[...] = d_pre + dbeta * beta * (1.0 - beta)
        p_al = jnp.sum((dg * nea * _softplus(pre)).reshape(bm // 8, 8, LANES), axis=0)
        p_dt = jnp.sum(d_pre.reshape(bm // 8, 8, LANES), axis=0)

        @pl.when(i == 0)
        def _():
            acc_al[...] = p_al
            acc_dt[...] = p_dt

        @pl.when(i > 0)
        def _():
            acc_al[...] += p_al
            acc_dt[...] += p_dt

        @pl.when(i == nsteps - 1)
        def _():
            dal_ref[...] = jnp.sum(acc_al[...], axis=0, keepdims=True)
            ddt_ref[...] = jnp.sum(acc_dt[...], axis=0, keepdims=True)

    wide = pl.BlockSpec((bm, n_heads * LANES), lambda i: (i, 0))
    one = pl.BlockSpec((1, LANES), lambda i: (0, 0))
    nar = pl.BlockSpec((bm, LANES), lambda i: (i, 0))
    return pl.pallas_call(
        body, name="gdn_gates_bwd", grid=(nsteps,),
        in_specs=[nar, one, one, wide, wide],
        out_specs=[nar, one, one],
        out_shape=[jax.ShapeDtypeStruct((m, LANES), F32), jax.ShapeDtypeStruct((1, LANES), F32),
                   jax.ShapeDtypeStruct((1, LANES), F32)],
        scratch_shapes=[pltpu.VMEM((8, LANES), F32)] * 2,
        compiler_params=_params("arbitrary"),
    )(ab, a_log, dt_bias, dg_b, dbeta_b)


def _chunk_masks():
    c = GDN_CHUNK
    ri, ci = _iota2((c, c), 0), _iota2((c, c), 1)
    return ri >= ci, ri > ci, ri == ci


def _chunk_decay(gbs):
    c = GDN_CHUNK
    incl, _, _ = _chunk_masks()
    inclf = incl.astype(F32)
    ones = jnp.ones((c, LANES), F32)
    g_cums = [_hdot(inclf, gb) for gb in gbs]
    g_rows = [_hdot(ones, g_cum, NT) * (1.0 / LANES) for g_cum in g_cums]
    dmats = [jnp.where(incl, jnp.exp(jnp.where(incl, g_cum[:, :c] - g_row, 0.0)), 0.0)
             for g_cum, g_row in zip(g_cums, g_rows)]
    return g_cums, dmats


def _unit_lower_inverse(mats):
    _, _, eye = _chunk_masks()
    xs = [eye.astype(F32) - a for a in mats]
    ps = list(mats)
    for _ in range(int(math.log2(GDN_CHUNK)) - 1):
        ps = [_hdot(p, p) for p in ps]
        xs = [x + _hdot(x, p) for x, p in zip(xs, ps)]
    return xs


def _gdn_prep(qkv, g_b, beta_b, n_heads):
    _, m, wd = qkv.shape
    c = GDN_CHUNK
    nc = m // c
    cb = _pick(nc, (10, 5, 3, 2))
    rows = cb * c
    qscale = GDN_HEAD_DIM ** -0.5

    def body(qkv_ref, g_ref, b_ref, ub_ref, wc_ref, qd_ref, kd_ref, gc_ref, p_ref, t_ref):
        _, strict, _ = _chunk_masks()
        sls = [pl.ds(s * c, c) for s in range(cb)]
        g_cums, dmats = _chunk_decay([g_ref[sl, :] for sl in sls])
        kks = [_bdot(qkv_ref[1, sl, :], qkv_ref[1, sl, :], NT) for sl in sls]
        ts = _unit_lower_inverse([jnp.where(strict, b_ref[sl, :][:, :c] * dmat * kk, 0.0)
                                  for sl, dmat, kk in zip(sls, dmats, kks)])
        for s, (sl, g_cum, dmat, t) in enumerate(zip(sls, g_cums, dmats, ts)):
            q = qkv_ref[0, sl, :] * qscale
            k = qkv_ref[1, sl, :]
            bb = b_ref[sl, :]
            gam = jnp.exp(g_cum)
            ub_ref[sl, :] = _hdot(t, bb * qkv_ref[2, sl, :])
            wc_ref[sl, :] = _hdot(t, bb * gam * k)
            p_ref[s] = _bdot(q, k, NT) * dmat
            qd_ref[sl, :] = q * gam
            kd_ref[sl, :] = k * jnp.exp(g_cum[c - 1:c, :] - g_cum)
            gc_ref[sl, :] = g_cum
            t_ref[s] = t

    blk = pl.BlockSpec((rows, LANES), lambda h, n: (n, h))
    sq = pl.BlockSpec((None, cb, c, c), lambda h, n: (h, n, 0, 0))
    big = jax.ShapeDtypeStruct((m, wd), F32)
    small = jax.ShapeDtypeStruct((n_heads, nc, c, c), F32)
    return pl.pallas_call(
        body, name="gdn_prep", grid=(n_heads, nc // cb),
        in_specs=[pl.BlockSpec((3, rows, LANES), lambda h, n: (0, n, h)), blk, blk],
        out_specs=[blk] * 5 + [sq, sq],
        out_shape=[big] * 5 + [small, small],
        compiler_params=_params("parallel", "parallel"),
    )(qkv, g_b, beta_b)


def _gdn_prep_bwd(qkv, g_b, beta_b, ub, wc, tinv, dub, dwc, dqd, dkd, dp, dgl, n_heads):
    _, m, wd = qkv.shape
    c = GDN_CHUNK
    nc = m // c
    cb = _pick(nc, (5, 3, 2))
    rows = cb * c
    qscale = GDN_HEAD_DIM ** -0.5

    def body(qkv_ref, g_ref, b_ref, ub_ref, wc_ref, t_ref, dub_ref, dwc_ref, dqd_ref, dkd_ref, dp_ref, dgl_ref,
             dqkv_ref, dg_ref, db_ref):
        incl, strict, _ = _chunk_masks()
        ones_c = jnp.ones((c, LANES), F32)
        ones_l = jnp.ones((LANES, LANES), F32)
        last_row = _iota2((c, LANES), 0) == c - 1

        def chunk(s):
            sl = pl.ds(s * c, c)
            q = qkv_ref[0, sl, :] * qscale
            k = qkv_ref[1, sl, :]
            v = qkv_ref[2, sl, :]
            bb = b_ref[sl, :]
            bcol = bb[:, :c]
            t = t_ref[s]
            g_cum = _hdot(incl.astype(F32), g_ref[sl, :])
            kk = _bdot(k, k, NT)
            qk = _bdot(q, k, NT)
            drhs_u = _hdot(t, dub_ref[sl, :], TN)
            drhs_w = _hdot(t, dwc_ref[sl, :], TN)
            yield
            g_row = _hdot(ones_c, g_cum, NT) * (1.0 / LANES)
            da = -jnp.where(strict, _bdot(drhs_u, ub_ref[sl, :], NT) + _bdot(drhs_w, wc_ref[sl, :], NT), 0.0)
            dbeta = _hdot(drhs_u * v, ones_l)
            tk = _hdot(drhs_w * k, ones_l)
            dqd = dqd_ref[sl, :]
            dgam = _hdot(dqd * q, ones_l)
            yield
            dmat = jnp.where(incl, jnp.exp(jnp.where(incl, g_cum[:, :c] - g_row, 0.0)), 0.0)
            gam = jnp.exp(g_cum)
            edec = jnp.exp(g_cum[c - 1:c, :] - g_cum)
            dkd = dkd_ref[sl, :]
            skd = _hdot(dkd * k * edec, ones_l)
            dbeta = dbeta + _hdot(da * dmat * kk, ones_c)
            dkk = da * bcol * dmat
            dpm = jnp.where(incl, dp_ref[s], 0.0)
            dpd = dpm * dmat
            dk = _bdot(dkk, k) + _bdot(dkk, k, TN) + _bdot(dpd, q, TN)
            dq = _bdot(dpd, k)
            e = (da * bcol * kk + dpm * qk) * dmat
            yield
            dk = dk + bb * gam * drhs_w + dkd * edec
            dq = dq + dqd * gam
            dbeta = dbeta + tk * gam
            dgam = dgam + bb * tk
            dgc = dgam * gam - skd + _hdot(e, ones_c) - _hdot(e, ones_c, TN)
            tot = _hdot(jnp.ones((c, c), F32), skd)
            yield
            dgc = dgc + jnp.where(last_row, tot + dgl_ref[s, 0:1, :], 0.0)
            dg_ref[sl, :] = _hdot((_iota2((c, c), 0) <= _iota2((c, c), 1)).astype(F32), dgc)
            db_ref[sl, :] = dbeta
            dqkv_ref[0, sl, :] = dq * qscale
            dqkv_ref[1, sl, :] = dk
            dqkv_ref[2, sl, :] = bb * drhs_u

        _lockstep([chunk(s) for s in range(cb)])

    blk = pl.BlockSpec((rows, LANES), lambda h, n: (n, h))
    blk3 = pl.BlockSpec((3, rows, LANES), lambda h, n: (0, n, h))
    sq = pl.BlockSpec((None, cb, c, c), lambda h, n: (h, n, 0, 0))
    sgl = pl.BlockSpec((None, cb, 8, LANES), lambda h, n: (h, n, 0, 0))
    big = jax.ShapeDtypeStruct((m, wd), F32)
    return pl.pallas_call(
        body, name="gdn_prep_bwd", grid=(n_heads, nc // cb),
        in_specs=[blk3, blk, blk, blk, blk, sq, blk, blk, blk, blk, sq, sgl],
        out_specs=[blk3, blk, blk],
        out_shape=[jax.ShapeDtypeStruct((3, m, wd), F32), big, big],
        compiler_params=_params("parallel", "parallel"),
    )(qkv, g_b, beta_b, ub, wc, tinv, dub, dwc, dqd, dkd, dp, dgl)


def _gdn_scan(ub, wc, qd, kd, gc, p, n_heads):
    m, wd = ub.shape
    c = GDN_CHUNK
    nc = m // c
    dh = GDN_HEAD_DIM

    def body(ub_ref, wc_ref, qd_ref, kd_ref, gc_ref, p_ref, o_ref, ssave_ref, s_ref):
        n = pl.program_id(0)

        @pl.when(n == 0)
        def _():
            s_ref[...] = jnp.zeros_like(s_ref)

        def head(h):
            sl = slice(h * LANES, (h + 1) * LANES)
            s = s_ref[h]
            ssave_ref[h] = s
            ws = _bdot(wc_ref[:, sl], s)
            qs = _bdot(qd_ref[:, sl], s)
            yield
            u = ub_ref[:, sl] - ws
            pu = _bdot(p_ref[h], u)
            ku = _bdot(kd_ref[:, sl], u, TN)
            yield
            o_ref[:, sl] = qs + pu
            s_ref[h] = s * jnp.exp(gc_ref[c - 1:c, sl]) + ku

        _lockstep([head(h) for h in range(n_heads)])

    blk = pl.BlockSpec((c, wd), lambda n: (n, 0))
    return pl.pallas_call(
        body, name="gdn_scan", grid=(nc,),
        in_specs=[blk] * 5 + [pl.BlockSpec((n_heads, None, c, c), lambda n: (0, n, 0, 0))],
        out_specs=[blk, pl.BlockSpec((n_heads, None, dh, dh), lambda n: (0, n, 0, 0))],
        out_shape=[jax.ShapeDtypeStruct((m, wd), F32), jax.ShapeDtypeStruct((n_heads, nc, dh, dh), F32)],
        scratch_shapes=[pltpu.VMEM((n_heads, dh, dh), F32)],
        compiler_params=_params("arbitrary"),
    )(ub, wc, qd, kd, gc, p)


def _gdn_scan_bwd(do, ssave, ub, wc, qd, kd, gc, p, n_heads):
    m, wd = ub.shape
    c = GDN_CHUNK
    nc = m // c
    dh = GDN_HEAD_DIM

    def body(do_ref, ss_ref, ub_ref, wc_ref, qd_ref, kd_ref, gc_ref, p_ref,
             dub_ref, dwc_ref, dqd_ref, dkd_ref, dp_ref, dgl_ref, ds_ref):
        n = pl.program_id(0)

        @pl.when(n == 0)
        def _():
            ds_ref[...] = jnp.zeros_like(ds_ref)

        def head(h):
            sl = slice(h * LANES, (h + 1) * LANES)
            ds_next = ds_ref[h]
            s = ss_ref[h]
            do_ = do_ref[:, sl]
            wcv = wc_ref[:, sl]
            kdv = kd_ref[:, sl]
            ws = _bdot(wcv, s)
            du = _bdot(p_ref[h], do_, TN) + _bdot(kdv, ds_next)
            dqd_ref[:, sl] = _bdot(do_, s, NT)
            col = _hdot(jnp.ones((8, dh), F32), s * ds_next)
            qdo = _bdot(qd_ref[:, sl], do_, TN)
            yield
            u = ub_ref[:, sl] - ws
            egl = jnp.exp(gc_ref[c - 1:c, sl])
            dub_ref[:, sl] = du
            dwc_ref[:, sl] = -_bdot(du, s, NT)
            dp_ref[h] = _bdot(do_, u, NT)
            dkd_ref[:, sl] = _bdot(u, ds_next, NT)
            dgl_ref[h] = egl * _hdot(col, jnp.ones((LANES, LANES), F32))
            ds_ref[h] = qdo + egl * ds_next - _bdot(wcv, du, TN)

        _lockstep([head(h) for h in range(n_heads)])

    blk = pl.BlockSpec((c, wd), lambda n: (nc - 1 - n, 0))
    sq = pl.BlockSpec((n_heads, None, c, c), lambda n: (0, nc - 1 - n, 0, 0))
    big = jax.ShapeDtypeStruct((m, wd), F32)
    return pl.pallas_call(
        body, name="gdn_scan_bwd", grid=(nc,),
        in_specs=[blk, pl.BlockSpec((n_heads, None, dh, dh), lambda n: (0, nc - 1 - n, 0, 0))] + [blk] * 5 + [sq],
        out_specs=[blk] * 4 + [sq, pl.BlockSpec((n_heads, None, 8, LANES), lambda n: (0, nc - 1 - n, 0, 0))],
        out_shape=[big] * 4 + [jax.ShapeDtypeStruct((n_heads, nc, c, c), F32),
                               jax.ShapeDtypeStruct((n_heads, nc, 8, LANES), F32)],
        scratch_shapes=[pltpu.VMEM((n_heads, dh, dh), F32)],
        compiler_params=_params("arbitrary"),
    )(do, ssave, ub, wc, qd, kd, gc, p)


def _gdn_outnorm(o, gate, gain, n_heads):
    m, wd = o.shape
    bm = _pick(m, (640, 384, 128))

    def body(o_ref, gt_ref, gn_ref, y_ref):
        gn = gn_ref[...]
        for h in range(n_heads):
            sl = slice(h * LANES, (h + 1) * LANES)
            x = o_ref[:, sl]
            gt = gt_ref[:, sl]
            r = lax.rsqrt(jnp.mean(x * x, axis=-1, keepdims=True) + NORM_EPS)
            y_ref[:, sl] = (x * r * gn * gt * _sigmoid(gt)).astype(BF16)

    row = pl.BlockSpec((bm, wd), lambda i: (i, 0))
    return pl.pallas_call(
        body, name="gdn_outnorm", grid=(m // bm,),
        in_specs=[row, row, pl.BlockSpec((1, LANES), lambda i: (0, 0))], out_specs=row,
        out_shape=jax.ShapeDtypeStruct((m, wd), BF16),
        compiler_params=_params("parallel"),
    )(o, gate, gain)


def _gdn_outnorm_bwd(o, gate, gain, dy, n_heads):
    m, wd = o.shape
    bm = _pick(m, (640, 384, 128))
    nsteps = m // bm

    def body(o_ref, gt_ref, gn_ref, dy_ref, do_ref, dgt_ref, dgn_ref, acc_ref):
        i = pl.program_id(0)
        gn = gn_ref[...]
        part = jnp.zeros((8, LANES), F32)
        for h in range(n_heads):
            sl = slice(h * LANES, (h + 1) * LANES)
            x = o_ref[:, sl]
            gt = gt_ref[:, sl]
            d_out = dy_ref[:, sl]
            r = lax.rsqrt(jnp.mean(x * x, axis=-1, keepdims=True) + NORM_EPS)
            xhat = x * r
            sg = _sigmoid(gt)
            dz = d_out * gt * sg
            dgt_ref[:, sl] = d_out * xhat * gn * sg * (1.0 + gt * (1.0 - sg))
            dxh = dz * gn
            do_ref[:, sl] = r * (dxh - xhat * jnp.mean(dxh * xhat, axis=-1, keepdims=True))
            part = part + jnp.sum((dz * xhat).reshape(bm // 8, 8, LANES), axis=0)

        @pl.when(i == 0)
        def _():
            acc_ref[...] = part

        @pl.when(i > 0)
        def _():
            acc_ref[...] += part

        @pl.when(i == nsteps - 1)
        def _():
            dgn_ref[...] = jnp.sum(acc_ref[...], axis=0, keepdims=True)

    row = pl.BlockSpec((bm, wd), lambda i: (i, 0))
    one = pl.BlockSpec((1, LANES), lambda i: (0, 0))
    return pl.pallas_call(
        body, name="gdn_outnorm_bwd", grid=(nsteps,),
        in_specs=[row, row, one, row], out_specs=[row, row, one],
        out_shape=[jax.ShapeDtypeStruct((m, wd), F32)] * 2 + [jax.ShapeDtypeStruct((1, LANES), F32)],
        scratch_shapes=[pltpu.VMEM((8, LANES), F32)],
        compiler_params=_params("arbitrary"),
    )(o, gate, gain, dy)


def _sb_consts(kind):
    ri, ci = _iota2((LANES, LANES), 0), _iota2((LANES, LANES), 1)
    tri = {"suffix_excl": ri > ci, "prefix_incl": ri <= ci, "prefix_excl": ri < ci}[kind]
    return jnp.concatenate([tri.astype(BF16), jnp.ones((LANES, LANES), BF16)], axis=1)


def _sb_stack(a):
    first = _iota2((LANES, LANES), 1) < SB_HEAD_DIM
    return jnp.concatenate([jnp.where(first, a, 0.0), jnp.where(first, 0.0, a)], axis=0)


def _sb_unstack(a2):
    first = _iota2((LANES, LANES), 1) < SB_HEAD_DIM
    return jnp.where(first, a2[:LANES], a2[LANES:])


def _sb_visible(qi, kb, live):
    kpos = _iota2((2 * LANES, LANES), 1) + kb * LANES
    qpos = (_iota2((2 * LANES, LANES), 0) & (LANES - 1)) + qi * LANES
    return (kpos < qpos) & (kpos >= PAD_ROWS) & live


def _sb_log_gates(z, vis):
    t = jnp.log(1.0 + jnp.exp(-jnp.abs(z)))
    lb = jnp.minimum(z, 0.0) - t
    lk = lb - z
    if vis is not None:
        lk = jnp.where(vis, lk, 0.0)
    return lb, lk


def _sb_attention(q, kv):
    m, wd = q.shape
    npair = wd // LANES
    nq = m // LANES
    scale = SB_HEAD_DIM ** -0.5

    def body(q_ref, k_ref, v_ref, o_ref, w_ref, b_ref):
        qi = pl.program_id(1)
        su = _sb_consts("suffix_excl")
        q2 = _sb_stack(q_ref[...] * scale).astype(BF16)

        def tiles(kbs, carry, live):
            run, acc = carry
            r0s = [pl.multiple_of(kb * LANES, LANES) for kb in kbs]
            zs = [lax.dot_general(q2, k_ref[pl.ds(r0, LANES), :], NT, preferred_element_type=F32) for r0 in r0s]
            parts = []
            for kb, r0, z in zip(kbs, r0s, zs):
                vis = None if live is None else _sb_visible(qi, kb, live)
                lb, lk = _sb_log_gates(z, vis)
                parts.append((kb, r0, vis, lb, lk))
            parts = [(kb, r0, vis, lb, _split_dot(lk, su)) for kb, r0, vis, lb, lk in parts]
            for kb, r0, vis, lb, cs in parts:
                w = jnp.exp(lb + cs[:, :LANES] + run)
                if vis is not None:
                    w = jnp.where(vis, w, 0.0)
                run = run + cs[:, LANES:]
                wb = w.astype(BF16)
                w_ref[kb] = wb
                b_ref[kb] = jnp.exp(lb).astype(BF16)
                acc = acc + lax.dot_general(wb, v_ref[pl.ds(r0, LANES), :], NN, preferred_element_type=F32)
            return run, acc

        su2 = (_iota2((2 * LANES, 2 * LANES), 0) > _iota2((2 * LANES, 2 * LANES), 1)).astype(BF16)

        def pairs(his, carry):
            run, acc = carry
            r0s = [pl.multiple_of((hi - 1) * LANES, LANES) for hi in his]
            zs = [lax.dot_general(q2, k_ref[pl.ds(r0, 2 * LANES), :], NT, preferred_element_type=F32) for r0 in r0s]
            gates = [_sb_log_gates(z, None) for z in zs]
            css = [_split_dot(lk, su2) for _, lk in gates]
            for hi, r0, (lb, lk), cs in zip(his, r0s, gates, css):
                w = jnp.exp(lb + cs + jnp.concatenate([run, run], axis=1))
                run = run + jnp.broadcast_to(cs[:, 0:1] + lk[:, 0:1], (2 * LANES, LANES))
                wb = w.astype(BF16)
                bb = jnp.exp(lb).astype(BF16)
                w_ref[hi - 1], w_ref[hi] = wb[:, :LANES], wb[:, LANES:]
                b_ref[hi - 1], b_ref[hi] = bb[:, :LANES], bb[:, LANES:]
                acc = acc + lax.dot_general(wb, v_ref[pl.ds(r0, 2 * LANES), :], NN, preferred_element_type=F32)
            return run, acc

        zero = jnp.zeros((2 * LANES, LANES), F32)
        carry = tiles([qi], (zero, zero), True)
        n_mid = jnp.maximum(qi - 1, 0)
        n8 = n_mid // 8
        carry = lax.fori_loop(0, n8, lambda j, c: pairs([qi - 1 - 8 * j - 2 * u for u in range(4)], c), carry)
        top = qi - 1 - 8 * n8
        take4, take2, take1 = (n_mid >> 2) & 1, (n_mid >> 1) & 1, n_mid & 1
        carry = lax.fori_loop(0, take4, lambda j, c: pairs([top, top - 2], c), carry)
        top = top - 4 * take4
        carry = lax.fori_loop(0, take2, lambda j, c: pairs([top], c), carry)
        carry = lax.fori_loop(0, take1, lambda j, c: tiles([1], c, None), carry)

        @pl.when(qi >= 1)
        def _():
            _, acc = tiles([0], carry, True)
            o_ref[...] = _sb_unstack(acc)

        @pl.when(qi == 0)
        def _():
            o_ref[...] = _sb_unstack(carry[1])

    qspec = pl.BlockSpec((LANES, LANES), lambda hp, qi: (qi, hp))
    saved = pl.BlockSpec((None, None, nq, 2 * LANES, LANES), lambda hp, qi: (hp, qi, 0, 0, 0))
    saved_shape = jax.ShapeDtypeStruct((npair, nq, nq, 2 * LANES, LANES), BF16)
    return pl.pallas_call(
        body, name="sb_attention", grid=(npair, nq),
        in_specs=[qspec, pl.BlockSpec((m, LANES), lambda hp, qi: (0, hp)),
                  pl.BlockSpec((m, LANES), lambda hp, qi: (0, npair + hp))],
        out_specs=[qspec, saved, saved],
        out_shape=[jax.ShapeDtypeStruct((m, wd), F32), saved_shape, saved_shape],
        compiler_params=_params("parallel", "arbitrary"),
    )(q, kv, kv)


def _sb_attention_bwd(q, kv, w_all, b_all, do):
    m, wd = q.shape
    npair = wd // LANES
    nq = m // LANES
    scale = SB_HEAD_DIM ** -0.5

    def body(q_ref, k_ref, v_ref, w_ref, b_ref, do_ref, dq_ref, dk_ref, dv_ref):
        qi = pl.program_id(1)

        @pl.when(qi == 0)
        def _():
            dk_ref[...] = jnp.zeros_like(dk_ref)
            dv_ref[...] = jnp.zeros_like(dv_ref)

        px = _sb_consts("prefix_excl")
        q2 = _sb_stack(q_ref[...] * scale).astype(BF16)
        do2 = _sb_stack(do_ref[...]).astype(BF16)

        def tiles(kbs, carry, live):
            run_p, dq_acc = carry
            r0s = [pl.multiple_of(kb * LANES, LANES) for kb in kbs]
            das = [lax.dot_general(do2, v_ref[pl.ds(r0, LANES), :], NT, preferred_element_type=F32) for r0 in r0s]
            wbs = [w_ref[kb] for kb in kbs]
            for wb, r0 in zip(wbs, r0s):
                dv_ref[pl.ds(r0, LANES), :] += lax.dot_general(wb, do2, TN, preferred_element_type=F32)
            pws = [wb.astype(F32) * da for wb, da in zip(wbs, das)]
            pss = [_split_dot(pw, px) for pw in pws]
            dzs = []
            for kb, pw, ps in zip(kbs, pws, pss):
                beta = b_ref[kb].astype(F32)
                dz = pw * (1.0 - beta) - beta * (run_p + ps[:, :LANES])
                if live is not None:
                    dz = jnp.where(_sb_visible(qi, kb, live), dz, 0.0)
                run_p = run_p + ps[:, LANES:]
                dzs.append(dz.astype(BF16))
            for r0, dz in zip(r0s, dzs):
                dq_acc = dq_acc + lax.dot_general(dz, k_ref[pl.ds(r0, LANES), :], NN, preferred_element_type=F32)
                dk_ref[pl.ds(r0, LANES), :] += lax.dot_general(dz, q2, TN, preferred_element_type=F32)
            return run_p, dq_acc

        px2 = (_iota2((2 * LANES, 2 * LANES), 0) < _iota2((2 * LANES, 2 * LANES), 1)).astype(BF16)

        def pairs(los, carry):
            run_p, dq_acc = carry
            r0s = [pl.multiple_of(lo * LANES, LANES) for lo in los]
            das = [lax.dot_general(do2, v_ref[pl.ds(r0, 2 * LANES), :], NT, preferred_element_type=F32) for r0 in r0s]
            wbs = [jnp.concatenate([w_ref[lo], w_ref[lo + 1]], axis=1) for lo in los]
            for wb, r0 in zip(wbs, r0s):
                dv_ref[pl.ds(r0, 2 * LANES), :] += lax.dot_general(wb, do2, TN, preferred_element_type=F32)
            pws = [wb.astype(F32) * da for wb, da in zip(wbs, das)]
            pss = [_split_dot(pw, px2) for pw in pws]
            dzs = []
            for lo, pw, ps in zip(los, pws, pss):
                beta = jnp.concatenate([b_ref[lo], b_ref[lo + 1]], axis=1).astype(F32)
                dz = pw * (1.0 - beta) - beta * (jnp.concatenate([run_p, run_p], axis=1) + ps)
                last = 2 * LANES - 1
                run_p = run_p + jnp.broadcast_to(ps[:, last:] + pw[:, last:], (2 * LANES, LANES))
                dzs.append(dz.astype(BF16))
            for r0, dz in zip(r0s, dzs):
                dq_acc = dq_acc + lax.dot_general(dz, k_ref[pl.ds(r0, 2 * LANES), :], NN, preferred_element_type=F32)
                dk_ref[pl.ds(r0, 2 * LANES), :] += lax.dot_general(dz, q2, TN, preferred_element_type=F32)
            return run_p, dq_acc

        zero = jnp.zeros((2 * LANES, LANES), F32)
        carry = tiles([0], (zero, zero), True)
        n_mid = jnp.maximum(qi - 1, 0)
        n8 = n_mid // 8
        carry = lax.fori_loop(0, n8, lambda j, c: pairs([1 + 8 * j + 2 * u for u in range(4)], c), carry)
        low = 1 + 8 * n8
        take4, take2, take1 = (n_mid >> 2) & 1, (n_mid >> 1) & 1, n_mid & 1
        carry = lax.fori_loop(0, take4, lambda j, c: pairs([low, low + 2], c), carry)
        low = low + 4 * take4
        carry = lax.fori_loop(0, take2, lambda j, c: pairs([low], c), carry)
        low = low + 2 * take2
        carry = lax.fori_loop(0, take1, lambda j, c: tiles([low], c, None), carry)

        @pl.when(qi >= 1)
        def _():
            _, dq_acc = tiles([qi], carry, True)
            dq_ref[...] = _sb_unstack(dq_acc) * scale

        @pl.when(qi == 0)
        def _():
            dq_ref[...] = _sb_unstack(carry[1]) * scale

    qspec = pl.BlockSpec((LANES, LANES), lambda hp, qi: (qi, hp))
    strip = pl.BlockSpec((m, LANES), lambda hp, qi: (0, hp))
    saved = pl.BlockSpec((None, None, nq, 2 * LANES, LANES), lambda hp, qi: (hp, qi, 0, 0, 0))
    big = jax.ShapeDtypeStruct((m, wd), F32)
    return pl.pallas_call(
        body, name="sb_attention_bwd", grid=(npair, nq),
        in_specs=[qspec, strip, pl.BlockSpec((m, LANES), lambda hp, qi: (0, npair + hp)), saved, saved, qspec],
        out_specs=[qspec, strip, strip],
        out_shape=[big, big, big],
        compiler_params=_params("parallel", "arbitrary"),
    )(q, kv, kv, w_all, b_all, do)


def _mesh_pos():
    return lax.axis_index("x"), lax.axis_index("y"), lax.axis_index("c")


def _exchange(srcs, name, scatter):
    n = len(srcs)
    rows = [s.shape[-2] for s in srcs]
    offs = [sum(rows[:i]) for i in range(n)]
    total = sum(rows)

    def body(*refs):
        src_refs, out_ref = refs[:n], refs[n]
        send_sems, recv_sems, local_sems = refs[n + 1:]
        x, y, c = _mesh_pos()
        me = 4 * x + 2 * y + c

        def piece(i, d):
            return src_refs[i].at[d] if scatter else src_refs[i]

        def window(slot, i):
            return out_ref.at[slot, pl.ds(offs[i], rows[i])]

        local = [pltpu.make_async_copy(piece(i, me), window(me, i), local_sems.at[i]) for i in range(n)]
        for cp in local:
            cp.start()
        peers = []
        for k in range(1, N_DEV):
            px = 1 - x if k & 4 else x
            py = 1 - y if k & 2 else y
            pc = 1 - c if k & 1 else c
            pid = 4 * px + 2 * py + pc
            for i in range(n):
                pltpu.make_async_remote_copy(
                    src_ref=piece(i, pid), dst_ref=window(me, i),
                    send_sem=send_sems.at[k - 1], recv_sem=recv_sems.at[k - 1],
                    device_id=(px, py, pc), device_id_type=pl.DeviceIdType.MESH).start()
            peers.append(pid)
        slabs = [pltpu.make_async_remote_copy(
            src_ref=out_ref.at[me], dst_ref=out_ref.at[peers[k - 1]],
            send_sem=send_sems.at[k - 1], recv_sem=recv_sems.at[k - 1],
            device_id=(x, y, c), device_id_type=pl.DeviceIdType.MESH) for k in range(1, N_DEV)]
        for cp in slabs:
            cp.wait_recv()
        for cp in slabs:
            cp.wait_send()
        for cp in local:
            cp.wait()

    return pl.pallas_call(
        body, name=name,
        in_specs=[pl.BlockSpec(memory_space=pl.ANY)] * n, out_specs=pl.BlockSpec(memory_space=pl.ANY),
        out_shape=jax.ShapeDtypeStruct((N_DEV, total, FLAT_COLS), srcs[0].dtype),
        scratch_shapes=[pltpu.SemaphoreType.DMA((N_DEV - 1,)), pltpu.SemaphoreType.DMA((N_DEV - 1,)),
                        pltpu.SemaphoreType.DMA((n,))],
        compiler_params=pltpu.CompilerParams(has_side_effects=True),
    )(*srcs)


def _adamw(parts, w, mom, var, name):
    rows = w.shape[0]
    br = FLAT_ROW_BLOCK
    c1 = 1.0 - ADAM_B1 ** ADAM_STEP
    c2 = 1.0 - ADAM_B2 ** ADAM_STEP

    def body(p_ref, w_ref, m_ref, v_ref, g_ref, d_ref, nm_ref, nv_ref):
        g = p_ref[0].astype(F32)
        for s in range(1, N_DEV):
            g = g + p_ref[s].astype(F32)
        m_new = ADAM_B1 * m_ref[...] + (1.0 - ADAM_B1) * g
        v_new = ADAM_B2 * v_ref[...] + (1.0 - ADAM_B2) * (g * g)
        m_hat = m_new / c1
        v_hat = v_new / c2
        g_ref[...] = g
        d_ref[...] = -ADAM_LR * (m_hat / (jnp.sqrt(v_hat) + ADAM_EPS) + ADAM_WD * w_ref[...])
        nm_ref[...] = m_new
        nv_ref[...] = v_new

    row = pl.BlockSpec((br, FLAT_COLS), lambda i: (i, 0))
    flat = jax.ShapeDtypeStruct((rows, FLAT_COLS), F32)
    return pl.pallas_call(
        body, name=name, grid=(rows // br,),
        in_specs=[pl.BlockSpec((N_DEV, br, FLAT_COLS), lambda i: (0, i, 0)), row, row, row],
        out_specs=[row] * 4, out_shape=[flat] * 4,
        compiler_params=_params("parallel"),
    )(parts, w, mom, var)


SMALL_SHARDED = (("meta_tokens", 1), ("gdn_norm_g", 1), ("gdn_conv_w", 2))
BIG_PIECES = (("gdn_w_in", 0, True), ("gdn_w_out", 0, False), ("w_kv", None, True), ("sb_w_q", 0, False),
              ("sb_w_o", 0, False), ("ffn_w_gate_up", 0, True), ("ffn_w_gate_up", 1, True),
              ("ffn_w_down", 0, False), ("ffn_w_down", 1, False))
REPLICATED = ("gdn_a_log", "gdn_dt_bias", "gdn_onorm_g", "kv_norm_g", "sb_norm_g", "ffn_norm_g", "final_norm_g")
WEIGHTS = ("meta_tokens", "gdn_norm_g", "gdn_w_in", "gdn_conv_w", "gdn_a_log", "gdn_dt_bias", "gdn_onorm_g",
           "gdn_w_out", "kv_norm_g", "w_kv", "sb_norm_g", "sb_w_q", "sb_w_o", "ffn_norm_g", "ffn_w_gate_up",
           "ffn_w_down", "final_norm_g")
F32_ROWS = 8
BF16_ROWS = 16


def _n_rows(shape, mult):
    return -(-math.prod(shape) // (mult * FLAT_COLS)) * mult


def _as_rows(a, nl, mult):
    lead = a.shape[:nl]
    size = math.prod(a.shape[nl:])
    nr = _n_rows(a.shape[nl:], mult)
    flat = a.reshape(lead + (size,))
    if nr * FLAT_COLS != size:
        flat = jnp.pad(flat, [(0, 0)] * nl + [(0, nr * FLAT_COLS - size)])
    return flat.reshape(lead + (nr, FLAT_COLS))


def _from_rows(flat, r0, shape, nl, mult):
    lead = flat.shape[:nl]
    nr = _n_rows(shape, mult)
    seg = lax.slice_in_dim(flat, r0, r0 + nr, axis=nl).reshape(lead + (nr * FLAT_COLS,))
    return lax.slice_in_dim(seg, 0, math.prod(shape), axis=nl).reshape(lead + tuple(shape)), r0 + nr


def _to_piece(shard, layer, transposed):
    mat = shard if layer is None else shard[layer]
    return mat.T if transposed else mat


def _from_pieces(pieces, name):
    mats = [p.T if tr else p for (n, layer, tr), p in zip(BIG_PIECES, pieces) if n == name]
    layers = [layer for n, layer, _ in BIG_PIECES if n == name]
    return mats[0] if layers[0] is None else jnp.stack(mats)


def _merge_shards(g, axis):
    t = jnp.moveaxis(g, 0, axis)
    shp = t.shape
    return t.reshape(shp[:axis] + (shp[axis] * shp[axis + 1],) + shp[axis + 2:])


def _split_shards(full, axis):
    shp = full.shape
    t = full.reshape(shp[:axis] + (N_DEV, shp[axis] // N_DEV) + shp[axis + 1:])
    return jnp.moveaxis(t, axis, 0)


def _pad_lanes(a, width=LANES):
    return jnp.pad(a, ((0, 0), (0, width - a.shape[1])))


def _local_step(x, target, w, mats):
    d = x.shape[1]
    gh = w["gdn_a_log"].shape[1]
    gw = gh * GDN_HEAD_DIM
    w_int = mats["gdn_w_in", 0]
    w_qkvt, w_gatet = w_int[:3 * gw], w_int[3 * gw:4 * gw]
    w_abt = jnp.pad(w_int[4 * gw:], ((0, LANES - 2 * gh), (0, 0)))
    conv_w = w["gdn_conv_w"][0]
    a_log, dt_bias = _pad_lanes(w["gdn_a_log"]), _pad_lanes(w["gdn_dt_bias"])
    w_out = mats["gdn_w_out", 0]
    w_kvt = mats["w_kv", None]
    sbw = w_kvt.shape[0] // 2
    w_kt, w_vt = w_kvt[:sbw], w_kvt[sbw:]
    w_q, w_o = mats["sb_w_q", 0], mats["sb_w_o", 0]
    w_gut = [mats["ffn_w_gate_up", l] for l in range(2)]
    w_dn = [mats["ffn_w_down", l] for l in range(2)]
    ffn_g = [w["ffn_norm_g"][l:l + 1] for l in range(2)]
    kv_g, fin_g = w["kv_norm_g"][None], w["final_norm_g"][None]

    h0 = jnp.concatenate([jnp.zeros((PAD_ROWS, d), F32), w["meta_tokens"], x], axis=0)

    (n1,) = _rmsnorm(h0, [w["gdn_norm_g"]], "norm_gdn")
    proj = _matmul(n1, w_qkvt, "nt", "mm_gdn_qkv")
    gate = _matmul(n1, w_gatet, "nt", "mm_gdn_gate")
    ab = _matmul(n1, w_abt, "nt", "mm_gdn_ab")
    qkv = _gdn_conv(proj, conv_w, gh)
    g_b, beta_b = _gdn_gates(ab, a_log, dt_bias, gh)
    ub, wc, qd, kd, gc, pmat, tinv = _gdn_prep(qkv, g_b, beta_b, gh)
    o_gdn, ssave = _gdn_scan(ub, wc, qd, kd, gc, pmat, gh)
    o2 = _gdn_outnorm(o_gdn, gate, w["gdn_onorm_g"], gh)
    h1 = _matmul(o2, w_out, "nn", "mm_gdn_out", add=h0)

    def ffn_fwd(h, l):
        (n,) = _rmsnorm(h, [ffn_g[l]], f"norm_ffn{l}")
        gu = _matmul(n, w_gut[l], "nt", f"mm_ffn{l}_gu")
        act = _swiglu(gu, f"swiglu{l}")
        return n, gu, act, _matmul(act, w_dn[l], "nn", f"mm_ffn{l}_down", add=h)

    n2, gu0, act0, h2 = ffn_fwd(h1, 0)

    nkv, n3 = _rmsnorm(h2, [kv_g, w["sb_norm_g"]], "norm_kv_sb")
    kv = _matmul(nkv, w_kvt, "nt", "mm_kv", out_dtype=BF16)
    q_sb = _matmul(n3, w_q, "nn", "mm_sb_q")
    o_sb, w_sb, beta_sb = _sb_attention(q_sb, kv)
    h3 = _matmul(o_sb, w_o, "nn", "mm_sb_o", add=h2)
    n4, gu1, act1, h4 = ffn_fwd(h3, 1)

    loss, dh4, d_fin = _loss_head(h4, fin_g, target)

    def ffn_bwd(dh, h, n, gu, act, l):
        d_act = _matmul(dh, w_dn[l], "nt", f"mm_ffn{l}_dact")
        dw_dn = _matmul(act, dh, "tn", f"mm_ffn{l}_dwdown")
        dgu = _swiglu_bwd(gu, d_act, f"swiglu_bwd{l}")
        dw_gu = _matmul(dgu, n, "tn", f"mm_ffn{l}_dwgu")
        dn = _matmul(dgu, w_gut[l], "nn", f"mm_ffn{l}_dn")
        dh_in, (dg,) = _rmsnorm_bwd(h, [ffn_g[l]], [dn], dh, f"norm_ffn{l}_bwd")
        return dh_in, dw_gu, dw_dn, dg

    dh3, dw_gu1, dw_dn1, dg_ffn1 = ffn_bwd(dh4, h3, n4, gu1, act1, 1)

    do_sb = _matmul(dh3, w_o, "nt", "mm_sb_do")
    dw_o = _matmul(o_sb, dh3, "tn", "mm_sb_dwo")
    dq_sb, dk_sb, dv_sb = _sb_attention_bwd(q_sb, kv, w_sb, beta_sb, do_sb)
    dw_q = _matmul(n3, dq_sb, "tn", "mm_sb_dwq")
    dn3 = _matmul(dq_sb, w_q, "nt", "mm_sb_dn")
    dw_k = _matmul(dk_sb, nkv, "tn", "mm_dwk")
    dw_v = _matmul(dv_sb, nkv, "tn", "mm_dwv")
    dnkv = _matmul(dk_sb, w_kt, "nn", "mm_dnk")
    dnkv = _matmul(dv_sb, w_vt, "nn", "mm_dnv", add=dnkv)
    dh2, (dg_kv, dg_sb) = _rmsnorm_bwd(h2, [kv_g, w["sb_norm_g"]], [dnkv, dn3], dh3, "norm_kv_sb_bwd")

    dh1, dw_gu0, dw_dn0, dg_ffn0 = ffn_bwd(dh2, h1, n2, gu0, act0, 0)

    do2 = _matmul(dh1, w_out, "nt", "mm_gdn_do")
    dw_out = _matmul(o2, dh1, "tn", "mm_gdn_dwout")
    do_gdn, dgate, d_onorm = _gdn_outnorm_bwd(o_gdn, gate, w["gdn_onorm_g"], do2, gh)
    dub, dwc, dqd, dkd, dpm, dgl = _gdn_scan_bwd(do_gdn, ssave, ub, wc, qd, kd, gc, pmat, gh)
    dqkv, dg_b, dbeta_b = _gdn_prep_bwd(qkv, g_b, beta_b, ub, wc, tinv, dub, dwc, dqd, dkd, dpm, dgl, gh)
    dab, d_alog, d_dtb = _gdn_gates_bwd(ab, a_log, dt_bias, dg_b, dbeta_b, gh)
    dproj, d_conv = _gdn_conv_bwd(proj, conv_w, dqkv, gh)
    dw_qkv = _matmul(dproj, n1, "tn", "mm_gdn_dwqkv")
    dw_gate = _matmul(dgate, n1, "tn", "mm_gdn_dwgate")
    dw_ab = _matmul(dab, n1, "tn", "mm_gdn_dwab")
    dn1 = _matmul(dproj, w_qkvt, "nn", "mm_gdn_dn_qkv")
    dn1 = _matmul(dgate, w_gatet, "nn", "mm_gdn_dn_gate", add=dn1)
    dn1 = _matmul(dab, w_abt, "nn", "mm_gdn_dn_ab", add=dn1)
    dh0, (dg_gdn,) = _rmsnorm_bwd(h0, [w["gdn_norm_g"]], [dn1], dh1, "norm_gdn_bwd")

    grads = {
        "meta_tokens": dh0[PAD_ROWS:LANES],
        "gdn_norm_g": dg_gdn,
        "gdn_conv_w": d_conv[None],
        "gdn_a_log": d_alog[:, :gh],
        "gdn_dt_bias": d_dtb[:, :gh],
        "gdn_onorm_g": d_onorm,
        "kv_norm_g": dg_kv[0],
        "sb_norm_g": dg_sb,
        "ffn_norm_g": jnp.concatenate([dg_ffn0, dg_ffn1], axis=0),
        "final_norm_g": d_fin[0],
    }
    gmats = {
        ("gdn_w_in", 0): jnp.concatenate([dw_qkv, dw_gate, dw_ab[:2 * gh]], axis=0),
        ("gdn_w_out", 0): dw_out,
        ("w_kv", None): jnp.concatenate([dw_k, dw_v], axis=0),
        ("sb_w_q", 0): dw_q,
        ("sb_w_o", 0): dw_o,
        ("ffn_w_gate_up", 0): dw_gu0,
        ("ffn_w_gate_up", 1): dw_gu1,
        ("ffn_w_down", 0): dw_dn0,
        ("ffn_w_down", 1): dw_dn1,
    }
    return loss, dh0[LANES:], grads, gmats


def kernel(x, meta_tokens, gdn_norm_g, gdn_w_in, gdn_conv_w, gdn_a_log, gdn_dt_bias, gdn_onorm_g, gdn_w_out, kv_norm_g, w_kv, sb_norm_g, sb_w_q, sb_w_o, ffn_norm_g, ffn_w_gate_up, ffn_w_down, final_norm_g, loss_target, m_meta_tokens, m_gdn_norm_g, m_gdn_w_in, m_gdn_conv_w, m_gdn_a_log, m_gdn_dt_bias, m_gdn_onorm_g, m_gdn_w_out, m_kv_norm_g, m_w_kv, m_sb_norm_g, m_sb_w_q, m_sb_w_o, m_ffn_norm_g, m_ffn_w_gate_up, m_ffn_w_down, m_final_norm_g, v_meta_tokens, v_gdn_norm_g, v_gdn_w_in, v_gdn_conv_w, v_gdn_a_log, v_gdn_dt_bias, v_gdn_onorm_g, v_gdn_w_out, v_kv_norm_g, v_w_kv, v_sb_norm_g, v_sb_w_q, v_sb_w_o, v_ffn_norm_g, v_ffn_w_gate_up, v_ffn_w_down, v_final_norm_g):
    given = dict(locals())
    small = [n for n, _ in SMALL_SHARDED]
    big_names = list(dict.fromkeys(n for n, _, _ in BIG_PIECES))

    def local_pieces(prefix):
        return [_to_piece(given[prefix + n], layer, tr) for n, layer, tr in BIG_PIECES]

    piece_shapes = [p.shape for p in local_pieces("")]

    g_small = _exchange([jnp.concatenate([_as_rows(given[n], 0, F32_ROWS) for n in small], axis=0)],
                        "all_gather_small", scatter=False)
    g_big = _exchange([_as_rows(p.astype(BF16), 0, BF16_ROWS) for p in local_pieces("")],
                      "all_gather_weights", scatter=False)
    full = {n: given[n] for n in REPLICATED}
    r = 0
    for n, ax in SMALL_SHARDED:
        shards, r = _from_rows(g_small, r, given[n].shape, 1, F32_ROWS)
        full[n] = _merge_shards(shards, ax)
    mats, r = {}, 0
    for (n, layer, _), shp in zip(BIG_PIECES, piece_shapes):
        rows8, r = _from_rows(g_big, r, shp, 1, BF16_ROWS)
        mats[n, layer] = rows8.reshape((N_DEV * shp[0],) + shp[1:])

    loss_b, grad_x, grads, gmats = _local_step(x[0], loss_target[0], full, mats)

    order = small + list(REPLICATED)
    fill_small = (-sum(_n_rows(given[n].shape, F32_ROWS) for n in order)) % FLAT_ROW_BLOCK
    fill_big = (-sum(_n_rows(shp, BF16_ROWS) for shp in piece_shapes)) % FLAT_ROW_BLOCK
    small_send = [_as_rows(_split_shards(grads[n], ax), 1, F32_ROWS) for n, ax in SMALL_SHARDED]
    small_send += [_as_rows(jnp.broadcast_to(grads[n][None], (N_DEV,) + grads[n].shape), 1, F32_ROWS)
                   for n in REPLICATED]
    if fill_small:
        small_send.append(jnp.zeros((N_DEV, fill_small, FLAT_COLS), F32))
    recv_small = _exchange([jnp.concatenate(small_send, axis=1)], "scatter_small_gradients", scatter=True)
    big_send = [_as_rows(gmats[n, layer].reshape((N_DEV,) + shp), 1, BF16_ROWS)
                for (n, layer, _), shp in zip(BIG_PIECES, piece_shapes)]
    if fill_big:
        big_send.append(jnp.zeros((N_DEV, fill_big, FLAT_COLS), BF16))
    recv_big = _exchange(big_send, "scatter_gradients", scatter=True)

    def flat_small(prefix):
        segs = [_as_rows(given[prefix + n], 0, F32_ROWS) for n in order]
        return jnp.concatenate(segs + ([jnp.zeros((fill_small, FLAT_COLS), F32)] if fill_small else []), axis=0)

    def flat_big(prefix):
        segs = [_as_rows(p, 0, BF16_ROWS) for p in local_pieces(prefix)]
        return jnp.concatenate(segs + ([jnp.zeros((fill_big, FLAT_COLS), F32)] if fill_big else []), axis=0)

    upd_small = _adamw(recv_small, flat_small(""), flat_small("m_"), flat_small("v_"), "adamw_small")
    upd_big = _adamw(recv_big, flat_big(""), flat_big("m_"), flat_big("v_"), "adamw")
    outs = {}
    for tag, f_small, f_big in zip(("grad", "delta", "new_m", "new_v"), upd_small, upd_big):
        outs[tag], r = {}, 0
        for n in order:
            outs[tag][n], r = _from_rows(f_small, r, given[n].shape, 0, F32_ROWS)
        pieces, r = [], 0
        for shp in piece_shapes:
            p, r = _from_rows(f_big, r, shp, 0, BF16_ROWS)
            pieces.append(p)
        for n in big_names:
            outs[tag][n] = _from_pieces(pieces, n)

    loss = lax.psum(loss_b[0, 0], ("x", "y", "c"))
    result = [loss, grad_x[None]]
    for tag in ("grad", "delta", "new_m", "new_v"):
        result += [outs[tag][n] for n in WEIGHTS]
    return tuple(result)
```

```python
import functools
import math

import jax
import jax.numpy as jnp
from jax import lax
from jax.experimental import pallas as pl
from jax.experimental.pallas import tpu as pltpu

F32 = jnp.float32
BF16 = jnp.bfloat16

N_DEV = 8
LANES = 128
N_META = 16
PAD_ROWS = LANES - N_META
GDN_CHUNK = 64
GDN_HEAD_DIM = 128
SB_HEAD_DIM = 64
SB_BWD_SUM_PARTS = 1
CONV_WIDTH = 4
NORM_EPS = 1e-6
FLAT_COLS = 1024
FLAT_ROW_BLOCK = 64
VMEM_LIMIT = 56 * 1024 * 1024

ADAM_LR = 0.001
ADAM_B1 = 0.9
ADAM_B2 = 0.999
ADAM_EPS = 1e-08
ADAM_WD = 0.01
ADAM_STEP = 10

NN = (((1,), (0,)), ((), ()))
NT = (((1,), (1,)), ((), ()))
TN = (((0,), (0,)), ((), ()))


def _params(*sem):
    return pltpu.CompilerParams(dimension_semantics=sem, vmem_limit_bytes=VMEM_LIMIT)


def _pick(n, cands):
    for c in cands:
        if n % c == 0:
            return c
    return n


def _bdot(a, b, dims=NN):
    return lax.dot_general(a.astype(BF16), b.astype(BF16), dims, preferred_element_type=F32)


def _hdot(a, b, dims=NN):
    return lax.dot_general(a, b, dims, preferred_element_type=F32, precision=lax.Precision.HIGH)


def _split_dot(a, m, parts=2):
    out = None
    for _ in range(parts):
        piece = a.astype(BF16)
        a = a - piece.astype(F32)
        term = lax.dot_general(piece, m, NN, preferred_element_type=F32)
        out = term if out is None else out + term
    return out


def _lockstep(gens):
    while gens:
        alive = []
        for g in gens:
            try:
                next(g)
                alive.append(g)
            except StopIteration:
                pass
        gens = alive


def _sigmoid(x):
    return 1.0 / (1.0 + jnp.exp(-x))


def _iota2(shape, axis):
    return lax.broadcasted_iota(jnp.int32, shape, axis)


def _matmul(a, b, mode, name, out_dtype=None, add=None):
    if out_dtype is None:
        out_dtype = BF16 if mode == "tn" else F32
    if mode == "nn":
        (m, k), n = a.shape, b.shape[1]
    elif mode == "nt":
        (m, k), n = a.shape, b.shape[0]
    else:
        (k, m), n = a.shape, b.shape[1]
    row_c = (640, 512, 384, 256, 128)
    col_c = (1024, 1408, 768, 512, 384, 256, 128)
    if mode == "tn":
        bm, bn, bk = _pick(m, col_c), _pick(n, col_c), _pick(k, row_c)
    else:
        bm, bn, bk = _pick(m, row_c), _pick(n, col_c), _pick(k, (1024, 1408, 768, 512, 256, 128))
    nk = k // bk
    dims = {"nn": NN, "nt": NT, "tn": TN}[mode]
    a_spec = {"nn": pl.BlockSpec((bm, bk), lambda i, j, q: (i, q)),
              "nt": pl.BlockSpec((bm, bk), lambda i, j, q: (i, q)),
              "tn": pl.BlockSpec((bk, bm), lambda i, j, q: (q, i))}[mode]
    b_spec = {"nn": pl.BlockSpec((bk, bn), lambda i, j, q: (q, j)),
              "nt": pl.BlockSpec((bn, bk), lambda i, j, q: (j, q)),
              "tn": pl.BlockSpec((bk, bn), lambda i, j, q: (q, j))}[mode]
    o_spec = pl.BlockSpec((bm, bn), lambda i, j, q: (i, j))
    has_add = add is not None

    def body(*refs):
        if has_add:
            a_ref, b_ref, add_ref, o_ref, acc_ref = refs
        else:
            a_ref, b_ref, o_ref, acc_ref = refs
        q = pl.program_id(2)
        part = _bdot(a_ref[...], b_ref[...], dims)

        @pl.when(q == 0)
        def _():
            acc_ref[...] = part

        @pl.when(q > 0)
        def _():
            acc_ref[...] += part

        @pl.when(q == nk - 1)
        def _():
            r = acc_ref[...]
            if has_add:
                r = r + add_ref[...]
            o_ref[...] = r.astype(out_dtype)

    ins = [a, b] + ([add] if has_add else [])
    in_specs = [a_spec, b_spec] + ([o_spec] if has_add else [])
    return pl.pallas_call(
        body, name=name, grid=(m // bm, n // bn, nk),
        in_specs=in_specs, out_specs=o_spec,
        out_shape=jax.ShapeDtypeStruct((m, n), out_dtype),
        scratch_shapes=[pltpu.VMEM((bm, bn), F32)],
        compiler_params=_params("parallel", "parallel", "arbitrary"),
    )(*ins)


def _rmsnorm(h, gains, name):
    m, d = h.shape
    bm = _pick(m, (640, 384, 128))
    ng = len(gains)

    def body(*refs):
        h_ref, g_refs, o_refs = refs[0], refs[1:1 + ng], refs[1 + ng:]
        x = h_ref[...]
        xhat = x * lax.rsqrt(jnp.mean(x * x, axis=-1, keepdims=True) + NORM_EPS)
        for g_ref, o_ref in zip(g_refs, o_refs):
            o_ref[...] = (xhat * g_ref[...]).astype(BF16)

    row = pl.BlockSpec((bm, d), lambda i: (i, 0))
    gain = pl.BlockSpec((1, d), lambda i: (0, 0))
    return pl.pallas_call(
        body, name=name, grid=(m // bm,),
        in_specs=[row] + [gain] * ng, out_specs=[row] * ng,
        out_shape=[jax.ShapeDtypeStruct((m, d), BF16)] * ng,
        compiler_params=_params("parallel"),
    )(h, *gains)


def _rmsnorm_bwd(h, gains, dns, dres, name):
    m, d = h.shape
    bm = _pick(m, (640, 384, 128))
    ng = len(gains)
    nsteps = m // bm

    def body(*refs):
        h_ref, dres_ref = refs[0], refs[1]
        g_refs = refs[2:2 + ng]
        dn_refs = refs[2 + ng:2 + 2 * ng]
        dh_ref = refs[2 + 2 * ng]
        dg_refs = refs[3 + 2 * ng:3 + 3 * ng]
        acc_refs = refs[3 + 3 * ng:]
        i = pl.program_id(0)
        x = h_ref[...]
        r = lax.rsqrt(jnp.mean(x * x, axis=-1, keepdims=True) + NORM_EPS)
        xhat = x * r
        dh = dres_ref[...]
        for g_ref, dn_ref, dg_ref, acc_ref in zip(g_refs, dn_refs, dg_refs, acc_refs):
            dn = dn_ref[...]
            dy = dn * g_ref[...]
            dh = dh + r * (dy - xhat * jnp.mean(dy * xhat, axis=-1, keepdims=True))
            part = jnp.sum((dn * xhat).reshape(bm // 8, 8, d), axis=0)

            @pl.when(i == 0)
            def _():
                acc_ref[...] = part

            @pl.when(i > 0)
            def _():
                acc_ref[...] += part

            @pl.when(i == nsteps - 1)
            def _():
                dg_ref[...] = jnp.sum(acc_ref[...], axis=0, keepdims=True)

        dh_ref[...] = dh

    row = pl.BlockSpec((bm, d), lambda i: (i, 0))
    gain = pl.BlockSpec((1, d), lambda i: (0, 0))
    outs = pl.pallas_call(
        body, name=name, grid=(nsteps,),
        in_specs=[row, row] + [gain] * ng + [row] * ng,
        out_specs=[row] + [gain] * ng,
        out_shape=[jax.ShapeDtypeStruct((m, d), F32)] + [jax.ShapeDtypeStruct((1, d), F32)] * ng,
        scratch_shapes=[pltpu.VMEM((8, d), F32)] * ng,
        compiler_params=_params("arbitrary"),
    )(h, dres, *gains, *dns)
    return outs[0], list(outs[1:])


def _loss_head(h, gain, target):
    m, d = h.shape
    nsteps = m // LANES

    def body(h_ref, g_ref, t_ref, loss_ref, dh_ref, dg_ref, lacc, gacc):
        i = pl.program_id(0)

        @pl.when(i == 0)
        def _():
            lacc[...] = jnp.zeros_like(lacc)
            gacc[...] = jnp.zeros_like(gacc)
            dh_ref[...] = jnp.zeros_like(dh_ref)

        @pl.when(i > 0)
        def _():
            x = h_ref[...]
            r = lax.rsqrt(jnp.mean(x * x, axis=-1, keepdims=True) + NORM_EPS)
            xhat = x * r
            g = g_ref[...]
            err = xhat * g - t_ref[...]
            lacc[...] += jnp.sum((err * err).reshape(LANES // 8, 8, d), axis=0)
            dn = err * (1.0 / d)
            dy = dn * g
            dh_ref[...] = r * (dy - xhat * jnp.mean(dy * xhat, axis=-1, keepdims=True))
            gacc[...] += jnp.sum((dn * xhat).reshape(LANES // 8, 8, d), axis=0)

        @pl.when(i == nsteps - 1)
        def _():
            tot = jnp.sum(jnp.sum(lacc[...], axis=1, keepdims=True), axis=0, keepdims=True)
            loss_ref[...] = jnp.broadcast_to(tot * (0.5 / d), (1, LANES))
            dg_ref[...] = jnp.sum(gacc[...], axis=0, keepdims=True)

    row = pl.BlockSpec((LANES, d), lambda i: (i, 0))
    trow = pl.BlockSpec((LANES, d), lambda i: (jnp.maximum(i - 1, 0), 0))
    gain_spec = pl.BlockSpec((1, d), lambda i: (0, 0))
    return pl.pallas_call(
        body, name="loss_head", grid=(nsteps,),
        in_specs=[row, gain_spec, trow],
        out_specs=[pl.BlockSpec((1, LANES), lambda i: (0, 0)), row, gain_spec],
        out_shape=[jax.ShapeDtypeStruct((1, LANES), F32), jax.ShapeDtypeStruct((m, d), F32),
                   jax.ShapeDtypeStruct((1, d), F32)],
        scratch_shapes=[pltpu.VMEM((8, d), F32), pltpu.VMEM((8, d), F32)],
        compiler_params=_params("arbitrary"),
    )(h, gain, target)


def _swiglu(gu, name):
    m, f2 = gu.shape
    f = f2 // 2

    def body(gu_ref, o_ref):
        g = gu_ref[:, :f]
        u = gu_ref[:, f:]
        o_ref[...] = (g * _sigmoid(g) * u).astype(BF16)

    return pl.pallas_call(
        body, name=name, grid=(m // LANES,),
        in_specs=[pl.BlockSpec((LANES, f2), lambda i: (i, 0))],
        out_specs=pl.BlockSpec((LANES, f), lambda i: (i, 0)),
        out_shape=jax.ShapeDtypeStruct((m, f), BF16),
        compiler_params=_params("parallel"),
    )(gu)


def _swiglu_bwd(gu, da, name):
    m, f2 = gu.shape
    f = f2 // 2

    def body(gu_ref, da_ref, o_ref):
        g = gu_ref[:, :f]
        u = gu_ref[:, f:]
        da_ = da_ref[...]
        s = _sigmoid(g)
        o_ref[:, :f] = (da_ * u * s * (1.0 + g * (1.0 - s))).astype(BF16)
        o_ref[:, f:] = (da_ * g * s).astype(BF16)

    return pl.pallas_call(
        body, name=name, grid=(m // LANES,),
        in_specs=[pl.BlockSpec((LANES, f2), lambda i: (i, 0)), pl.BlockSpec((LANES, f), lambda i: (i, 0))],
        out_specs=pl.BlockSpec((LANES, f2), lambda i: (i, 0)),
        out_shape=jax.ShapeDtypeStruct((m, f2), BF16),
        compiler_params=_params("parallel"),
    )(gu, da)


def _conv_taps(xa, w_ref):
    acc = xa * w_ref[CONV_WIDTH - 1:CONV_WIDTH, :]
    for i in range(CONV_WIDTH - 1):
        acc = acc + pltpu.roll(xa, CONV_WIDTH - 1 - i, 0) * w_ref[i:i + 1, :]
    return acc[8:]


def _gdn_conv(proj, conv_w, n_heads):
    m, w3 = proj.shape
    wd = w3 // 3
    nblk = m // LANES

    def body(x_ref, w_ref, o_ref):
        c = pl.program_id(0)
        is_qk = c < 2 * n_heads

        def finish(cv):
            s = cv * _sigmoid(cv)
            nrm = s * lax.rsqrt(jnp.sum(s * s, axis=-1, keepdims=True) + NORM_EPS)
            return jnp.where(is_qk, nrm, s)

        x0 = jnp.concatenate([jnp.zeros((8, LANES), F32), x_ref[pl.ds(0, LANES), :]], axis=0)
        o_ref[pl.ds(0, LANES), :] = finish(_conv_taps(x0, w_ref))

        def step(b, carry):
            r0 = pl.multiple_of(b * LANES, LANES)
            xa = x_ref[pl.ds(r0 - 8, LANES + 8), :]
            o_ref[pl.ds(r0, LANES), :] = finish(_conv_taps(xa, w_ref))
            return carry

        lax.fori_loop(1, nblk, step, 0)

    return pl.pallas_call(
        body, name="gdn_conv", grid=(w3 // LANES,),
        in_specs=[pl.BlockSpec((m, LANES), lambda c: (0, c)), pl.BlockSpec((CONV_WIDTH, LANES), lambda c: (0, c))],
        out_specs=pl.BlockSpec((None, m, LANES), lambda c: (c // n_heads, 0, c % n_heads)),
        out_shape=jax.ShapeDtypeStruct((3, m, wd), F32),
        compiler_params=_params("parallel"),
    )(proj, conv_w)


def _gdn_conv_bwd(proj, conv_w, dact, n_heads):
    m, w3 = proj.shape
    nblk = m // LANES

    def body(x_ref, w_ref, dy_ref, dx_ref, dw_ref, dc_ref):
        c = pl.program_id(0)
        is_qk = c < 2 * n_heads
        dc_ref[pl.ds(m, 8), :] = jnp.zeros((8, LANES), F32)

        def pre_act(xa):
            cv = _conv_taps(xa, w_ref)
            sg = _sigmoid(cv)
            s = cv * sg
            return cv, sg, s

        def dconv_in(b, xa, dwacc):
            r0 = pl.multiple_of(b * LANES, LANES)
            cv, sg, s = pre_act(xa)
            dy = dy_ref[pl.ds(r0, LANES), :]
            rn = lax.rsqrt(jnp.sum(s * s, axis=-1, keepdims=True) + NORM_EPS)
            yn = s * rn
            ds_qk = rn * (dy - yn * jnp.sum(dy * yn, axis=-1, keepdims=True))
            ds = jnp.where(is_qk, ds_qk, dy)
            dcv = ds * sg * (1.0 + cv * (1.0 - sg))
            dc_ref[pl.ds(r0, LANES), :] = dcv
            new = []
            for i in range(CONV_WIDTH):
                xs = xa[8:] if i == CONV_WIDTH - 1 else pltpu.roll(xa, CONV_WIDTH - 1 - i, 0)[8:]
                new.append(dwacc[i] + jnp.sum((dcv * xs).reshape(LANES // 8, 8, LANES), axis=0))
            return tuple(new)

        x0 = jnp.concatenate([jnp.zeros((8, LANES), F32), x_ref[pl.ds(0, LANES), :]], axis=0)
        zero = jnp.zeros((8, LANES), F32)
        dwacc = dconv_in(0, x0, (zero,) * CONV_WIDTH)

        def step1(b, acc):
            r0 = pl.multiple_of(b * LANES, LANES)
            return dconv_in(b, x_ref[pl.ds(r0 - 8, LANES + 8), :], acc)

        dwacc = lax.fori_loop(1, nblk, step1, dwacc)
        for i in range(CONV_WIDTH):
            dw_ref[i:i + 1, :] = jnp.sum(dwacc[i], axis=0, keepdims=True)

        def step2(b, carry):
            r0 = pl.multiple_of(b * LANES, LANES)
            da = dc_ref[pl.ds(r0, LANES + 8), :]
            acc = da * w_ref[CONV_WIDTH - 1:CONV_WIDTH, :]
            for i in range(CONV_WIDTH - 1):
                sh = CONV_WIDTH - 1 - i
                acc = acc + pltpu.roll(da, LANES + 8 - sh, 0) * w_ref[i:i + 1, :]
            dx_ref[pl.ds(r0, LANES), :] = acc[:LANES]
            return carry

        lax.fori_loop(0, nblk, step2, 0)

    return pl.pallas_call(
        body, name="gdn_conv_bwd", grid=(w3 // LANES,),
        in_specs=[pl.BlockSpec((m, LANES), lambda c: (0, c)),
                  pl.BlockSpec((CONV_WIDTH, LANES), lambda c: (0, c)),
                  pl.BlockSpec((None, m, LANES), lambda c: (c // n_heads, 0, c % n_heads))],
        out_specs=[pl.BlockSpec((m, LANES), lambda c: (0, c)), pl.BlockSpec((CONV_WIDTH, LANES), lambda c: (0, c))],
        out_shape=[jax.ShapeDtypeStruct((m, w3), F32), jax.ShapeDtypeStruct((CONV_WIDTH, w3), F32)],
        scratch_shapes=[pltpu.VMEM((m + 8, LANES), F32)],
        compiler_params=_params("parallel"),
    )(proj, conv_w, dact)


def _softplus(x):
    return jnp.maximum(x, 0.0) + jnp.log(1.0 + jnp.exp(-jnp.abs(x)))


def _gdn_gates(ab, a_log, dt_bias, n_heads):
    m = ab.shape[0]
    bm = _pick(m, (640, 384, 128))

    def body(ab_ref, al_ref, dt_ref, g_ref, b_ref):
        i = pl.program_id(0)
        x = ab_ref[...]
        live = (_iota2((bm, LANES), 0) + i * bm) >= PAD_ROWS
        g = jnp.where(live, -jnp.exp(al_ref[...]) * _softplus(x + dt_ref[...]), 0.0)
        beta = jnp.where(live, _sigmoid(x), 0.0)
        for h in range(n_heads):
            g_ref[:, h * LANES:(h + 1) * LANES] = jnp.broadcast_to(g[:, h:h + 1], (bm, LANES))
            b_ref[:, h * LANES:(h + 1) * LANES] = jnp.broadcast_to(
                beta[:, n_heads + h:n_heads + h + 1], (bm, LANES))

    wide = pl.BlockSpec((bm, n_heads * LANES), lambda i: (i, 0))
    return pl.pallas_call(
        body, name="gdn_gates", grid=(m // bm,),
        in_specs=[pl.BlockSpec((bm, LANES), lambda i: (i, 0))] + [pl.BlockSpec((1, LANES), lambda i: (0, 0))] * 2,
        out_specs=[wide, wide],
        out_shape=[jax.ShapeDtypeStruct((m, n_heads * LANES), F32)] * 2,
        compiler_params=_params("parallel"),
    )(ab, a_log, dt_bias)


def _gdn_gates_bwd(ab, a_log, dt_bias, dg_b, dbeta_b, n_heads):
    m = ab.shape[0]
    bm = _pick(m, (640, 384, 128))
    nsteps = m // bm

    def body(ab_ref, al_ref, dt_ref, dg_ref, db_ref, dab_ref, dal_ref, ddt_ref, acc_al, acc_dt):
        i = pl.program_id(0)
        x = ab_ref[...]
        lane = _iota2((bm, LANES), 1)
        live = (_iota2((bm, LANES), 0) + i * bm) >= PAD_ROWS
        dg = jnp.zeros((bm, LANES), F32)
        dbeta = jnp.zeros((bm, LANES), F32)
        for h in range(n_heads):
            dg = jnp.where(lane == h, dg_ref[:, h * LANES:(h + 1) * LANES], dg)
            dbeta = jnp.where(lane == n_heads + h, db_ref[:, h * LANES:(h + 1) * LANES], dbeta)
        dg = jnp.where(live, dg, 0.0)
        dbeta = jnp.where(live, dbeta, 0.0)
        nea = -jnp.exp(al_ref[...])
        pre = x + dt_ref[...]
        d_pre = dg * nea * _sigmoid(pre)
        beta = _sigmoid(x)
        dab_ref[...] = d_pre + dbeta * beta * (1.0 - beta)
        p_al = jnp.sum((dg * nea * _softplus(pre)).reshape(bm // 8, 8, LANES), axis=0)
        p_dt = jnp.sum(d_pre.reshape(bm // 8, 8, LANES), axis=0)

        @pl.when(i == 0)
        def _():
            acc_al[...] = p_al
            acc_dt[...] = p_dt

        @pl.when(i > 0)
        def _():
            acc_al[...] += p_al
            acc_dt[...] += p_dt

        @pl.when(i == nsteps - 1)
        def _():
            dal_ref[...] = jnp.sum(acc_al[...], axis=0, keepdims=True)
            ddt_ref[...] = jnp.sum(acc_dt[...], axis=0, keepdims=True)

    wide = pl.BlockSpec((bm, n_heads * LANES), lambda i: (i, 0))
    one = pl.BlockSpec((1, LANES), lambda i: (0, 0))
    nar = pl.BlockSpec((bm, LANES), lambda i: (i, 0))
    return pl.pallas_call(
        body, name="gdn_gates_bwd", grid=(nsteps,),
        in_specs=[nar, one, one, wide, wide],
        out_specs=[nar, one, one],
        out_shape=[jax.ShapeDtypeStruct((m, LANES), F32), jax.ShapeDtypeStruct((1, LANES), F32),
                   jax.ShapeDtypeStruct((1, LANES), F32)],
        scratch_shapes=[pltpu.VMEM((8, LANES), F32)] * 2,
        compiler_params=_params("arbitrary"),
    )(ab, a_log, dt_bias, dg_b, dbeta_b)


def _chunk_masks():
    c = GDN_CHUNK
    ri, ci = _iota2((c, c), 0), _iota2((c, c), 1)
    return ri >= ci, ri > ci, ri == ci


def _chunk_decay(gbs):
    c = GDN_CHUNK
    incl, _, _ = _chunk_masks()
    inclf = incl.astype(F32)
    ones = jnp.ones((c, LANES), F32)
    g_cums = [_hdot(inclf, gb) for gb in gbs]
    g_rows = [_hdot(ones, g_cum, NT) * (1.0 / LANES) for g_cum in g_cums]
    dmats = [jnp.where(incl, jnp.exp(jnp.where(incl, g_cum[:, :c] - g_row, 0.0)), 0.0)
             for g_cum, g_row in zip(g_cums, g_rows)]
    return g_cums, dmats


def _unit_lower_inverse(mats):
    _, _, eye = _chunk_masks()
    xs = [eye.astype(F32) - a for a in mats]
    ps = list(mats)
    for _ in range(int(math.log2(GDN_CHUNK)) - 1):
        ps = [_hdot(p, p) for p in ps]
        xs = [x + _hdot(x, p) for x, p in zip(xs, ps)]
    return xs


def _gdn_prep(qkv, g_b, beta_b, n_heads):
    _, m, wd = qkv.shape
    c = GDN_CHUNK
    nc = m // c
    cb = _pick(nc, (10, 5, 3, 2))
    rows = cb * c
    qscale = GDN_HEAD_DIM ** -0.5

    def body(qkv_ref, g_ref, b_ref, ub_ref, wc_ref, qd_ref, kd_ref, gc_ref, p_ref, t_ref):
        _, strict, _ = _chunk_masks()
        sls = [pl.ds(s * c, c) for s in range(cb)]
        g_cums, dmats = _chunk_decay([g_ref[sl, :] for sl in sls])
        kks = [_bdot(qkv_ref[1, sl, :], qkv_ref[1, sl, :], NT) for sl in sls]
        ts = _unit_lower_inverse([jnp.where(strict, b_ref[sl, :][:, :c] * dmat * kk, 0.0)
                                  for sl, dmat, kk in zip(sls, dmats, kks)])
        for s, (sl, g_cum, dmat, t) in enumerate(zip(sls, g_cums, dmats, ts)):
            q = qkv_ref[0, sl, :] * qscale
            k = qkv_ref[1, sl, :]
            bb = b_ref[sl, :]
            gam = jnp.exp(g_cum)
            ub_ref[sl, :] = _hdot(t, bb * qkv_ref[2, sl, :])
            wc_ref[sl, :] = _hdot(t, bb * gam * k)
            p_ref[s] = _bdot(q, k, NT) * dmat
            qd_ref[sl, :] = q * gam
            kd_ref[sl, :] = k * jnp.exp(g_cum[c - 1:c, :] - g_cum)
            gc_ref[sl, :] = g_cum
            t_ref[s] = t

    blk = pl.BlockSpec((rows, LANES), lambda h, n: (n, h))
    sq = pl.BlockSpec((None, cb, c, c), lambda h, n: (h, n, 0, 0))
    big = jax.ShapeDtypeStruct((m, wd), F32)
    small = jax.ShapeDtypeStruct((n_heads, nc, c, c), F32)
    return pl.pallas_call(
        body, name="gdn_prep", grid=(n_heads, nc // cb),
        in_specs=[pl.BlockSpec((3, rows, LANES), lambda h, n: (0, n, h)), blk, blk],
        out_specs=[blk] * 5 + [sq, sq],
        out_shape=[big] * 5 + [small, small],
        compiler_params=_params("parallel", "parallel"),
    )(qkv, g_b, beta_b)


def _gdn_prep_bwd(qkv, g_b, beta_b, ub, wc, tinv, dub, dwc, dqd, dkd, dp, dgl, n_heads):
    _, m, wd = qkv.shape
    c = GDN_CHUNK
    nc = m // c
    cb = _pick(nc, (10, 5, 3, 2))
    rows = cb * c
    qscale = GDN_HEAD_DIM ** -0.5

    def body(qkv_ref, g_ref, b_ref, ub_ref, wc_ref, t_ref, dub_ref, dwc_ref, dqd_ref, dkd_ref, dp_ref, dgl_ref,
             dqkv_ref, dg_ref, db_ref):
        incl, strict, _ = _chunk_masks()
        ones_c = jnp.ones((c, LANES), F32)
        ones_l = jnp.ones((LANES, LANES), F32)
        last_row = _iota2((c, LANES), 0) == c - 1

        def chunk(s):
            sl = pl.ds(s * c, c)
            q = qkv_ref[0, sl, :] * qscale
            k = qkv_ref[1, sl, :]
            v = qkv_ref[2, sl, :]
            bb = b_ref[sl, :]
            bcol = bb[:, :c]
            t = t_ref[s]
            g_cum = _hdot(incl.astype(F32), g_ref[sl, :])
            kk = _bdot(k, k, NT)
            qk = _bdot(q, k, NT)
            drhs_u = _hdot(t, dub_ref[sl, :], TN)
            drhs_w = _hdot(t, dwc_ref[sl, :], TN)
            yield
            g_row = _hdot(ones_c, g_cum, NT) * (1.0 / LANES)
            da = -jnp.where(strict, _bdot(drhs_u, ub_ref[sl, :], NT) + _bdot(drhs_w, wc_ref[sl, :], NT), 0.0)
            dbeta = _hdot(drhs_u * v, ones_l)
            tk = _hdot(drhs_w * k, ones_l)
            dqd = dqd_ref[sl, :]
            dgam = _hdot(dqd * q, ones_l)
            yield
            dmat = jnp.where(incl, jnp.exp(jnp.where(incl, g_cum[:, :c] - g_row, 0.0)), 0.0)
            gam = jnp.exp(g_cum)
            edec = jnp.exp(g_cum[c - 1:c, :] - g_cum)
            dkd = dkd_ref[sl, :]
            skd = _hdot(dkd * k * edec, ones_l)
            dbeta = dbeta + _hdot(da * dmat * kk, ones_c)
            dkk = da * bcol * dmat
            dpm = jnp.where(incl, dp_ref[s], 0.0)
            dpd = dpm * dmat
            dk = _bdot(dkk, k) + _bdot(dkk, k, TN) + _bdot(dpd, q, TN)
            dq = _bdot(dpd, k)
            e = (da * bcol * kk + dpm * qk) * dmat
            yield
            dk = dk + bb * gam * drhs_w + dkd * edec
            dq = dq + dqd * gam
            dbeta = dbeta + tk * gam
            dgam = dgam + bb * tk
            dgc = dgam * gam - skd + _hdot(e, ones_c) - _hdot(e, ones_c, TN)
            tot = _hdot(jnp.ones((c, c), F32), skd)
            yield
            dgc = dgc + jnp.where(last_row, tot + dgl_ref[s, 0:1, :], 0.0)
            dg_ref[sl, :] = _hdot((_iota2((c, c), 0) <= _iota2((c, c), 1)).astype(F32), dgc)
            db_ref[sl, :] = dbeta
            dqkv_ref[0, sl, :] = dq * qscale
            dqkv_ref[1, sl, :] = dk
            dqkv_ref[2, sl, :] = bb * drhs_u

        _lockstep([chunk(s) for s in range(cb)])

    blk = pl.BlockSpec((rows, LANES), lambda h, n: (n, h))
    blk3 = pl.BlockSpec((3, rows, LANES), lambda h, n: (0, n, h))
    sq = pl.BlockSpec((None, cb, c, c), lambda h, n: (h, n, 0, 0))
    sgl = pl.BlockSpec((None, cb, 8, LANES), lambda h, n: (h, n, 0, 0))
    big = jax.ShapeDtypeStruct((m, wd), F32)
    return pl.pallas_call(
        body, name="gdn_prep_bwd", grid=(n_heads, nc // cb),
        in_specs=[blk3, blk, blk, blk, blk, sq, blk, blk, blk, blk, sq, sgl],
        out_specs=[blk3, blk, blk],
        out_shape=[jax.ShapeDtypeStruct((3, m, wd), F32), big, big],
        compiler_params=_params("parallel", "parallel"),
    )(qkv, g_b, beta_b, ub, wc, tinv, dub, dwc, dqd, dkd, dp, dgl)


def _gdn_scan(ub, wc, qd, kd, gc, p, n_heads):
    m, wd = ub.shape
    c = GDN_CHUNK
    nc = m // c
    dh = GDN_HEAD_DIM

    def body(ub_ref, wc_ref, qd_ref, kd_ref, gc_ref, p_ref, o_ref, ssave_ref, s_ref):
        n = pl.program_id(0)

        @pl.when(n == 0)
        def _():
            s_ref[...] = jnp.zeros_like(s_ref)

        def head(h):
            sl = slice(h * LANES, (h + 1) * LANES)
            s = s_ref[h]
            ssave_ref[h] = s
            ws = _bdot(wc_ref[:, sl], s)
            qs = _bdot(qd_ref[:, sl], s)
            yield
            u = ub_ref[:, sl] - ws
            pu = _bdot(p_ref[h], u)
            ku = _bdot(kd_ref[:, sl], u, TN)
            yield
            o_ref[:, sl] = qs + pu
            s_ref[h] = s * jnp.exp(gc_ref[c - 1:c, sl]) + ku

        _lockstep([head(h) for h in range(n_heads)])

    blk = pl.BlockSpec((c, wd), lambda n: (n, 0))
    return pl.pallas_call(
        body, name="gdn_scan", grid=(nc,),
        in_specs=[blk] * 5 + [pl.BlockSpec((n_heads, None, c, c), lambda n: (0, n, 0, 0))],
        out_specs=[blk, pl.BlockSpec((n_heads, None, dh, dh), lambda n: (0, n, 0, 0))],
        out_shape=[jax.ShapeDtypeStruct((m, wd), F32), jax.ShapeDtypeStruct((n_heads, nc, dh, dh), F32)],
        scratch_shapes=[pltpu.VMEM((n_heads, dh, dh), F32)],
        compiler_params=_params("arbitrary"),
    )(ub, wc, qd, kd, gc, p)


def _gdn_scan_bwd(do, ssave, ub, wc, qd, kd, gc, p, n_heads):
    m, wd = ub.shape
    c = GDN_CHUNK
    nc = m // c
    dh = GDN_HEAD_DIM

    def body(do_ref, ss_ref, ub_ref, wc_ref, qd_ref, kd_ref, gc_ref, p_ref,
             dub_ref, dwc_ref, dqd_ref, dkd_ref, dp_ref, dgl_ref, ds_ref):
        n = pl.program_id(0)

        @pl.when(n == 0)
        def _():
            ds_ref[...] = jnp.zeros_like(ds_ref)

        def head(h):
            sl = slice(h * LANES, (h + 1) * LANES)
            ds_next = ds_ref[h]
            s = ss_ref[h]
            do_ = do_ref[:, sl]
            wcv = wc_ref[:, sl]
            kdv = kd_ref[:, sl]
            ws = _bdot(wcv, s)
            du = _bdot(p_ref[h], do_, TN) + _bdot(kdv, ds_next)
            dqd_ref[:, sl] = _bdot(do_, s, NT)
            col = _hdot(jnp.ones((8, dh), F32), s * ds_next)
            qdo = _bdot(qd_ref[:, sl], do_, TN)
            yield
            u = ub_ref[:, sl] - ws
            egl = jnp.exp(gc_ref[c - 1:c, sl])
            dub_ref[:, sl] = du
            dwc_ref[:, sl] = -_bdot(du, s, NT)
            dp_ref[h] = _bdot(do_, u, NT)
            dkd_ref[:, sl] = _bdot(u, ds_next, NT)
            dgl_ref[h] = egl * _hdot(col, jnp.ones((LANES, LANES), F32))
            ds_ref[h] = qdo + egl * ds_next - _bdot(wcv, du, TN)

        _lockstep([head(h) for h in range(n_heads)])

    blk = pl.BlockSpec((c, wd), lambda n: (nc - 1 - n, 0))
    sq = pl.BlockSpec((n_heads, None, c, c), lambda n: (0, nc - 1 - n, 0, 0))
    big = jax.ShapeDtypeStruct((m, wd), F32)
    return pl.pallas_call(
        body, name="gdn_scan_bwd", grid=(nc,),
        in_specs=[blk, pl.BlockSpec((n_heads, None, dh, dh), lambda n: (0, nc - 1 - n, 0, 0))] + [blk] * 5 + [sq],
        out_specs=[blk] * 4 + [sq, pl.BlockSpec((n_heads, None, 8, LANES), lambda n: (0, nc - 1 - n, 0, 0))],
        out_shape=[big] * 4 + [jax.ShapeDtypeStruct((n_heads, nc, c, c), F32),
                               jax.ShapeDtypeStruct((n_heads, nc, 8, LANES), F32)],
        scratch_shapes=[pltpu.VMEM((n_heads, dh, dh), F32)],
        compiler_params=_params("arbitrary"),
    )(do, ssave, ub, wc, qd, kd, gc, p)


def _gdn_outnorm(o, gate, gain, n_heads):
    m, wd = o.shape
    bm = _pick(m, (640, 384, 128))

    def body(o_ref, gt_ref, gn_ref, y_ref):
        gn = gn_ref[...]
        for h in range(n_heads):
            sl = slice(h * LANES, (h + 1) * LANES)
            x = o_ref[:, sl]
            gt = gt_ref[:, sl]
            r = lax.rsqrt(jnp.mean(x * x, axis=-1, keepdims=True) + NORM_EPS)
            y_ref[:, sl] = (x * r * gn * gt * _sigmoid(gt)).astype(BF16)

    row = pl.BlockSpec((bm, wd), lambda i: (i, 0))
    return pl.pallas_call(
        body, name="gdn_outnorm", grid=(m // bm,),
        in_specs=[row, row, pl.BlockSpec((1, LANES), lambda i: (0, 0))], out_specs=row,
        out_shape=jax.ShapeDtypeStruct((m, wd), BF16),
        compiler_params=_params("parallel"),
    )(o, gate, gain)


def _gdn_outnorm_bwd(o, gate, gain, dy, n_heads):
    m, wd = o.shape
    bm = _pick(m, (640, 384, 128))
    nsteps = m // bm

    def body(o_ref, gt_ref, gn_ref, dy_ref, do_ref, dgt_ref, dgn_ref, acc_ref):
        i = pl.program_id(0)
        gn = gn_ref[...]
        part = jnp.zeros((8, LANES), F32)
        for h in range(n_heads):
            sl = slice(h * LANES, (h + 1) * LANES)
            x = o_ref[:, sl]
            gt = gt_ref[:, sl]
            d_out = dy_ref[:, sl]
            r = lax.rsqrt(jnp.mean(x * x, axis=-1, keepdims=True) + NORM_EPS)
            xhat = x * r
            sg = _sigmoid(gt)
            dz = d_out * gt * sg
            dgt_ref[:, sl] = d_out * xhat * gn * sg * (1.0 + gt * (1.0 - sg))
            dxh = dz * gn
            do_ref[:, sl] = r * (dxh - xhat * jnp.mean(dxh * xhat, axis=-1, keepdims=True))
            part = part + jnp.sum((dz * xhat).reshape(bm // 8, 8, LANES), axis=0)

        @pl.when(i == 0)
        def _():
            acc_ref[...] = part

        @pl.when(i > 0)
        def _():
            acc_ref[...] += part

        @pl.when(i == nsteps - 1)
        def _():
            dgn_ref[...] = jnp.sum(acc_ref[...], axis=0, keepdims=True)

    row = pl.BlockSpec((bm, wd), lambda i: (i, 0))
    one = pl.BlockSpec((1, LANES), lambda i: (0, 0))
    return pl.pallas_call(
        body, name="gdn_outnorm_bwd", grid=(nsteps,),
        in_specs=[row, row, one, row], out_specs=[row, row, one],
        out_shape=[jax.ShapeDtypeStruct((m, wd), F32)] * 2 + [jax.ShapeDtypeStruct((1, LANES), F32)],
        scratch_shapes=[pltpu.VMEM((8, LANES), F32)],
        compiler_params=_params("arbitrary"),
    )(o, gate, gain, dy)


def _sb_consts(kind):
    ri, ci = _iota2((LANES, LANES), 0), _iota2((LANES, LANES), 1)
    tri = {"suffix_excl": ri > ci, "prefix_incl": ri <= ci, "prefix_excl": ri < ci}[kind]
    return jnp.concatenate([tri.astype(BF16), jnp.ones((LANES, LANES), BF16)], axis=1)


def _sb_stack(a):
    first = _iota2((LANES, LANES), 1) < SB_HEAD_DIM
    return jnp.concatenate([jnp.where(first, a, 0.0), jnp.where(first, 0.0, a)], axis=0)


def _sb_unstack(a2):
    first = _iota2((LANES, LANES), 1) < SB_HEAD_DIM
    return jnp.where(first, a2[:LANES], a2[LANES:])


def _sb_visible(qi, kb, blocks=1):
    shape = (2 * LANES, blocks * LANES)
    kpos = _iota2(shape, 1) + kb * LANES
    qpos = (_iota2(shape, 0) & (LANES - 1)) + qi * LANES
    return (kpos < qpos) & (kpos >= PAD_ROWS)


def _sb_log_gates(z, vis):
    t = jnp.log2(1.0 + jnp.exp2(-jnp.abs(z)))
    lb = jnp.minimum(z, 0.0) - t
    lk = lb - z
    if vis is not None:
        lk = jnp.where(vis, lk, 0.0)
    return lb, lk


def _sb_attention(q, kv):
    m, wd = q.shape
    npair = wd // LANES
    nq = m // LANES
    scale = SB_HEAD_DIM ** -0.5

    def body(q_ref, k_ref, v_ref, o_ref, w_ref, b_ref):
        qi = pl.program_id(1)
        su = _sb_consts("suffix_excl")
        q2 = _sb_stack(q_ref[...] * (scale * math.log2(math.e))).astype(BF16)

        def tiles(kbs, carry, masked):
            run, acc = carry
            r0s = [pl.multiple_of(kb * LANES, LANES) for kb in kbs]
            zs = [lax.dot_general(q2, k_ref[pl.ds(r0, LANES), :], NT, preferred_element_type=F32) for r0 in r0s]
            parts = []
            for kb, r0, z in zip(kbs, r0s, zs):
                vis = _sb_visible(qi, kb) if masked else None
                lb, lk = _sb_log_gates(z, vis)
                parts.append((kb, r0, vis, lb, lk))
            parts = [(kb, r0, vis, lb, _split_dot(lk, su)) for kb, r0, vis, lb, lk in parts]
            for kb, r0, vis, lb, cs in parts:
                w = jnp.exp2(lb + cs[:, :LANES] + run)
                if vis is not None:
                    w = jnp.where(vis, w, 0.0)
                run = run + cs[:, LANES:]
                wb = w.astype(BF16)
                w_ref[kb] = wb
                b_ref[kb] = jnp.exp2(lb).astype(BF16)
                acc = acc + lax.dot_general(wb, v_ref[pl.ds(r0, LANES), :], NN, preferred_element_type=F32)
            return run, acc

        su2 = (_iota2((2 * LANES, 2 * LANES), 0) > _iota2((2 * LANES, 2 * LANES), 1)).astype(BF16)

        def pairs(his, carry, masked):
            run, acc = carry
            r0s = [pl.multiple_of((hi - 1) * LANES, LANES) for hi in his]
            zs = [lax.dot_general(q2, k_ref[pl.ds(r0, 2 * LANES), :], NT, preferred_element_type=F32) for r0 in r0s]
            viss = [_sb_visible(qi, hi - 1, 2) if masked else None for hi in his]
            gates = [_sb_log_gates(z, vis) for z, vis in zip(zs, viss)]
            css = [_split_dot(lk, su2) for _, lk in gates]
            for hi, r0, vis, (lb, lk), cs in zip(his, r0s, viss, gates, css):
                w = jnp.exp2(lb + cs + jnp.concatenate([run, run], axis=1))
                if vis is not None:
                    w = jnp.where(vis, w, 0.0)
                run = run + jnp.broadcast_to(cs[:, 0:1] + lk[:, 0:1], (2 * LANES, LANES))
                wb = w.astype(BF16)
                bb = jnp.exp2(lb).astype(BF16)
                w_ref[hi - 1], w_ref[hi] = wb[:, :LANES], wb[:, LANES:]
                b_ref[hi - 1], b_ref[hi] = bb[:, :LANES], bb[:, LANES:]
                acc = acc + lax.dot_general(wb, v_ref[pl.ds(r0, 2 * LANES), :], NN, preferred_element_type=F32)
            return run, acc

        def maybe(flag, fn, carry):
            return lax.fori_loop(0, flag, lambda j, c: fn(c), carry)

        zero = jnp.zeros((2 * LANES, LANES), F32)
        big = (qi >= 3).astype(jnp.int32)
        carry = maybe(big, lambda c: pairs([qi], c, True), (zero, zero))
        carry = maybe(1 - big, lambda c: tiles([qi], c, True), carry)
        top, bottom = qi - 1 - big, 1 + big
        n_mid = jnp.maximum(top - bottom + 1, 0)
        n8 = n_mid // 8
        carry = lax.fori_loop(0, n8, lambda j, c: pairs([top - 8 * j - 2 * u for u in range(4)], c, False), carry)
        top = top - 8 * n8
        take4, take2, take1 = (n_mid >> 2) & 1, (n_mid >> 1) & 1, n_mid & 1
        carry = maybe(take4, lambda c: pairs([top, top - 2], c, False), carry)
        top = top - 4 * take4
        carry = maybe(take2, lambda c: pairs([top], c, False), carry)
        carry = maybe(take1, lambda c: tiles([bottom], c, False), carry)
        carry = maybe(big, lambda c: pairs([1], c, True), carry)
        carry = maybe((1 - big) * (qi >= 1).astype(jnp.int32), lambda c: tiles([0], c, True), carry)
        o_ref[...] = _sb_unstack(carry[1])

    qspec = pl.BlockSpec((LANES, LANES), lambda hp, qi: (qi, hp))
    saved = pl.BlockSpec((None, None, nq, 2 * LANES, LANES), lambda hp, qi: (hp, qi, 0, 0, 0))
    saved_shape = jax.ShapeDtypeStruct((npair, nq, nq, 2 * LANES, LANES), BF16)
    return pl.pallas_call(
        body, name="sb_attention", grid=(npair, nq),
        in_specs=[qspec, pl.BlockSpec((m, LANES), lambda hp, qi: (0, hp)),
                  pl.BlockSpec((m, LANES), lambda hp, qi: (0, npair + hp))],
        out_specs=[qspec, saved, saved],
        out_shape=[jax.ShapeDtypeStruct((m, wd), F32), saved_shape, saved_shape],
        compiler_params=_params("parallel", "arbitrary"),
    )(q, kv, kv)


def _sb_attention_bwd(q, kv, w_all, b_all, do):
    m, wd = q.shape
    npair = wd // LANES
    nq = m // LANES
    scale = SB_HEAD_DIM ** -0.5

    def body(q_ref, k_ref, v_ref, w_ref, b_ref, do_ref, dq_ref, dk_ref, dv_ref):
        qi = pl.program_id(1)

        @pl.when(qi == 0)
        def _():
            dk_ref[...] = jnp.zeros_like(dk_ref)
            dv_ref[...] = jnp.zeros_like(dv_ref)

        px = _sb_consts("prefix_excl")
        q2 = _sb_stack(q_ref[...] * scale).astype(BF16)
        do2 = _sb_stack(do_ref[...]).astype(BF16)

        def tiles(kbs, carry, masked):
            run_p, dq_acc = carry
            r0s = [pl.multiple_of(kb * LANES, LANES) for kb in kbs]
            das = [lax.dot_general(do2, v_ref[pl.ds(r0, LANES), :], NT, preferred_element_type=F32) for r0 in r0s]
            wbs = [w_ref[kb] for kb in kbs]
            for wb, r0 in zip(wbs, r0s):
                dv_ref[pl.ds(r0, LANES), :] += lax.dot_general(wb, do2, TN, preferred_element_type=F32)
            pws = [wb.astype(F32) * da for wb, da in zip(wbs, das)]
            pss = [_split_dot(pw, px, parts=SB_BWD_SUM_PARTS) for pw in pws]
            dzs = []
            for kb, pw, ps in zip(kbs, pws, pss):
                beta = b_ref[kb].astype(F32)
                dz = pw * (1.0 - beta) - beta * (run_p + ps[:, :LANES])
                if masked:
                    dz = jnp.where(_sb_visible(qi, kb), dz, 0.0)
                run_p = run_p + ps[:, LANES:]
                dzs.append(dz.astype(BF16))
            for r0, dz in zip(r0s, dzs):
                dq_acc = dq_acc + lax.dot_general(dz, k_ref[pl.ds(r0, LANES), :], NN, preferred_element_type=F32)
                dk_ref[pl.ds(r0, LANES), :] += lax.dot_general(dz, q2, TN, preferred_element_type=F32)
            return run_p, dq_acc

        px2 = (_iota2((2 * LANES, 2 * LANES), 0) < _iota2((2 * LANES, 2 * LANES), 1)).astype(BF16)

        def pairs(los, carry, masked):
            run_p, dq_acc = carry
            r0s = [pl.multiple_of(lo * LANES, LANES) for lo in los]
            das = [lax.dot_general(do2, v_ref[pl.ds(r0, 2 * LANES), :], NT, preferred_element_type=F32) for r0 in r0s]
            wbs = [jnp.concatenate([w_ref[lo], w_ref[lo + 1]], axis=1) for lo in los]
            for wb, r0 in zip(wbs, r0s):
                dv_ref[pl.ds(r0, 2 * LANES), :] += lax.dot_general(wb, do2, TN, preferred_element_type=F32)
            pws = [wb.astype(F32) * da for wb, da in zip(wbs, das)]
            pss = [_split_dot(pw, px2, parts=SB_BWD_SUM_PARTS) for pw in pws]
            dzs = []
            for lo, pw, ps in zip(los, pws, pss):
                beta = jnp.concatenate([b_ref[lo], b_ref[lo + 1]], axis=1).astype(F32)
                dz = pw * (1.0 - beta) - beta * (jnp.concatenate([run_p, run_p], axis=1) + ps)
                if masked:
                    dz = jnp.where(_sb_visible(qi, lo, 2), dz, 0.0)
                last = 2 * LANES - 1
                run_p = run_p + jnp.broadcast_to(ps[:, last:] + pw[:, last:], (2 * LANES, LANES))
                dzs.append(dz.astype(BF16))
            for r0, dz in zip(r0s, dzs):
                dq_acc = dq_acc + lax.dot_general(dz, k_ref[pl.ds(r0, 2 * LANES), :], NN, preferred_element_type=F32)
                dk_ref[pl.ds(r0, 2 * LANES), :] += lax.dot_general(dz, q2, TN, preferred_element_type=F32)
            return run_p, dq_acc

        def maybe(flag, fn, carry):
            return lax.fori_loop(0, flag, lambda j, c: fn(c), carry)

        zero = jnp.zeros((2 * LANES, LANES), F32)
        big = (qi >= 3).astype(jnp.int32)
        carry = maybe(big, lambda c: pairs([0], c, True), (zero, zero))
        carry = maybe(1 - big, lambda c: tiles([0], c, True), carry)
        low, high = 1 + big, qi - 1 - big
        n_mid = jnp.maximum(high - low + 1, 0)
        n8 = n_mid // 8
        carry = lax.fori_loop(0, n8, lambda j, c: pairs([low + 8 * j + 2 * u for u in range(4)], c, False), carry)
        low = low + 8 * n8
        take4, take2, take1 = (n_mid >> 2) & 1, (n_mid >> 1) & 1, n_mid & 1
        carry = maybe(take4, lambda c: pairs([low, low + 2], c, False), carry)
        low = low + 4 * take4
        carry = maybe(take2, lambda c: pairs([low], c, False), carry)
        carry = maybe(take1, lambda c: tiles([high], c, False), carry)
        carry = maybe(big, lambda c: pairs([qi - 1], c, True), carry)
        carry = maybe((1 - big) * (qi >= 1).astype(jnp.int32), lambda c: tiles([qi], c, True), carry)
        dq_ref[...] = _sb_unstack(carry[1]) * scale

    qspec = pl.BlockSpec((LANES, LANES), lambda hp, qi: (qi, hp))
    strip = pl.BlockSpec((m, LANES), lambda hp, qi: (0, hp))
    saved = pl.BlockSpec((None, None, nq, 2 * LANES, LANES), lambda hp, qi: (hp, qi, 0, 0, 0))
    big = jax.ShapeDtypeStruct((m, wd), F32)
    return pl.pallas_call(
        body, name="sb_attention_bwd", grid=(npair, nq),
        in_specs=[qspec, strip, pl.BlockSpec((m, LANES), lambda hp, qi: (0, npair + hp)), saved, saved, qspec],
        out_specs=[qspec, strip, strip],
        out_shape=[big, big, big],
        compiler_params=_params("parallel", "arbitrary"),
    )(q, kv, kv, w_all, b_all, do)


def _mesh_pos():
    return lax.axis_index("x"), lax.axis_index("y"), lax.axis_index("c")


def _exchange(srcs, name, scatter):
    n = len(srcs)
    rows = [s.shape[-2] for s in srcs]
    offs = [sum(rows[:i]) for i in range(n)]
    total = sum(rows)

    def body(*refs):
        src_refs, out_ref = refs[:n], refs[n]
        send_sems, recv_sems, local_sems = refs[n + 1:]
        x, y, c = _mesh_pos()
        me = 4 * x + 2 * y + c

        def piece(i, d):
            return src_refs[i].at[d] if scatter else src_refs[i]

        def window(slot, i):
            return out_ref.at[slot, pl.ds(offs[i], rows[i])]

        local = [pltpu.make_async_copy(piece(i, me), window(me, i), local_sems.at[i]) for i in range(n)]
        for cp in local:
            cp.start()
        peers = []
        for k in range(1, N_DEV):
            px = 1 - x if k & 4 else x
            py = 1 - y if k & 2 else y
            pc = 1 - c if k & 1 else c
            pid = 4 * px + 2 * py + pc
            for i in range(n):
                pltpu.make_async_remote_copy(
                    src_ref=piece(i, pid), dst_ref=window(me, i),
                    send_sem=send_sems.at[k - 1], recv_sem=recv_sems.at[k - 1],
                    device_id=(px, py, pc), device_id_type=pl.DeviceIdType.MESH).start()
            peers.append(pid)
        slabs = [pltpu.make_async_remote_copy(
            src_ref=out_ref.at[me], dst_ref=out_ref.at[peers[k - 1]],
            send_sem=send_sems.at[k - 1], recv_sem=recv_sems.at[k - 1],
            device_id=(x, y, c), device_id_type=pl.DeviceIdType.MESH) for k in range(1, N_DEV)]
        for cp in slabs:
            cp.wait_recv()
        for cp in slabs:
            cp.wait_send()
        for cp in local:
            cp.wait()

    return pl.pallas_call(
        body, name=name,
        in_specs=[pl.BlockSpec(memory_space=pl.ANY)] * n, out_specs=pl.BlockSpec(memory_space=pl.ANY),
        out_shape=jax.ShapeDtypeStruct((N_DEV, total, FLAT_COLS), srcs[0].dtype),
        scratch_shapes=[pltpu.SemaphoreType.DMA((N_DEV - 1,)), pltpu.SemaphoreType.DMA((N_DEV - 1,)),
                        pltpu.SemaphoreType.DMA((n,))],
        compiler_params=pltpu.CompilerParams(has_side_effects=True),
    )(*srcs)


def _adamw(parts, w, mom, var, name):
    rows = w.shape[0]
    br = FLAT_ROW_BLOCK
    c1 = 1.0 - ADAM_B1 ** ADAM_STEP
    c2 = 1.0 - ADAM_B2 ** ADAM_STEP

    def body(p_ref, w_ref, m_ref, v_ref, g_ref, d_ref, nm_ref, nv_ref):
        g = p_ref[0].astype(F32)
        for s in range(1, N_DEV):
            g = g + p_ref[s].astype(F32)
        m_new = ADAM_B1 * m_ref[...] + (1.0 - ADAM_B1) * g
        v_new = ADAM_B2 * v_ref[...] + (1.0 - ADAM_B2) * (g * g)
        m_hat = m_new / c1
        v_hat = v_new / c2
        g_ref[...] = g
        d_ref[...] = -ADAM_LR * (m_hat / (jnp.sqrt(v_hat) + ADAM_EPS) + ADAM_WD * w_ref[...])
        nm_ref[...] = m_new
        nv_ref[...] = v_new

    row = pl.BlockSpec((br, FLAT_COLS), lambda i: (i, 0))
    flat = jax.ShapeDtypeStruct((rows, FLAT_COLS), F32)
    return pl.pallas_call(
        body, name=name, grid=(rows // br,),
        in_specs=[pl.BlockSpec((N_DEV, br, FLAT_COLS), lambda i: (0, i, 0)), row, row, row],
        out_specs=[row] * 4, out_shape=[flat] * 4,
        compiler_params=_params("parallel"),
    )(parts, w, mom, var)


SMALL_SHARDED = (("meta_tokens", 1), ("gdn_norm_g", 1), ("gdn_conv_w", 2))
BIG_PIECES = (("gdn_w_in", 0, True), ("gdn_w_out", 0, False), ("w_kv", None, True), ("sb_w_q", 0, False),
              ("sb_w_o", 0, False), ("ffn_w_gate_up", 0, True), ("ffn_w_gate_up", 1, True),
              ("ffn_w_down", 0, False), ("ffn_w_down", 1, False))
REPLICATED = ("gdn_a_log", "gdn_dt_bias", "gdn_onorm_g", "kv_norm_g", "sb_norm_g", "ffn_norm_g", "final_norm_g")
WEIGHTS = ("meta_tokens", "gdn_norm_g", "gdn_w_in", "gdn_conv_w", "gdn_a_log", "gdn_dt_bias", "gdn_onorm_g",
           "gdn_w_out", "kv_norm_g", "w_kv", "sb_norm_g", "sb_w_q", "sb_w_o", "ffn_norm_g", "ffn_w_gate_up",
           "ffn_w_down", "final_norm_g")
F32_ROWS = 8
BF16_ROWS = 16


def _n_rows(shape, mult):
    return -(-math.prod(shape) // (mult * FLAT_COLS)) * mult


def _as_rows(a, nl, mult):
    lead = a.shape[:nl]
    size = math.prod(a.shape[nl:])
    nr = _n_rows(a.shape[nl:], mult)
    flat = a.reshape(lead + (size,))
    if nr * FLAT_COLS != size:
        flat = jnp.pad(flat, [(0, 0)] * nl + [(0, nr * FLAT_COLS - size)])
    return flat.reshape(lead + (nr, FLAT_COLS))


def _from_rows(flat, r0, shape, nl, mult):
    lead = flat.shape[:nl]
    nr = _n_rows(shape, mult)
    seg = lax.slice_in_dim(flat, r0, r0 + nr, axis=nl).reshape(lead + (nr * FLAT_COLS,))
    return lax.slice_in_dim(seg, 0, math.prod(shape), axis=nl).reshape(lead + tuple(shape)), r0 + nr


def _to_piece(shard, layer, transposed):
    mat = shard if layer is None else shard[layer]
    return mat.T if transposed else mat


def _from_pieces(pieces, name):
    mats = [p.T if tr else p for (n, layer, tr), p in zip(BIG_PIECES, pieces) if n == name]
    layers = [layer for n, layer, _ in BIG_PIECES if n == name]
    return mats[0] if layers[0] is None else jnp.stack(mats)


def _merge_shards(g, axis):
    t = jnp.moveaxis(g, 0, axis)
    shp = t.shape
    return t.reshape(shp[:axis] + (shp[axis] * shp[axis + 1],) + shp[axis + 2:])


def _split_shards(full, axis):
    shp = full.shape
    t = full.reshape(shp[:axis] + (N_DEV, shp[axis] // N_DEV) + shp[axis + 1:])
    return jnp.moveaxis(t, axis, 0)


def _pad_lanes(a, width=LANES):
    return jnp.pad(a, ((0, 0), (0, width - a.shape[1])))


def _local_step(x, target, w, mats):
    d = x.shape[1]
    gh = w["gdn_a_log"].shape[1]
    gw = gh * GDN_HEAD_DIM
    w_int = mats["gdn_w_in", 0]
    w_qkvt, w_gatet = w_int[:3 * gw], w_int[3 * gw:4 * gw]
    w_abt = jnp.pad(w_int[4 * gw:], ((0, LANES - 2 * gh), (0, 0)))
    conv_w = w["gdn_conv_w"][0]
    a_log, dt_bias = _pad_lanes(w["gdn_a_log"]), _pad_lanes(w["gdn_dt_bias"])
    w_out = mats["gdn_w_out", 0]
    w_kvt = mats["w_kv", None]
    sbw = w_kvt.shape[0] // 2
    w_kt, w_vt = w_kvt[:sbw], w_kvt[sbw:]
    w_q, w_o = mats["sb_w_q", 0], mats["sb_w_o", 0]
    w_gut = [mats["ffn_w_gate_up", l] for l in range(2)]
    w_dn = [mats["ffn_w_down", l] for l in range(2)]
    ffn_g = [w["ffn_norm_g"][l:l + 1] for l in range(2)]
    kv_g, fin_g = w["kv_norm_g"][None], w["final_norm_g"][None]

    h0 = jnp.concatenate([jnp.zeros((PAD_ROWS, d), F32), w["meta_tokens"], x], axis=0)

    (n1,) = _rmsnorm(h0, [w["gdn_norm_g"]], "norm_gdn")
    proj = _matmul(n1, w_qkvt, "nt", "mm_gdn_qkv")
    gate = _matmul(n1, w_gatet, "nt", "mm_gdn_gate")
    ab = _matmul(n1, w_abt, "nt", "mm_gdn_ab")
    qkv = _gdn_conv(proj, conv_w, gh)
    g_b, beta_b = _gdn_gates(ab, a_log, dt_bias, gh)
    ub, wc, qd, kd, gc, pmat, tinv = _gdn_prep(qkv, g_b, beta_b, gh)
    o_gdn, ssave = _gdn_scan(ub, wc, qd, kd, gc, pmat, gh)
    o2 = _gdn_outnorm(o_gdn, gate, w["gdn_onorm_g"], gh)
    h1 = _matmul(o2, w_out, "nn", "mm_gdn_out", add=h0)

    def ffn_fwd(h, l):
        (n,) = _rmsnorm(h, [ffn_g[l]], f"norm_ffn{l}")
        gu = _matmul(n, w_gut[l], "nt", f"mm_ffn{l}_gu")
        act = _swiglu(gu, f"swiglu{l}")
        return n, gu, act, _matmul(act, w_dn[l], "nn", f"mm_ffn{l}_down", add=h)

    n2, gu0, act0, h2 = ffn_fwd(h1, 0)

    nkv, n3 = _rmsnorm(h2, [kv_g, w["sb_norm_g"]], "norm_kv_sb")
    kv = _matmul(nkv, w_kvt, "nt", "mm_kv", out_dtype=BF16)
    q_sb = _matmul(n3, w_q, "nn", "mm_sb_q")
    o_sb, w_sb, beta_sb = _sb_attention(q_sb, kv)
    h3 = _matmul(o_sb, w_o, "nn", "mm_sb_o", add=h2)
    n4, gu1, act1, h4 = ffn_fwd(h3, 1)

    loss, dh4, d_fin = _loss_head(h4, fin_g, target)

    def ffn_bwd(dh, h, n, gu, act, l):
        d_act = _matmul(dh, w_dn[l], "nt", f"mm_ffn{l}_dact")
        dw_dn = _matmul(act, dh, "tn", f"mm_ffn{l}_dwdown")
        dgu = _swiglu_bwd(gu, d_act, f"swiglu_bwd{l}")
        dw_gu = _matmul(dgu, n, "tn", f"mm_ffn{l}_dwgu")
        dn = _matmul(dgu, w_gut[l], "nn", f"mm_ffn{l}_dn")
        dh_in, (dg,) = _rmsnorm_bwd(h, [ffn_g[l]], [dn], dh, f"norm_ffn{l}_bwd")
        return dh_in, dw_gu, dw_dn, dg

    dh3, dw_gu1, dw_dn1, dg_ffn1 = ffn_bwd(dh4, h3, n4, gu1, act1, 1)

    do_sb = _matmul(dh3, w_o, "nt", "mm_sb_do")
    dw_o = _matmul(o_sb, dh3, "tn", "mm_sb_dwo")
    dq_sb, dk_sb, dv_sb = _sb_attention_bwd(q_sb, kv, w_sb, beta_sb, do_sb)
    dw_q = _matmul(n3, dq_sb, "tn", "mm_sb_dwq")
    dn3 = _matmul(dq_sb, w_q, "nt", "mm_sb_dn")
    dw_k = _matmul(dk_sb, nkv, "tn", "mm_dwk")
    dw_v = _matmul(dv_sb, nkv, "tn", "mm_dwv")
    dnkv = _matmul(dk_sb, w_kt, "nn", "mm_dnk")
    dnkv = _matmul(dv_sb, w_vt, "nn", "mm_dnv", add=dnkv)
    dh2, (dg_kv, dg_sb) = _rmsnorm_bwd(h2, [kv_g, w["sb_norm_g"]], [dnkv, dn3], dh3, "norm_kv_sb_bwd")

    dh1, dw_gu0, dw_dn0, dg_ffn0 = ffn_bwd(dh2, h1, n2, gu0, act0, 0)

    do2 = _matmul(dh1, w_out, "nt", "mm_gdn_do")
    dw_out = _matmul(o2, dh1, "tn", "mm_gdn_dwout")
    do_gdn, dgate, d_onorm = _gdn_outnorm_bwd(o_gdn, gate, w["gdn_onorm_g"], do2, gh)
    dub, dwc, dqd, dkd, dpm, dgl = _gdn_scan_bwd(do_gdn, ssave, ub, wc, qd, kd, gc, pmat, gh)
    dqkv, dg_b, dbeta_b = _gdn_prep_bwd(qkv, g_b, beta_b, ub, wc, tinv, dub, dwc, dqd, dkd, dpm, dgl, gh)
    dab, d_alog, d_dtb = _gdn_gates_bwd(ab, a_log, dt_bias, dg_b, dbeta_b, gh)
    dproj, d_conv = _gdn_conv_bwd(proj, conv_w, dqkv, gh)
    dw_qkv = _matmul(dproj, n1, "tn", "mm_gdn_dwqkv")
    dw_gate = _matmul(dgate, n1, "tn", "mm_gdn_dwgate")
    dw_ab = _matmul(dab, n1, "tn", "mm_gdn_dwab")
    dn1 = _matmul(dproj, w_qkvt, "nn", "mm_gdn_dn_qkv")
    dn1 = _matmul(dgate, w_gatet, "nn", "mm_gdn_dn_gate", add=dn1)
    dn1 = _matmul(dab, w_abt, "nn", "mm_gdn_dn_ab", add=dn1)
    dh0, (dg_gdn,) = _rmsnorm_bwd(h0, [w["gdn_norm_g"]], [dn1], dh1, "norm_gdn_bwd")

    grads = {
        "meta_tokens": dh0[PAD_ROWS:LANES],
        "gdn_norm_g": dg_gdn,
        "gdn_conv_w": d_conv[None],
        "gdn_a_log": d_alog[:, :gh],
        "gdn_dt_bias": d_dtb[:, :gh],
        "gdn_onorm_g": d_onorm,
        "kv_norm_g": dg_kv[0],
        "sb_norm_g": dg_sb,
        "ffn_norm_g": jnp.concatenate([dg_ffn0, dg_ffn1], axis=0),
        "final_norm_g": d_fin[0],
    }
    gmats = {
        ("gdn_w_in", 0): jnp.concatenate([dw_qkv, dw_gate, dw_ab[:2 * gh]], axis=0),
        ("gdn_w_out", 0): dw_out,
        ("w_kv", None): jnp.concatenate([dw_k, dw_v], axis=0),
        ("sb_w_q", 0): dw_q,
        ("sb_w_o", 0): dw_o,
        ("ffn_w_gate_up", 0): dw_gu0,
        ("ffn_w_gate_up", 1): dw_gu1,
        ("ffn_w_down", 0): dw_dn0,
        ("ffn_w_down", 1): dw_dn1,
    }
    return loss, dh0[LANES:], grads, gmats


def kernel(x, meta_tokens, gdn_norm_g, gdn_w_in, gdn_conv_w, gdn_a_log, gdn_dt_bias, gdn_onorm_g, gdn_w_out, kv_norm_g, w_kv, sb_norm_g, sb_w_q, sb_w_o, ffn_norm_g, ffn_w_gate_up, ffn_w_down, final_norm_g, loss_target, m_meta_tokens, m_gdn_norm_g, m_gdn_w_in, m_gdn_conv_w, m_gdn_a_log, m_gdn_dt_bias, m_gdn_onorm_g, m_gdn_w_out, m_kv_norm_g, m_w_kv, m_sb_norm_g, m_sb_w_q, m_sb_w_o, m_ffn_norm_g, m_ffn_w_gate_up, m_ffn_w_down, m_final_norm_g, v_meta_tokens, v_gdn_norm_g, v_gdn_w_in, v_gdn_conv_w, v_gdn_a_log, v_gdn_dt_bias, v_gdn_onorm_g, v_gdn_w_out, v_kv_norm_g, v_w_kv, v_sb_norm_g, v_sb_w_q, v_sb_w_o, v_ffn_norm_g, v_ffn_w_gate_up, v_ffn_w_down, v_final_norm_g):
    given = dict(locals())
    small = [n for n, _ in SMALL_SHARDED]
    big_names = list(dict.fromkeys(n for n, _, _ in BIG_PIECES))

    def local_pieces(prefix):
        return [_to_piece(given[prefix + n], layer, tr) for n, layer, tr in BIG_PIECES]

    piece_shapes = [p.shape for p in local_pieces("")]

    g_small = _exchange([jnp.concatenate([_as_rows(given[n], 0, F32_ROWS) for n in small], axis=0)],
                        "all_gather_small", scatter=False)
    g_big = _exchange([_as_rows(p.astype(BF16), 0, BF16_ROWS) for p in local_pieces("")],
                      "all_gather_weights", scatter=False)
    full = {n: given[n] for n in REPLICATED}
    r = 0
    for n, ax in SMALL_SHARDED:
        shards, r = _from_rows(g_small, r, given[n].shape, 1, F32_ROWS)
        full[n] = _merge_shards(shards, ax)
    mats, r = {}, 0
    for (n, layer, _), shp in zip(BIG_PIECES, piece_shapes):
        rows8, r = _from_rows(g_big, r, shp, 1, BF16_ROWS)
        mats[n, layer] = rows8.reshape((N_DEV * shp[0],) + shp[1:])

    loss_b, grad_x, grads, gmats = _local_step(x[0], loss_target[0], full, mats)

    order = small + list(REPLICATED)
    fill_small = (-sum(_n_rows(given[n].shape, F32_ROWS) for n in order)) % FLAT_ROW_BLOCK
    fill_big = (-sum(_n_rows(shp, BF16_ROWS) for shp in piece_shapes)) % FLAT_ROW_BLOCK
    small_send = [_as_rows(_split_shards(grads[n], ax), 1, F32_ROWS) for n, ax in SMALL_SHARDED]
    small_send += [_as_rows(jnp.broadcast_to(grads[n][None], (N_DEV,) + grads[n].shape), 1, F32_ROWS)
                   for n in REPLICATED]
    if fill_small:
        small_send.append(jnp.zeros((N_DEV, fill_small, FLAT_COLS), F32))
    recv_small = _exchange([jnp.concatenate(small_send, axis=1)], "scatter_small_gradients", scatter=True)
    big_send = [_as_rows(gmats[n, layer].reshape((N_DEV,) + shp), 1, BF16_ROWS)
                for (n, layer, _), shp in zip(BIG_PIECES, piece_shapes)]
    if fill_big:
        big_send.append(jnp.zeros((N_DEV, fill_big, FLAT_COLS), BF16))
    recv_big = _exchange(big_send, "scatter_gradients", scatter=True)

    def flat_small(prefix):
        segs = [_as_rows(given[prefix + n], 0, F32_ROWS) for n in order]
        return jnp.concatenate(segs + ([jnp.zeros((fill_small, FLAT_COLS), F32)] if fill_small else []), axis=0)

    def flat_big(prefix):
        segs = [_as_rows(p, 0, BF16_ROWS) for p in local_pieces(prefix)]
        return jnp.concatenate(segs + ([jnp.zeros((fill_big, FLAT_COLS), F32)] if fill_big else []), axis=0)

    upd_small = _adamw(recv_small, flat_small(""), flat_small("m_"), flat_small("v_"), "adamw_small")
    upd_big = _adamw(recv_big, flat_big(""), flat_big("m_"), flat_big("v_"), "adamw")
    outs = {}
    for tag, f_small, f_big in zip(("grad", "delta", "new_m", "new_v"), upd_small, upd_big):
        outs[tag], r = {}, 0
        for n in order:
            outs[tag][n], r = _from_rows(f_small, r, given[n].shape, 0, F32_ROWS)
        pieces, r = [], 0
        for shp in piece_shapes:
            p, r = _from_rows(f_big, r, shp, 0, BF16_ROWS)
            pieces.append(p)
        for n in big_names:
            outs[tag][n] = _from_pieces(pieces, n)

    loss = lax.psum(loss_b[0, 0], ("x", "y", "c"))
    result = [loss, grad_x[None]]
    for tag in ("grad", "delta", "new_m", "new_v"):
        result += [outs[tag][n] for n in WEIGHTS]
    return tuple(result)
```

```python
import functools
import math

import jax
import jax.numpy as jnp
from jax import lax
from jax.experimental import pallas as pl
from jax.experimental.pallas import tpu as pltpu

F32 = jnp.float32
BF16 = jnp.bfloat16

N_DEV = 8
LANES = 128
N_META = 16
PAD_ROWS = LANES - N_META
GDN_CHUNK = 64
GDN_HEAD_DIM = 128
SB_HEAD_DIM = 64
SB_BWD_SUM_PARTS = 1
CONV_WIDTH = 4
NORM_EPS = 1e-6
FLAT_COLS = 1024
FLAT_ROW_BLOCK = 64
VMEM_LIMIT = 56 * 1024 * 1024

ADAM_LR = 0.001
ADAM_B1 = 0.9
ADAM_B2 = 0.999
ADAM_EPS = 1e-08
ADAM_WD = 0.01
ADAM_STEP = 10

NN = (((1,), (0,)), ((), ()))
NT = (((1,), (1,)), ((), ()))
TN = (((0,), (0,)), ((), ()))


def _params(*sem):
    return pltpu.CompilerParams(dimension_semantics=sem, vmem_limit_bytes=VMEM_LIMIT)


def _pick(n, cands):
    for c in cands:
        if n % c == 0:
            return c
    return n


def _bdot(a, b, dims=NN):
    return lax.dot_general(a.astype(BF16), b.astype(BF16), dims, preferred_element_type=F32)


def _hdot(a, b, dims=NN):
    return lax.dot_general(a, b, dims, preferred_element_type=F32, precision=lax.Precision.HIGH)


def _split_dot(a, m, parts=2):
    out = None
    for _ in range(parts):
        piece = a.astype(BF16)
        a = a - piece.astype(F32)
        term = lax.dot_general(piece, m, NN, preferred_element_type=F32)
        out = term if out is None else out + term
    return out


def _lockstep(gens):
    while gens:
        alive = []
        for g in gens:
            try:
                next(g)
                alive.append(g)
            except StopIteration:
                pass
        gens = alive


def _sigmoid(x):
    return 1.0 / (1.0 + jnp.exp(-x))


def _iota2(shape, axis):
    return lax.broadcasted_iota(jnp.int32, shape, axis)


def _matmul(a, b, mode, name, out_dtype=None, add=None):
    if out_dtype is None:
        out_dtype = BF16 if mode == "tn" else F32
    if mode == "nn":
        (m, k), n = a.shape, b.shape[1]
    elif mode == "nt":
        (m, k), n = a.shape, b.shape[0]
    else:
        (k, m), n = a.shape, b.shape[1]
    row_c = (640, 512, 384, 256, 128)
    col_c = (1024, 1408, 768, 512, 384, 256, 128)
    if mode == "tn":
        bm, bn, bk = _pick(m, col_c), _pick(n, col_c), _pick(k, row_c)
    else:
        bm, bn, bk = _pick(m, row_c), _pick(n, col_c), _pick(k, (1024, 1408, 768, 512, 256, 128))
    nk = k // bk
    ni, nj = m // bm, n // bn
    dims = {"nn": NN, "nt": NT, "tn": TN}[mode]
    a_bytes, b_bytes = a.size * a.dtype.itemsize, b.size * b.dtype.itemsize
    rows_outer = (a_bytes * (1 if nk == 1 else nj) + b_bytes * ni
                  <= b_bytes * (1 if nk == 1 else ni) + a_bytes * nj)

    def at(f):
        return (lambda g0, g1, q: f(g0, g1, q)) if rows_outer else (lambda g0, g1, q: f(g1, g0, q))

    a_spec = {"nn": pl.BlockSpec((bm, bk), at(lambda i, j, q: (i, q))),
              "nt": pl.BlockSpec((bm, bk), at(lambda i, j, q: (i, q))),
              "tn": pl.BlockSpec((bk, bm), at(lambda i, j, q: (q, i)))}[mode]
    b_spec = {"nn": pl.BlockSpec((bk, bn), at(lambda i, j, q: (q, j))),
              "nt": pl.BlockSpec((bn, bk), at(lambda i, j, q: (j, q))),
              "tn": pl.BlockSpec((bk, bn), at(lambda i, j, q: (q, j)))}[mode]
    o_spec = pl.BlockSpec((bm, bn), at(lambda i, j, q: (i, j)))
    has_add = add is not None

    def body(*refs):
        if has_add:
            a_ref, b_ref, add_ref, o_ref, acc_ref = refs
        else:
            a_ref, b_ref, o_ref, acc_ref = refs
        q = pl.program_id(2)
        part = _bdot(a_ref[...], b_ref[...], dims)

        @pl.when(q == 0)
        def _():
            acc_ref[...] = part

        @pl.when(q > 0)
        def _():
            acc_ref[...] += part

        @pl.when(q == nk - 1)
        def _():
            r = acc_ref[...]
            if has_add:
                r = r + add_ref[...]
            o_ref[...] = r.astype(out_dtype)

    ins = [a, b] + ([add] if has_add else [])
    in_specs = [a_spec, b_spec] + ([o_spec] if has_add else [])
    return pl.pallas_call(
        body, name=name, grid=(ni, nj, nk) if rows_outer else (nj, ni, nk),
        in_specs=in_specs, out_specs=o_spec,
        out_shape=jax.ShapeDtypeStruct((m, n), out_dtype),
        scratch_shapes=[pltpu.VMEM((bm, bn), F32)],
        compiler_params=_params("parallel", "parallel", "arbitrary"),
    )(*ins)


def _rmsnorm(h, gains, name):
    m, d = h.shape
    bm = _pick(m, (640, 384, 128))
    ng = len(gains)

    def body(*refs):
        h_ref, g_refs, o_refs = refs[0], refs[1:1 + ng], refs[1 + ng:]
        x = h_ref[...]
        xhat = x * lax.rsqrt(jnp.mean(x * x, axis=-1, keepdims=True) + NORM_EPS)
        for g_ref, o_ref in zip(g_refs, o_refs):
            o_ref[...] = (xhat * g_ref[...]).astype(BF16)

    row = pl.BlockSpec((bm, d), lambda i: (i, 0))
    gain = pl.BlockSpec((1, d), lambda i: (0, 0))
    return pl.pallas_call(
        body, name=name, grid=(m // bm,),
        in_specs=[row] + [gain] * ng, out_specs=[row] * ng,
        out_shape=[jax.ShapeDtypeStruct((m, d), BF16)] * ng,
        compiler_params=_params("parallel"),
    )(h, *gains)


def _rmsnorm_bwd(h, gains, dns, dres, name):
    m, d = h.shape
    bm = _pick(m, (640, 384, 128))
    ng = len(gains)
    nsteps = m // bm

    def body(*refs):
        h_ref, dres_ref = refs[0], refs[1]
        g_refs = refs[2:2 + ng]
        dn_refs = refs[2 + ng:2 + 2 * ng]
        dh_ref = refs[2 + 2 * ng]
        dg_refs = refs[3 + 2 * ng:3 + 3 * ng]
        acc_refs = refs[3 + 3 * ng:]
        i = pl.program_id(0)
        x = h_ref[...]
        r = lax.rsqrt(jnp.mean(x * x, axis=-1, keepdims=True) + NORM_EPS)
        xhat = x * r
        dh = dres_ref[...]
        for g_ref, dn_ref, dg_ref, acc_ref in zip(g_refs, dn_refs, dg_refs, acc_refs):
            dn = dn_ref[...]
            dy = dn * g_ref[...]
            dh = dh + r * (dy - xhat * jnp.mean(dy * xhat, axis=-1, keepdims=True))
            part = jnp.sum((dn * xhat).reshape(bm // 8, 8, d), axis=0)

            @pl.when(i == 0)
            def _():
                acc_ref[...] = part

            @pl.when(i > 0)
            def _():
                acc_ref[...] += part

            @pl.when(i == nsteps - 1)
            def _():
                dg_ref[...] = jnp.sum(acc_ref[...], axis=0, keepdims=True)

        dh_ref[...] = dh

    row = pl.BlockSpec((bm, d), lambda i: (i, 0))
    gain = pl.BlockSpec((1, d), lambda i: (0, 0))
    outs = pl.pallas_call(
        body, name=name, grid=(nsteps,),
        in_specs=[row, row] + [gain] * ng + [row] * ng,
        out_specs=[row] + [gain] * ng,
        out_shape=[jax.ShapeDtypeStruct((m, d), F32)] + [jax.ShapeDtypeStruct((1, d), F32)] * ng,
        scratch_shapes=[pltpu.VMEM((8, d), F32)] * ng,
        compiler_params=_params("arbitrary"),
    )(h, dres, *gains, *dns)
    return outs[0], list(outs[1:])


def _loss_head(h, gain, target):
    m, d = h.shape
    nsteps = m // LANES

    def body(h_ref, g_ref, t_ref, loss_ref, dh_ref, dg_ref, lacc, gacc):
        i = pl.program_id(0)

        @pl.when(i == 0)
        def _():
            lacc[...] = jnp.zeros_like(lacc)
            gacc[...] = jnp.zeros_like(gacc)
            dh_ref[...] = jnp.zeros_like(dh_ref)

        @pl.when(i > 0)
        def _():
            x = h_ref[...]
            r = lax.rsqrt(jnp.mean(x * x, axis=-1, keepdims=True) + NORM_EPS)
            xhat = x * r
            g = g_ref[...]
            err = xhat * g - t_ref[...]
            lacc[...] += jnp.sum((err * err).reshape(LANES // 8, 8, d), axis=0)
            dn = err * (1.0 / d)
            dy = dn * g
            dh_ref[...] = r * (dy - xhat * jnp.mean(dy * xhat, axis=-1, keepdims=True))
            gacc[...] += jnp.sum((dn * xhat).reshape(LANES // 8, 8, d), axis=0)

        @pl.when(i == nsteps - 1)
        def _():
            tot = jnp.sum(jnp.sum(lacc[...], axis=1, keepdims=True), axis=0, keepdims=True)
            loss_ref[...] = jnp.broadcast_to(tot * (0.5 / d), (1, LANES))
            dg_ref[...] = jnp.sum(gacc[...], axis=0, keepdims=True)

    row = pl.BlockSpec((LANES, d), lambda i: (i, 0))
    trow = pl.BlockSpec((LANES, d), lambda i: (jnp.maximum(i - 1, 0), 0))
    gain_spec = pl.BlockSpec((1, d), lambda i: (0, 0))
    return pl.pallas_call(
        body, name="loss_head", grid=(nsteps,),
        in_specs=[row, gain_spec, trow],
        out_specs=[pl.BlockSpec((1, LANES), lambda i: (0, 0)), row, gain_spec],
        out_shape=[jax.ShapeDtypeStruct((1, LANES), F32), jax.ShapeDtypeStruct((m, d), F32),
                   jax.ShapeDtypeStruct((1, d), F32)],
        scratch_shapes=[pltpu.VMEM((8, d), F32), pltpu.VMEM((8, d), F32)],
        compiler_params=_params("arbitrary"),
    )(h, gain, target)


def _swiglu(gu, name):
    m, f2 = gu.shape
    f = f2 // 2

    def body(gu_ref, o_ref):
        g = gu_ref[:, :f].astype(F32)
        u = gu_ref[:, f:].astype(F32)
        o_ref[...] = (g * _sigmoid(g) * u).astype(BF16)

    return pl.pallas_call(
        body, name=name, grid=(m // LANES,),
        in_specs=[pl.BlockSpec((LANES, f2), lambda i: (i, 0))],
        out_specs=pl.BlockSpec((LANES, f), lambda i: (i, 0)),
        out_shape=jax.ShapeDtypeStruct((m, f), BF16),
        compiler_params=_params("parallel"),
    )(gu)


def _swiglu_bwd(gu, da, name):
    m, f2 = gu.shape
    f = f2 // 2

    def body(gu_ref, da_ref, o_ref):
        g = gu_ref[:, :f].astype(F32)
        u = gu_ref[:, f:].astype(F32)
        da_ = da_ref[...]
        s = _sigmoid(g)
        o_ref[:, :f] = (da_ * u * s * (1.0 + g * (1.0 - s))).astype(BF16)
        o_ref[:, f:] = (da_ * g * s).astype(BF16)

    return pl.pallas_call(
        body, name=name, grid=(m // LANES,),
        in_specs=[pl.BlockSpec((LANES, f2), lambda i: (i, 0)), pl.BlockSpec((LANES, f), lambda i: (i, 0))],
        out_specs=pl.BlockSpec((LANES, f2), lambda i: (i, 0)),
        out_shape=jax.ShapeDtypeStruct((m, f2), BF16),
        compiler_params=_params("parallel"),
    )(gu, da)


def _conv_taps(xa, w_ref):
    acc = xa * w_ref[CONV_WIDTH - 1:CONV_WIDTH, :]
    for i in range(CONV_WIDTH - 1):
        acc = acc + pltpu.roll(xa, CONV_WIDTH - 1 - i, 0) * w_ref[i:i + 1, :]
    return acc[8:]


def _gdn_conv(proj, conv_w, n_heads):
    m, w3 = proj.shape
    wd = w3 // 3
    nblk = m // LANES
    unroll = _pick(nblk - 1, (4, 2, 1))

    def body(x_ref, w_ref, o_ref):
        c = pl.program_id(0)
        is_qk = c < 2 * n_heads

        def block(b, xa=None):
            r0 = pl.multiple_of(b * LANES, LANES)
            if xa is None:
                xa = x_ref[pl.ds(r0 - 8, LANES + 8), :]
            cv = _conv_taps(xa, w_ref)
            s = cv * _sigmoid(cv)
            ss = jnp.sum(s * s, axis=-1, keepdims=True)
            yield
            o_ref[pl.ds(r0, LANES), :] = jnp.where(is_qk, s * lax.rsqrt(ss + NORM_EPS), s)

        x0 = jnp.concatenate([jnp.zeros((8, LANES), F32), x_ref[pl.ds(0, LANES), :]], axis=0)
        _lockstep([block(0, x0)])

        def step(j, carry):
            _lockstep([block(1 + unroll * j + u) for u in range(unroll)])
            return carry

        lax.fori_loop(0, (nblk - 1) // unroll, step, 0)

    return pl.pallas_call(
        body, name="gdn_conv", grid=(w3 // LANES,),
        in_specs=[pl.BlockSpec((m, LANES), lambda c: (0, c)), pl.BlockSpec((CONV_WIDTH, LANES), lambda c: (0, c))],
        out_specs=pl.BlockSpec((None, m, LANES), lambda c: (c // n_heads, 0, c % n_heads)),
        out_shape=jax.ShapeDtypeStruct((3, m, wd), F32),
        compiler_params=_params("parallel"),
    )(proj, conv_w)


def _gdn_conv_bwd(proj, conv_w, dact, n_heads):
    m, w3 = proj.shape
    nblk = m // LANES
    unroll = _pick(nblk - 1, (4, 2, 1))

    def body(x_ref, w_ref, dy_ref, dx_ref, dw_ref, dc_ref):
        c = pl.program_id(0)
        is_qk = c < 2 * n_heads
        dc_ref[pl.ds(m, 8), :] = jnp.zeros((8, LANES), F32)

        def block(b, out, xa=None):
            r0 = pl.multiple_of(b * LANES, LANES)
            if xa is None:
                xa = x_ref[pl.ds(r0 - 8, LANES + 8), :]
            cv = _conv_taps(xa, w_ref)
            sg = _sigmoid(cv)
            s = cv * sg
            dy = dy_ref[pl.ds(r0, LANES), :]
            ss = jnp.sum(s * s, axis=-1, keepdims=True)
            yield
            rn = lax.rsqrt(ss + NORM_EPS)
            yn = s * rn
            proj_len = jnp.sum(dy * yn, axis=-1, keepdims=True)
            yield
            ds = jnp.where(is_qk, rn * (dy - yn * proj_len), dy)
            dcv = ds * sg * (1.0 + cv * (1.0 - sg))
            dc_ref[pl.ds(r0, LANES), :] = dcv
            parts = []
            for i in range(CONV_WIDTH):
                xs = xa[8:] if i == CONV_WIDTH - 1 else pltpu.roll(xa, CONV_WIDTH - 1 - i, 0)[8:]
                parts.append(jnp.sum((dcv * xs).reshape(LANES // 8, 8, LANES), axis=0))
            out.append(parts)

        def add_parts(acc, outs):
            for parts in outs:
                acc = tuple(a + p for a, p in zip(acc, parts))
            return acc

        x0 = jnp.concatenate([jnp.zeros((8, LANES), F32), x_ref[pl.ds(0, LANES), :]], axis=0)
        first = []
        _lockstep([block(0, first, x0)])
        dwacc = tuple(first[0])

        def step1(j, acc):
            outs = []
            _lockstep([block(1 + unroll * j + u, outs) for u in range(unroll)])
            return add_parts(acc, outs)

        dwacc = lax.fori_loop(0, (nblk - 1) // unroll, step1, dwacc)
        for i in range(CONV_WIDTH):
            dw_ref[i:i + 1, :] = jnp.sum(dwacc[i], axis=0, keepdims=True)

        def step2(b, carry):
            r0 = pl.multiple_of(b * LANES, LANES)
            da = dc_ref[pl.ds(r0, LANES + 8), :]
            acc = da * w_ref[CONV_WIDTH - 1:CONV_WIDTH, :]
            for i in range(CONV_WIDTH - 1):
                sh = CONV_WIDTH - 1 - i
                acc = acc + pltpu.roll(da, LANES + 8 - sh, 0) * w_ref[i:i + 1, :]
            dx_ref[pl.ds(r0, LANES), :] = acc[:LANES]
            return carry

        lax.fori_loop(0, nblk, step2, 0)

    return pl.pallas_call(
        body, name="gdn_conv_bwd", grid=(w3 // LANES,),
        in_specs=[pl.BlockSpec((m, LANES), lambda c: (0, c)),
                  pl.BlockSpec((CONV_WIDTH, LANES), lambda c: (0, c)),
                  pl.BlockSpec((None, m, LANES), lambda c: (c // n_heads, 0, c % n_heads))],
        out_specs=[pl.BlockSpec((m, LANES), lambda c: (0, c)), pl.BlockSpec((CONV_WIDTH, LANES), lambda c: (0, c))],
        out_shape=[jax.ShapeDtypeStruct((m, w3), F32), jax.ShapeDtypeStruct((CONV_WIDTH, w3), F32)],
        scratch_shapes=[pltpu.VMEM((m + 8, LANES), F32)],
        compiler_params=_params("parallel"),
    )(proj, conv_w, dact)


def _softplus(x):
    return jnp.maximum(x, 0.0) + jnp.log(1.0 + jnp.exp(-jnp.abs(x)))


def _gdn_gates(ab, a_log, dt_bias, n_heads):
    m = ab.shape[0]
    bm = _pick(m, (640, 384, 128))

    def body(ab_ref, al_ref, dt_ref, g_ref, b_ref):
        i = pl.program_id(0)
        x = ab_ref[...]
        live = (_iota2((bm, LANES), 0) + i * bm) >= PAD_ROWS
        g = jnp.where(live, -jnp.exp(al_ref[...]) * _softplus(x + dt_ref[...]), 0.0)
        beta = jnp.where(live, _sigmoid(x), 0.0)
        for h in range(n_heads):
            g_ref[:, h * LANES:(h + 1) * LANES] = jnp.broadcast_to(g[:, h:h + 1], (bm, LANES))
            b_ref[:, h * LANES:(h + 1) * LANES] = jnp.broadcast_to(
                beta[:, n_heads + h:n_heads + h + 1], (bm, LANES))

    wide = pl.BlockSpec((bm, n_heads * LANES), lambda i: (i, 0))
    return pl.pallas_call(
        body, name="gdn_gates", grid=(m // bm,),
        in_specs=[pl.BlockSpec((bm, LANES), lambda i: (i, 0))] + [pl.BlockSpec((1, LANES), lambda i: (0, 0))] * 2,
        out_specs=[wide, wide],
        out_shape=[jax.ShapeDtypeStruct((m, n_heads * LANES), F32)] * 2,
        compiler_params=_params("parallel"),
    )(ab, a_log, dt_bias)


def _gdn_gates_bwd(ab, a_log, dt_bias, dg_b, dbeta_b, n_heads):
    m = ab.shape[0]
    bm = _pick(m, (640, 384, 128))
    nsteps = m // bm

    def body(ab_ref, al_ref, dt_ref, dg_ref, db_ref, dab_ref, dal_ref, ddt_ref, acc_al, acc_dt):
        i = pl.program_id(0)
        x = ab_ref[...]
        lane = _iota2((bm, LANES), 1)
        live = (_iota2((bm, LANES), 0) + i * bm) >= PAD_ROWS
        dg = jnp.zeros((bm, LANES), F32)
        dbeta = jnp.zeros((bm, LANES), F32)
        for h in range(n_heads):
            dg = jnp.where(lane == h, dg_ref[:, h * LANES:(h + 1) * LANES], dg)
            dbeta = jnp.where(lane == n_heads + h, db_ref[:, h * LANES:(h + 1) * LANES], dbeta)
        dg = jnp.where(live, dg, 0.0)
        dbeta = jnp.where(live, dbeta, 0.0)
        nea = -jnp.exp(al_ref[...])
        pre = x + dt_ref[...]
        d_pre = dg * nea * _sigmoid(pre)
        beta = _sigmoid(x)
        dab_ref[...] = d_pre + dbeta * beta * (1.0 - beta)
        p_al = jnp.sum((dg * nea * _softplus(pre)).reshape(bm // 8, 8, LANES), axis=0)
        p_dt = jnp.sum(d_pre.reshape(bm // 8, 8, LANES), axis=0)

        @pl.when(i == 0)
        def _():
            acc_al[...] = p_al
            acc_dt[...] = p_dt

        @pl.when(i > 0)
        def _():
            acc_al[...] += p_al
            acc_dt[...] += p_dt

        @pl.when(i == nsteps - 1)
        def _():
            dal_ref[...] = jnp.sum(acc_al[...], axis=0, keepdims=True)
            ddt_ref[...] = jnp.sum(acc_dt[...], axis=0, keepdims=True)

    wide = pl.BlockSpec((bm, n_heads * LANES), lambda i: (i, 0))
    one = pl.BlockSpec((1, LANES), lambda i: (0, 0))
    nar = pl.BlockSpec((bm, LANES), lambda i: (i, 0))
    return pl.pallas_call(
        body, name="gdn_gates_bwd", grid=(nsteps,),
        in_specs=[nar, one, one, wide, wide],
        out_specs=[nar, one, one],
        out_shape=[jax.ShapeDtypeStruct((m, LANES), F32), jax.ShapeDtypeStruct((1, LANES), F32),
                   jax.ShapeDtypeStruct((1, LANES), F32)],
        scratch_shapes=[pltpu.VMEM((8, LANES), F32)] * 2,
        compiler_params=_params("arbitrary"),
    )(ab, a_log, dt_bias, dg_b, dbeta_b)


def _chunk_masks():
    c = GDN_CHUNK
    ri, ci = _iota2((c, c), 0), _iota2((c, c), 1)
    return ri >= ci, ri > ci, ri == ci


def _chunk_decay(gbs):
    c = GDN_CHUNK
    incl, _, _ = _chunk_masks()
    inclf = incl.astype(F32)
    ones = jnp.ones((c, LANES), F32)
    g_cums = [_hdot(inclf, gb) for gb in gbs]
    g_rows = [_hdot(ones, g_cum, NT) * (1.0 / LANES) for g_cum in g_cums]
    dmats = [jnp.where(incl, jnp.exp(jnp.where(incl, g_cum[:, :c] - g_row, 0.0)), 0.0)
             for g_cum, g_row in zip(g_cums, g_rows)]
    return g_cums, dmats


def _unit_lower_inverse(mats):
    _, _, eye = _chunk_masks()
    xs = [eye.astype(F32) - a for a in mats]
    ps = list(mats)
    for _ in range(int(math.log2(GDN_CHUNK)) - 1):
        ps = [_hdot(p, p) for p in ps]
        xs = [x + _hdot(x, p) for x, p in zip(xs, ps)]
    return xs


def _gdn_prep(qkv, g_b, beta_b, n_heads):
    _, m, wd = qkv.shape
    c = GDN_CHUNK
    nc = m // c
    cb = _pick(nc, (10, 5, 3, 2))
    rows = cb * c
    qscale = GDN_HEAD_DIM ** -0.5

    def body(qkv_ref, g_ref, b_ref, ub_ref, wc_ref, qd_ref, kd_ref, gc_ref, p_ref, t_ref):
        _, strict, _ = _chunk_masks()
        sls = [pl.ds(s * c, c) for s in range(cb)]
        g_cums, dmats = _chunk_decay([g_ref[sl, :] for sl in sls])
        kks = [_bdot(qkv_ref[1, sl, :], qkv_ref[1, sl, :], NT) for sl in sls]
        ts = _unit_lower_inverse([jnp.where(strict, b_ref[sl, :][:, :c] * dmat * kk, 0.0)
                                  for sl, dmat, kk in zip(sls, dmats, kks)])
        for s, (sl, g_cum, dmat, t) in enumerate(zip(sls, g_cums, dmats, ts)):
            q = qkv_ref[0, sl, :] * qscale
            k = qkv_ref[1, sl, :]
            bb = b_ref[sl, :]
            gam = jnp.exp(g_cum)
            ub_ref[sl, :] = _hdot(t, bb * qkv_ref[2, sl, :])
            wc_ref[sl, :] = _hdot(t, bb * gam * k)
            p_ref[s] = _bdot(q, k, NT) * dmat
            qd_ref[sl, :] = q * gam
            kd_ref[sl, :] = k * jnp.exp(g_cum[c - 1:c, :] - g_cum)
            gc_ref[sl, :] = g_cum
            t_ref[s] = t

    blk = pl.BlockSpec((rows, LANES), lambda h, n: (n, h))
    sq = pl.BlockSpec((None, cb, c, c), lambda h, n: (h, n, 0, 0))
    big = jax.ShapeDtypeStruct((m, wd), F32)
    small = jax.ShapeDtypeStruct((n_heads, nc, c, c), F32)
    return pl.pallas_call(
        body, name="gdn_prep", grid=(n_heads, nc // cb),
        in_specs=[pl.BlockSpec((3, rows, LANES), lambda h, n: (0, n, h)), blk, blk],
        out_specs=[blk] * 5 + [sq, sq],
        out_shape=[big] * 5 + [small, small],
        compiler_params=_params("parallel", "parallel"),
    )(qkv, g_b, beta_b)


def _gdn_prep_bwd(qkv, g_b, beta_b, ub, wc, tinv, dub, dwc, dqd, dkd, dp, dgl, n_heads):
    _, m, wd = qkv.shape
    c = GDN_CHUNK
    nc = m // c
    cb = _pick(nc, (10, 5, 3, 2))
    rows = cb * c
    qscale = GDN_HEAD_DIM ** -0.5

    def body(qkv_ref, g_ref, b_ref, ub_ref, wc_ref, t_ref, dub_ref, dwc_ref, dqd_ref, dkd_ref, dp_ref, dgl_ref,
             dqkv_ref, dg_ref, db_ref):
        incl, strict, _ = _chunk_masks()
        ones_c = jnp.ones((c, LANES), F32)
        ones_l = jnp.ones((LANES, LANES), F32)
        last_row = _iota2((c, LANES), 0) == c - 1

        def chunk(s):
            sl = pl.ds(s * c, c)
            q = qkv_ref[0, sl, :] * qscale
            k = qkv_ref[1, sl, :]
            v = qkv_ref[2, sl, :]
            bb = b_ref[sl, :]
            bcol = bb[:, :c]
            t = t_ref[s]
            g_cum = _hdot(incl.astype(F32), g_ref[sl, :])
            kk = _bdot(k, k, NT)
            qk = _bdot(q, k, NT)
            drhs_u = _hdot(t, dub_ref[sl, :], TN)
            drhs_w = _hdot(t, dwc_ref[sl, :], TN)
            yield
            g_row = _hdot(ones_c, g_cum, NT) * (1.0 / LANES)
            da = -jnp.where(strict, _bdot(drhs_u, ub_ref[sl, :], NT) + _bdot(drhs_w, wc_ref[sl, :], NT), 0.0)
            dbeta = _hdot(drhs_u * v, ones_l)
            tk = _hdot(drhs_w * k, ones_l)
            dqd = dqd_ref[sl, :]
            dgam = _hdot(dqd * q, ones_l)
            yield
            dmat = jnp.where(incl, jnp.exp(jnp.where(incl, g_cum[:, :c] - g_row, 0.0)), 0.0)
            gam = jnp.exp(g_cum)
            edec = jnp.exp(g_cum[c - 1:c, :] - g_cum)
            dkd = dkd_ref[sl, :]
            skd = _hdot(dkd * k * edec, ones_l)
            dbeta = dbeta + _hdot(da * dmat * kk, ones_c)
            dkk = da * bcol * dmat
            dpm = jnp.where(incl, dp_ref[s], 0.0)
            dpd = dpm * dmat
            dk = _bdot(dkk, k) + _bdot(dkk, k, TN) + _bdot(dpd, q, TN)
            dq = _bdot(dpd, k)
            e = (da * bcol * kk + dpm * qk) * dmat
            yield
            dk = dk + bb * gam * drhs_w + dkd * edec
            dq = dq + dqd * gam
            dbeta = dbeta + tk * gam
            dgam = dgam + bb * tk
            dgc = dgam * gam - skd + _hdot(e, ones_c) - _hdot(e, ones_c, TN)
            tot = _hdot(jnp.ones((c, c), F32), skd)
            yield
            dgc = dgc + jnp.where(last_row, tot + dgl_ref[s, 0:1, :], 0.0)
            dg_ref[sl, :] = _hdot((_iota2((c, c), 0) <= _iota2((c, c), 1)).astype(F32), dgc)
            db_ref[sl, :] = dbeta
            dqkv_ref[0, sl, :] = dq * qscale
            dqkv_ref[1, sl, :] = dk
            dqkv_ref[2, sl, :] = bb * drhs_u

        _lockstep([chunk(s) for s in range(cb)])

    blk = pl.BlockSpec((rows, LANES), lambda h, n: (n, h))
    blk3 = pl.BlockSpec((3, rows, LANES), lambda h, n: (0, n, h))
    sq = pl.BlockSpec((None, cb, c, c), lambda h, n: (h, n, 0, 0))
    sgl = pl.BlockSpec((None, cb, 8, LANES), lambda h, n: (h, n, 0, 0))
    big = jax.ShapeDtypeStruct((m, wd), F32)
    return pl.pallas_call(
        body, name="gdn_prep_bwd", grid=(n_heads, nc // cb),
        in_specs=[blk3, blk, blk, blk, blk, sq, blk, blk, blk, blk, sq, sgl],
        out_specs=[blk3, blk, blk],
        out_shape=[jax.ShapeDtypeStruct((3, m, wd), F32), big, big],
        compiler_params=_params("parallel", "parallel"),
    )(qkv, g_b, beta_b, ub, wc, tinv, dub, dwc, dqd, dkd, dp, dgl)


def _gdn_scan(ub, wc, qd, kd, gc, p, n_heads):
    m, wd = ub.shape
    c = GDN_CHUNK
    nc = m // c
    dh = GDN_HEAD_DIM

    def body(ub_ref, wc_ref, qd_ref, kd_ref, gc_ref, p_ref, o_ref, ssave_ref, s_ref):
        n = pl.program_id(0)

        @pl.when(n == 0)
        def _():
            s_ref[...] = jnp.zeros_like(s_ref)

        def head(h):
            sl = slice(h * LANES, (h + 1) * LANES)
            s = s_ref[h]
            ssave_ref[h] = s
            ws = _bdot(wc_ref[:, sl], s)
            qs = _bdot(qd_ref[:, sl], s)
            yield
            u = ub_ref[:, sl] - ws
            pu = _bdot(p_ref[h], u)
            ku = _bdot(kd_ref[:, sl], u, TN)
            yield
            o_ref[:, sl] = qs + pu
            s_ref[h] = s * jnp.exp(gc_ref[c - 1:c, sl]) + ku

        _lockstep([head(h) for h in range(n_heads)])

    blk = pl.BlockSpec((c, wd), lambda n: (n, 0))
    return pl.pallas_call(
        body, name="gdn_scan", grid=(nc,),
        in_specs=[blk] * 5 + [pl.BlockSpec((n_heads, None, c, c), lambda n: (0, n, 0, 0))],
        out_specs=[blk, pl.BlockSpec((n_heads, None, dh, dh), lambda n: (0, n, 0, 0))],
        out_shape=[jax.ShapeDtypeStruct((m, wd), F32), jax.ShapeDtypeStruct((n_heads, nc, dh, dh), F32)],
        scratch_shapes=[pltpu.VMEM((n_heads, dh, dh), F32)],
        compiler_params=_params("arbitrary"),
    )(ub, wc, qd, kd, gc, p)


def _gdn_scan_bwd(do, ssave, ub, wc, qd, kd, gc, p, n_heads):
    m, wd = ub.shape
    c = GDN_CHUNK
    nc = m // c
    dh = GDN_HEAD_DIM

    def body(do_ref, ss_ref, ub_ref, wc_ref, qd_ref, kd_ref, gc_ref, p_ref,
             dub_ref, dwc_ref, dqd_ref, dkd_ref, dp_ref, dgl_ref, ds_ref):
        n = pl.program_id(0)

        @pl.when(n == 0)
        def _():
            ds_ref[...] = jnp.zeros_like(ds_ref)

        def head(h):
            sl = slice(h * LANES, (h + 1) * LANES)
            ds_next = ds_ref[h]
            s = ss_ref[h]
            do_ = do_ref[:, sl]
            wcv = wc_ref[:, sl]
            kdv = kd_ref[:, sl]
            ws = _bdot(wcv, s)
            du = _bdot(p_ref[h], do_, TN) + _bdot(kdv, ds_next)
            dqd_ref[:, sl] = _bdot(do_, s, NT)
            col = _hdot(jnp.ones((8, dh), F32), s * ds_next)
            qdo = _bdot(qd_ref[:, sl], do_, TN)
            yield
            u = ub_ref[:, sl] - ws
            egl = jnp.exp(gc_ref[c - 1:c, sl])
            dub_ref[:, sl] = du
            dwc_ref[:, sl] = -_bdot(du, s, NT)
            dp_ref[h] = _bdot(do_, u, NT)
            dkd_ref[:, sl] = _bdot(u, ds_next, NT)
            dgl_ref[h] = egl * _hdot(col, jnp.ones((LANES, LANES), F32))
            ds_ref[h] = qdo + egl * ds_next - _bdot(wcv, du, TN)

        _lockstep([head(h) for h in range(n_heads)])

    blk = pl.BlockSpec((c, wd), lambda n: (nc - 1 - n, 0))
    sq = pl.BlockSpec((n_heads, None, c, c), lambda n: (0, nc - 1 - n, 0, 0))
    big = jax.ShapeDtypeStruct((m, wd), F32)
    return pl.pallas_call(
        body, name="gdn_scan_bwd", grid=(nc,),
        in_specs=[blk, pl.BlockSpec((n_heads, None, dh, dh), lambda n: (0, nc - 1 - n, 0, 0))] + [blk] * 5 + [sq],
        out_specs=[blk] * 4 + [sq, pl.BlockSpec((n_heads, None, 8, LANES), lambda n: (0, nc - 1 - n, 0, 0))],
        out_shape=[big] * 4 + [jax.ShapeDtypeStruct((n_heads, nc, c, c), F32),
                               jax.ShapeDtypeStruct((n_heads, nc, 8, LANES), F32)],
        scratch_shapes=[pltpu.VMEM((n_heads, dh, dh), F32)],
        compiler_params=_params("arbitrary"),
    )(do, ssave, ub, wc, qd, kd, gc, p)


def _gdn_outnorm(o, gate, gain, n_heads):
    m, wd = o.shape
    bm = _pick(m, (640, 384, 128))

    def body(o_ref, gt_ref, gn_ref, y_ref):
        gn = gn_ref[...]
        for h in range(n_heads):
            sl = slice(h * LANES, (h + 1) * LANES)
            x = o_ref[:, sl]
            gt = gt_ref[:, sl]
            r = lax.rsqrt(jnp.mean(x * x, axis=-1, keepdims=True) + NORM_EPS)
            y_ref[:, sl] = (x * r * gn * gt * _sigmoid(gt)).astype(BF16)

    row = pl.BlockSpec((bm, wd), lambda i: (i, 0))
    return pl.pallas_call(
        body, name="gdn_outnorm", grid=(m // bm,),
        in_specs=[row, row, pl.BlockSpec((1, LANES), lambda i: (0, 0))], out_specs=row,
        out_shape=jax.ShapeDtypeStruct((m, wd), BF16),
        compiler_params=_params("parallel"),
    )(o, gate, gain)


def _gdn_outnorm_bwd(o, gate, gain, dy, n_heads):
    m, wd = o.shape
    bm = _pick(m, (640, 384, 128))
    nsteps = m // bm

    def body(o_ref, gt_ref, gn_ref, dy_ref, do_ref, dgt_ref, dgn_ref, acc_ref):
        i = pl.program_id(0)
        gn = gn_ref[...]
        part = jnp.zeros((8, LANES), F32)
        for h in range(n_heads):
            sl = slice(h * LANES, (h + 1) * LANES)
            x = o_ref[:, sl]
            gt = gt_ref[:, sl]
            d_out = dy_ref[:, sl]
            r = lax.rsqrt(jnp.mean(x * x, axis=-1, keepdims=True) + NORM_EPS)
            xhat = x * r
            sg = _sigmoid(gt)
            dz = d_out * gt * sg
            dgt_ref[:, sl] = d_out * xhat * gn * sg * (1.0 + gt * (1.0 - sg))
            dxh = dz * gn
            do_ref[:, sl] = r * (dxh - xhat * jnp.mean(dxh * xhat, axis=-1, keepdims=True))
            part = part + jnp.sum((dz * xhat).reshape(bm // 8, 8, LANES), axis=0)

        @pl.when(i == 0)
        def _():
            acc_ref[...] = part

        @pl.when(i > 0)
        def _():
            acc_ref[...] += part

        @pl.when(i == nsteps - 1)
        def _():
            dgn_ref[...] = jnp.sum(acc_ref[...], axis=0, keepdims=True)

    row = pl.BlockSpec((bm, wd), lambda i: (i, 0))
    one = pl.BlockSpec((1, LANES), lambda i: (0, 0))
    return pl.pallas_call(
        body, name="gdn_outnorm_bwd", grid=(nsteps,),
        in_specs=[row, row, one, row], out_specs=[row, row, one],
        out_shape=[jax.ShapeDtypeStruct((m, wd), F32)] * 2 + [jax.ShapeDtypeStruct((1, LANES), F32)],
        scratch_shapes=[pltpu.VMEM((8, LANES), F32)],
        compiler_params=_params("arbitrary"),
    )(o, gate, gain, dy)


def _sb_consts(kind):
    ri, ci = _iota2((LANES, LANES), 0), _iota2((LANES, LANES), 1)
    tri = {"suffix_excl": ri > ci, "prefix_incl": ri <= ci, "prefix_excl": ri < ci}[kind]
    return jnp.concatenate([tri.astype(BF16), jnp.ones((LANES, LANES), BF16)], axis=1)


def _sb_stack(a):
    first = _iota2((LANES, LANES), 1) < SB_HEAD_DIM
    return jnp.concatenate([jnp.where(first, a, 0.0), jnp.where(first, 0.0, a)], axis=0)


def _sb_unstack(a2):
    first = _iota2((LANES, LANES), 1) < SB_HEAD_DIM
    return jnp.where(first, a2[:LANES], a2[LANES:])


def _sb_visible(qi, kb, blocks=1):
    shape = (2 * LANES, blocks * LANES)
    kpos = _iota2(shape, 1) + kb * LANES
    qpos = (_iota2(shape, 0) & (LANES - 1)) + qi * LANES
    return (kpos < qpos) & (kpos >= PAD_ROWS)


def _sb_log_gates(z, vis):
    t = jnp.log2(1.0 + jnp.exp2(-jnp.abs(z)))
    lb = jnp.minimum(z, 0.0) - t
    lk = lb - z
    if vis is not None:
        lk = jnp.where(vis, lk, 0.0)
    return lb, lk


def _sb_attention(q, kv):
    m, wd = q.shape
    npair = wd // LANES
    nq = m // LANES
    scale = SB_HEAD_DIM ** -0.5

    def body(q_ref, k_ref, v_ref, o_ref, w_ref, b_ref):
        qi = pl.program_id(1)
        su = _sb_consts("suffix_excl")
        q2 = _sb_stack(q_ref[...] * (scale * math.log2(math.e))).astype(BF16)

        def tiles(kbs, carry, masked):
            run, acc = carry
            r0s = [pl.multiple_of(kb * LANES, LANES) for kb in kbs]
            zs = [lax.dot_general(q2, k_ref[pl.ds(r0, LANES), :], NT, preferred_element_type=F32) for r0 in r0s]
            parts = []
            for kb, r0, z in zip(kbs, r0s, zs):
                vis = _sb_visible(qi, kb) if masked else None
                lb, lk = _sb_log_gates(z, vis)
                parts.append((kb, r0, vis, lb, lk))
            parts = [(kb, r0, vis, lb, _split_dot(lk, su)) for kb, r0, vis, lb, lk in parts]
            for kb, r0, vis, lb, cs in parts:
                w = jnp.exp2(lb + cs[:, :LANES] + run)
                if vis is not None:
                    w = jnp.where(vis, w, 0.0)
                run = run + cs[:, LANES:]
                wb = w.astype(BF16)
                w_ref[kb] = wb
                b_ref[kb] = jnp.exp2(lb).astype(BF16)
                acc = acc + lax.dot_general(wb, v_ref[pl.ds(r0, LANES), :], NN, preferred_element_type=F32)
            return run, acc

        su2 = (_iota2((2 * LANES, 2 * LANES), 0) > _iota2((2 * LANES, 2 * LANES), 1)).astype(BF16)

        def pairs(his, carry, masked):
            run, acc = carry
            r0s = [pl.multiple_of((hi - 1) * LANES, LANES) for hi in his]
            zs = [lax.dot_general(q2, k_ref[pl.ds(r0, 2 * LANES), :], NT, preferred_element_type=F32) for r0 in r0s]
            viss = [_sb_visible(qi, hi - 1, 2) if masked else None for hi in his]
            gates = [_sb_log_gates(z, vis) for z, vis in zip(zs, viss)]
            css = [_split_dot(lk, su2) for _, lk in gates]
            for hi, r0, vis, (lb, lk), cs in zip(his, r0s, viss, gates, css):
                w = jnp.exp2(lb + cs + jnp.concatenate([run, run], axis=1))
                if vis is not None:
                    w = jnp.where(vis, w, 0.0)
                run = run + jnp.broadcast_to(cs[:, 0:1] + lk[:, 0:1], (2 * LANES, LANES))
                wb = w.astype(BF16)
                bb = jnp.exp2(lb).astype(BF16)
                w_ref[hi - 1], w_ref[hi] = wb[:, :LANES], wb[:, LANES:]
                b_ref[hi - 1], b_ref[hi] = bb[:, :LANES], bb[:, LANES:]
                acc = acc + lax.dot_general(wb, v_ref[pl.ds(r0, 2 * LANES), :], NN, preferred_element_type=F32)
            return run, acc

        def maybe(flag, fn, carry):
            return lax.fori_loop(0, flag, lambda j, c: fn(c), carry)

        zero = jnp.zeros((2 * LANES, LANES), F32)
        big = (qi >= 3).astype(jnp.int32)
        carry = maybe(big, lambda c: pairs([qi], c, True), (zero, zero))
        carry = maybe(1 - big, lambda c: tiles([qi], c, True), carry)
        top, bottom = qi - 1 - big, 1 + big
        n_mid = jnp.maximum(top - bottom + 1, 0)
        n8 = n_mid // 8
        carry = lax.fori_loop(0, n8, lambda j, c: pairs([top - 8 * j - 2 * u for u in range(4)], c, False), carry)
        top = top - 8 * n8
        take4, take2, take1 = (n_mid >> 2) & 1, (n_mid >> 1) & 1, n_mid & 1
        carry = maybe(take4, lambda c: pairs([top, top - 2], c, False), carry)
        top = top - 4 * take4
        carry = maybe(take2, lambda c: pairs([top], c, False), carry)
        carry = maybe(take1, lambda c: tiles([bottom], c, False), carry)
        carry = maybe(big, lambda c: pairs([1], c, True), carry)
        carry = maybe((1 - big) * (qi >= 1).astype(jnp.int32), lambda c: tiles([0], c, True), carry)
        o_ref[...] = _sb_unstack(carry[1])

    qspec = pl.BlockSpec((LANES, LANES), lambda hp, qi: (qi, hp))
    saved = pl.BlockSpec((None, None, nq, 2 * LANES, LANES), lambda hp, qi: (hp, qi, 0, 0, 0))
    saved_shape = jax.ShapeDtypeStruct((npair, nq, nq, 2 * LANES, LANES), BF16)
    return pl.pallas_call(
        body, name="sb_attention", grid=(npair, nq),
        in_specs=[qspec, pl.BlockSpec((m, LANES), lambda hp, qi: (0, hp)),
                  pl.BlockSpec((m, LANES), lambda hp, qi: (0, npair + hp))],
        out_specs=[qspec, saved, saved],
        out_shape=[jax.ShapeDtypeStruct((m, wd), F32), saved_shape, saved_shape],
        compiler_params=_params("parallel", "arbitrary"),
    )(q, kv, kv)


def _sb_attention_bwd(q, kv, w_all, b_all, do):
    m, wd = q.shape
    npair = wd // LANES
    nq = m // LANES
    scale = SB_HEAD_DIM ** -0.5

    def body(q_ref, k_ref, v_ref, w_ref, b_ref, do_ref, dq_ref, dk_ref, dv_ref):
        qi = pl.program_id(1)

        @pl.when(qi == 0)
        def _():
            dk_ref[...] = jnp.zeros_like(dk_ref)
            dv_ref[...] = jnp.zeros_like(dv_ref)

        px = _sb_consts("prefix_excl")
        q2 = _sb_stack(q_ref[...] * scale).astype(BF16)
        do2 = _sb_stack(do_ref[...]).astype(BF16)

        def tiles(kbs, carry, masked):
            run_p, dq_acc = carry
            r0s = [pl.multiple_of(kb * LANES, LANES) for kb in kbs]
            das = [lax.dot_general(do2, v_ref[pl.ds(r0, LANES), :], NT, preferred_element_type=F32) for r0 in r0s]
            wbs = [w_ref[kb] for kb in kbs]
            for wb, r0 in zip(wbs, r0s):
                dv_ref[pl.ds(r0, LANES), :] += lax.dot_general(wb, do2, TN, preferred_element_type=F32)
            pws = [wb.astype(F32) * da for wb, da in zip(wbs, das)]
            pss = [_split_dot(pw, px, parts=SB_BWD_SUM_PARTS) for pw in pws]
            dzs = []
            for kb, pw, ps in zip(kbs, pws, pss):
                beta = b_ref[kb].astype(F32)
                dz = pw * (1.0 - beta) - beta * (run_p + ps[:, :LANES])
                if masked:
                    dz = jnp.where(_sb_visible(qi, kb), dz, 0.0)
                run_p = run_p + ps[:, LANES:]
                dzs.append(dz.astype(BF16))
            for r0, dz in zip(r0s, dzs):
                dq_acc = dq_acc + lax.dot_general(dz, k_ref[pl.ds(r0, LANES), :], NN, preferred_element_type=F32)
                dk_ref[pl.ds(r0, LANES), :] += lax.dot_general(dz, q2, TN, preferred_element_type=F32)
            return run_p, dq_acc

        px2 = (_iota2((2 * LANES, 2 * LANES), 0) < _iota2((2 * LANES, 2 * LANES), 1)).astype(BF16)

        def pairs(los, carry, masked):
            run_p, dq_acc = carry
            r0s = [pl.multiple_of(lo * LANES, LANES) for lo in los]
            das = [lax.dot_general(do2, v_ref[pl.ds(r0, 2 * LANES), :], NT, preferred_element_type=F32) for r0 in r0s]
            wbs = [jnp.concatenate([w_ref[lo], w_ref[lo + 1]], axis=1) for lo in los]
            for wb, r0 in zip(wbs, r0s):
                dv_ref[pl.ds(r0, 2 * LANES), :] += lax.dot_general(wb, do2, TN, preferred_element_type=F32)
            pws = [wb.astype(F32) * da for wb, da in zip(wbs, das)]
            pss = [_split_dot(pw, px2, parts=SB_BWD_SUM_PARTS) for pw in pws]
            dzs = []
            for lo, pw, ps in zip(los, pws, pss):
                beta = jnp.concatenate([b_ref[lo], b_ref[lo + 1]], axis=1).astype(F32)
                dz = pw * (1.0 - beta) - beta * (jnp.concatenate([run_p, run_p], axis=1) + ps)
                if masked:
                    dz = jnp.where(_sb_visible(qi, lo, 2), dz, 0.0)
                last = 2 * LANES - 1
                run_p = run_p + jnp.broadcast_to(ps[:, last:] + pw[:, last:], (2 * LANES, LANES))
                dzs.append(dz.astype(BF16))
            for r0, dz in zip(r0s, dzs):
                dq_acc = dq_acc + lax.dot_general(dz, k_ref[pl.ds(r0, 2 * LANES), :], NN, preferred_element_type=F32)
                dk_ref[pl.ds(r0, 2 * LANES), :] += lax.dot_general(dz, q2, TN, preferred_element_type=F32)
            return run_p, dq_acc

        def maybe(flag, fn, carry):
            return lax.fori_loop(0, flag, lambda j, c: fn(c), carry)

        zero = jnp.zeros((2 * LANES, LANES), F32)
        big = (qi >= 3).astype(jnp.int32)
        carry = maybe(big, lambda c: pairs([0], c, True), (zero, zero))
        carry = maybe(1 - big, lambda c: tiles([0], c, True), carry)
        low, high = 1 + big, qi - 1 - big
        n_mid = jnp.maximum(high - low + 1, 0)
        n8 = n_mid // 8
        carry = lax.fori_loop(0, n8, lambda j, c: pairs([low + 8 * j + 2 * u for u in range(4)], c, False), carry)
        low = low + 8 * n8
        take4, take2, take1 = (n_mid >> 2) & 1, (n_mid >> 1) & 1, n_mid & 1
        carry = maybe(take4, lambda c: pairs([low, low + 2], c, False), carry)
        low = low + 4 * take4
        carry = maybe(take2, lambda c: pairs([low], c, False), carry)
        carry = maybe(take1, lambda c: tiles([high], c, False), carry)
        carry = maybe(big, lambda c: pairs([qi - 1], c, True), carry)
        carry = maybe((1 - big) * (qi >= 1).astype(jnp.int32), lambda c: tiles([qi], c, True), carry)
        dq_ref[...] = _sb_unstack(carry[1]) * scale

    qspec = pl.BlockSpec((LANES, LANES), lambda hp, qi: (qi, hp))
    strip = pl.BlockSpec((m, LANES), lambda hp, qi: (0, hp))
    saved = pl.BlockSpec((None, None, nq, 2 * LANES, LANES), lambda hp, qi: (hp, qi, 0, 0, 0))
    big = jax.ShapeDtypeStruct((m, wd), F32)
    return pl.pallas_call(
        body, name="sb_attention_bwd", grid=(npair, nq),
        in_specs=[qspec, strip, pl.BlockSpec((m, LANES), lambda hp, qi: (0, npair + hp)), saved, saved, qspec],
        out_specs=[qspec, strip, strip],
        out_shape=[big, big, big],
        compiler_params=_params("parallel", "arbitrary"),
    )(q, kv, kv, w_all, b_all, do)


def _mesh_pos():
    return lax.axis_index("x"), lax.axis_index("y"), lax.axis_index("c")


def _exchange(srcs, name, scatter):
    n = len(srcs)
    rows = [s.shape[-2] for s in srcs]
    offs = [sum(rows[:i]) for i in range(n)]
    total = sum(rows)

    def body(*refs):
        src_refs, out_ref = refs[:n], refs[n]
        send_sems, recv_sems, local_sems = refs[n + 1:]
        x, y, c = _mesh_pos()
        me = 4 * x + 2 * y + c

        def piece(i, d):
            return src_refs[i].at[d] if scatter else src_refs[i]

        def window(slot, i):
            return out_ref.at[slot, pl.ds(offs[i], rows[i])]

        local = [pltpu.make_async_copy(piece(i, me), window(me, i), local_sems.at[i]) for i in range(n)]
        for cp in local:
            cp.start()
        peers = []
        for k in range(1, N_DEV):
            px = 1 - x if k & 4 else x
            py = 1 - y if k & 2 else y
            pc = 1 - c if k & 1 else c
            pid = 4 * px + 2 * py + pc
            for i in range(n):
                pltpu.make_async_remote_copy(
                    src_ref=piece(i, pid), dst_ref=window(me, i),
                    send_sem=send_sems.at[k - 1], recv_sem=recv_sems.at[k - 1],
                    device_id=(px, py, pc), device_id_type=pl.DeviceIdType.MESH).start()
            peers.append(pid)
        slabs = [pltpu.make_async_remote_copy(
            src_ref=out_ref.at[me], dst_ref=out_ref.at[peers[k - 1]],
            send_sem=send_sems.at[k - 1], recv_sem=recv_sems.at[k - 1],
            device_id=(x, y, c), device_id_type=pl.DeviceIdType.MESH) for k in range(1, N_DEV)]
        for cp in slabs:
            cp.wait_recv()
        for cp in slabs:
            cp.wait_send()
        for cp in local:
            cp.wait()

    return pl.pallas_call(
        body, name=name,
        in_specs=[pl.BlockSpec(memory_space=pl.ANY)] * n, out_specs=pl.BlockSpec(memory_space=pl.ANY),
        out_shape=jax.ShapeDtypeStruct((N_DEV, total, FLAT_COLS), srcs[0].dtype),
        scratch_shapes=[pltpu.SemaphoreType.DMA((N_DEV - 1,)), pltpu.SemaphoreType.DMA((N_DEV - 1,)),
                        pltpu.SemaphoreType.DMA((n,))],
        compiler_params=pltpu.CompilerParams(has_side_effects=True),
    )(*srcs)


def _adamw(parts, w, mom, var, name):
    rows = w.shape[0]
    br = FLAT_ROW_BLOCK
    c1 = 1.0 - ADAM_B1 ** ADAM_STEP
    c2 = 1.0 - ADAM_B2 ** ADAM_STEP

    def body(p_ref, w_ref, m_ref, v_ref, g_ref, d_ref, nm_ref, nv_ref):
        g = p_ref[0].astype(F32)
        for s in range(1, N_DEV):
            g = g + p_ref[s].astype(F32)
        m_new = ADAM_B1 * m_ref[...] + (1.0 - ADAM_B1) * g
        v_new = ADAM_B2 * v_ref[...] + (1.0 - ADAM_B2) * (g * g)
        m_hat = m_new / c1
        v_hat = v_new / c2
        g_ref[...] = g
        d_ref[...] = -ADAM_LR * (m_hat / (jnp.sqrt(v_hat) + ADAM_EPS) + ADAM_WD * w_ref[...])
        nm_ref[...] = m_new
        nv_ref[...] = v_new

    row = pl.BlockSpec((br, FLAT_COLS), lambda i: (i, 0))
    flat = jax.ShapeDtypeStruct((rows, FLAT_COLS), F32)
    return pl.pallas_call(
        body, name=name, grid=(rows // br,),
        in_specs=[pl.BlockSpec((N_DEV, br, FLAT_COLS), lambda i: (0, i, 0)), row, row, row],
        out_specs=[row] * 4, out_shape=[flat] * 4,
        compiler_params=_params("parallel"),
    )(parts, w, mom, var)


SMALL_SHARDED = (("meta_tokens", 1), ("gdn_norm_g", 1), ("gdn_conv_w", 2))
BIG_PIECES = (("gdn_w_in", 0, True), ("gdn_w_out", 0, False), ("w_kv", None, True), ("sb_w_q", 0, False),
              ("sb_w_o", 0, False), ("ffn_w_gate_up", 0, True), ("ffn_w_gate_up", 1, True),
              ("ffn_w_down", 0, False), ("ffn_w_down", 1, False))
REPLICATED = ("gdn_a_log", "gdn_dt_bias", "gdn_onorm_g", "kv_norm_g", "sb_norm_g", "ffn_norm_g", "final_norm_g")
WEIGHTS = ("meta_tokens", "gdn_norm_g", "gdn_w_in", "gdn_conv_w", "gdn_a_log", "gdn_dt_bias", "gdn_onorm_g",
           "gdn_w_out", "kv_norm_g", "w_kv", "sb_norm_g", "sb_w_q", "sb_w_o", "ffn_norm_g", "ffn_w_gate_up",
           "ffn_w_down", "final_norm_g")
F32_ROWS = 8
BF16_ROWS = 16


def _n_rows(shape, mult):
    return -(-math.prod(shape) // (mult * FLAT_COLS)) * mult


def _as_rows(a, nl, mult):
    lead = a.shape[:nl]
    size = math.prod(a.shape[nl:])
    nr = _n_rows(a.shape[nl:], mult)
    flat = a.reshape(lead + (size,))
    if nr * FLAT_COLS != size:
        flat = jnp.pad(flat, [(0, 0)] * nl + [(0, nr * FLAT_COLS - size)])
    return flat.reshape(lead + (nr, FLAT_COLS))


def _from_rows(flat, r0, shape, nl, mult):
    lead = flat.shape[:nl]
    nr = _n_rows(shape, mult)
    seg = lax.slice_in_dim(flat, r0, r0 + nr, axis=nl).reshape(lead + (nr * FLAT_COLS,))
    return lax.slice_in_dim(seg, 0, math.prod(shape), axis=nl).reshape(lead + tuple(shape)), r0 + nr


def _to_piece(shard, layer, transposed):
    mat = shard if layer is None else shard[layer]
    return mat.T if transposed else mat


def _from_pieces(pieces, name):
    mats = [p.T if tr else p for (n, layer, tr), p in zip(BIG_PIECES, pieces) if n == name]
    layers = [layer for n, layer, _ in BIG_PIECES if n == name]
    return mats[0] if layers[0] is None else jnp.stack(mats)


def _merge_shards(g, axis):
    t = jnp.moveaxis(g, 0, axis)
    shp = t.shape
    return t.reshape(shp[:axis] + (shp[axis] * shp[axis + 1],) + shp[axis + 2:])


def _split_shards(full, axis):
    shp = full.shape
    t = full.reshape(shp[:axis] + (N_DEV, shp[axis] // N_DEV) + shp[axis + 1:])
    return jnp.moveaxis(t, axis, 0)


def _pad_lanes(a, width=LANES):
    return jnp.pad(a, ((0, 0), (0, width - a.shape[1])))


def _local_step(x, target, w, mats):
    d = x.shape[1]
    gh = w["gdn_a_log"].shape[1]
    gw = gh * GDN_HEAD_DIM
    w_int = mats["gdn_w_in", 0]
    w_qkvt, w_gatet = w_int[:3 * gw], w_int[3 * gw:4 * gw]
    w_abt = jnp.pad(w_int[4 * gw:], ((0, LANES - 2 * gh), (0, 0)))
    conv_w = w["gdn_conv_w"][0]
    a_log, dt_bias = _pad_lanes(w["gdn_a_log"]), _pad_lanes(w["gdn_dt_bias"])
    w_out = mats["gdn_w_out", 0]
    w_kvt = mats["w_kv", None]
    sbw = w_kvt.shape[0] // 2
    w_kt, w_vt = w_kvt[:sbw], w_kvt[sbw:]
    w_q, w_o = mats["sb_w_q", 0], mats["sb_w_o", 0]
    w_gut = [mats["ffn_w_gate_up", l] for l in range(2)]
    w_dn = [mats["ffn_w_down", l] for l in range(2)]
    ffn_g = [w["ffn_norm_g"][l:l + 1] for l in range(2)]
    kv_g, fin_g = w["kv_norm_g"][None], w["final_norm_g"][None]

    h0 = jnp.concatenate([jnp.zeros((PAD_ROWS, d), F32), w["meta_tokens"], x], axis=0)

    (n1,) = _rmsnorm(h0, [w["gdn_norm_g"]], "norm_gdn")
    proj = _matmul(n1, w_qkvt, "nt", "mm_gdn_qkv")
    gate = _matmul(n1, w_gatet, "nt", "mm_gdn_gate")
    ab = _matmul(n1, w_abt, "nt", "mm_gdn_ab")
    qkv = _gdn_conv(proj, conv_w, gh)
    g_b, beta_b = _gdn_gates(ab, a_log, dt_bias, gh)
    ub, wc, qd, kd, gc, pmat, tinv = _gdn_prep(qkv, g_b, beta_b, gh)
    o_gdn, ssave = _gdn_scan(ub, wc, qd, kd, gc, pmat, gh)
    o2 = _gdn_outnorm(o_gdn, gate, w["gdn_onorm_g"], gh)
    h1 = _matmul(o2, w_out, "nn", "mm_gdn_out", add=h0)

    def ffn_fwd(h, l):
        (n,) = _rmsnorm(h, [ffn_g[l]], f"norm_ffn{l}")
        gu = _matmul(n, w_gut[l], "nt", f"mm_ffn{l}_gu", out_dtype=BF16)
        act = _swiglu(gu, f"swiglu{l}")
        return n, gu, act, _matmul(act, w_dn[l], "nn", f"mm_ffn{l}_down", add=h)

    n2, gu0, act0, h2 = ffn_fwd(h1, 0)

    nkv, n3 = _rmsnorm(h2, [kv_g, w["sb_norm_g"]], "norm_kv_sb")
    kv = _matmul(nkv, w_kvt, "nt", "mm_kv", out_dtype=BF16)
    q_sb = _matmul(n3, w_q, "nn", "mm_sb_q")
    o_sb, w_sb, beta_sb = _sb_attention(q_sb, kv)
    h3 = _matmul(o_sb, w_o, "nn", "mm_sb_o", add=h2)
    n4, gu1, act1, h4 = ffn_fwd(h3, 1)

    loss, dh4, d_fin = _loss_head(h4, fin_g, target)

    def ffn_bwd(dh, h, n, gu, act, l):
        d_act = _matmul(dh, w_dn[l], "nt", f"mm_ffn{l}_dact")
        dw_dn = _matmul(act, dh, "tn", f"mm_ffn{l}_dwdown")
        dgu = _swiglu_bwd(gu, d_act, f"swiglu_bwd{l}")
        dw_gu = _matmul(dgu, n, "tn", f"mm_ffn{l}_dwgu")
        dn = _matmul(dgu, w_gut[l], "nn", f"mm_ffn{l}_dn")
        dh_in, (dg,) = _rmsnorm_bwd(h, [ffn_g[l]], [dn], dh, f"norm_ffn{l}_bwd")
        return dh_in, dw_gu, dw_dn, dg

    dh3, dw_gu1, dw_dn1, dg_ffn1 = ffn_bwd(dh4, h3, n4, gu1, act1, 1)

    do_sb = _matmul(dh3, w_o, "nt", "mm_sb_do")
    dw_o = _matmul(o_sb, dh3, "tn", "mm_sb_dwo")
    dq_sb, dk_sb, dv_sb = _sb_attention_bwd(q_sb, kv, w_sb, beta_sb, do_sb)
    dw_q = _matmul(n3, dq_sb, "tn", "mm_sb_dwq")
    dn3 = _matmul(dq_sb, w_q, "nt", "mm_sb_dn")
    dw_k = _matmul(dk_sb, nkv, "tn", "mm_dwk")
    dw_v = _matmul(dv_sb, nkv, "tn", "mm_dwv")
    dnkv = _matmul(dk_sb, w_kt, "nn", "mm_dnk")
    dnkv = _matmul(dv_sb, w_vt, "nn", "mm_dnv", add=dnkv)
    dh2, (dg_kv, dg_sb) = _rmsnorm_bwd(h2, [kv_g, w["sb_norm_g"]], [dnkv, dn3], dh3, "norm_kv_sb_bwd")

    dh1, dw_gu0, dw_dn0, dg_ffn0 = ffn_bwd(dh2, h1, n2, gu0, act0, 0)

    do2 = _matmul(dh1, w_out, "nt", "mm_gdn_do")
    dw_out = _matmul(o2, dh1, "tn", "mm_gdn_dwout")
    do_gdn, dgate, d_onorm = _gdn_outnorm_bwd(o_gdn, gate, w["gdn_onorm_g"], do2, gh)
    dub, dwc, dqd, dkd, dpm, dgl = _gdn_scan_bwd(do_gdn, ssave, ub, wc, qd, kd, gc, pmat, gh)
    dqkv, dg_b, dbeta_b = _gdn_prep_bwd(qkv, g_b, beta_b, ub, wc, tinv, dub, dwc, dqd, dkd, dpm, dgl, gh)
    dab, d_alog, d_dtb = _gdn_gates_bwd(ab, a_log, dt_bias, dg_b, dbeta_b, gh)
    dproj, d_conv = _gdn_conv_bwd(proj, conv_w, dqkv, gh)
    dw_qkv = _matmul(dproj, n1, "tn", "mm_gdn_dwqkv")
    dw_gate = _matmul(dgate, n1, "tn", "mm_gdn_dwgate")
    dw_ab = _matmul(dab, n1, "tn", "mm_gdn_dwab")
    dn1 = _matmul(dproj, w_qkvt, "nn", "mm_gdn_dn_qkv")
    dn1 = _matmul(dgate, w_gatet, "nn", "mm_gdn_dn_gate", add=dn1)
    dn1 = _matmul(dab, w_abt, "nn", "mm_gdn_dn_ab", add=dn1)
    dh0, (dg_gdn,) = _rmsnorm_bwd(h0, [w["gdn_norm_g"]], [dn1], dh1, "norm_gdn_bwd")

    grads = {
        "meta_tokens": dh0[PAD_ROWS:LANES],
        "gdn_norm_g": dg_gdn,
        "gdn_conv_w": d_conv[None],
        "gdn_a_log": d_alog[:, :gh],
        "gdn_dt_bias": d_dtb[:, :gh],
        "gdn_onorm_g": d_onorm,
        "kv_norm_g": dg_kv[0],
        "sb_norm_g": dg_sb,
        "ffn_norm_g": jnp.concatenate([dg_ffn0, dg_ffn1], axis=0),
        "final_norm_g": d_fin[0],
    }
    gmats = {
        ("gdn_w_in", 0): jnp.concatenate([dw_qkv, dw_gate, dw_ab[:2 * gh]], axis=0),
        ("gdn_w_out", 0): dw_out,
        ("w_kv", None): jnp.concatenate([dw_k, dw_v], axis=0),
        ("sb_w_q", 0): dw_q,
        ("sb_w_o", 0): dw_o,
        ("ffn_w_gate_up", 0): dw_gu0,
        ("ffn_w_gate_up", 1): dw_gu1,
        ("ffn_w_down", 0): dw_dn0,
        ("ffn_w_down", 1): dw_dn1,
    }
    return loss, dh0[LANES:], grads, gmats


def kernel(x, meta_tokens, gdn_norm_g, gdn_w_in, gdn_conv_w, gdn_a_log, gdn_dt_bias, gdn_onorm_g, gdn_w_out, kv_norm_g, w_kv, sb_norm_g, sb_w_q, sb_w_o, ffn_norm_g, ffn_w_gate_up, ffn_w_down, final_norm_g, loss_target, m_meta_tokens, m_gdn_norm_g, m_gdn_w_in, m_gdn_conv_w, m_gdn_a_log, m_gdn_dt_bias, m_gdn_onorm_g, m_gdn_w_out, m_kv_norm_g, m_w_kv, m_sb_norm_g, m_sb_w_q, m_sb_w_o, m_ffn_norm_g, m_ffn_w_gate_up, m_ffn_w_down, m_final_norm_g, v_meta_tokens, v_gdn_norm_g, v_gdn_w_in, v_gdn_conv_w, v_gdn_a_log, v_gdn_dt_bias, v_gdn_onorm_g, v_gdn_w_out, v_kv_norm_g, v_w_kv, v_sb_norm_g, v_sb_w_q, v_sb_w_o, v_ffn_norm_g, v_ffn_w_gate_up, v_ffn_w_down, v_final_norm_g):
    given = dict(locals())
    small = [n for n, _ in SMALL_SHARDED]
    big_names = list(dict.fromkeys(n for n, _, _ in BIG_PIECES))

    def local_pieces(prefix):
        return [_to_piece(given[prefix + n], layer, tr) for n, layer, tr in BIG_PIECES]

    piece_shapes = [p.shape for p in local_pieces("")]

    g_small = _exchange([jnp.concatenate([_as_rows(given[n], 0, F32_ROWS) for n in small], axis=0)],
                        "all_gather_small", scatter=False)
    g_big = _exchange([_as_rows(p.astype(BF16), 0, BF16_ROWS) for p in local_pieces("")],
                      "all_gather_weights", scatter=False)
    full = {n: given[n] for n in REPLICATED}
    r = 0
    for n, ax in SMALL_SHARDED:
        shards, r = _from_rows(g_small, r, given[n].shape, 1, F32_ROWS)
        full[n] = _merge_shards(shards, ax)
    mats, r = {}, 0
    for (n, layer, _), shp in zip(BIG_PIECES, piece_shapes):
        rows8, r = _from_rows(g_big, r, shp, 1, BF16_ROWS)
        mats[n, layer] = rows8.reshape((N_DEV * shp[0],) + shp[1:])

    loss_b, grad_x, grads, gmats = _local_step(x[0], loss_target[0], full, mats)

    order = small + list(REPLICATED)
    fill_small = (-sum(_n_rows(given[n].shape, F32_ROWS) for n in order)) % FLAT_ROW_BLOCK
    fill_big = (-sum(_n_rows(shp, BF16_ROWS) for shp in piece_shapes)) % FLAT_ROW_BLOCK
    small_send = [_as_rows(_split_shards(grads[n], ax), 1, F32_ROWS) for n, ax in SMALL_SHARDED]
    small_send += [_as_rows(jnp.broadcast_to(grads[n][None], (N_DEV,) + grads[n].shape), 1, F32_ROWS)
                   for n in REPLICATED]
    if fill_small:
        small_send.append(jnp.zeros((N_DEV, fill_small, FLAT_COLS), F32))
    recv_small = _exchange([jnp.concatenate(small_send, axis=1)], "scatter_small_gradients", scatter=True)
    big_send = [_as_rows(gmats[n, layer].reshape((N_DEV,) + shp), 1, BF16_ROWS)
                for (n, layer, _), shp in zip(BIG_PIECES, piece_shapes)]
    if fill_big:
        big_send.append(jnp.zeros((N_DEV, fill_big, FLAT_COLS), BF16))
    recv_big = _exchange(big_send, "scatter_gradients", scatter=True)

    def flat_small(prefix):
        segs = [_as_rows(given[prefix + n], 0, F32_ROWS) for n in order]
        return jnp.concatenate(segs + ([jnp.zeros((fill_small, FLAT_COLS), F32)] if fill_small else []), axis=0)

    def flat_big(prefix):
        segs = [_as_rows(p, 0, BF16_ROWS) for p in local_pieces(prefix)]
        return jnp.concatenate(segs + ([jnp.zeros((fill_big, FLAT_COLS), F32)] if fill_big else []), axis=0)

    upd_small = _adamw(recv_small, flat_small(""), flat_small("m_"), flat_small("v_"), "adamw_small")
    upd_big = _adamw(recv_big, flat_big(""), flat_big("m_"), flat_big("v_"), "adamw")
    outs = {}
    for tag, f_small, f_big in zip(("grad", "delta", "new_m", "new_v"), upd_small, upd_big):
        outs[tag], r = {}, 0
        for n in order:
            outs[tag][n], r = _from_rows(f_small, r, given[n].shape, 0, F32_ROWS)
        pieces, r = [], 0
        for shp in piece_shapes:
            p, r = _from_rows(f_big, r, shp, 0, BF16_ROWS)
            pieces.append(p)
        for n in big_names:
            outs[tag][n] = _from_pieces(pieces, n)

    loss = lax.psum(loss_b[0, 0], ("x", "y", "c"))
    result = [loss, grad_x[None]]
    for tag in ("grad", "delta", "new_m", "new_v"):
        result += [outs[tag][n] for n in WEIGHTS]
    return tuple(result)
```

```python
import functools
import math

import jax
import jax.numpy as jnp
from jax import lax
from jax.experimental import pallas as pl
from jax.experimental.pallas import tpu as pltpu

F32 = jnp.float32
BF16 = jnp.bfloat16

N_DEV = 8
LANES = 128
N_META = 16
PAD_ROWS = LANES - N_META
GDN_CHUNK = 64
GDN_HEAD_DIM = 128
SB_HEAD_DIM = 64
SB_FAR = 11
SB_BWD_SUM_PARTS = 1
CONV_WIDTH = 4
NORM_EPS = 1e-6
FLAT_COLS = 1024
FLAT_ROW_BLOCK = 64
VMEM_LIMIT = 56 * 1024 * 1024

ADAM_LR = 0.001
ADAM_B1 = 0.9
ADAM_B2 = 0.999
ADAM_EPS = 1e-08
ADAM_WD = 0.01
ADAM_STEP = 10

NN = (((1,), (0,)), ((), ()))
NT = (((1,), (1,)), ((), ()))
TN = (((0,), (0,)), ((), ()))


def _params(*sem):
    return pltpu.CompilerParams(dimension_semantics=sem, vmem_limit_bytes=VMEM_LIMIT)


def _pick(n, cands):
    for c in cands:
        if n % c == 0:
            return c
    return n


def _bdot(a, b, dims=NN):
    return lax.dot_general(a.astype(BF16), b.astype(BF16), dims, preferred_element_type=F32)


def _hdot(a, b, dims=NN):
    return lax.dot_general(a, b, dims, preferred_element_type=F32, precision=lax.Precision.HIGH)


def _split_dot(a, m, parts=2):
    out = None
    for _ in range(parts):
        piece = a.astype(BF16)
        a = a - piece.astype(F32)
        term = lax.dot_general(piece, m, NN, preferred_element_type=F32)
        out = term if out is None else out + term
    return out


def _lockstep(gens):
    while gens:
        alive = []
        for g in gens:
            try:
                next(g)
                alive.append(g)
            except StopIteration:
                pass
        gens = alive


def _sigmoid(x):
    return 1.0 / (1.0 + jnp.exp(-x))


def _iota2(shape, axis):
    return lax.broadcasted_iota(jnp.int32, shape, axis)


def _matmul(a, b, mode, name, out_dtype=None, add=None):
    if out_dtype is None:
        out_dtype = BF16 if mode == "tn" else F32
    if mode == "nn":
        (m, k), n = a.shape, b.shape[1]
    elif mode == "nt":
        (m, k), n = a.shape, b.shape[0]
    else:
        (k, m), n = a.shape, b.shape[1]
    row_c = (640, 512, 384, 256, 128)
    col_c = (1024, 1408, 768, 512, 384, 256, 128)
    if mode == "tn":
        bm, bn, bk = _pick(m, col_c), _pick(n, col_c), _pick(k, row_c)
    else:
        bm, bn, bk = _pick(m, row_c), _pick(n, col_c), _pick(k, (1024, 1408, 768, 512, 256, 128))
    nk = k // bk
    ni, nj = m // bm, n // bn
    dims = {"nn": NN, "nt": NT, "tn": TN}[mode]
    a_bytes, b_bytes = a.size * a.dtype.itemsize, b.size * b.dtype.itemsize
    rows_outer = (a_bytes * (1 if nk == 1 else nj) + b_bytes * ni
                  <= b_bytes * (1 if nk == 1 else ni) + a_bytes * nj)

    def at(f):
        return (lambda g0, g1, q: f(g0, g1, q)) if rows_outer else (lambda g0, g1, q: f(g1, g0, q))

    a_spec = {"nn": pl.BlockSpec((bm, bk), at(lambda i, j, q: (i, q))),
              "nt": pl.BlockSpec((bm, bk), at(lambda i, j, q: (i, q))),
              "tn": pl.BlockSpec((bk, bm), at(lambda i, j, q: (q, i)))}[mode]
    b_spec = {"nn": pl.BlockSpec((bk, bn), at(lambda i, j, q: (q, j))),
              "nt": pl.BlockSpec((bn, bk), at(lambda i, j, q: (j, q))),
              "tn": pl.BlockSpec((bk, bn), at(lambda i, j, q: (q, j)))}[mode]
    o_spec = pl.BlockSpec((bm, bn), at(lambda i, j, q: (i, j)))
    has_add = add is not None

    def body(*refs):
        if has_add:
            a_ref, b_ref, add_ref, o_ref, acc_ref = refs
        else:
            a_ref, b_ref, o_ref, acc_ref = refs
        q = pl.program_id(2)
        part = _bdot(a_ref[...], b_ref[...], dims)

        @pl.when(q == 0)
        def _():
            acc_ref[...] = part

        @pl.when(q > 0)
        def _():
            acc_ref[...] += part

        @pl.when(q == nk - 1)
        def _():
            r = acc_ref[...]
            if has_add:
                r = r + add_ref[...]
            o_ref[...] = r.astype(out_dtype)

    ins = [a, b] + ([add] if has_add else [])
    in_specs = [a_spec, b_spec] + ([o_spec] if has_add else [])
    return pl.pallas_call(
        body, name=name, grid=(ni, nj, nk) if rows_outer else (nj, ni, nk),
        in_specs=in_specs, out_specs=o_spec,
        out_shape=jax.ShapeDtypeStruct((m, n), out_dtype),
        scratch_shapes=[pltpu.VMEM((bm, bn), F32)],
        compiler_params=_params("parallel", "parallel", "arbitrary"),
    )(*ins)


def _rmsnorm(h, gains, name):
    m, d = h.shape
    bm = _pick(m, (640, 384, 128))
    ng = len(gains)

    def body(*refs):
        h_ref, g_refs, o_refs = refs[0], refs[1:1 + ng], refs[1 + ng:]
        x = h_ref[...]
        xhat = x * lax.rsqrt(jnp.mean(x * x, axis=-1, keepdims=True) + NORM_EPS)
        for g_ref, o_ref in zip(g_refs, o_refs):
            o_ref[...] = (xhat * g_ref[...]).astype(BF16)

    row = pl.BlockSpec((bm, d), lambda i: (i, 0))
    gain = pl.BlockSpec((1, d), lambda i: (0, 0))
    return pl.pallas_call(
        body, name=name, grid=(m // bm,),
        in_specs=[row] + [gain] * ng, out_specs=[row] * ng,
        out_shape=[jax.ShapeDtypeStruct((m, d), BF16)] * ng,
        compiler_params=_params("parallel"),
    )(h, *gains)


def _rmsnorm_bwd(h, gains, dns, dres, name):
    m, d = h.shape
    bm = _pick(m, (640, 384, 128))
    ng = len(gains)
    nsteps = m // bm

    def body(*refs):
        h_ref, dres_ref = refs[0], refs[1]
        g_refs = refs[2:2 + ng]
        dn_refs = refs[2 + ng:2 + 2 * ng]
        dh_ref = refs[2 + 2 * ng]
        dg_refs = refs[3 + 2 * ng:3 + 3 * ng]
        acc_refs = refs[3 + 3 * ng:]
        i = pl.program_id(0)
        x = h_ref[...]
        r = lax.rsqrt(jnp.mean(x * x, axis=-1, keepdims=True) + NORM_EPS)
        xhat = x * r
        dh = dres_ref[...]
        for g_ref, dn_ref, dg_ref, acc_ref in zip(g_refs, dn_refs, dg_refs, acc_refs):
            dn = dn_ref[...]
            dy = dn * g_ref[...]
            dh = dh + r * (dy - xhat * jnp.mean(dy * xhat, axis=-1, keepdims=True))
            part = jnp.sum((dn * xhat).reshape(bm // 8, 8, d), axis=0)

            @pl.when(i == 0)
            def _():
                acc_ref[...] = part

            @pl.when(i > 0)
            def _():
                acc_ref[...] += part

            @pl.when(i == nsteps - 1)
            def _():
                dg_ref[...] = jnp.sum(acc_ref[...], axis=0, keepdims=True)

        dh_ref[...] = dh

    row = pl.BlockSpec((bm, d), lambda i: (i, 0))
    gain = pl.BlockSpec((1, d), lambda i: (0, 0))
    outs = pl.pallas_call(
        body, name=name, grid=(nsteps,),
        in_specs=[row, row] + [gain] * ng + [row] * ng,
        out_specs=[row] + [gain] * ng,
        out_shape=[jax.ShapeDtypeStruct((m, d), F32)] + [jax.ShapeDtypeStruct((1, d), F32)] * ng,
        scratch_shapes=[pltpu.VMEM((8, d), F32)] * ng,
        compiler_params=_params("arbitrary"),
    )(h, dres, *gains, *dns)
    return outs[0], list(outs[1:])


def _loss_head(h, gain, target):
    m, d = h.shape
    nsteps = m // LANES

    def body(h_ref, g_ref, t_ref, loss_ref, dh_ref, dg_ref, lacc, gacc):
        i = pl.program_id(0)

        @pl.when(i == 0)
        def _():
            lacc[...] = jnp.zeros_like(lacc)
            gacc[...] = jnp.zeros_like(gacc)
            dh_ref[...] = jnp.zeros_like(dh_ref)

        @pl.when(i > 0)
        def _():
            x = h_ref[...]
            r = lax.rsqrt(jnp.mean(x * x, axis=-1, keepdims=True) + NORM_EPS)
            xhat = x * r
            g = g_ref[...]
            err = xhat * g - t_ref[...]
            lacc[...] += jnp.sum((err * err).reshape(LANES // 8, 8, d), axis=0)
            dn = err * (1.0 / d)
            dy = dn * g
            dh_ref[...] = r * (dy - xhat * jnp.mean(dy * xhat, axis=-1, keepdims=True))
            gacc[...] += jnp.sum((dn * xhat).reshape(LANES // 8, 8, d), axis=0)

        @pl.when(i == nsteps - 1)
        def _():
            tot = jnp.sum(jnp.sum(lacc[...], axis=1, keepdims=True), axis=0, keepdims=True)
            loss_ref[...] = jnp.broadcast_to(tot * (0.5 / d), (1, LANES))
            dg_ref[...] = jnp.sum(gacc[...], axis=0, keepdims=True)

    row = pl.BlockSpec((LANES, d), lambda i: (i, 0))
    trow = pl.BlockSpec((LANES, d), lambda i: (jnp.maximum(i - 1, 0), 0))
    gain_spec = pl.BlockSpec((1, d), lambda i: (0, 0))
    return pl.pallas_call(
        body, name="loss_head", grid=(nsteps,),
        in_specs=[row, gain_spec, trow],
        out_specs=[pl.BlockSpec((1, LANES), lambda i: (0, 0)), row, gain_spec],
        out_shape=[jax.ShapeDtypeStruct((1, LANES), F32), jax.ShapeDtypeStruct((m, d), F32),
                   jax.ShapeDtypeStruct((1, d), F32)],
        scratch_shapes=[pltpu.VMEM((8, d), F32), pltpu.VMEM((8, d), F32)],
        compiler_params=_params("arbitrary"),
    )(h, gain, target)


def _swiglu(gu, name):
    m, f2 = gu.shape
    f = f2 // 2

    def body(gu_ref, o_ref):
        g = gu_ref[:, :f].astype(F32)
        u = gu_ref[:, f:].astype(F32)
        o_ref[...] = (g * _sigmoid(g) * u).astype(BF16)

    return pl.pallas_call(
        body, name=name, grid=(m // LANES,),
        in_specs=[pl.BlockSpec((LANES, f2), lambda i: (i, 0))],
        out_specs=pl.BlockSpec((LANES, f), lambda i: (i, 0)),
        out_shape=jax.ShapeDtypeStruct((m, f), BF16),
        compiler_params=_params("parallel"),
    )(gu)


def _swiglu_bwd(gu, da, name):
    m, f2 = gu.shape
    f = f2 // 2

    def body(gu_ref, da_ref, o_ref):
        g = gu_ref[:, :f].astype(F32)
        u = gu_ref[:, f:].astype(F32)
        da_ = da_ref[...]
        s = _sigmoid(g)
        o_ref[:, :f] = (da_ * u * s * (1.0 + g * (1.0 - s))).astype(BF16)
        o_ref[:, f:] = (da_ * g * s).astype(BF16)

    return pl.pallas_call(
        body, name=name, grid=(m // LANES,),
        in_specs=[pl.BlockSpec((LANES, f2), lambda i: (i, 0)), pl.BlockSpec((LANES, f), lambda i: (i, 0))],
        out_specs=pl.BlockSpec((LANES, f2), lambda i: (i, 0)),
        out_shape=jax.ShapeDtypeStruct((m, f2), BF16),
        compiler_params=_params("parallel"),
    )(gu, da)


def _conv_taps(xa, w_ref):
    acc = xa * w_ref[CONV_WIDTH - 1:CONV_WIDTH, :]
    for i in range(CONV_WIDTH - 1):
        acc = acc + pltpu.roll(xa, CONV_WIDTH - 1 - i, 0) * w_ref[i:i + 1, :]
    return acc[8:]


def _gdn_conv(proj, conv_w, n_heads):
    m, w3 = proj.shape
    wd = w3 // 3
    nblk = m // LANES
    unroll = _pick(nblk - 1, (4, 2, 1))

    def body(x_ref, w_ref, o_ref):
        c = pl.program_id(0)
        is_qk = c < 2 * n_heads

        def block(b, xa=None):
            r0 = pl.multiple_of(b * LANES, LANES)
            if xa is None:
                xa = x_ref[pl.ds(r0 - 8, LANES + 8), :]
            cv = _conv_taps(xa, w_ref)
            s = cv * _sigmoid(cv)
            ss = jnp.sum(s * s, axis=-1, keepdims=True)
            yield
            o_ref[pl.ds(r0, LANES), :] = jnp.where(is_qk, s * lax.rsqrt(ss + NORM_EPS), s)

        x0 = jnp.concatenate([jnp.zeros((8, LANES), F32), x_ref[pl.ds(0, LANES), :]], axis=0)
        _lockstep([block(0, x0)])

        def step(j, carry):
            _lockstep([block(1 + unroll * j + u) for u in range(unroll)])
            return carry

        lax.fori_loop(0, (nblk - 1) // unroll, step, 0)

    return pl.pallas_call(
        body, name="gdn_conv", grid=(w3 // LANES,),
        in_specs=[pl.BlockSpec((m, LANES), lambda c: (0, c)), pl.BlockSpec((CONV_WIDTH, LANES), lambda c: (0, c))],
        out_specs=pl.BlockSpec((None, m, LANES), lambda c: (c // n_heads, 0, c % n_heads)),
        out_shape=jax.ShapeDtypeStruct((3, m, wd), F32),
        compiler_params=_params("parallel"),
    )(proj, conv_w)


def _gdn_conv_bwd(proj, conv_w, dact, n_heads):
    m, w3 = proj.shape
    nblk = m // LANES
    unroll = _pick(nblk - 1, (4, 2, 1))

    def body(x_ref, w_ref, dy_ref, dx_ref, dw_ref, dc_ref):
        c = pl.program_id(0)
        is_qk = c < 2 * n_heads
        dc_ref[pl.ds(m, 8), :] = jnp.zeros((8, LANES), F32)

        def block(b, out, xa=None):
            r0 = pl.multiple_of(b * LANES, LANES)
            if xa is None:
                xa = x_ref[pl.ds(r0 - 8, LANES + 8), :]
            cv = _conv_taps(xa, w_ref)
            sg = _sigmoid(cv)
            s = cv * sg
            dy = dy_ref[pl.ds(r0, LANES), :]
            ss = jnp.sum(s * s, axis=-1, keepdims=True)
            yield
            rn = lax.rsqrt(ss + NORM_EPS)
            yn = s * rn
            proj_len = jnp.sum(dy * yn, axis=-1, keepdims=True)
            yield
            ds = jnp.where(is_qk, rn * (dy - yn * proj_len), dy)
            dcv = ds * sg * (1.0 + cv * (1.0 - sg))
            dc_ref[pl.ds(r0, LANES), :] = dcv
            parts = []
            for i in range(CONV_WIDTH):
                xs = xa[8:] if i == CONV_WIDTH - 1 else pltpu.roll(xa, CONV_WIDTH - 1 - i, 0)[8:]
                parts.append(jnp.sum((dcv * xs).reshape(LANES // 8, 8, LANES), axis=0))
            out.append(parts)

        def add_parts(acc, outs):
            for parts in outs:
                acc = tuple(a + p for a, p in zip(acc, parts))
            return acc

        x0 = jnp.concatenate([jnp.zeros((8, LANES), F32), x_ref[pl.ds(0, LANES), :]], axis=0)
        first = []
        _lockstep([block(0, first, x0)])
        dwacc = tuple(first[0])

        def step1(j, acc):
            outs = []
            _lockstep([block(1 + unroll * j + u, outs) for u in range(unroll)])
            return add_parts(acc, outs)

        dwacc = lax.fori_loop(0, (nblk - 1) // unroll, step1, dwacc)
        for i in range(CONV_WIDTH):
            dw_ref[i:i + 1, :] = jnp.sum(dwacc[i], axis=0, keepdims=True)

        def step2(b, carry):
            r0 = pl.multiple_of(b * LANES, LANES)
            da = dc_ref[pl.ds(r0, LANES + 8), :]
            acc = da * w_ref[CONV_WIDTH - 1:CONV_WIDTH, :]
            for i in range(CONV_WIDTH - 1):
                sh = CONV_WIDTH - 1 - i
                acc = acc + pltpu.roll(da, LANES + 8 - sh, 0) * w_ref[i:i + 1, :]
            dx_ref[pl.ds(r0, LANES), :] = acc[:LANES]
            return carry

        lax.fori_loop(0, nblk, step2, 0)

    return pl.pallas_call(
        body, name="gdn_conv_bwd", grid=(w3 // LANES,),
        in_specs=[pl.BlockSpec((m, LANES), lambda c: (0, c)),
                  pl.BlockSpec((CONV_WIDTH, LANES), lambda c: (0, c)),
                  pl.BlockSpec((None, m, LANES), lambda c: (c // n_heads, 0, c % n_heads))],
        out_specs=[pl.BlockSpec((m, LANES), lambda c: (0, c)), pl.BlockSpec((CONV_WIDTH, LANES), lambda c: (0, c))],
        out_shape=[jax.ShapeDtypeStruct((m, w3), F32), jax.ShapeDtypeStruct((CONV_WIDTH, w3), F32)],
        scratch_shapes=[pltpu.VMEM((m + 8, LANES), F32)],
        compiler_params=_params("parallel"),
    )(proj, conv_w, dact)


def _softplus(x):
    return jnp.maximum(x, 0.0) + jnp.log(1.0 + jnp.exp(-jnp.abs(x)))


def _gdn_gates(ab, a_log, dt_bias, n_heads):
    m = ab.shape[0]
    bm = _pick(m, (640, 384, 128))

    def body(ab_ref, al_ref, dt_ref, g_ref, b_ref):
        i = pl.program_id(0)
        x = ab_ref[...]
        live = (_iota2((bm, LANES), 0) + i * bm) >= PAD_ROWS
        g = jnp.where(live, -jnp.exp(al_ref[...]) * _softplus(x + dt_ref[...]), 0.0)
        beta = jnp.where(live, _sigmoid(x), 0.0)
        for h in range(n_heads):
            g_ref[:, h * LANES:(h + 1) * LANES] = jnp.broadcast_to(g[:, h:h + 1], (bm, LANES))
            b_ref[:, h * LANES:(h + 1) * LANES] = jnp.broadcast_to(
                beta[:, n_heads + h:n_heads + h + 1], (bm, LANES))

    wide = pl.BlockSpec((bm, n_heads * LANES), lambda i: (i, 0))
    return pl.pallas_call(
        body, name="gdn_gates", grid=(m // bm,),
        in_specs=[pl.BlockSpec((bm, LANES), lambda i: (i, 0))] + [pl.BlockSpec((1, LANES), lambda i: (0, 0))] * 2,
        out_specs=[wide, wide],
        out_shape=[jax.ShapeDtypeStruct((m, n_heads * LANES), F32)] * 2,
        compiler_params=_params("parallel"),
    )(ab, a_log, dt_bias)


def _gdn_gates_bwd(ab, a_log, dt_bias, dg_b, dbeta_b, n_heads):
    m = ab.shape[0]
    bm = _pick(m, (640, 384, 128))
    nsteps = m // bm

    def body(ab_ref, al_ref, dt_ref, dg_ref, db_ref, dab_ref, dal_ref, ddt_ref, acc_al, acc_dt):
        i = pl.program_id(0)
        x = ab_ref[...]
        lane = _iota2((bm, LANES), 1)
        live = (_iota2((bm, LANES), 0) + i * bm) >= PAD_ROWS
        dg = jnp.zeros((bm, LANES), F32)
        dbeta = jnp.zeros((bm, LANES), F32)
        for h in range(n_heads):
            dg = jnp.where(lane == h, dg_ref[:, h * LANES:(h + 1) * LANES], dg)
            dbeta = jnp.where(lane == n_heads + h, db_ref[:, h * LANES:(h + 1) * LANES], dbeta)
        dg = jnp.where(live, dg, 0.0)
        dbeta = jnp.where(live, dbeta, 0.0)
        nea = -jnp.exp(al_ref[...])
        pre = x + dt_ref[...]
        d_pre = dg * nea * _sigmoid(pre)
        beta = _sigmoid(x)
        dab_ref[...] = d_pre + dbeta * beta * (1.0 - beta)
        p_al = jnp.sum((dg * nea * _softplus(pre)).reshape(bm // 8, 8, LANES), axis=0)
        p_dt = jnp.sum(d_pre.reshape(bm // 8, 8, LANES), axis=0)

        @pl.when(i == 0)
        def _():
            acc_al[...] = p_al
            acc_dt[...] = p_dt

        @pl.when(i > 0)
        def _():
            acc_al[...] += p_al
            acc_dt[...] += p_dt

        @pl.when(i == nsteps - 1)
        def _():
            dal_ref[...] = jnp.sum(acc_al[...], axis=0, keepdims=True)
            ddt_ref[...] = jnp.sum(acc_dt[...], axis=0, keepdims=True)

    wide = pl.BlockSpec((bm, n_heads * LANES), lambda i: (i, 0))
    one = pl.BlockSpec((1, LANES), lambda i: (0, 0))
    nar = pl.BlockSpec((bm, LANES), lambda i: (i, 0))
    return pl.pallas_call(
        body, name="gdn_gates_bwd", grid=(nsteps,),
        in_specs=[nar, one, one, wide, wide],
        out_specs=[nar, one, one],
        out_shape=[jax.ShapeDtypeStruct((m, LANES), F32), jax.ShapeDtypeStruct((1, LANES), F32),
                   jax.ShapeDtypeStruct((1, LANES), F32)],
        scratch_shapes=[pltpu.VMEM((8, LANES), F32)] * 2,
        compiler_params=_params("arbitrary"),
    )(ab, a_log, dt_bias, dg_b, dbeta_b)


def _chunk_masks():
    c = GDN_CHUNK
    ri, ci = _iota2((c, c), 0), _iota2((c, c), 1)
    return ri >= ci, ri > ci, ri == ci


def _chunk_decay(gbs):
    c = GDN_CHUNK
    incl, _, _ = _chunk_masks()
    inclf = incl.astype(F32)
    ones = jnp.ones((c, LANES), F32)
    g_cums = [_hdot(inclf, gb) for gb in gbs]
    g_rows = [_hdot(ones, g_cum, NT) * (1.0 / LANES) for g_cum in g_cums]
    dmats = [jnp.where(incl, jnp.exp(jnp.where(incl, g_cum[:, :c] - g_row, 0.0)), 0.0)
             for g_cum, g_row in zip(g_cums, g_rows)]
    return g_cums, dmats


def _unit_lower_inverse(mats):
    _, _, eye = _chunk_masks()
    xs = [eye.astype(F32) - a for a in mats]
    ps = list(mats)
    for _ in range(int(math.log2(GDN_CHUNK)) - 1):
        ps = [_hdot(p, p) for p in ps]
        xs = [x + _hdot(x, p) for x, p in zip(xs, ps)]
    return xs


def _gdn_prep(qkv, g_b, beta_b, n_heads):
    _, m, wd = qkv.shape
    c = GDN_CHUNK
    nc = m // c
    cb = _pick(nc, (10, 5, 3, 2))
    rows = cb * c
    qscale = GDN_HEAD_DIM ** -0.5

    def body(qkv_ref, g_ref, b_ref, ub_ref, wc_ref, qd_ref, kd_ref, gc_ref, p_ref, t_ref):
        _, strict, _ = _chunk_masks()
        sls = [pl.ds(s * c, c) for s in range(cb)]
        g_cums, dmats = _chunk_decay([g_ref[sl, :] for sl in sls])
        kks = [_bdot(qkv_ref[1, sl, :], qkv_ref[1, sl, :], NT) for sl in sls]
        ts = _unit_lower_inverse([jnp.where(strict, b_ref[sl, :][:, :c] * dmat * kk, 0.0)
                                  for sl, dmat, kk in zip(sls, dmats, kks)])
        for s, (sl, g_cum, dmat, t) in enumerate(zip(sls, g_cums, dmats, ts)):
            q = qkv_ref[0, sl, :] * qscale
            k = qkv_ref[1, sl, :]
            bb = b_ref[sl, :]
            gam = jnp.exp(g_cum)
            ub_ref[sl, :] = _hdot(t, bb * qkv_ref[2, sl, :])
            wc_ref[sl, :] = _hdot(t, bb * gam * k)
            p_ref[s] = _bdot(q, k, NT) * dmat
            qd_ref[sl, :] = q * gam
            kd_ref[sl, :] = k * jnp.exp(g_cum[c - 1:c, :] - g_cum)
            gc_ref[sl, :] = g_cum
            t_ref[s] = t

    blk = pl.BlockSpec((rows, LANES), lambda h, n: (n, h))
    sq = pl.BlockSpec((None, cb, c, c), lambda h, n: (h, n, 0, 0))
    big = jax.ShapeDtypeStruct((m, wd), F32)
    small = jax.ShapeDtypeStruct((n_heads, nc, c, c), F32)
    return pl.pallas_call(
        body, name="gdn_prep", grid=(n_heads, nc // cb),
        in_specs=[pl.BlockSpec((3, rows, LANES), lambda h, n: (0, n, h)), blk, blk],
        out_specs=[blk] * 5 + [sq, sq],
        out_shape=[big] * 5 + [small, small],
        compiler_params=_params("parallel", "parallel"),
    )(qkv, g_b, beta_b)


def _gdn_prep_bwd(qkv, g_b, beta_b, ub, wc, tinv, dub, dwc, dqd, dkd, dp, dgl, n_heads):
    _, m, wd = qkv.shape
    c = GDN_CHUNK
    nc = m // c
    cb = _pick(nc, (10, 5, 3, 2))
    rows = cb * c
    qscale = GDN_HEAD_DIM ** -0.5

    def body(qkv_ref, g_ref, b_ref, ub_ref, wc_ref, t_ref, dub_ref, dwc_ref, dqd_ref, dkd_ref, dp_ref, dgl_ref,
             dqkv_ref, dg_ref, db_ref):
        incl, strict, _ = _chunk_masks()
        ones_c = jnp.ones((c, LANES), F32)
        ones_l = jnp.ones((LANES, LANES), F32)
        last_row = _iota2((c, LANES), 0) == c - 1

        def chunk(s):
            sl = pl.ds(s * c, c)
            q = qkv_ref[0, sl, :] * qscale
            k = qkv_ref[1, sl, :]
            v = qkv_ref[2, sl, :]
            bb = b_ref[sl, :]
            bcol = bb[:, :c]
            t = t_ref[s]
            g_cum = _hdot(incl.astype(F32), g_ref[sl, :])
            kk = _bdot(k, k, NT)
            qk = _bdot(q, k, NT)
            drhs_u = _hdot(t, dub_ref[sl, :], TN)
            drhs_w = _hdot(t, dwc_ref[sl, :], TN)
            yield
            g_row = _hdot(ones_c, g_cum, NT) * (1.0 / LANES)
            da = -jnp.where(strict, _bdot(drhs_u, ub_ref[sl, :], NT) + _bdot(drhs_w, wc_ref[sl, :], NT), 0.0)
            dbeta = _hdot(drhs_u * v, ones_l)
            tk = _hdot(drhs_w * k, ones_l)
            dqd = dqd_ref[sl, :]
            dgam = _hdot(dqd * q, ones_l)
            yield
            dmat = jnp.where(incl, jnp.exp(jnp.where(incl, g_cum[:, :c] - g_row, 0.0)), 0.0)
            gam = jnp.exp(g_cum)
            edec = jnp.exp(g_cum[c - 1:c, :] - g_cum)
            dkd = dkd_ref[sl, :]
            skd = _hdot(dkd * k * edec, ones_l)
            dbeta = dbeta + _hdot(da * dmat * kk, ones_c)
            dkk = da * bcol * dmat
            dpm = jnp.where(incl, dp_ref[s], 0.0)
            dpd = dpm * dmat
            dk = _bdot(dkk, k) + _bdot(dkk, k, TN) + _bdot(dpd, q, TN)
            dq = _bdot(dpd, k)
            e = (da * bcol * kk + dpm * qk) * dmat
            yield
            dk = dk + bb * gam * drhs_w + dkd * edec
            dq = dq + dqd * gam
            dbeta = dbeta + tk * gam
            dgam = dgam + bb * tk
            dgc = dgam * gam - skd + _hdot(e, ones_c) - _hdot(e, ones_c, TN)
            tot = _hdot(jnp.ones((c, c), F32), skd)
            yield
            dgc = dgc + jnp.where(last_row, tot + dgl_ref[s, 0:1, :], 0.0)
            dg_ref[sl, :] = _hdot((_iota2((c, c), 0) <= _iota2((c, c), 1)).astype(F32), dgc)
            db_ref[sl, :] = dbeta
            dqkv_ref[0, sl, :] = dq * qscale
            dqkv_ref[1, sl, :] = dk
            dqkv_ref[2, sl, :] = bb * drhs_u

        _lockstep([chunk(s) for s in range(cb)])

    blk = pl.BlockSpec((rows, LANES), lambda h, n: (n, h))
    blk3 = pl.BlockSpec((3, rows, LANES), lambda h, n: (0, n, h))
    sq = pl.BlockSpec((None, cb, c, c), lambda h, n: (h, n, 0, 0))
    sgl = pl.BlockSpec((None, cb, 8, LANES), lambda h, n: (h, n, 0, 0))
    big = jax.ShapeDtypeStruct((m, wd), F32)
    return pl.pallas_call(
        body, name="gdn_prep_bwd", grid=(n_heads, nc // cb),
        in_specs=[blk3, blk, blk, blk, blk, sq, blk, blk, blk, blk, sq, sgl],
        out_specs=[blk3, blk, blk],
        out_shape=[jax.ShapeDtypeStruct((3, m, wd), F32), big, big],
        compiler_params=_params("parallel", "parallel"),
    )(qkv, g_b, beta_b, ub, wc, tinv, dub, dwc, dqd, dkd, dp, dgl)


def _gdn_scan(ub, wc, qd, kd, gc, p, n_heads):
    m, wd = ub.shape
    c = GDN_CHUNK
    nc = m // c
    dh = GDN_HEAD_DIM

    def body(ub_ref, wc_ref, qd_ref, kd_ref, gc_ref, p_ref, o_ref, ssave_ref, s_ref):
        n = pl.program_id(0)

        @pl.when(n == 0)
        def _():
            s_ref[...] = jnp.zeros_like(s_ref)

        def head(h):
            sl = slice(h * LANES, (h + 1) * LANES)
            s = s_ref[h]
            ssave_ref[h] = s
            ws = _bdot(wc_ref[:, sl], s)
            qs = _bdot(qd_ref[:, sl], s)
            yield
            u = ub_ref[:, sl] - ws
            pu = _bdot(p_ref[h], u)
            ku = _bdot(kd_ref[:, sl], u, TN)
            yield
            o_ref[:, sl] = qs + pu
            s_ref[h] = s * jnp.exp(gc_ref[c - 1:c, sl]) + ku

        _lockstep([head(h) for h in range(n_heads)])

    blk = pl.BlockSpec((c, wd), lambda n: (n, 0))
    return pl.pallas_call(
        body, name="gdn_scan", grid=(nc,),
        in_specs=[blk] * 5 + [pl.BlockSpec((n_heads, None, c, c), lambda n: (0, n, 0, 0))],
        out_specs=[blk, pl.BlockSpec((n_heads, None, dh, dh), lambda n: (0, n, 0, 0))],
        out_shape=[jax.ShapeDtypeStruct((m, wd), F32), jax.ShapeDtypeStruct((n_heads, nc, dh, dh), F32)],
        scratch_shapes=[pltpu.VMEM((n_heads, dh, dh), F32)],
        compiler_params=_params("arbitrary"),
    )(ub, wc, qd, kd, gc, p)


def _gdn_scan_bwd(do, ssave, ub, wc, qd, kd, gc, p, n_heads):
    m, wd = ub.shape
    c = GDN_CHUNK
    nc = m // c
    dh = GDN_HEAD_DIM

    def body(do_ref, ss_ref, ub_ref, wc_ref, qd_ref, kd_ref, gc_ref, p_ref,
             dub_ref, dwc_ref, dqd_ref, dkd_ref, dp_ref, dgl_ref, ds_ref):
        n = pl.program_id(0)

        @pl.when(n == 0)
        def _():
            ds_ref[...] = jnp.zeros_like(ds_ref)

        def head(h):
            sl = slice(h * LANES, (h + 1) * LANES)
            ds_next = ds_ref[h]
            s = ss_ref[h]
            do_ = do_ref[:, sl]
            wcv = wc_ref[:, sl]
            kdv = kd_ref[:, sl]
            ws = _bdot(wcv, s)
            du = _bdot(p_ref[h], do_, TN) + _bdot(kdv, ds_next)
            dqd_ref[:, sl] = _bdot(do_, s, NT)
            col = _hdot(jnp.ones((8, dh), F32), s * ds_next)
            qdo = _bdot(qd_ref[:, sl], do_, TN)
            yield
            u = ub_ref[:, sl] - ws
            egl = jnp.exp(gc_ref[c - 1:c, sl])
            dub_ref[:, sl] = du
            dwc_ref[:, sl] = -_bdot(du, s, NT)
            dp_ref[h] = _bdot(do_, u, NT)
            dkd_ref[:, sl] = _bdot(u, ds_next, NT)
            dgl_ref[h] = egl * _hdot(col, jnp.ones((LANES, LANES), F32))
            ds_ref[h] = qdo + egl * ds_next - _bdot(wcv, du, TN)

        _lockstep([head(h) for h in range(n_heads)])

    blk = pl.BlockSpec((c, wd), lambda n: (nc - 1 - n, 0))
    sq = pl.BlockSpec((n_heads, None, c, c), lambda n: (0, nc - 1 - n, 0, 0))
    big = jax.ShapeDtypeStruct((m, wd), F32)
    return pl.pallas_call(
        body, name="gdn_scan_bwd", grid=(nc,),
        in_specs=[blk, pl.BlockSpec((n_heads, None, dh, dh), lambda n: (0, nc - 1 - n, 0, 0))] + [blk] * 5 + [sq],
        out_specs=[blk] * 4 + [sq, pl.BlockSpec((n_heads, None, 8, LANES), lambda n: (0, nc - 1 - n, 0, 0))],
        out_shape=[big] * 4 + [jax.ShapeDtypeStruct((n_heads, nc, c, c), F32),
                               jax.ShapeDtypeStruct((n_heads, nc, 8, LANES), F32)],
        scratch_shapes=[pltpu.VMEM((n_heads, dh, dh), F32)],
        compiler_params=_params("arbitrary"),
    )(do, ssave, ub, wc, qd, kd, gc, p)


def _gdn_outnorm(o, gate, gain, n_heads):
    m, wd = o.shape
    bm = _pick(m, (640, 384, 128))

    def body(o_ref, gt_ref, gn_ref, y_ref):
        gn = gn_ref[...]
        for h in range(n_heads):
            sl = slice(h * LANES, (h + 1) * LANES)
            x = o_ref[:, sl]
            gt = gt_ref[:, sl]
            r = lax.rsqrt(jnp.mean(x * x, axis=-1, keepdims=True) + NORM_EPS)
            y_ref[:, sl] = (x * r * gn * gt * _sigmoid(gt)).astype(BF16)

    row = pl.BlockSpec((bm, wd), lambda i: (i, 0))
    return pl.pallas_call(
        body, name="gdn_outnorm", grid=(m // bm,),
        in_specs=[row, row, pl.BlockSpec((1, LANES), lambda i: (0, 0))], out_specs=row,
        out_shape=jax.ShapeDtypeStruct((m, wd), BF16),
        compiler_params=_params("parallel"),
    )(o, gate, gain)


def _gdn_outnorm_bwd(o, gate, gain, dy, n_heads):
    m, wd = o.shape
    bm = _pick(m, (640, 384, 128))
    nsteps = m // bm

    def body(o_ref, gt_ref, gn_ref, dy_ref, do_ref, dgt_ref, dgn_ref, acc_ref):
        i = pl.program_id(0)
        gn = gn_ref[...]
        part = jnp.zeros((8, LANES), F32)
        for h in range(n_heads):
            sl = slice(h * LANES, (h + 1) * LANES)
            x = o_ref[:, sl]
            gt = gt_ref[:, sl]
            d_out = dy_ref[:, sl]
            r = lax.rsqrt(jnp.mean(x * x, axis=-1, keepdims=True) + NORM_EPS)
            xhat = x * r
            sg = _sigmoid(gt)
            dz = d_out * gt * sg
            dgt_ref[:, sl] = d_out * xhat * gn * sg * (1.0 + gt * (1.0 - sg))
            dxh = dz * gn
            do_ref[:, sl] = r * (dxh - xhat * jnp.mean(dxh * xhat, axis=-1, keepdims=True))
            part = part + jnp.sum((dz * xhat).reshape(bm // 8, 8, LANES), axis=0)

        @pl.when(i == 0)
        def _():
            acc_ref[...] = part

        @pl.when(i > 0)
        def _():
            acc_ref[...] += part

        @pl.when(i == nsteps - 1)
        def _():
            dgn_ref[...] = jnp.sum(acc_ref[...], axis=0, keepdims=True)

    row = pl.BlockSpec((bm, wd), lambda i: (i, 0))
    one = pl.BlockSpec((1, LANES), lambda i: (0, 0))
    return pl.pallas_call(
        body, name="gdn_outnorm_bwd", grid=(nsteps,),
        in_specs=[row, row, one, row], out_specs=[row, row, one],
        out_shape=[jax.ShapeDtypeStruct((m, wd), F32)] * 2 + [jax.ShapeDtypeStruct((1, LANES), F32)],
        scratch_shapes=[pltpu.VMEM((8, LANES), F32)],
        compiler_params=_params("arbitrary"),
    )(o, gate, gain, dy)


def _sb_consts(kind):
    ri, ci = _iota2((LANES, LANES), 0), _iota2((LANES, LANES), 1)
    tri = {"suffix_excl": ri > ci, "prefix_incl": ri <= ci, "prefix_excl": ri < ci}[kind]
    return jnp.concatenate([tri.astype(BF16), jnp.ones((LANES, LANES), BF16)], axis=1)


def _sb_stack(a):
    first = _iota2((LANES, LANES), 1) < SB_HEAD_DIM
    return jnp.concatenate([jnp.where(first, a, 0.0), jnp.where(first, 0.0, a)], axis=0)


def _sb_unstack(a2):
    first = _iota2((LANES, LANES), 1) < SB_HEAD_DIM
    return jnp.where(first, a2[:LANES], a2[LANES:])


def _sb_visible(qi, kb, blocks=1):
    shape = (2 * LANES, blocks * LANES)
    kpos = _iota2(shape, 1) + kb * LANES
    qpos = (_iota2(shape, 0) & (LANES - 1)) + qi * LANES
    return (kpos < qpos) & (kpos >= PAD_ROWS)


def _sb_log_gates(z, vis):
    t = jnp.log2(1.0 + jnp.exp2(-jnp.abs(z)))
    lb = jnp.minimum(z, 0.0) - t
    lk = lb - z
    if vis is not None:
        lk = jnp.where(vis, lk, 0.0)
    return lb, lk


def _sb_attention(q, kv):
    m, wd = q.shape
    npair = wd // LANES
    nq = m // LANES
    scale = SB_HEAD_DIM ** -0.5

    def body(q_ref, k_ref, v_ref, o_ref, w_ref, b_ref):
        qi = pl.program_id(1)
        su = _sb_consts("suffix_excl")
        q2 = _sb_stack(q_ref[...] * (scale * math.log2(math.e))).astype(BF16)

        def tiles(kbs, carry, masked):
            run, acc = carry
            r0s = [pl.multiple_of(kb * LANES, LANES) for kb in kbs]
            zs = [lax.dot_general(q2, k_ref[pl.ds(r0, LANES), :], NT, preferred_element_type=F32) for r0 in r0s]
            parts = []
            for kb, r0, z in zip(kbs, r0s, zs):
                vis = _sb_visible(qi, kb) if masked else None
                lb, lk = _sb_log_gates(z, vis)
                parts.append((kb, r0, vis, lb, lk))
            parts = [(kb, r0, vis, lb, _split_dot(lk, su)) for kb, r0, vis, lb, lk in parts]
            for kb, r0, vis, lb, cs in parts:
                w = jnp.exp2(lb + cs[:, :LANES] + run)
                if vis is not None:
                    w = jnp.where(vis, w, 0.0)
                run = run + cs[:, LANES:]
                wb = w.astype(BF16)
                w_ref[kb] = wb
                b_ref[kb] = jnp.exp2(lb).astype(BF16)
                acc = acc + lax.dot_general(wb, v_ref[pl.ds(r0, LANES), :], NN, preferred_element_type=F32)
            return run, acc

        su2 = (_iota2((2 * LANES, 2 * LANES), 0) > _iota2((2 * LANES, 2 * LANES), 1)).astype(BF16)

        def pairs(his, carry, masked):
            run, acc = carry
            r0s = [pl.multiple_of((hi - 1) * LANES, LANES) for hi in his]
            zs = [lax.dot_general(q2, k_ref[pl.ds(r0, 2 * LANES), :], NT, preferred_element_type=F32) for r0 in r0s]
            masks = masked if isinstance(masked, list) else [masked] * len(his)
            viss = [_sb_visible(qi, hi - 1, 2) if mk else None for hi, mk in zip(his, masks)]
            gates = [_sb_log_gates(z, vis) for z, vis in zip(zs, viss)]
            css = [_split_dot(lk, su2) for _, lk in gates]
            for hi, r0, vis, (lb, lk), cs in zip(his, r0s, viss, gates, css):
                w = jnp.exp2(lb + cs + jnp.concatenate([run, run], axis=1))
                if vis is not None:
                    w = jnp.where(vis, w, 0.0)
                run = run + jnp.broadcast_to(cs[:, 0:1] + lk[:, 0:1], (2 * LANES, LANES))
                wb = w.astype(BF16)
                bb = jnp.exp2(lb).astype(BF16)
                w_ref[hi - 1], w_ref[hi] = wb[:, :LANES], wb[:, LANES:]
                b_ref[hi - 1], b_ref[hi] = bb[:, :LANES], bb[:, LANES:]
                acc = acc + lax.dot_general(wb, v_ref[pl.ds(r0, 2 * LANES), :], NN, preferred_element_type=F32)
            return run, acc

        def maybe(flag, fn, carry):
            return lax.fori_loop(0, flag, lambda j, c: fn(c), carry)

        zero = jnp.zeros((2 * LANES, LANES), F32)
        far = (qi >= SB_FAR).astype(jnp.int32)
        near = (qi >= 3).astype(jnp.int32) - far
        big = far + near
        carry = maybe(far, lambda c: pairs([qi, qi - 2, qi - 4, qi - 6], c, [True, False, False, False]), (zero, zero))
        carry = maybe(near, lambda c: pairs([qi], c, True), carry)
        carry = maybe(1 - big, lambda c: tiles([qi], c, True), carry)
        top, bottom = qi - 1 - big - 6 * far, 1 + big + 2 * far
        n_mid = jnp.maximum(top - bottom + 1, 0)
        n8 = n_mid // 8
        carry = lax.fori_loop(0, n8, lambda j, c: pairs([top - 8 * j - 2 * u for u in range(4)], c, False), carry)
        top = top - 8 * n8
        take4, take2, take1 = (n_mid >> 2) & 1, (n_mid >> 1) & 1, n_mid & 1
        carry = maybe(take4, lambda c: pairs([top, top - 2], c, False), carry)
        top = top - 4 * take4
        carry = maybe(take2, lambda c: pairs([top], c, False), carry)
        carry = maybe(take1, lambda c: tiles([bottom], c, False), carry)
        carry = maybe(far, lambda c: pairs([3, 1], c, [False, True]), carry)
        carry = maybe(near, lambda c: pairs([1], c, True), carry)
        carry = maybe((1 - big) * (qi >= 1).astype(jnp.int32), lambda c: tiles([0], c, True), carry)
        o_ref[...] = _sb_unstack(carry[1])

    qspec = pl.BlockSpec((LANES, LANES), lambda hp, qi: (qi, hp))
    saved = pl.BlockSpec((None, None, nq, 2 * LANES, LANES), lambda hp, qi: (hp, qi, 0, 0, 0))
    saved_shape = jax.ShapeDtypeStruct((npair, nq, nq, 2 * LANES, LANES), BF16)
    return pl.pallas_call(
        body, name="sb_attention", grid=(npair, nq),
        in_specs=[qspec, pl.BlockSpec((m, LANES), lambda hp, qi: (0, hp)),
                  pl.BlockSpec((m, LANES), lambda hp, qi: (0, npair + hp))],
        out_specs=[qspec, saved, saved],
        out_shape=[jax.ShapeDtypeStruct((m, wd), F32), saved_shape, saved_shape],
        compiler_params=_params("parallel", "arbitrary"),
    )(q, kv, kv)


def _sb_attention_bwd(q, kv, w_all, b_all, do):
    m, wd = q.shape
    npair = wd // LANES
    nq = m // LANES
    scale = SB_HEAD_DIM ** -0.5

    def body(q_ref, k_ref, v_ref, w_ref, b_ref, do_ref, dq_ref, dk_ref, dv_ref):
        qi = pl.program_id(1)

        @pl.when(qi == 0)
        def _():
            dk_ref[...] = jnp.zeros_like(dk_ref)
            dv_ref[...] = jnp.zeros_like(dv_ref)

        px = _sb_consts("prefix_excl")
        q2 = _sb_stack(q_ref[...] * scale).astype(BF16)
        do2 = _sb_stack(do_ref[...]).astype(BF16)

        def tiles(kbs, carry, masked):
            run_p, dq_acc = carry
            r0s = [pl.multiple_of(kb * LANES, LANES) for kb in kbs]
            das = [lax.dot_general(do2, v_ref[pl.ds(r0, LANES), :], NT, preferred_element_type=F32) for r0 in r0s]
            wbs = [w_ref[kb] for kb in kbs]
            for wb, r0 in zip(wbs, r0s):
                dv_ref[pl.ds(r0, LANES), :] += lax.dot_general(wb, do2, TN, preferred_element_type=F32)
            pws = [wb.astype(F32) * da for wb, da in zip(wbs, das)]
            pss = [_split_dot(pw, px, parts=SB_BWD_SUM_PARTS) for pw in pws]
            dzs = []
            for kb, pw, ps in zip(kbs, pws, pss):
                beta = b_ref[kb].astype(F32)
                dz = pw * (1.0 - beta) - beta * (run_p + ps[:, :LANES])
                if masked:
                    dz = jnp.where(_sb_visible(qi, kb), dz, 0.0)
                run_p = run_p + ps[:, LANES:]
                dzs.append(dz.astype(BF16))
            for r0, dz in zip(r0s, dzs):
                dq_acc = dq_acc + lax.dot_general(dz, k_ref[pl.ds(r0, LANES), :], NN, preferred_element_type=F32)
                dk_ref[pl.ds(r0, LANES), :] += lax.dot_general(dz, q2, TN, preferred_element_type=F32)
            return run_p, dq_acc

        px2 = (_iota2((2 * LANES, 2 * LANES), 0) < _iota2((2 * LANES, 2 * LANES), 1)).astype(BF16)

        def pairs(los, carry, masked):
            run_p, dq_acc = carry
            r0s = [pl.multiple_of(lo * LANES, LANES) for lo in los]
            das = [lax.dot_general(do2, v_ref[pl.ds(r0, 2 * LANES), :], NT, preferred_element_type=F32) for r0 in r0s]
            wbs = [jnp.concatenate([w_ref[lo], w_ref[lo + 1]], axis=1) for lo in los]
            for wb, r0 in zip(wbs, r0s):
                dv_ref[pl.ds(r0, 2 * LANES), :] += lax.dot_general(wb, do2, TN, preferred_element_type=F32)
            pws = [wb.astype(F32) * da for wb, da in zip(wbs, das)]
            pss = [_split_dot(pw, px2, parts=SB_BWD_SUM_PARTS) for pw in pws]
            masks = masked if isinstance(masked, list) else [masked] * len(los)
            dzs = []
            for lo, pw, ps, mk in zip(los, pws, pss, masks):
                beta = jnp.concatenate([b_ref[lo], b_ref[lo + 1]], axis=1).astype(F32)
                dz = pw * (1.0 - beta) - beta * (jnp.concatenate([run_p, run_p], axis=1) + ps)
                if mk:
                    dz = jnp.where(_sb_visible(qi, lo, 2), dz, 0.0)
                last = 2 * LANES - 1
                run_p = run_p + jnp.broadcast_to(ps[:, last:] + pw[:, last:], (2 * LANES, LANES))
                dzs.append(dz.astype(BF16))
            for r0, dz in zip(r0s, dzs):
                dq_acc = dq_acc + lax.dot_general(dz, k_ref[pl.ds(r0, 2 * LANES), :], NN, preferred_element_type=F32)
                dk_ref[pl.ds(r0, 2 * LANES), :] += lax.dot_general(dz, q2, TN, preferred_element_type=F32)
            return run_p, dq_acc

        def maybe(flag, fn, carry):
            return lax.fori_loop(0, flag, lambda j, c: fn(c), carry)

        zero = jnp.zeros((2 * LANES, LANES), F32)
        far = (qi >= SB_FAR).astype(jnp.int32)
        near = (qi >= 3).astype(jnp.int32) - far
        big = far + near
        carry = maybe(far, lambda c: pairs([0, 2], c, [True, False]), (zero, zero))
        carry = maybe(near, lambda c: pairs([0], c, True), carry)
        carry = maybe(1 - big, lambda c: tiles([0], c, True), carry)
        low, high = 1 + big + 2 * far, qi - 1 - big - 6 * far
        n_mid = jnp.maximum(high - low + 1, 0)
        n8 = n_mid // 8
        carry = lax.fori_loop(0, n8, lambda j, c: pairs([low + 8 * j + 2 * u for u in range(4)], c, False), carry)
        low = low + 8 * n8
        take4, take2, take1 = (n_mid >> 2) & 1, (n_mid >> 1) & 1, n_mid & 1
        carry = maybe(take4, lambda c: pairs([low, low + 2], c, False), carry)
        low = low + 4 * take4
        carry = maybe(take2, lambda c: pairs([low], c, False), carry)
        carry = maybe(take1, lambda c: tiles([high], c, False), carry)
        carry = maybe(far, lambda c: pairs([qi - 7, qi - 5, qi - 3, qi - 1], c, [False, False, False, True]), carry)
        carry = maybe(near, lambda c: pairs([qi - 1], c, True), carry)
        carry = maybe((1 - big) * (qi >= 1).astype(jnp.int32), lambda c: tiles([qi], c, True), carry)
        dq_ref[...] = _sb_unstack(carry[1]) * scale

    qspec = pl.BlockSpec((LANES, LANES), lambda hp, qi: (qi, hp))
    strip = pl.BlockSpec((m, LANES), lambda hp, qi: (0, hp))
    saved = pl.BlockSpec((None, None, nq, 2 * LANES, LANES), lambda hp, qi: (hp, qi, 0, 0, 0))
    big = jax.ShapeDtypeStruct((m, wd), F32)
    return pl.pallas_call(
        body, name="sb_attention_bwd", grid=(npair, nq),
        in_specs=[qspec, strip, pl.BlockSpec((m, LANES), lambda hp, qi: (0, npair + hp)), saved, saved, qspec],
        out_specs=[qspec, strip, strip],
        out_shape=[big, big, big],
        compiler_params=_params("parallel", "arbitrary"),
    )(q, kv, kv, w_all, b_all, do)


def _mesh_pos():
    return lax.axis_index("x"), lax.axis_index("y"), lax.axis_index("c")


def _exchange(srcs, name, scatter):
    n = len(srcs)
    rows = [s.shape[-2] for s in srcs]
    offs = [sum(rows[:i]) for i in range(n)]
    total = sum(rows)

    def body(*refs):
        src_refs, out_ref = refs[:n], refs[n]
        send_sems, recv_sems, local_sems = refs[n + 1:]
        x, y, c = _mesh_pos()
        me = 4 * x + 2 * y + c

        def piece(i, d):
            return src_refs[i].at[d] if scatter else src_refs[i]

        def window(slot, i):
            return out_ref.at[slot, pl.ds(offs[i], rows[i])]

        local = [pltpu.make_async_copy(piece(i, me), window(me, i), local_sems.at[i]) for i in range(n)]
        for cp in local:
            cp.start()
        peers = []
        for k in range(1, N_DEV):
            px = 1 - x if k & 4 else x
            py = 1 - y if k & 2 else y
            pc = 1 - c if k & 1 else c
            pid = 4 * px + 2 * py + pc
            for i in range(n):
                pltpu.make_async_remote_copy(
                    src_ref=piece(i, pid), dst_ref=window(me, i),
                    send_sem=send_sems.at[k - 1], recv_sem=recv_sems.at[k - 1],
                    device_id=(px, py, pc), device_id_type=pl.DeviceIdType.MESH).start()
            peers.append(pid)
        slabs = [pltpu.make_async_remote_copy(
            src_ref=out_ref.at[me], dst_ref=out_ref.at[peers[k - 1]],
            send_sem=send_sems.at[k - 1], recv_sem=recv_sems.at[k - 1],
            device_id=(x, y, c), device_id_type=pl.DeviceIdType.MESH) for k in range(1, N_DEV)]
        for cp in slabs:
            cp.wait_recv()
        for cp in slabs:
            cp.wait_send()
        for cp in local:
            cp.wait()

    return pl.pallas_call(
        body, name=name,
        in_specs=[pl.BlockSpec(memory_space=pl.ANY)] * n, out_specs=pl.BlockSpec(memory_space=pl.ANY),
        out_shape=jax.ShapeDtypeStruct((N_DEV, total, FLAT_COLS), srcs[0].dtype),
        scratch_shapes=[pltpu.SemaphoreType.DMA((N_DEV - 1,)), pltpu.SemaphoreType.DMA((N_DEV - 1,)),
                        pltpu.SemaphoreType.DMA((n,))],
        compiler_params=pltpu.CompilerParams(has_side_effects=True),
    )(*srcs)


def _adamw(parts, w, mom, var, name):
    rows = w.shape[0]
    br = FLAT_ROW_BLOCK
    c1 = 1.0 - ADAM_B1 ** ADAM_STEP
    c2 = 1.0 - ADAM_B2 ** ADAM_STEP

    def body(p_ref, w_ref, m_ref, v_ref, g_ref, d_ref, nm_ref, nv_ref):
        g = p_ref[0].astype(F32)
        for s in range(1, N_DEV):
            g = g + p_ref[s].astype(F32)
        m_new = ADAM_B1 * m_ref[...] + (1.0 - ADAM_B1) * g
        v_new = ADAM_B2 * v_ref[...] + (1.0 - ADAM_B2) * (g * g)
        m_hat = m_new / c1
        v_hat = v_new / c2
        g_ref[...] = g
        d_ref[...] = -ADAM_LR * (m_hat / (jnp.sqrt(v_hat) + ADAM_EPS) + ADAM_WD * w_ref[...])
        nm_ref[...] = m_new
        nv_ref[...] = v_new

    row = pl.BlockSpec((br, FLAT_COLS), lambda i: (i, 0))
    flat = jax.ShapeDtypeStruct((rows, FLAT_COLS), F32)
    return pl.pallas_call(
        body, name=name, grid=(rows // br,),
        in_specs=[pl.BlockSpec((N_DEV, br, FLAT_COLS), lambda i: (0, i, 0)), row, row, row],
        out_specs=[row] * 4, out_shape=[flat] * 4,
        compiler_params=_params("parallel"),
    )(parts, w, mom, var)


SMALL_SHARDED = (("meta_tokens", 1), ("gdn_norm_g", 1), ("gdn_conv_w", 2))
BIG_PIECES = (("gdn_w_in", 0, True), ("gdn_w_out", 0, False), ("w_kv", None, True), ("sb_w_q", 0, False),
              ("sb_w_o", 0, False), ("ffn_w_gate_up", 0, True), ("ffn_w_gate_up", 1, True),
              ("ffn_w_down", 0, False), ("ffn_w_down", 1, False))
REPLICATED = ("gdn_a_log", "gdn_dt_bias", "gdn_onorm_g", "kv_norm_g", "sb_norm_g", "ffn_norm_g", "final_norm_g")
WEIGHTS = ("meta_tokens", "gdn_norm_g", "gdn_w_in", "gdn_conv_w", "gdn_a_log", "gdn_dt_bias", "gdn_onorm_g",
           "gdn_w_out", "kv_norm_g", "w_kv", "sb_norm_g", "sb_w_q", "sb_w_o", "ffn_norm_g", "ffn_w_gate_up",
           "ffn_w_down", "final_norm_g")
F32_ROWS = 8
BF16_ROWS = 16


def _n_rows(shape, mult):
    return -(-math.prod(shape) // (mult * FLAT_COLS)) * mult


def _as_rows(a, nl, mult):
    lead = a.shape[:nl]
    size = math.prod(a.shape[nl:])
    nr = _n_rows(a.shape[nl:], mult)
    flat = a.reshape(lead + (size,))
    if nr * FLAT_COLS != size:
        flat = jnp.pad(flat, [(0, 0)] * nl + [(0, nr * FLAT_COLS - size)])
    return flat.reshape(lead + (nr, FLAT_COLS))


def _from_rows(flat, r0, shape, nl, mult):
    lead = flat.shape[:nl]
    nr = _n_rows(shape, mult)
    seg = lax.slice_in_dim(flat, r0, r0 + nr, axis=nl).reshape(lead + (nr * FLAT_COLS,))
    return lax.slice_in_dim(seg, 0, math.prod(shape), axis=nl).reshape(lead + tuple(shape)), r0 + nr


def _to_piece(shard, layer, transposed):
    mat = shard if layer is None else shard[layer]
    return mat.T if transposed else mat


def _from_pieces(pieces, name):
    mats = [p.T if tr else p for (n, layer, tr), p in zip(BIG_PIECES, pieces) if n == name]
    layers = [layer for n, layer, _ in BIG_PIECES if n == name]
    return mats[0] if layers[0] is None else jnp.stack(mats)


def _merge_shards(g, axis):
    t = jnp.moveaxis(g, 0, axis)
    shp = t.shape
    return t.reshape(shp[:axis] + (shp[axis] * shp[axis + 1],) + shp[axis + 2:])


def _split_shards(full, axis):
    shp = full.shape
    t = full.reshape(shp[:axis] + (N_DEV, shp[axis] // N_DEV) + shp[axis + 1:])
    return jnp.moveaxis(t, axis, 0)


def _pad_lanes(a, width=LANES):
    return jnp.pad(a, ((0, 0), (0, width - a.shape[1])))


def _local_step(x, target, w, mats):
    d = x.shape[1]
    gh = w["gdn_a_log"].shape[1]
    gw = gh * GDN_HEAD_DIM
    w_int = mats["gdn_w_in", 0]
    w_qkvt, w_gatet = w_int[:3 * gw], w_int[3 * gw:4 * gw]
    w_abt = jnp.pad(w_int[4 * gw:], ((0, LANES - 2 * gh), (0, 0)))
    conv_w = w["gdn_conv_w"][0]
    a_log, dt_bias = _pad_lanes(w["gdn_a_log"]), _pad_lanes(w["gdn_dt_bias"])
    w_out = mats["gdn_w_out", 0]
    w_kvt = mats["w_kv", None]
    sbw = w_kvt.shape[0] // 2
    w_kt, w_vt = w_kvt[:sbw], w_kvt[sbw:]
    w_q, w_o = mats["sb_w_q", 0], mats["sb_w_o", 0]
    w_gut = [mats["ffn_w_gate_up", l] for l in range(2)]
    w_dn = [mats["ffn_w_down", l] for l in range(2)]
    ffn_g = [w["ffn_norm_g"][l:l + 1] for l in range(2)]
    kv_g, fin_g = w["kv_norm_g"][None], w["final_norm_g"][None]

    h0 = jnp.concatenate([jnp.zeros((PAD_ROWS, d), F32), w["meta_tokens"], x], axis=0)

    (n1,) = _rmsnorm(h0, [w["gdn_norm_g"]], "norm_gdn")
    proj = _matmul(n1, w_qkvt, "nt", "mm_gdn_qkv")
    gate = _matmul(n1, w_gatet, "nt", "mm_gdn_gate")
    ab = _matmul(n1, w_abt, "nt", "mm_gdn_ab")
    qkv = _gdn_conv(proj, conv_w, gh)
    g_b, beta_b = _gdn_gates(ab, a_log, dt_bias, gh)
    ub, wc, qd, kd, gc, pmat, tinv = _gdn_prep(qkv, g_b, beta_b, gh)
    o_gdn, ssave = _gdn_scan(ub, wc, qd, kd, gc, pmat, gh)
    o2 = _gdn_outnorm(o_gdn, gate, w["gdn_onorm_g"], gh)
    h1 = _matmul(o2, w_out, "nn", "mm_gdn_out", add=h0)

    def ffn_fwd(h, l):
        (n,) = _rmsnorm(h, [ffn_g[l]], f"norm_ffn{l}")
        gu = _matmul(n, w_gut[l], "nt", f"mm_ffn{l}_gu", out_dtype=BF16)
        act = _swiglu(gu, f"swiglu{l}")
        return n, gu, act, _matmul(act, w_dn[l], "nn", f"mm_ffn{l}_down", add=h)

    n2, gu0, act0, h2 = ffn_fwd(h1, 0)

    nkv, n3 = _rmsnorm(h2, [kv_g, w["sb_norm_g"]], "norm_kv_sb")
    kv = _matmul(nkv, w_kvt, "nt", "mm_kv", out_dtype=BF16)
    q_sb = _matmul(n3, w_q, "nn", "mm_sb_q")
    o_sb, w_sb, beta_sb = _sb_attention(q_sb, kv)
    h3 = _matmul(o_sb, w_o, "nn", "mm_sb_o", add=h2)
    n4, gu1, act1, h4 = ffn_fwd(h3, 1)

    loss, dh4, d_fin = _loss_head(h4, fin_g, target)

    def ffn_bwd(dh, h, n, gu, act, l):
        d_act = _matmul(dh, w_dn[l], "nt", f"mm_ffn{l}_dact")
        dw_dn = _matmul(act, dh, "tn", f"mm_ffn{l}_dwdown")
        dgu = _swiglu_bwd(gu, d_act, f"swiglu_bwd{l}")
        dw_gu = _matmul(dgu, n, "tn", f"mm_ffn{l}_dwgu")
        dn = _matmul(dgu, w_gut[l], "nn", f"mm_ffn{l}_dn")
        dh_in, (dg,) = _rmsnorm_bwd(h, [ffn_g[l]], [dn], dh, f"norm_ffn{l}_bwd")
        return dh_in, dw_gu, dw_dn, dg

    dh3, dw_gu1, dw_dn1, dg_ffn1 = ffn_bwd(dh4, h3, n4, gu1, act1, 1)

    do_sb = _matmul(dh3, w_o, "nt", "mm_sb_do")
    dw_o = _matmul(o_sb, dh3, "tn", "mm_sb_dwo")
    dq_sb, dk_sb, dv_sb = _sb_attention_bwd(q_sb, kv, w_sb, beta_sb, do_sb)
    dw_q = _matmul(n3, dq_sb, "tn", "mm_sb_dwq")
    dn3 = _matmul(dq_sb, w_q, "nt", "mm_sb_dn")
    dw_k = _matmul(dk_sb, nkv, "tn", "mm_dwk")
    dw_v = _matmul(dv_sb, nkv, "tn", "mm_dwv")
    dnkv = _matmul(dk_sb, w_kt, "nn", "mm_dnk")
    dnkv = _matmul(dv_sb, w_vt, "nn", "mm_dnv", add=dnkv)
    dh2, (dg_kv, dg_sb) = _rmsnorm_bwd(h2, [kv_g, w["sb_norm_g"]], [dnkv, dn3], dh3, "norm_kv_sb_bwd")

    dh1, dw_gu0, dw_dn0, dg_ffn0 = ffn_bwd(dh2, h1, n2, gu0, act0, 0)

    do2 = _matmul(dh1, w_out, "nt", "mm_gdn_do")
    dw_out = _matmul(o2, dh1, "tn", "mm_gdn_dwout")
    do_gdn, dgate, d_onorm = _gdn_outnorm_bwd(o_gdn, gate, w["gdn_onorm_g"], do2, gh)
    dub, dwc, dqd, dkd, dpm, dgl = _gdn_scan_bwd(do_gdn, ssave, ub, wc, qd, kd, gc, pmat, gh)
    dqkv, dg_b, dbeta_b = _gdn_prep_bwd(qkv, g_b, beta_b, ub, wc, tinv, dub, dwc, dqd, dkd, dpm, dgl, gh)
    dab, d_alog, d_dtb = _gdn_gates_bwd(ab, a_log, dt_bias, dg_b, dbeta_b, gh)
    dproj, d_conv = _gdn_conv_bwd(proj, conv_w, dqkv, gh)
    dw_qkv = _matmul(dproj, n1, "tn", "mm_gdn_dwqkv")
    dw_gate = _matmul(dgate, n1, "tn", "mm_gdn_dwgate")
    dw_ab = _matmul(dab, n1, "tn", "mm_gdn_dwab")
    dn1 = _matmul(dproj, w_qkvt, "nn", "mm_gdn_dn_qkv")
    dn1 = _matmul(dgate, w_gatet, "nn", "mm_gdn_dn_gate", add=dn1)
    dn1 = _matmul(dab, w_abt, "nn", "mm_gdn_dn_ab", add=dn1)
    dh0, (dg_gdn,) = _rmsnorm_bwd(h0, [w["gdn_norm_g"]], [dn1], dh1, "norm_gdn_bwd")

    grads = {
        "meta_tokens": dh0[PAD_ROWS:LANES],
        "gdn_norm_g": dg_gdn,
        "gdn_conv_w": d_conv[None],
        "gdn_a_log": d_alog[:, :gh],
        "gdn_dt_bias": d_dtb[:, :gh],
        "gdn_onorm_g": d_onorm,
        "kv_norm_g": dg_kv[0],
        "sb_norm_g": dg_sb,
        "ffn_norm_g": jnp.concatenate([dg_ffn0, dg_ffn1], axis=0),
        "final_norm_g": d_fin[0],
    }
    gmats = {
        ("gdn_w_in", 0): jnp.concatenate([dw_qkv, dw_gate, dw_ab[:2 * gh]], axis=0),
        ("gdn_w_out", 0): dw_out,
        ("w_kv", None): jnp.concatenate([dw_k, dw_v], axis=0),
        ("sb_w_q", 0): dw_q,
        ("sb_w_o", 0): dw_o,
        ("ffn_w_gate_up", 0): dw_gu0,
        ("ffn_w_gate_up", 1): dw_gu1,
        ("ffn_w_down", 0): dw_dn0,
        ("ffn_w_down", 1): dw_dn1,
    }
    return loss, dh0[LANES:], grads, gmats


def kernel(x, meta_tokens, gdn_norm_g, gdn_w_in, gdn_conv_w, gdn_a_log, gdn_dt_bias, gdn_onorm_g, gdn_w_out, kv_norm_g, w_kv, sb_norm_g, sb_w_q, sb_w_o, ffn_norm_g, ffn_w_gate_up, ffn_w_down, final_norm_g, loss_target, m_meta_tokens, m_gdn_norm_g, m_gdn_w_in, m_gdn_conv_w, m_gdn_a_log, m_gdn_dt_bias, m_gdn_onorm_g, m_gdn_w_out, m_kv_norm_g, m_w_kv, m_sb_norm_g, m_sb_w_q, m_sb_w_o, m_ffn_norm_g, m_ffn_w_gate_up, m_ffn_w_down, m_final_norm_g, v_meta_tokens, v_gdn_norm_g, v_gdn_w_in, v_gdn_conv_w, v_gdn_a_log, v_gdn_dt_bias, v_gdn_onorm_g, v_gdn_w_out, v_kv_norm_g, v_w_kv, v_sb_norm_g, v_sb_w_q, v_sb_w_o, v_ffn_norm_g, v_ffn_w_gate_up, v_ffn_w_down, v_final_norm_g):
    given = dict(locals())
    small = [n for n, _ in SMALL_SHARDED]
    big_names = list(dict.fromkeys(n for n, _, _ in BIG_PIECES))

    def local_pieces(prefix):
        return [_to_piece(given[prefix + n], layer, tr) for n, layer, tr in BIG_PIECES]

    piece_shapes = [p.shape for p in local_pieces("")]

    g_small = _exchange([jnp.concatenate([_as_rows(given[n], 0, F32_ROWS) for n in small], axis=0)],
                        "all_gather_small", scatter=False)
    g_big = _exchange([_as_rows(p.astype(BF16), 0, BF16_ROWS) for p in local_pieces("")],
                      "all_gather_weights", scatter=False)
    full = {n: given[n] for n in REPLICATED}
    r = 0
    for n, ax in SMALL_SHARDED:
        shards, r = _from_rows(g_small, r, given[n].shape, 1, F32_ROWS)
        full[n] = _merge_shards(shards, ax)
    mats, r = {}, 0
    for (n, layer, _), shp in zip(BIG_PIECES, piece_shapes):
        rows8, r = _from_rows(g_big, r, shp, 1, BF16_ROWS)
        mats[n, layer] = rows8.reshape((N_DEV * shp[0],) + shp[1:])

    loss_b, grad_x, grads, gmats = _local_step(x[0], loss_target[0], full, mats)

    order = small + list(REPLICATED)
    fill_small = (-sum(_n_rows(given[n].shape, F32_ROWS) for n in order)) % FLAT_ROW_BLOCK
    fill_big = (-sum(_n_rows(shp, BF16_ROWS) for shp in piece_shapes)) % FLAT_ROW_BLOCK
    small_send = [_as_rows(_split_shards(grads[n], ax), 1, F32_ROWS) for n, ax in SMALL_SHARDED]
    small_send += [_as_rows(jnp.broadcast_to(grads[n][None], (N_DEV,) + grads[n].shape), 1, F32_ROWS)
                   for n in REPLICATED]
    if fill_small:
        small_send.append(jnp.zeros((N_DEV, fill_small, FLAT_COLS), F32))
    recv_small = _exchange([jnp.concatenate(small_send, axis=1)], "scatter_small_gradients", scatter=True)
    big_send = [_as_rows(gmats[n, layer].reshape((N_DEV,) + shp), 1, BF16_ROWS)
                for (n, layer, _), shp in zip(BIG_PIECES, piece_shapes)]
    if fill_big:
        big_send.append(jnp.zeros((N_DEV, fill_big, FLAT_COLS), BF16))
    recv_big = _exchange(big_send, "scatter_gradients", scatter=True)

    def flat_small(prefix):
        segs = [_as_rows(given[prefix + n], 0, F32_ROWS) for n in order]
        return jnp.concatenate(segs + ([jnp.zeros((fill_small, FLAT_COLS), F32)] if fill_small else []), axis=0)

    def flat_big(prefix):
        segs = [_as_rows(p, 0, BF16_ROWS) for p in local_pieces(prefix)]
        return jnp.concatenate(segs + ([jnp.zeros((fill_big, FLAT_COLS), F32)] if fill_big else []), axis=0)

    upd_small = _adamw(recv_small, flat_small(""), flat_small("m_"), flat_small("v_"), "adamw_small")
    upd_big = _adamw(recv_big, flat_big(""), flat_big("m_"), flat_big("v_"), "adamw")
    outs = {}
    for tag, f_small, f_big in zip(("grad", "delta", "new_m", "new_v"), upd_small, upd_big):
        outs[tag], r = {}, 0
        for n in order:
            outs[tag][n], r = _from_rows(f_small, r, given[n].shape, 0, F32_ROWS)
        pieces, r = [], 0
        for shp in piece_shapes:
            p, r = _from_rows(f_big, r, shp, 0, BF16_ROWS)
            pieces.append(p)
        for n in big_names:
            outs[tag][n] = _from_pieces(pieces, n)

    loss = lax.psum(loss_b[0, 0], ("x", "y", "c"))
    result = [loss, grad_x[None]]
    for tag in ("grad", "delta", "new_m", "new_v"):
        result += [outs[tag][n] for n in WEIGHTS]
    return tuple(result)
```

```python
import functools
import math

import jax
import jax.numpy as jnp
from jax import lax
from jax.experimental import pallas as pl
from jax.experimental.pallas import tpu as pltpu

F32 = jnp.float32
BF16 = jnp.bfloat16

N_DEV = 8
LANES = 128
N_META = 16
PAD_ROWS = LANES - N_META
GDN_CHUNK = 64
GDN_HEAD_DIM = 128
SB_HEAD_DIM = 64
SB_FAR = 11
SB_BWD_SUM_PARTS = 1
CONV_WIDTH = 4
NORM_EPS = 1e-6
FLAT_COLS = 1024
FLAT_ROW_BLOCK = 64
VMEM_LIMIT = 56 * 1024 * 1024

ADAM_LR = 0.001
ADAM_B1 = 0.9
ADAM_B2 = 0.999
ADAM_EPS = 1e-08
ADAM_WD = 0.01
ADAM_STEP = 10

NN = (((1,), (0,)), ((), ()))
NT = (((1,), (1,)), ((), ()))
TN = (((0,), (0,)), ((), ()))


def _params(*sem):
    return pltpu.CompilerParams(dimension_semantics=sem, vmem_limit_bytes=VMEM_LIMIT)


def _pick(n, cands):
    for c in cands:
        if n % c == 0:
            return c
    return n


def _bdot(a, b, dims=NN):
    return lax.dot_general(a.astype(BF16), b.astype(BF16), dims, preferred_element_type=F32)


def _hdot(a, b, dims=NN):
    return lax.dot_general(a, b, dims, preferred_element_type=F32, precision=lax.Precision.HIGH)


def _split_dot(a, m, parts=2):
    out = None
    for _ in range(parts):
        piece = a.astype(BF16)
        a = a - piece.astype(F32)
        term = lax.dot_general(piece, m, NN, preferred_element_type=F32)
        out = term if out is None else out + term
    return out


def _lockstep(gens):
    while gens:
        alive = []
        for g in gens:
            try:
                next(g)
                alive.append(g)
            except StopIteration:
                pass
        gens = alive


def _sigmoid(x):
    return 1.0 / (1.0 + jnp.exp(-x))


def _iota2(shape, axis):
    return lax.broadcasted_iota(jnp.int32, shape, axis)


def _matmul(a, b, mode, name, out_dtype=None, add=None):
    if out_dtype is None:
        out_dtype = BF16 if mode == "tn" else F32
    if mode == "nn":
        (m, k), n = a.shape, b.shape[1]
    elif mode == "nt":
        (m, k), n = a.shape, b.shape[0]
    else:
        (k, m), n = a.shape, b.shape[1]
    row_c = (640, 512, 384, 256, 128)
    col_c = (1024, 1408, 768, 512, 384, 256, 128)
    if mode == "tn":
        bm, bn, bk = _pick(m, col_c), _pick(n, col_c), _pick(k, row_c)
    else:
        bm, bn, bk = _pick(m, row_c), _pick(n, col_c), _pick(k, (1024, 1408, 768, 512, 256, 128))
    nk = k // bk
    ni, nj = m // bm, n // bn
    dims = {"nn": NN, "nt": NT, "tn": TN}[mode]
    a_bytes, b_bytes = a.size * a.dtype.itemsize, b.size * b.dtype.itemsize
    rows_outer = (a_bytes * (1 if nk == 1 else nj) + b_bytes * ni
                  <= b_bytes * (1 if nk == 1 else ni) + a_bytes * nj)

    def at(f):
        return (lambda g0, g1, q: f(g0, g1, q)) if rows_outer else (lambda g0, g1, q: f(g1, g0, q))

    a_spec = {"nn": pl.BlockSpec((bm, bk), at(lambda i, j, q: (i, q))),
              "nt": pl.BlockSpec((bm, bk), at(lambda i, j, q: (i, q))),
              "tn": pl.BlockSpec((bk, bm), at(lambda i, j, q: (q, i)))}[mode]
    b_spec = {"nn": pl.BlockSpec((bk, bn), at(lambda i, j, q: (q, j))),
              "nt": pl.BlockSpec((bn, bk), at(lambda i, j, q: (j, q))),
              "tn": pl.BlockSpec((bk, bn), at(lambda i, j, q: (q, j)))}[mode]
    o_spec = pl.BlockSpec((bm, bn), at(lambda i, j, q: (i, j)))
    has_add = add is not None

    def body(*refs):
        if has_add:
            a_ref, b_ref, add_ref, o_ref, acc_ref = refs
        else:
            a_ref, b_ref, o_ref, acc_ref = refs
        q = pl.program_id(2)
        part = _bdot(a_ref[...], b_ref[...], dims)

        @pl.when(q == 0)
        def _():
            acc_ref[...] = part

        @pl.when(q > 0)
        def _():
            acc_ref[...] += part

        @pl.when(q == nk - 1)
        def _():
            r = acc_ref[...]
            if has_add:
                r = r + add_ref[...]
            o_ref[...] = r.astype(out_dtype)

    ins = [a, b] + ([add] if has_add else [])
    in_specs = [a_spec, b_spec] + ([o_spec] if has_add else [])
    return pl.pallas_call(
        body, name=name, grid=(ni, nj, nk) if rows_outer else (nj, ni, nk),
        in_specs=in_specs, out_specs=o_spec,
        out_shape=jax.ShapeDtypeStruct((m, n), out_dtype),
        scratch_shapes=[pltpu.VMEM((bm, bn), F32)],
        compiler_params=_params("parallel", "parallel", "arbitrary"),
    )(*ins)


def _rmsnorm(h, gains, name):
    m, d = h.shape
    bm = _pick(m, (640, 384, 128))
    ng = len(gains)

    def body(*refs):
        h_ref, g_refs, o_refs = refs[0], refs[1:1 + ng], refs[1 + ng:]
        x = h_ref[...]
        xhat = x * lax.rsqrt(jnp.mean(x * x, axis=-1, keepdims=True) + NORM_EPS)
        for g_ref, o_ref in zip(g_refs, o_refs):
            o_ref[...] = (xhat * g_ref[...]).astype(BF16)

    row = pl.BlockSpec((bm, d), lambda i: (i, 0))
    gain = pl.BlockSpec((1, d), lambda i: (0, 0))
    return pl.pallas_call(
        body, name=name, grid=(m // bm,),
        in_specs=[row] + [gain] * ng, out_specs=[row] * ng,
        out_shape=[jax.ShapeDtypeStruct((m, d), BF16)] * ng,
        compiler_params=_params("parallel"),
    )(h, *gains)


def _rmsnorm_bwd(h, gains, dns, dres, name):
    m, d = h.shape
    bm = _pick(m, (640, 384, 128))
    ng = len(gains)
    nsteps = m // bm

    def body(*refs):
        h_ref, dres_ref = refs[0], refs[1]
        g_refs = refs[2:2 + ng]
        dn_refs = refs[2 + ng:2 + 2 * ng]
        dh_ref = refs[2 + 2 * ng]
        dg_refs = refs[3 + 2 * ng:3 + 3 * ng]
        acc_refs = refs[3 + 3 * ng:]
        i = pl.program_id(0)
        x = h_ref[...]
        r = lax.rsqrt(jnp.mean(x * x, axis=-1, keepdims=True) + NORM_EPS)
        xhat = x * r
        dh = dres_ref[...]
        for g_ref, dn_ref, dg_ref, acc_ref in zip(g_refs, dn_refs, dg_refs, acc_refs):
            dn = dn_ref[...]
            dy = dn * g_ref[...]
            dh = dh + r * (dy - xhat * jnp.mean(dy * xhat, axis=-1, keepdims=True))
            part = jnp.sum((dn * xhat).reshape(bm // 8, 8, d), axis=0)

            @pl.when(i == 0)
            def _():
                acc_ref[...] = part

            @pl.when(i > 0)
            def _():
                acc_ref[...] += part

            @pl.when(i == nsteps - 1)
            def _():
                dg_ref[...] = jnp.sum(acc_ref[...], axis=0, keepdims=True)

        dh_ref[...] = dh

    row = pl.BlockSpec((bm, d), lambda i: (i, 0))
    gain = pl.BlockSpec((1, d), lambda i: (0, 0))
    outs = pl.pallas_call(
        body, name=name, grid=(nsteps,),
        in_specs=[row, row] + [gain] * ng + [row] * ng,
        out_specs=[row] + [gain] * ng,
        out_shape=[jax.ShapeDtypeStruct((m, d), F32)] + [jax.ShapeDtypeStruct((1, d), F32)] * ng,
        scratch_shapes=[pltpu.VMEM((8, d), F32)] * ng,
        compiler_params=_params("arbitrary"),
    )(h, dres, *gains, *dns)
    return outs[0], list(outs[1:])


def _loss_head(h, gain, target):
    m, d = h.shape
    nsteps = m // LANES

    def body(h_ref, g_ref, t_ref, loss_ref, dh_ref, dg_ref, lacc, gacc):
        i = pl.program_id(0)

        @pl.when(i == 0)
        def _():
            lacc[...] = jnp.zeros_like(lacc)
            gacc[...] = jnp.zeros_like(gacc)
            dh_ref[...] = jnp.zeros_like(dh_ref)

        @pl.when(i > 0)
        def _():
            x = h_ref[...]
            r = lax.rsqrt(jnp.mean(x * x, axis=-1, keepdims=True) + NORM_EPS)
            xhat = x * r
            g = g_ref[...]
            err = xhat * g - t_ref[...]
            lacc[...] += jnp.sum((err * err).reshape(LANES // 8, 8, d), axis=0)
            dn = err * (1.0 / d)
            dy = dn * g
            dh_ref[...] = r * (dy - xhat * jnp.mean(dy * xhat, axis=-1, keepdims=True))
            gacc[...] += jnp.sum((dn * xhat).reshape(LANES // 8, 8, d), axis=0)

        @pl.when(i == nsteps - 1)
        def _():
            tot = jnp.sum(jnp.sum(lacc[...], axis=1, keepdims=True), axis=0, keepdims=True)
            loss_ref[...] = jnp.broadcast_to(tot * (0.5 / d), (1, LANES))
            dg_ref[...] = jnp.sum(gacc[...], axis=0, keepdims=True)

    row = pl.BlockSpec((LANES, d), lambda i: (i, 0))
    trow = pl.BlockSpec((LANES, d), lambda i: (jnp.maximum(i - 1, 0), 0))
    gain_spec = pl.BlockSpec((1, d), lambda i: (0, 0))
    return pl.pallas_call(
        body, name="loss_head", grid=(nsteps,),
        in_specs=[row, gain_spec, trow],
        out_specs=[pl.BlockSpec((1, LANES), lambda i: (0, 0)), row, gain_spec],
        out_shape=[jax.ShapeDtypeStruct((1, LANES), F32), jax.ShapeDtypeStruct((m, d), F32),
                   jax.ShapeDtypeStruct((1, d), F32)],
        scratch_shapes=[pltpu.VMEM((8, d), F32), pltpu.VMEM((8, d), F32)],
        compiler_params=_params("arbitrary"),
    )(h, gain, target)


def _swiglu(gu, name):
    m, f2 = gu.shape
    f = f2 // 2

    def body(gu_ref, o_ref):
        g = gu_ref[:, :f].astype(F32)
        u = gu_ref[:, f:].astype(F32)
        o_ref[...] = (g * _sigmoid(g) * u).astype(BF16)

    return pl.pallas_call(
        body, name=name, grid=(m // LANES,),
        in_specs=[pl.BlockSpec((LANES, f2), lambda i: (i, 0))],
        out_specs=pl.BlockSpec((LANES, f), lambda i: (i, 0)),
        out_shape=jax.ShapeDtypeStruct((m, f), BF16),
        compiler_params=_params("parallel"),
    )(gu)


def _swiglu_bwd(gu, da, name):
    m, f2 = gu.shape
    f = f2 // 2

    def body(gu_ref, da_ref, o_ref):
        g = gu_ref[:, :f].astype(F32)
        u = gu_ref[:, f:].astype(F32)
        da_ = da_ref[...]
        s = _sigmoid(g)
        o_ref[:, :f] = (da_ * u * s * (1.0 + g * (1.0 - s))).astype(BF16)
        o_ref[:, f:] = (da_ * g * s).astype(BF16)

    return pl.pallas_call(
        body, name=name, grid=(m // LANES,),
        in_specs=[pl.BlockSpec((LANES, f2), lambda i: (i, 0)), pl.BlockSpec((LANES, f), lambda i: (i, 0))],
        out_specs=pl.BlockSpec((LANES, f2), lambda i: (i, 0)),
        out_shape=jax.ShapeDtypeStruct((m, f2), BF16),
        compiler_params=_params("parallel"),
    )(gu, da)


def _conv_taps(xa, w_ref):
    acc = xa * w_ref[CONV_WIDTH - 1:CONV_WIDTH, :]
    for i in range(CONV_WIDTH - 1):
        acc = acc + pltpu.roll(xa, CONV_WIDTH - 1 - i, 0) * w_ref[i:i + 1, :]
    return acc[8:]


def _gdn_conv(proj, conv_w, n_heads):
    m, w3 = proj.shape
    wd = w3 // 3
    nblk = m // LANES
    unroll = _pick(nblk - 1, (4, 2, 1))

    def body(x_ref, w_ref, o_ref):
        c = pl.program_id(0)
        is_qk = c < 2 * n_heads

        def block(b, xa=None):
            r0 = pl.multiple_of(b * LANES, LANES)
            if xa is None:
                xa = x_ref[pl.ds(r0 - 8, LANES + 8), :]
            cv = _conv_taps(xa, w_ref)
            s = cv * _sigmoid(cv)
            ss = jnp.sum(s * s, axis=-1, keepdims=True)
            yield
            o_ref[pl.ds(r0, LANES), :] = jnp.where(is_qk, s * lax.rsqrt(ss + NORM_EPS), s)

        x0 = jnp.concatenate([jnp.zeros((8, LANES), F32), x_ref[pl.ds(0, LANES), :]], axis=0)
        _lockstep([block(0, x0)])

        def step(j, carry):
            _lockstep([block(1 + unroll * j + u) for u in range(unroll)])
            return carry

        lax.fori_loop(0, (nblk - 1) // unroll, step, 0)

    return pl.pallas_call(
        body, name="gdn_conv", grid=(w3 // LANES,),
        in_specs=[pl.BlockSpec((m, LANES), lambda c: (0, c)), pl.BlockSpec((CONV_WIDTH, LANES), lambda c: (0, c))],
        out_specs=pl.BlockSpec((None, m, LANES), lambda c: (c // n_heads, 0, c % n_heads)),
        out_shape=jax.ShapeDtypeStruct((3, m, wd), F32),
        compiler_params=_params("parallel"),
    )(proj, conv_w)


def _gdn_conv_bwd(proj, conv_w, dact, n_heads):
    m, w3 = proj.shape
    nblk = m // LANES
    unroll = _pick(nblk - 1, (4, 2, 1))

    def body(x_ref, w_ref, dy_ref, dx_ref, dw_ref, dc_ref):
        c = pl.program_id(0)
        is_qk = c < 2 * n_heads
        dc_ref[pl.ds(m, 8), :] = jnp.zeros((8, LANES), F32)

        def block(b, out, xa=None):
            r0 = pl.multiple_of(b * LANES, LANES)
            if xa is None:
                xa = x_ref[pl.ds(r0 - 8, LANES + 8), :]
            cv = _conv_taps(xa, w_ref)
            sg = _sigmoid(cv)
            s = cv * sg
            dy = dy_ref[pl.ds(r0, LANES), :]
            ss = jnp.sum(s * s, axis=-1, keepdims=True)
            yield
            rn = lax.rsqrt(ss + NORM_EPS)
            yn = s * rn
            proj_len = jnp.sum(dy * yn, axis=-1, keepdims=True)
            yield
            ds = jnp.where(is_qk, rn * (dy - yn * proj_len), dy)
            dcv = ds * sg * (1.0 + cv * (1.0 - sg))
            dc_ref[pl.ds(r0, LANES), :] = dcv
            parts = []
            for i in range(CONV_WIDTH):
                xs = xa[8:] if i == CONV_WIDTH - 1 else pltpu.roll(xa, CONV_WIDTH - 1 - i, 0)[8:]
                parts.append(jnp.sum((dcv * xs).reshape(LANES // 8, 8, LANES), axis=0))
            out.append(parts)

        def add_parts(acc, outs):
            for parts in outs:
                acc = tuple(a + p for a, p in zip(acc, parts))
            return acc

        x0 = jnp.concatenate([jnp.zeros((8, LANES), F32), x_ref[pl.ds(0, LANES), :]], axis=0)
        first = []
        _lockstep([block(0, first, x0)])
        dwacc = tuple(first[0])

        def step1(j, acc):
            outs = []
            _lockstep([block(1 + unroll * j + u, outs) for u in range(unroll)])
            return add_parts(acc, outs)

        dwacc = lax.fori_loop(0, (nblk - 1) // unroll, step1, dwacc)
        for i in range(CONV_WIDTH):
            dw_ref[i:i + 1, :] = jnp.sum(dwacc[i], axis=0, keepdims=True)

        def step2(b, carry):
            r0 = pl.multiple_of(b * LANES, LANES)
            da = dc_ref[pl.ds(r0, LANES + 8), :]
            acc = da * w_ref[CONV_WIDTH - 1:CONV_WIDTH, :]
            for i in range(CONV_WIDTH - 1):
                sh = CONV_WIDTH - 1 - i
                acc = acc + pltpu.roll(da, LANES + 8 - sh, 0) * w_ref[i:i + 1, :]
            dx_ref[pl.ds(r0, LANES), :] = acc[:LANES]
            return carry

        lax.fori_loop(0, nblk, step2, 0)

    return pl.pallas_call(
        body, name="gdn_conv_bwd", grid=(w3 // LANES,),
        in_specs=[pl.BlockSpec((m, LANES), lambda c: (0, c)),
                  pl.BlockSpec((CONV_WIDTH, LANES), lambda c: (0, c)),
                  pl.BlockSpec((None, m, LANES), lambda c: (c // n_heads, 0, c % n_heads))],
        out_specs=[pl.BlockSpec((m, LANES), lambda c: (0, c)), pl.BlockSpec((CONV_WIDTH, LANES), lambda c: (0, c))],
        out_shape=[jax.ShapeDtypeStruct((m, w3), F32), jax.ShapeDtypeStruct((CONV_WIDTH, w3), F32)],
        scratch_shapes=[pltpu.VMEM((m + 8, LANES), F32)],
        compiler_params=_params("parallel"),
    )(proj, conv_w, dact)


def _softplus(x):
    return jnp.maximum(x, 0.0) + jnp.log(1.0 + jnp.exp(-jnp.abs(x)))


def _gdn_gates(ab, a_log, dt_bias, n_heads):
    m = ab.shape[0]
    bm = _pick(m, (640, 384, 128))

    def body(ab_ref, al_ref, dt_ref, g_ref, b_ref):
        i = pl.program_id(0)
        x = ab_ref[...]
        live = (_iota2((bm, LANES), 0) + i * bm) >= PAD_ROWS
        g = jnp.where(live, -jnp.exp(al_ref[...]) * _softplus(x + dt_ref[...]), 0.0)
        beta = jnp.where(live, _sigmoid(x), 0.0)
        for h in range(n_heads):
            g_ref[:, h * LANES:(h + 1) * LANES] = jnp.broadcast_to(g[:, h:h + 1], (bm, LANES))
            b_ref[:, h * LANES:(h + 1) * LANES] = jnp.broadcast_to(
                beta[:, n_heads + h:n_heads + h + 1], (bm, LANES))

    wide = pl.BlockSpec((bm, n_heads * LANES), lambda i: (i, 0))
    return pl.pallas_call(
        body, name="gdn_gates", grid=(m // bm,),
        in_specs=[pl.BlockSpec((bm, LANES), lambda i: (i, 0))] + [pl.BlockSpec((1, LANES), lambda i: (0, 0))] * 2,
        out_specs=[wide, wide],
        out_shape=[jax.ShapeDtypeStruct((m, n_heads * LANES), F32)] * 2,
        compiler_params=_params("parallel"),
    )(ab, a_log, dt_bias)


def _gdn_gates_bwd(ab, a_log, dt_bias, dg_b, dbeta_b, n_heads):
    m = ab.shape[0]
    bm = _pick(m, (640, 384, 128))
    nsteps = m // bm

    def body(ab_ref, al_ref, dt_ref, dg_ref, db_ref, dab_ref, dal_ref, ddt_ref, acc_al, acc_dt):
        i = pl.program_id(0)
        x = ab_ref[...]
        lane = _iota2((bm, LANES), 1)
        live = (_iota2((bm, LANES), 0) + i * bm) >= PAD_ROWS
        dg = jnp.zeros((bm, LANES), F32)
        dbeta = jnp.zeros((bm, LANES), F32)
        for h in range(n_heads):
            dg = jnp.where(lane == h, dg_ref[:, h * LANES:(h + 1) * LANES], dg)
            dbeta = jnp.where(lane == n_heads + h, db_ref[:, h * LANES:(h + 1) * LANES], dbeta)
        dg = jnp.where(live, dg, 0.0)
        dbeta = jnp.where(live, dbeta, 0.0)
        nea = -jnp.exp(al_ref[...])
        pre = x + dt_ref[...]
        d_pre = dg * nea * _sigmoid(pre)
        beta = _sigmoid(x)
        dab_ref[...] = d_pre + dbeta * beta * (1.0 - beta)
        p_al = jnp.sum((dg * nea * _softplus(pre)).reshape(bm // 8, 8, LANES), axis=0)
        p_dt = jnp.sum(d_pre.reshape(bm // 8, 8, LANES), axis=0)

        @pl.when(i == 0)
        def _():
            acc_al[...] = p_al
            acc_dt[...] = p_dt

        @pl.when(i > 0)
        def _():
            acc_al[...] += p_al
            acc_dt[...] += p_dt

        @pl.when(i == nsteps - 1)
        def _():
            dal_ref[...] = jnp.sum(acc_al[...], axis=0, keepdims=True)
            ddt_ref[...] = jnp.sum(acc_dt[...], axis=0, keepdims=True)

    wide = pl.BlockSpec((bm, n_heads * LANES), lambda i: (i, 0))
    one = pl.BlockSpec((1, LANES), lambda i: (0, 0))
    nar = pl.BlockSpec((bm, LANES), lambda i: (i, 0))
    return pl.pallas_call(
        body, name="gdn_gates_bwd", grid=(nsteps,),
        in_specs=[nar, one, one, wide, wide],
        out_specs=[nar, one, one],
        out_shape=[jax.ShapeDtypeStruct((m, LANES), F32), jax.ShapeDtypeStruct((1, LANES), F32),
                   jax.ShapeDtypeStruct((1, LANES), F32)],
        scratch_shapes=[pltpu.VMEM((8, LANES), F32)] * 2,
        compiler_params=_params("arbitrary"),
    )(ab, a_log, dt_bias, dg_b, dbeta_b)


def _chunk_masks():
    c = GDN_CHUNK
    ri, ci = _iota2((c, c), 0), _iota2((c, c), 1)
    return ri >= ci, ri > ci, ri == ci


def _chunk_decay(gbs):
    c = GDN_CHUNK
    incl, _, _ = _chunk_masks()
    inclf = incl.astype(F32)
    ones = jnp.ones((c, LANES), F32)
    g_cums = [_hdot(inclf, gb) for gb in gbs]
    g_rows = [_hdot(ones, g_cum, NT) * (1.0 / LANES) for g_cum in g_cums]
    dmats = [jnp.where(incl, jnp.exp(jnp.where(incl, g_cum[:, :c] - g_row, 0.0)), 0.0)
             for g_cum, g_row in zip(g_cums, g_rows)]
    return g_cums, dmats


def _unit_lower_inverse(mats):
    _, _, eye = _chunk_masks()
    xs = [eye.astype(F32) - a for a in mats]
    ps = list(mats)
    for _ in range(int(math.log2(GDN_CHUNK)) - 1):
        ps = [_hdot(p, p) for p in ps]
        xs = [x + _hdot(x, p) for x, p in zip(xs, ps)]
    return xs


def _gdn_prep(qkv, g_b, beta_b, n_heads):
    _, m, wd = qkv.shape
    c = GDN_CHUNK
    nc = m // c
    cb = _pick(nc, (10, 5, 3, 2))
    rows = cb * c
    qscale = GDN_HEAD_DIM ** -0.5

    def body(qkv_ref, g_ref, b_ref, ub_ref, wc_ref, qd_ref, kd_ref, gc_ref, p_ref, t_ref):
        _, strict, _ = _chunk_masks()
        sls = [pl.ds(s * c, c) for s in range(cb)]
        g_cums, dmats = _chunk_decay([g_ref[sl, :] for sl in sls])
        kks = [_bdot(qkv_ref[1, sl, :], qkv_ref[1, sl, :], NT) for sl in sls]
        ts = _unit_lower_inverse([jnp.where(strict, b_ref[sl, :][:, :c] * dmat * kk, 0.0)
                                  for sl, dmat, kk in zip(sls, dmats, kks)])
        for s, (sl, g_cum, dmat, t) in enumerate(zip(sls, g_cums, dmats, ts)):
            q = qkv_ref[0, sl, :] * qscale
            k = qkv_ref[1, sl, :]
            bb = b_ref[sl, :]
            gam = jnp.exp(g_cum)
            ub_ref[sl, :] = _hdot(t, bb * qkv_ref[2, sl, :])
            wc_ref[sl, :] = _hdot(t, bb * gam * k)
            p_ref[s] = _bdot(q, k, NT) * dmat
            qd_ref[sl, :] = q * gam
            kd_ref[sl, :] = k * jnp.exp(g_cum[c - 1:c, :] - g_cum)
            gc_ref[sl, :] = g_cum
            t_ref[s] = t

    blk = pl.BlockSpec((rows, LANES), lambda h, n: (n, h))
    sq = pl.BlockSpec((None, cb, c, c), lambda h, n: (h, n, 0, 0))
    big = jax.ShapeDtypeStruct((m, wd), F32)
    small = jax.ShapeDtypeStruct((n_heads, nc, c, c), F32)
    return pl.pallas_call(
        body, name="gdn_prep", grid=(n_heads, nc // cb),
        in_specs=[pl.BlockSpec((3, rows, LANES), lambda h, n: (0, n, h)), blk, blk],
        out_specs=[blk] * 5 + [sq, sq],
        out_shape=[big] * 5 + [small, small],
        compiler_params=_params("parallel", "parallel"),
    )(qkv, g_b, beta_b)


def _gdn_prep_bwd(qkv, g_b, beta_b, ub, wc, tinv, dub, dwc, dqd, dkd, dp, dgl, n_heads):
    _, m, wd = qkv.shape
    c = GDN_CHUNK
    nc = m // c
    cb = _pick(nc, (10, 5, 3, 2))
    rows = cb * c
    qscale = GDN_HEAD_DIM ** -0.5

    def body(qkv_ref, g_ref, b_ref, ub_ref, wc_ref, t_ref, dub_ref, dwc_ref, dqd_ref, dkd_ref, dp_ref, dgl_ref,
             dqkv_ref, dg_ref, db_ref):
        incl, strict, _ = _chunk_masks()
        ones_c = jnp.ones((c, LANES), F32)
        ones_l = jnp.ones((LANES, LANES), F32)
        last_row = _iota2((c, LANES), 0) == c - 1

        def chunk(s):
            sl = pl.ds(s * c, c)
            q = qkv_ref[0, sl, :] * qscale
            k = qkv_ref[1, sl, :]
            v = qkv_ref[2, sl, :]
            bb = b_ref[sl, :]
            bcol = bb[:, :c]
            t = t_ref[s]
            g_cum = _hdot(incl.astype(F32), g_ref[sl, :])
            kk = _bdot(k, k, NT)
            qk = _bdot(q, k, NT)
            drhs_u = _hdot(t, dub_ref[sl, :], TN)
            drhs_w = _hdot(t, dwc_ref[sl, :], TN)
            yield
            g_row = _hdot(ones_c, g_cum, NT) * (1.0 / LANES)
            da = -jnp.where(strict, _bdot(drhs_u, ub_ref[sl, :], NT) + _bdot(drhs_w, wc_ref[sl, :], NT), 0.0)
            dbeta = _hdot(drhs_u * v, ones_l)
            tk = _hdot(drhs_w * k, ones_l)
            dqd = dqd_ref[sl, :]
            dgam = _hdot(dqd * q, ones_l)
            yield
            dmat = jnp.where(incl, jnp.exp(jnp.where(incl, g_cum[:, :c] - g_row, 0.0)), 0.0)
            gam = jnp.exp(g_cum)
            edec = jnp.exp(g_cum[c - 1:c, :] - g_cum)
            dkd = dkd_ref[sl, :]
            skd = _hdot(dkd * k * edec, ones_l)
            dbeta = dbeta + _hdot(da * dmat * kk, ones_c)
            dkk = da * bcol * dmat
            dpm = jnp.where(incl, dp_ref[s], 0.0)
            dpd = dpm * dmat
            dk = _bdot(dkk, k) + _bdot(dkk, k, TN) + _bdot(dpd, q, TN)
            dq = _bdot(dpd, k)
            e = (da * bcol * kk + dpm * qk) * dmat
            yield
            dk = dk + bb * gam * drhs_w + dkd * edec
            dq = dq + dqd * gam
            dbeta = dbeta + tk * gam
            dgam = dgam + bb * tk
            dgc = dgam * gam - skd + _hdot(e, ones_c) - _hdot(e, ones_c, TN)
            tot = _hdot(jnp.ones((c, c), F32), skd)
            yield
            dgc = dgc + jnp.where(last_row, tot + dgl_ref[s, 0:1, :], 0.0)
            dg_ref[sl, :] = _hdot((_iota2((c, c), 0) <= _iota2((c, c), 1)).astype(F32), dgc)
            db_ref[sl, :] = dbeta
            dqkv_ref[0, sl, :] = dq * qscale
            dqkv_ref[1, sl, :] = dk
            dqkv_ref[2, sl, :] = bb * drhs_u

        _lockstep([chunk(s) for s in range(cb)])

    blk = pl.BlockSpec((rows, LANES), lambda h, n: (n, h))
    blk3 = pl.BlockSpec((3, rows, LANES), lambda h, n: (0, n, h))
    sq = pl.BlockSpec((None, cb, c, c), lambda h, n: (h, n, 0, 0))
    sgl = pl.BlockSpec((None, cb, 8, LANES), lambda h, n: (h, n, 0, 0))
    big = jax.ShapeDtypeStruct((m, wd), F32)
    return pl.pallas_call(
        body, name="gdn_prep_bwd", grid=(n_heads, nc // cb),
        in_specs=[blk3, blk, blk, blk, blk, sq, blk, blk, blk, blk, sq, sgl],
        out_specs=[blk3, blk, blk],
        out_shape=[jax.ShapeDtypeStruct((3, m, wd), F32), big, big],
        compiler_params=_params("parallel", "parallel"),
    )(qkv, g_b, beta_b, ub, wc, tinv, dub, dwc, dqd, dkd, dp, dgl)


def _gdn_scan(ub, wc, qd, kd, gc, p, n_heads):
    m, wd = ub.shape
    c = GDN_CHUNK
    nc = m // c
    dh = GDN_HEAD_DIM

    def body(ub_ref, wc_ref, qd_ref, kd_ref, gc_ref, p_ref, o_ref, ssave_ref, s_ref):
        n = pl.program_id(0)

        @pl.when(n == 0)
        def _():
            s_ref[...] = jnp.zeros_like(s_ref)

        def head(h):
            sl = slice(h * LANES, (h + 1) * LANES)
            s = s_ref[h]
            ssave_ref[h] = s
            ws = _bdot(wc_ref[:, sl], s)
            qs = _bdot(qd_ref[:, sl], s)
            yield
            u = ub_ref[:, sl] - ws
            pu = _bdot(p_ref[h], u)
            ku = _bdot(kd_ref[:, sl], u, TN)
            yield
            o_ref[:, sl] = qs + pu
            s_ref[h] = s * jnp.exp(gc_ref[c - 1:c, sl]) + ku

        _lockstep([head(h) for h in range(n_heads)])

    blk = pl.BlockSpec((c, wd), lambda n: (n, 0))
    return pl.pallas_call(
        body, name="gdn_scan", grid=(nc,),
        in_specs=[blk] * 5 + [pl.BlockSpec((n_heads, None, c, c), lambda n: (0, n, 0, 0))],
        out_specs=[blk, pl.BlockSpec((n_heads, None, dh, dh), lambda n: (0, n, 0, 0))],
        out_shape=[jax.ShapeDtypeStruct((m, wd), F32), jax.ShapeDtypeStruct((n_heads, nc, dh, dh), F32)],
        scratch_shapes=[pltpu.VMEM((n_heads, dh, dh), F32)],
        compiler_params=_params("arbitrary"),
    )(ub, wc, qd, kd, gc, p)


def _gdn_scan_bwd(do, ssave, ub, wc, qd, kd, gc, p, n_heads):
    m, wd = ub.shape
    c = GDN_CHUNK
    nc = m // c
    dh = GDN_HEAD_DIM

    def body(do_ref, ss_ref, ub_ref, wc_ref, qd_ref, kd_ref, gc_ref, p_ref,
             dub_ref, dwc_ref, dqd_ref, dkd_ref, dp_ref, dgl_ref, ds_ref):
        n = pl.program_id(0)

        @pl.when(n == 0)
        def _():
            ds_ref[...] = jnp.zeros_like(ds_ref)

        def head(h):
            sl = slice(h * LANES, (h + 1) * LANES)
            ds_next = ds_ref[h]
            s = ss_ref[h]
            do_ = do_ref[:, sl]
            wcv = wc_ref[:, sl]
            kdv = kd_ref[:, sl]
            ws = _bdot(wcv, s)
            du = _bdot(p_ref[h], do_, TN) + _bdot(kdv, ds_next)
            dqd_ref[:, sl] = _bdot(do_, s, NT)
            col = _hdot(jnp.ones((8, dh), F32), s * ds_next)
            qdo = _bdot(qd_ref[:, sl], do_, TN)
            yield
            u = ub_ref[:, sl] - ws
            egl = jnp.exp(gc_ref[c - 1:c, sl])
            dub_ref[:, sl] = du
            dwc_ref[:, sl] = -_bdot(du, s, NT)
            dp_ref[h] = _bdot(do_, u, NT)
            dkd_ref[:, sl] = _bdot(u, ds_next, NT)
            dgl_ref[h] = egl * _hdot(col, jnp.ones((LANES, LANES), F32))
            ds_ref[h] = qdo + egl * ds_next - _bdot(wcv, du, TN)

        _lockstep([head(h) for h in range(n_heads)])

    blk = pl.BlockSpec((c, wd), lambda n: (nc - 1 - n, 0))
    sq = pl.BlockSpec((n_heads, None, c, c), lambda n: (0, nc - 1 - n, 0, 0))
    big = jax.ShapeDtypeStruct((m, wd), F32)
    return pl.pallas_call(
        body, name="gdn_scan_bwd", grid=(nc,),
        in_specs=[blk, pl.BlockSpec((n_heads, None, dh, dh), lambda n: (0, nc - 1 - n, 0, 0))] + [blk] * 5 + [sq],
        out_specs=[blk] * 4 + [sq, pl.BlockSpec((n_heads, None, 8, LANES), lambda n: (0, nc - 1 - n, 0, 0))],
        out_shape=[big] * 4 + [jax.ShapeDtypeStruct((n_heads, nc, c, c), F32),
                               jax.ShapeDtypeStruct((n_heads, nc, 8, LANES), F32)],
        scratch_shapes=[pltpu.VMEM((n_heads, dh, dh), F32)],
        compiler_params=_params("arbitrary"),
    )(do, ssave, ub, wc, qd, kd, gc, p)


def _gdn_outnorm(o, gate, gain, n_heads):
    m, wd = o.shape
    bm = _pick(m, (640, 384, 128))

    def body(o_ref, gt_ref, gn_ref, y_ref):
        gn = gn_ref[...]
        for h in range(n_heads):
            sl = slice(h * LANES, (h + 1) * LANES)
            x = o_ref[:, sl]
            gt = gt_ref[:, sl]
            r = lax.rsqrt(jnp.mean(x * x, axis=-1, keepdims=True) + NORM_EPS)
            y_ref[:, sl] = (x * r * gn * gt * _sigmoid(gt)).astype(BF16)

    row = pl.BlockSpec((bm, wd), lambda i: (i, 0))
    return pl.pallas_call(
        body, name="gdn_outnorm", grid=(m // bm,),
        in_specs=[row, row, pl.BlockSpec((1, LANES), lambda i: (0, 0))], out_specs=row,
        out_shape=jax.ShapeDtypeStruct((m, wd), BF16),
        compiler_params=_params("parallel"),
    )(o, gate, gain)


def _gdn_outnorm_bwd(o, gate, gain, dy, n_heads):
    m, wd = o.shape
    bm = _pick(m, (640, 384, 128))
    nsteps = m // bm

    def body(o_ref, gt_ref, gn_ref, dy_ref, do_ref, dgt_ref, dgn_ref, acc_ref):
        i = pl.program_id(0)
        gn = gn_ref[...]
        part = jnp.zeros((8, LANES), F32)
        for h in range(n_heads):
            sl = slice(h * LANES, (h + 1) * LANES)
            x = o_ref[:, sl]
            gt = gt_ref[:, sl]
            d_out = dy_ref[:, sl]
            r = lax.rsqrt(jnp.mean(x * x, axis=-1, keepdims=True) + NORM_EPS)
            xhat = x * r
            sg = _sigmoid(gt)
            dz = d_out * gt * sg
            dgt_ref[:, sl] = d_out * xhat * gn * sg * (1.0 + gt * (1.0 - sg))
            dxh = dz * gn
            do_ref[:, sl] = r * (dxh - xhat * jnp.mean(dxh * xhat, axis=-1, keepdims=True))
            part = part + jnp.sum((dz * xhat).reshape(bm // 8, 8, LANES), axis=0)

        @pl.when(i == 0)
        def _():
            acc_ref[...] = part

        @pl.when(i > 0)
        def _():
            acc_ref[...] += part

        @pl.when(i == nsteps - 1)
        def _():
            dgn_ref[...] = jnp.sum(acc_ref[...], axis=0, keepdims=True)

    row = pl.BlockSpec((bm, wd), lambda i: (i, 0))
    one = pl.BlockSpec((1, LANES), lambda i: (0, 0))
    return pl.pallas_call(
        body, name="gdn_outnorm_bwd", grid=(nsteps,),
        in_specs=[row, row, one, row], out_specs=[row, row, one],
        out_shape=[jax.ShapeDtypeStruct((m, wd), F32)] * 2 + [jax.ShapeDtypeStruct((1, LANES), F32)],
        scratch_shapes=[pltpu.VMEM((8, LANES), F32)],
        compiler_params=_params("arbitrary"),
    )(o, gate, gain, dy)


def _sb_consts(kind):
    ri, ci = _iota2((LANES, LANES), 0), _iota2((LANES, LANES), 1)
    tri = {"suffix_excl": ri > ci, "prefix_incl": ri <= ci, "prefix_excl": ri < ci}[kind]
    return jnp.concatenate([tri.astype(BF16), jnp.ones((LANES, LANES), BF16)], axis=1)


def _sb_stack(a):
    first = _iota2((LANES, LANES), 1) < SB_HEAD_DIM
    return jnp.concatenate([jnp.where(first, a, 0.0), jnp.where(first, 0.0, a)], axis=0)


def _sb_unstack(a2):
    first = _iota2((LANES, LANES), 1) < SB_HEAD_DIM
    return jnp.where(first, a2[:LANES], a2[LANES:])


def _sb_visible(qi, kb, blocks=1):
    shape = (2 * LANES, blocks * LANES)
    kpos = _iota2(shape, 1) + kb * LANES
    qpos = (_iota2(shape, 0) & (LANES - 1)) + qi * LANES
    return (kpos < qpos) & (kpos >= PAD_ROWS)


def _sb_log_gates(z, vis):
    t = jnp.log2(1.0 + jnp.exp2(-jnp.abs(z)))
    lb = jnp.minimum(z, 0.0) - t
    lk = lb - z
    if vis is not None:
        lk = jnp.where(vis, lk, 0.0)
    return lb, lk


def _sb_attention(q, kv):
    m, wd = q.shape
    npair = wd // LANES
    nq = m // LANES
    scale = SB_HEAD_DIM ** -0.5

    def body(q_ref, k_ref, v_ref, o_ref, w_ref, b_ref):
        qi = pl.program_id(1)
        su = _sb_consts("suffix_excl")
        q2 = _sb_stack(q_ref[...] * (scale * math.log2(math.e))).astype(BF16)

        def tiles(kbs, carry, masked):
            run, acc = carry
            r0s = [pl.multiple_of(kb * LANES, LANES) for kb in kbs]
            zs = [lax.dot_general(q2, k_ref[pl.ds(r0, LANES), :], NT, preferred_element_type=F32) for r0 in r0s]
            parts = []
            for kb, r0, z in zip(kbs, r0s, zs):
                vis = _sb_visible(qi, kb) if masked else None
                lb, lk = _sb_log_gates(z, vis)
                parts.append((kb, r0, vis, lb, lk))
            parts = [(kb, r0, vis, lb, _split_dot(lk, su)) for kb, r0, vis, lb, lk in parts]
            for kb, r0, vis, lb, cs in parts:
                w = jnp.exp2(lb + cs[:, :LANES] + run)
                if vis is not None:
                    w = jnp.where(vis, w, 0.0)
                run = run + cs[:, LANES:]
                wb = w.astype(BF16)
                w_ref[kb] = wb
                b_ref[kb] = jnp.exp2(lb).astype(BF16)
                acc = acc + lax.dot_general(wb, v_ref[pl.ds(r0, LANES), :], NN, preferred_element_type=F32)
            return run, acc

        su2 = (_iota2((2 * LANES, 2 * LANES), 0) > _iota2((2 * LANES, 2 * LANES), 1)).astype(BF16)

        def pairs(his, carry, masked):
            run, acc = carry
            r0s = [pl.multiple_of((hi - 1) * LANES, LANES) for hi in his]
            zs = [lax.dot_general(q2, k_ref[pl.ds(r0, 2 * LANES), :], NT, preferred_element_type=F32) for r0 in r0s]
            masks = masked if isinstance(masked, list) else [masked] * len(his)
            viss = [_sb_visible(qi, hi - 1, 2) if mk else None for hi, mk in zip(his, masks)]
            gates = [_sb_log_gates(z, vis) for z, vis in zip(zs, viss)]
            css = [_split_dot(lk, su2) for _, lk in gates]
            for hi, r0, vis, (lb, lk), cs in zip(his, r0s, viss, gates, css):
                w = jnp.exp2(lb + cs + jnp.concatenate([run, run], axis=1))
                if vis is not None:
                    w = jnp.where(vis, w, 0.0)
                run = run + jnp.broadcast_to(cs[:, 0:1] + lk[:, 0:1], (2 * LANES, LANES))
                wb = w.astype(BF16)
                bb = jnp.exp2(lb).astype(BF16)
                w_ref[hi - 1], w_ref[hi] = wb[:, :LANES], wb[:, LANES:]
                b_ref[hi - 1], b_ref[hi] = bb[:, :LANES], bb[:, LANES:]
                acc = acc + lax.dot_general(wb, v_ref[pl.ds(r0, 2 * LANES), :], NN, preferred_element_type=F32)
            return run, acc

        def maybe(flag, fn, carry):
            return lax.fori_loop(0, flag, lambda j, c: fn(c), carry)

        zero = jnp.zeros((2 * LANES, LANES), F32)
        far = (qi >= SB_FAR).astype(jnp.int32)
        near = (qi >= 3).astype(jnp.int32) - far
        big = far + near
        carry = maybe(far, lambda c: pairs([qi, qi - 2, qi - 4, qi - 6], c, [True, False, False, False]), (zero, zero))
        carry = maybe(near, lambda c: pairs([qi], c, True), carry)
        carry = maybe(1 - big, lambda c: tiles([qi], c, True), carry)
        top, bottom = qi - 1 - big - 6 * far, 1 + big + 2 * far
        n_mid = jnp.maximum(top - bottom + 1, 0)
        n8 = n_mid // 8
        carry = lax.fori_loop(0, n8, lambda j, c: pairs([top - 8 * j - 2 * u for u in range(4)], c, False), carry)
        top = top - 8 * n8
        take4, take2, take1 = (n_mid >> 2) & 1, (n_mid >> 1) & 1, n_mid & 1
        carry = maybe(take4, lambda c: pairs([top, top - 2], c, False), carry)
        top = top - 4 * take4
        carry = maybe(take2, lambda c: pairs([top], c, False), carry)
        carry = maybe(take1, lambda c: tiles([bottom], c, False), carry)
        carry = maybe(far, lambda c: pairs([3, 1], c, [False, True]), carry)
        carry = maybe(near, lambda c: pairs([1], c, True), carry)
        carry = maybe((1 - big) * (qi >= 1).astype(jnp.int32), lambda c: tiles([0], c, True), carry)
        o_ref[...] = _sb_unstack(carry[1])

    qspec = pl.BlockSpec((LANES, LANES), lambda hp, qi: (qi, hp))
    saved = pl.BlockSpec((None, None, nq, 2 * LANES, LANES), lambda hp, qi: (hp, qi, 0, 0, 0))
    saved_shape = jax.ShapeDtypeStruct((npair, nq, nq, 2 * LANES, LANES), BF16)
    return pl.pallas_call(
        body, name="sb_attention", grid=(npair, nq),
        in_specs=[qspec, pl.BlockSpec((m, LANES), lambda hp, qi: (0, hp)),
                  pl.BlockSpec((m, LANES), lambda hp, qi: (0, npair + hp))],
        out_specs=[qspec, saved, saved],
        out_shape=[jax.ShapeDtypeStruct((m, wd), F32), saved_shape, saved_shape],
        compiler_params=_params("parallel", "arbitrary"),
    )(q, kv, kv)


def _sb_attention_bwd(q, kv, w_all, b_all, do):
    m, wd = q.shape
    npair = wd // LANES
    nq = m // LANES
    scale = SB_HEAD_DIM ** -0.5

    def body(q_ref, k_ref, v_ref, w_ref, b_ref, do_ref, dq_ref, dk_ref, dv_ref):
        qi = pl.program_id(1)

        @pl.when(qi == 0)
        def _():
            dk_ref[...] = jnp.zeros_like(dk_ref)
            dv_ref[...] = jnp.zeros_like(dv_ref)

        px = _sb_consts("prefix_excl")
        q2 = _sb_stack(q_ref[...] * scale).astype(BF16)
        do2 = _sb_stack(do_ref[...]).astype(BF16)

        def tiles(kbs, carry, masked):
            run_p, dq_acc = carry
            r0s = [pl.multiple_of(kb * LANES, LANES) for kb in kbs]
            das = [lax.dot_general(do2, v_ref[pl.ds(r0, LANES), :], NT, preferred_element_type=F32) for r0 in r0s]
            wbs = [w_ref[kb] for kb in kbs]
            for wb, r0 in zip(wbs, r0s):
                dv_ref[pl.ds(r0, LANES), :] += lax.dot_general(wb, do2, TN, preferred_element_type=F32)
            pws = [wb.astype(F32) * da for wb, da in zip(wbs, das)]
            pss = [_split_dot(pw, px, parts=SB_BWD_SUM_PARTS) for pw in pws]
            dzs = []
            for kb, pw, ps in zip(kbs, pws, pss):
                beta = b_ref[kb].astype(F32)
                dz = pw * (1.0 - beta) - beta * (run_p + ps[:, :LANES])
                if masked:
                    dz = jnp.where(_sb_visible(qi, kb), dz, 0.0)
                run_p = run_p + ps[:, LANES:]
                dzs.append(dz.astype(BF16))
            for r0, dz in zip(r0s, dzs):
                dq_acc = dq_acc + lax.dot_general(dz, k_ref[pl.ds(r0, LANES), :], NN, preferred_element_type=F32)
                dk_ref[pl.ds(r0, LANES), :] += lax.dot_general(dz, q2, TN, preferred_element_type=F32)
            return run_p, dq_acc

        px2 = (_iota2((2 * LANES, 2 * LANES), 0) < _iota2((2 * LANES, 2 * LANES), 1)).astype(BF16)

        def pairs(los, carry, masked):
            run_p, dq_acc = carry
            r0s = [pl.multiple_of(lo * LANES, LANES) for lo in los]
            das = [lax.dot_general(do2, v_ref[pl.ds(r0, 2 * LANES), :], NT, preferred_element_type=F32) for r0 in r0s]
            wbs = [jnp.concatenate([w_ref[lo], w_ref[lo + 1]], axis=1) for lo in los]
            for wb, r0 in zip(wbs, r0s):
                dv_ref[pl.ds(r0, 2 * LANES), :] += lax.dot_general(wb, do2, TN, preferred_element_type=F32)
            pws = [wb.astype(F32) * da for wb, da in zip(wbs, das)]
            pss = [_split_dot(pw, px2, parts=SB_BWD_SUM_PARTS) for pw in pws]
            masks = masked if isinstance(masked, list) else [masked] * len(los)
            dzs = []
            for lo, pw, ps, mk in zip(los, pws, pss, masks):
                beta = jnp.concatenate([b_ref[lo], b_ref[lo + 1]], axis=1).astype(F32)
                dz = pw * (1.0 - beta) - beta * (jnp.concatenate([run_p, run_p], axis=1) + ps)
                if mk:
                    dz = jnp.where(_sb_visible(qi, lo, 2), dz, 0.0)
                last = 2 * LANES - 1
                run_p = run_p + jnp.broadcast_to(ps[:, last:] + pw[:, last:], (2 * LANES, LANES))
                dzs.append(dz.astype(BF16))
            for r0, dz in zip(r0s, dzs):
                dq_acc = dq_acc + lax.dot_general(dz, k_ref[pl.ds(r0, 2 * LANES), :], NN, preferred_element_type=F32)
                dk_ref[pl.ds(r0, 2 * LANES), :] += lax.dot_general(dz, q2, TN, preferred_element_type=F32)
            return run_p, dq_acc

        def maybe(flag, fn, carry):
            return lax.fori_loop(0, flag, lambda j, c: fn(c), carry)

        zero = jnp.zeros((2 * LANES, LANES), F32)
        far = (qi >= SB_FAR).astype(jnp.int32)
        near = (qi >= 3).astype(jnp.int32) - far
        big = far + near
        carry = maybe(far, lambda c: pairs([0, 2], c, [True, False]), (zero, zero))
        carry = maybe(near, lambda c: pairs([0], c, True), carry)
        carry = maybe(1 - big, lambda c: tiles([0], c, True), carry)
        low, high = 1 + big + 2 * far, qi - 1 - big - 6 * far
        n_mid = jnp.maximum(high - low + 1, 0)
        n8 = n_mid // 8
        carry = lax.fori_loop(0, n8, lambda j, c: pairs([low + 8 * j + 2 * u for u in range(4)], c, False), carry)
        low = low + 8 * n8
        take4, take2, take1 = (n_mid >> 2) & 1, (n_mid >> 1) & 1, n_mid & 1
        carry = maybe(take4, lambda c: pairs([low, low + 2], c, False), carry)
        low = low + 4 * take4
        carry = maybe(take2, lambda c: pairs([low], c, False), carry)
        carry = maybe(take1, lambda c: tiles([high], c, False), carry)
        carry = maybe(far, lambda c: pairs([qi - 7, qi - 5, qi - 3, qi - 1], c, [False, False, False, True]), carry)
        carry = maybe(near, lambda c: pairs([qi - 1], c, True), carry)
        carry = maybe((1 - big) * (qi >= 1).astype(jnp.int32), lambda c: tiles([qi], c, True), carry)
        dq_ref[...] = _sb_unstack(carry[1]) * scale

    qspec = pl.BlockSpec((LANES, LANES), lambda hp, qi: (qi, hp))
    strip = pl.BlockSpec((m, LANES), lambda hp, qi: (0, hp))
    saved = pl.BlockSpec((None, None, nq, 2 * LANES, LANES), lambda hp, qi: (hp, qi, 0, 0, 0))
    big = jax.ShapeDtypeStruct((m, wd), F32)
    return pl.pallas_call(
        body, name="sb_attention_bwd", grid=(npair, nq),
        in_specs=[qspec, strip, pl.BlockSpec((m, LANES), lambda hp, qi: (0, npair + hp)), saved, saved, qspec],
        out_specs=[qspec, strip, strip],
        out_shape=[big, big, big],
        compiler_params=_params("parallel", "arbitrary"),
    )(q, kv, kv, w_all, b_all, do)


def _mesh_pos():
    return lax.axis_index("x"), lax.axis_index("y"), lax.axis_index("c")


def _exchange(srcs, name, scatter):
    n = len(srcs)
    rows = [s.shape[-2] for s in srcs]
    offs = [sum(rows[:i]) for i in range(n)]
    total = sum(rows)

    def body(*refs):
        src_refs, out_ref = refs[:n], refs[n]
        send_sems, recv_sems, local_sems = refs[n + 1:]
        x, y, c = _mesh_pos()
        me = 4 * x + 2 * y + c

        def piece(i, d):
            return src_refs[i].at[d] if scatter else src_refs[i]

        def window(slot, i):
            return out_ref.at[slot, pl.ds(offs[i], rows[i])]

        local = [pltpu.make_async_copy(piece(i, me), window(me, i), local_sems.at[i]) for i in range(n)]
        for cp in local:
            cp.start()
        peers = []
        for k in range(1, N_DEV):
            px = 1 - x if k & 4 else x
            py = 1 - y if k & 2 else y
            pc = 1 - c if k & 1 else c
            pid = 4 * px + 2 * py + pc
            for i in range(n):
                pltpu.make_async_remote_copy(
                    src_ref=piece(i, pid), dst_ref=window(me, i),
                    send_sem=send_sems.at[k - 1], recv_sem=recv_sems.at[k - 1],
                    device_id=(px, py, pc), device_id_type=pl.DeviceIdType.MESH).start()
            peers.append(pid)
        slabs = [pltpu.make_async_remote_copy(
            src_ref=out_ref.at[me], dst_ref=out_ref.at[peers[k - 1]],
            send_sem=send_sems.at[k - 1], recv_sem=recv_sems.at[k - 1],
            device_id=(x, y, c), device_id_type=pl.DeviceIdType.MESH) for k in range(1, N_DEV)]
        for cp in slabs:
            cp.wait_recv()
        for cp in slabs:
            cp.wait_send()
        for cp in local:
            cp.wait()

    return pl.pallas_call(
        body, name=name,
        in_specs=[pl.BlockSpec(memory_space=pl.ANY)] * n, out_specs=pl.BlockSpec(memory_space=pl.ANY),
        out_shape=jax.ShapeDtypeStruct((N_DEV, total, FLAT_COLS), srcs[0].dtype),
        scratch_shapes=[pltpu.SemaphoreType.DMA((N_DEV - 1,)), pltpu.SemaphoreType.DMA((N_DEV - 1,)),
                        pltpu.SemaphoreType.DMA((n,))],
        compiler_params=pltpu.CompilerParams(has_side_effects=True),
    )(*srcs)


def _all_gather_two_level(srcs, name):
    n = len(srcs)
    rows = [s.shape[-2] for s in srcs]
    offs = [sum(rows[:i]) for i in range(n)]
    total = sum(rows)

    def body(*refs):
        src_refs, out_ref = refs[:n], refs[n]
        send_sems, recv_sems, local_sems = refs[n + 1:]
        x, y, c = _mesh_pos()
        me = 4 * x + 2 * y + c
        sibling = (x, y, 1 - c)
        chips = [(1 - x, y), (x, 1 - y), (1 - x, 1 - y)]

        def slot_of(px, py, pc):
            return 4 * px + 2 * py + pc

        def window(slot, i):
            return out_ref.at[slot, pl.ds(offs[i], rows[i])]

        def slab(slot, k, to):
            return pltpu.make_async_remote_copy(
                src_ref=out_ref.at[slot], dst_ref=out_ref.at[slot],
                send_sem=send_sems.at[k], recv_sem=recv_sems.at[k],
                device_id=to, device_id_type=pl.DeviceIdType.MESH)

        local = [pltpu.make_async_copy(src_refs[i], window(me, i), local_sems.at[i]) for i in range(n)]
        for cp in local:
            cp.start()
        targets = [sibling] + [(cx, cy, c) for cx, cy in chips]
        for k, to in enumerate(targets):
            for i in range(n):
                pltpu.make_async_remote_copy(
                    src_ref=src_refs[i], dst_ref=window(me, i),
                    send_sem=send_sems.at[k], recv_sem=recv_sems.at[k],
                    device_id=to, device_id_type=pl.DeviceIdType.MESH).start()
        passed = []
        for j, (cx, cy) in enumerate(chips):
            slab(slot_of(cx, cy, c), 1 + j, (x, y, c)).wait_recv()
            cp = slab(slot_of(cx, cy, c), 4 + j, sibling)
            cp.start()
            passed.append(cp)
        slab(slot_of(x, y, 1 - c), 0, (x, y, c)).wait_recv()
        for j, (cx, cy) in enumerate(chips):
            slab(slot_of(cx, cy, 1 - c), 4 + j, (x, y, c)).wait_recv()
        for k in range(len(targets)):
            slab(me, k, (x, y, c)).wait_send()
        for cp in passed:
            cp.wait_send()
        for cp in local:
            cp.wait()

    return pl.pallas_call(
        body, name=name,
        in_specs=[pl.BlockSpec(memory_space=pl.ANY)] * n, out_specs=pl.BlockSpec(memory_space=pl.ANY),
        out_shape=jax.ShapeDtypeStruct((N_DEV, total, FLAT_COLS), srcs[0].dtype),
        scratch_shapes=[pltpu.SemaphoreType.DMA((N_DEV - 1,)), pltpu.SemaphoreType.DMA((N_DEV - 1,)),
                        pltpu.SemaphoreType.DMA((n,))],
        compiler_params=pltpu.CompilerParams(has_side_effects=True),
    )(*srcs)


def _adamw(parts, w, mom, var, name):
    rows = w.shape[0]
    br = FLAT_ROW_BLOCK
    c1 = 1.0 - ADAM_B1 ** ADAM_STEP
    c2 = 1.0 - ADAM_B2 ** ADAM_STEP

    def body(p_ref, w_ref, m_ref, v_ref, g_ref, d_ref, nm_ref, nv_ref):
        g = p_ref[0].astype(F32)
        for s in range(1, N_DEV):
            g = g + p_ref[s].astype(F32)
        m_new = ADAM_B1 * m_ref[...] + (1.0 - ADAM_B1) * g
        v_new = ADAM_B2 * v_ref[...] + (1.0 - ADAM_B2) * (g * g)
        m_hat = m_new / c1
        v_hat = v_new / c2
        g_ref[...] = g
        d_ref[...] = -ADAM_LR * (m_hat / (jnp.sqrt(v_hat) + ADAM_EPS) + ADAM_WD * w_ref[...])
        nm_ref[...] = m_new
        nv_ref[...] = v_new

    row = pl.BlockSpec((br, FLAT_COLS), lambda i: (i, 0))
    flat = jax.ShapeDtypeStruct((rows, FLAT_COLS), F32)
    return pl.pallas_call(
        body, name=name, grid=(rows // br,),
        in_specs=[pl.BlockSpec((N_DEV, br, FLAT_COLS), lambda i: (0, i, 0)), row, row, row],
        out_specs=[row] * 4, out_shape=[flat] * 4,
        compiler_params=_params("parallel"),
    )(parts, w, mom, var)


SMALL_SHARDED = (("meta_tokens", 1), ("gdn_norm_g", 1), ("gdn_conv_w", 2))
BIG_PIECES = (("gdn_w_in", 0, True), ("gdn_w_out", 0, False), ("w_kv", None, True), ("sb_w_q", 0, False),
              ("sb_w_o", 0, False), ("ffn_w_gate_up", 0, True), ("ffn_w_gate_up", 1, True),
              ("ffn_w_down", 0, False), ("ffn_w_down", 1, False))
REPLICATED = ("gdn_a_log", "gdn_dt_bias", "gdn_onorm_g", "kv_norm_g", "sb_norm_g", "ffn_norm_g", "final_norm_g")
WEIGHTS = ("meta_tokens", "gdn_norm_g", "gdn_w_in", "gdn_conv_w", "gdn_a_log", "gdn_dt_bias", "gdn_onorm_g",
           "gdn_w_out", "kv_norm_g", "w_kv", "sb_norm_g", "sb_w_q", "sb_w_o", "ffn_norm_g", "ffn_w_gate_up",
           "ffn_w_down", "final_norm_g")
F32_ROWS = 8
BF16_ROWS = 16


def _n_rows(shape, mult):
    return -(-math.prod(shape) // (mult * FLAT_COLS)) * mult


def _as_rows(a, nl, mult):
    lead = a.shape[:nl]
    size = math.prod(a.shape[nl:])
    nr = _n_rows(a.shape[nl:], mult)
    flat = a.reshape(lead + (size,))
    if nr * FLAT_COLS != size:
        flat = jnp.pad(flat, [(0, 0)] * nl + [(0, nr * FLAT_COLS - size)])
    return flat.reshape(lead + (nr, FLAT_COLS))


def _from_rows(flat, r0, shape, nl, mult):
    lead = flat.shape[:nl]
    nr = _n_rows(shape, mult)
    seg = lax.slice_in_dim(flat, r0, r0 + nr, axis=nl).reshape(lead + (nr * FLAT_COLS,))
    return lax.slice_in_dim(seg, 0, math.prod(shape), axis=nl).reshape(lead + tuple(shape)), r0 + nr


def _to_piece(shard, layer, transposed):
    mat = shard if layer is None else shard[layer]
    return mat.T if transposed else mat


def _from_pieces(pieces, name):
    mats = [p.T if tr else p for (n, layer, tr), p in zip(BIG_PIECES, pieces) if n == name]
    layers = [layer for n, layer, _ in BIG_PIECES if n == name]
    return mats[0] if layers[0] is None else jnp.stack(mats)


def _merge_shards(g, axis):
    t = jnp.moveaxis(g, 0, axis)
    shp = t.shape
    return t.reshape(shp[:axis] + (shp[axis] * shp[axis + 1],) + shp[axis + 2:])


def _split_shards(full, axis):
    shp = full.shape
    t = full.reshape(shp[:axis] + (N_DEV, shp[axis] // N_DEV) + shp[axis + 1:])
    return jnp.moveaxis(t, axis, 0)


def _pad_lanes(a, width=LANES):
    return jnp.pad(a, ((0, 0), (0, width - a.shape[1])))


def _local_step(x, target, w, mats):
    d = x.shape[1]
    gh = w["gdn_a_log"].shape[1]
    gw = gh * GDN_HEAD_DIM
    w_int = mats["gdn_w_in", 0]
    w_qkvt, w_gatet = w_int[:3 * gw], w_int[3 * gw:4 * gw]
    w_abt = jnp.pad(w_int[4 * gw:], ((0, LANES - 2 * gh), (0, 0)))
    conv_w = w["gdn_conv_w"][0]
    a_log, dt_bias = _pad_lanes(w["gdn_a_log"]), _pad_lanes(w["gdn_dt_bias"])
    w_out = mats["gdn_w_out", 0]
    w_kvt = mats["w_kv", None]
    sbw = w_kvt.shape[0] // 2
    w_kt, w_vt = w_kvt[:sbw], w_kvt[sbw:]
    w_q, w_o = mats["sb_w_q", 0], mats["sb_w_o", 0]
    w_gut = [mats["ffn_w_gate_up", l] for l in range(2)]
    w_dn = [mats["ffn_w_down", l] for l in range(2)]
    ffn_g = [w["ffn_norm_g"][l:l + 1] for l in range(2)]
    kv_g, fin_g = w["kv_norm_g"][None], w["final_norm_g"][None]

    h0 = jnp.concatenate([jnp.zeros((PAD_ROWS, d), F32), w["meta_tokens"], x], axis=0)

    (n1,) = _rmsnorm(h0, [w["gdn_norm_g"]], "norm_gdn")
    proj = _matmul(n1, w_qkvt, "nt", "mm_gdn_qkv")
    gate = _matmul(n1, w_gatet, "nt", "mm_gdn_gate")
    ab = _matmul(n1, w_abt, "nt", "mm_gdn_ab")
    qkv = _gdn_conv(proj, conv_w, gh)
    g_b, beta_b = _gdn_gates(ab, a_log, dt_bias, gh)
    ub, wc, qd, kd, gc, pmat, tinv = _gdn_prep(qkv, g_b, beta_b, gh)
    o_gdn, ssave = _gdn_scan(ub, wc, qd, kd, gc, pmat, gh)
    o2 = _gdn_outnorm(o_gdn, gate, w["gdn_onorm_g"], gh)
    h1 = _matmul(o2, w_out, "nn", "mm_gdn_out", add=h0)

    def ffn_fwd(h, l):
        (n,) = _rmsnorm(h, [ffn_g[l]], f"norm_ffn{l}")
        gu = _matmul(n, w_gut[l], "nt", f"mm_ffn{l}_gu", out_dtype=BF16)
        act = _swiglu(gu, f"swiglu{l}")
        return n, gu, act, _matmul(act, w_dn[l], "nn", f"mm_ffn{l}_down", add=h)

    n2, gu0, act0, h2 = ffn_fwd(h1, 0)

    nkv, n3 = _rmsnorm(h2, [kv_g, w["sb_norm_g"]], "norm_kv_sb")
    kv = _matmul(nkv, w_kvt, "nt", "mm_kv", out_dtype=BF16)
    q_sb = _matmul(n3, w_q, "nn", "mm_sb_q")
    o_sb, w_sb, beta_sb = _sb_attention(q_sb, kv)
    h3 = _matmul(o_sb, w_o, "nn", "mm_sb_o", add=h2)
    n4, gu1, act1, h4 = ffn_fwd(h3, 1)

    loss, dh4, d_fin = _loss_head(h4, fin_g, target)

    def ffn_bwd(dh, h, n, gu, act, l):
        d_act = _matmul(dh, w_dn[l], "nt", f"mm_ffn{l}_dact")
        dw_dn = _matmul(act, dh, "tn", f"mm_ffn{l}_dwdown")
        dgu = _swiglu_bwd(gu, d_act, f"swiglu_bwd{l}")
        dw_gu = _matmul(dgu, n, "tn", f"mm_ffn{l}_dwgu")
        dn = _matmul(dgu, w_gut[l], "nn", f"mm_ffn{l}_dn")
        dh_in, (dg,) = _rmsnorm_bwd(h, [ffn_g[l]], [dn], dh, f"norm_ffn{l}_bwd")
        return dh_in, dw_gu, dw_dn, dg

    dh3, dw_gu1, dw_dn1, dg_ffn1 = ffn_bwd(dh4, h3, n4, gu1, act1, 1)

    do_sb = _matmul(dh3, w_o, "nt", "mm_sb_do")
    dw_o = _matmul(o_sb, dh3, "tn", "mm_sb_dwo")
    dq_sb, dk_sb, dv_sb = _sb_attention_bwd(q_sb, kv, w_sb, beta_sb, do_sb)
    dw_q = _matmul(n3, dq_sb, "tn", "mm_sb_dwq")
    dn3 = _matmul(dq_sb, w_q, "nt", "mm_sb_dn")
    dw_k = _matmul(dk_sb, nkv, "tn", "mm_dwk")
    dw_v = _matmul(dv_sb, nkv, "tn", "mm_dwv")
    dnkv = _matmul(dk_sb, w_kt, "nn", "mm_dnk")
    dnkv = _matmul(dv_sb, w_vt, "nn", "mm_dnv", add=dnkv)
    dh2, (dg_kv, dg_sb) = _rmsnorm_bwd(h2, [kv_g, w["sb_norm_g"]], [dnkv, dn3], dh3, "norm_kv_sb_bwd")

    dh1, dw_gu0, dw_dn0, dg_ffn0 = ffn_bwd(dh2, h1, n2, gu0, act0, 0)

    do2 = _matmul(dh1, w_out, "nt", "mm_gdn_do")
    dw_out = _matmul(o2, dh1, "tn", "mm_gdn_dwout")
    do_gdn, dgate, d_onorm = _gdn_outnorm_bwd(o_gdn, gate, w["gdn_onorm_g"], do2, gh)
    dub, dwc, dqd, dkd, dpm, dgl = _gdn_scan_bwd(do_gdn, ssave, ub, wc, qd, kd, gc, pmat, gh)
    dqkv, dg_b, dbeta_b = _gdn_prep_bwd(qkv, g_b, beta_b, ub, wc, tinv, dub, dwc, dqd, dkd, dpm, dgl, gh)
    dab, d_alog, d_dtb = _gdn_gates_bwd(ab, a_log, dt_bias, dg_b, dbeta_b, gh)
    dproj, d_conv = _gdn_conv_bwd(proj, conv_w, dqkv, gh)
    dw_qkv = _matmul(dproj, n1, "tn", "mm_gdn_dwqkv")
    dw_gate = _matmul(dgate, n1, "tn", "mm_gdn_dwgate")
    dw_ab = _matmul(dab, n1, "tn", "mm_gdn_dwab")
    dn1 = _matmul(dproj, w_qkvt, "nn", "mm_gdn_dn_qkv")
    dn1 = _matmul(dgate, w_gatet, "nn", "mm_gdn_dn_gate", add=dn1)
    dn1 = _matmul(dab, w_abt, "nn", "mm_gdn_dn_ab", add=dn1)
    dh0, (dg_gdn,) = _rmsnorm_bwd(h0, [w["gdn_norm_g"]], [dn1], dh1, "norm_gdn_bwd")

    grads = {
        "meta_tokens": dh0[PAD_ROWS:LANES],
        "gdn_norm_g": dg_gdn,
        "gdn_conv_w": d_conv[None],
        "gdn_a_log": d_alog[:, :gh],
        "gdn_dt_bias": d_dtb[:, :gh],
        "gdn_onorm_g": d_onorm,
        "kv_norm_g": dg_kv[0],
        "sb_norm_g": dg_sb,
        "ffn_norm_g": jnp.concatenate([dg_ffn0, dg_ffn1], axis=0),
        "final_norm_g": d_fin[0],
    }
    gmats = {
        ("gdn_w_in", 0): jnp.concatenate([dw_qkv, dw_gate, dw_ab[:2 * gh]], axis=0),
        ("gdn_w_out", 0): dw_out,
        ("w_kv", None): jnp.concatenate([dw_k, dw_v], axis=0),
        ("sb_w_q", 0): dw_q,
        ("sb_w_o", 0): dw_o,
        ("ffn_w_gate_up", 0): dw_gu0,
        ("ffn_w_gate_up", 1): dw_gu1,
        ("ffn_w_down", 0): dw_dn0,
        ("ffn_w_down", 1): dw_dn1,
    }
    return loss, dh0[LANES:], grads, gmats


def kernel(x, meta_tokens, gdn_norm_g, gdn_w_in, gdn_conv_w, gdn_a_log, gdn_dt_bias, gdn_onorm_g, gdn_w_out, kv_norm_g, w_kv, sb_norm_g, sb_w_q, sb_w_o, ffn_norm_g, ffn_w_gate_up, ffn_w_down, final_norm_g, loss_target, m_meta_tokens, m_gdn_norm_g, m_gdn_w_in, m_gdn_conv_w, m_gdn_a_log, m_gdn_dt_bias, m_gdn_onorm_g, m_gdn_w_out, m_kv_norm_g, m_w_kv, m_sb_norm_g, m_sb_w_q, m_sb_w_o, m_ffn_norm_g, m_ffn_w_gate_up, m_ffn_w_down, m_final_norm_g, v_meta_tokens, v_gdn_norm_g, v_gdn_w_in, v_gdn_conv_w, v_gdn_a_log, v_gdn_dt_bias, v_gdn_onorm_g, v_gdn_w_out, v_kv_norm_g, v_w_kv, v_sb_norm_g, v_sb_w_q, v_sb_w_o, v_ffn_norm_g, v_ffn_w_gate_up, v_ffn_w_down, v_final_norm_g):
    given = dict(locals())
    small = [n for n, _ in SMALL_SHARDED]
    big_names = list(dict.fromkeys(n for n, _, _ in BIG_PIECES))

    def local_pieces(prefix):
        return [_to_piece(given[prefix + n], layer, tr) for n, layer, tr in BIG_PIECES]

    piece_shapes = [p.shape for p in local_pieces("")]

    g_small = _exchange([jnp.concatenate([_as_rows(given[n], 0, F32_ROWS) for n in small], axis=0)],
                        "all_gather_small", scatter=False)
    g_big = _all_gather_two_level([_as_rows(p.astype(BF16), 0, BF16_ROWS) for p in local_pieces("")],
                                  "all_gather_weights")
    full = {n: given[n] for n in REPLICATED}
    r = 0
    for n, ax in SMALL_SHARDED:
        shards, r = _from_rows(g_small, r, given[n].shape, 1, F32_ROWS)
        full[n] = _merge_shards(shards, ax)
    mats, r = {}, 0
    for (n, layer, _), shp in zip(BIG_PIECES, piece_shapes):
        rows8, r = _from_rows(g_big, r, shp, 1, BF16_ROWS)
        mats[n, layer] = rows8.reshape((N_DEV * shp[0],) + shp[1:])

    loss_b, grad_x, grads, gmats = _local_step(x[0], loss_target[0], full, mats)

    order = small + list(REPLICATED)
    fill_small = (-sum(_n_rows(given[n].shape, F32_ROWS) for n in order)) % FLAT_ROW_BLOCK
    fill_big = (-sum(_n_rows(shp, BF16_ROWS) for shp in piece_shapes)) % FLAT_ROW_BLOCK
    small_send = [_as_rows(_split_shards(grads[n], ax), 1, F32_ROWS) for n, ax in SMALL_SHARDED]
    small_send += [_as_rows(jnp.broadcast_to(grads[n][None], (N_DEV,) + grads[n].shape), 1, F32_ROWS)
                   for n in REPLICATED]
    if fill_small:
        small_send.append(jnp.zeros((N_DEV, fill_small, FLAT_COLS), F32))
    recv_small = _exchange([jnp.concatenate(small_send, axis=1)], "scatter_small_gradients", scatter=True)
    big_send = [_as_rows(gmats[n, layer].reshape((N_DEV,) + shp), 1, BF16_ROWS)
                for (n, layer, _), shp in zip(BIG_PIECES, piece_shapes)]
    if fill_big:
        big_send.append(jnp.zeros((N_DEV, fill_big, FLAT_COLS), BF16))
    recv_big = _exchange(big_send, "scatter_gradients", scatter=True)

    def flat_small(prefix):
        segs = [_as_rows(given[prefix + n], 0, F32_ROWS) for n in order]
        return jnp.concatenate(segs + ([jnp.zeros((fill_small, FLAT_COLS), F32)] if fill_small else []), axis=0)

    def flat_big(prefix):
        segs = [_as_rows(p, 0, BF16_ROWS) for p in local_pieces(prefix)]
        return jnp.concatenate(segs + ([jnp.zeros((fill_big, FLAT_COLS), F32)] if fill_big else []), axis=0)

    upd_small = _adamw(recv_small, flat_small(""), flat_small("m_"), flat_small("v_"), "adamw_small")
    upd_big = _adamw(recv_big, flat_big(""), flat_big("m_"), flat_big("v_"), "adamw")
    outs = {}
    for tag, f_small, f_big in zip(("grad", "delta", "new_m", "new_v"), upd_small, upd_big):
        outs[tag], r = {}, 0
        for n in order:
            outs[tag][n], r = _from_rows(f_small, r, given[n].shape, 0, F32_ROWS)
        pieces, r = [], 0
        for shp in piece_shapes:
            p, r = _from_rows(f_big, r, shp, 0, BF16_ROWS)
            pieces.append(p)
        for n in big_names:
            outs[tag][n] = _from_pieces(pieces, n)

    loss = lax.psum(loss_b[0, 0], ("x", "y", "c"))
    result = [loss, grad_x[None]]
    for tag in ("grad", "delta", "new_m", "new_v"):
        result += [outs[tag][n] for n in WEIGHTS]
    return tuple(result)
```

```python
import functools
import math

import jax
import jax.numpy as jnp
from jax import lax
from jax.experimental import pallas as pl
from jax.experimental.pallas import tpu as pltpu

F32 = jnp.float32
BF16 = jnp.bfloat16

N_DEV = 8
LANES = 128
N_META = 16
PAD_ROWS = LANES - N_META
GDN_CHUNK = 64
GDN_HEAD_DIM = 128
SB_HEAD_DIM = 64
SB_FAR = 11
SB_BWD_SUM_PARTS = 1
CONV_WIDTH = 4
NORM_EPS = 1e-6
FLAT_COLS = 1024
FLAT_ROW_BLOCK = 64
VMEM_LIMIT = 56 * 1024 * 1024

ADAM_LR = 0.001
ADAM_B1 = 0.9
ADAM_B2 = 0.999
ADAM_EPS = 1e-08
ADAM_WD = 0.01
ADAM_STEP = 10

NN = (((1,), (0,)), ((), ()))
NT = (((1,), (1,)), ((), ()))
TN = (((0,), (0,)), ((), ()))


def _params(*sem):
    return pltpu.CompilerParams(dimension_semantics=sem, vmem_limit_bytes=VMEM_LIMIT)


def _pick(n, cands):
    for c in cands:
        if n % c == 0:
            return c
    return n


def _bdot(a, b, dims=NN):
    return lax.dot_general(a.astype(BF16), b.astype(BF16), dims, preferred_element_type=F32)


def _hdot(a, b, dims=NN):
    return lax.dot_general(a, b, dims, preferred_element_type=F32, precision=lax.Precision.HIGH)


def _split_dot(a, m, parts=2):
    out = None
    for _ in range(parts):
        piece = a.astype(BF16)
        a = a - piece.astype(F32)
        term = lax.dot_general(piece, m, NN, preferred_element_type=F32)
        out = term if out is None else out + term
    return out


def _lockstep(gens):
    while gens:
        alive = []
        for g in gens:
            try:
                next(g)
                alive.append(g)
            except StopIteration:
                pass
        gens = alive


def _sigmoid(x):
    return 1.0 / (1.0 + jnp.exp(-x))


def _iota2(shape, axis):
    return lax.broadcasted_iota(jnp.int32, shape, axis)


def _matmul(a, b, mode, name, out_dtype=None, add=None):
    if out_dtype is None:
        out_dtype = BF16 if mode == "tn" else F32
    if mode == "nn":
        (m, k), n = a.shape, b.shape[1]
    elif mode == "nt":
        (m, k), n = a.shape, b.shape[0]
    else:
        (k, m), n = a.shape, b.shape[1]
    row_c = (640, 512, 384, 256, 128)
    col_c = (1024, 1408, 768, 512, 384, 256, 128)
    if mode == "tn":
        bm, bn, bk = _pick(m, col_c), _pick(n, col_c), _pick(k, row_c)
    else:
        bm, bn, bk = _pick(m, row_c), _pick(n, col_c), _pick(k, (1024, 1408, 768, 512, 256, 128))
    nk = k // bk
    ni, nj = m // bm, n // bn
    dims = {"nn": NN, "nt": NT, "tn": TN}[mode]
    a_bytes, b_bytes = a.size * a.dtype.itemsize, b.size * b.dtype.itemsize
    rows_outer = (a_bytes * (1 if nk == 1 else nj) + b_bytes * ni
                  <= b_bytes * (1 if nk == 1 else ni) + a_bytes * nj)

    def at(f):
        return (lambda g0, g1, q: f(g0, g1, q)) if rows_outer else (lambda g0, g1, q: f(g1, g0, q))

    a_spec = {"nn": pl.BlockSpec((bm, bk), at(lambda i, j, q: (i, q))),
              "nt": pl.BlockSpec((bm, bk), at(lambda i, j, q: (i, q))),
              "tn": pl.BlockSpec((bk, bm), at(lambda i, j, q: (q, i)))}[mode]
    b_spec = {"nn": pl.BlockSpec((bk, bn), at(lambda i, j, q: (q, j))),
              "nt": pl.BlockSpec((bn, bk), at(lambda i, j, q: (j, q))),
              "tn": pl.BlockSpec((bk, bn), at(lambda i, j, q: (q, j)))}[mode]
    o_spec = pl.BlockSpec((bm, bn), at(lambda i, j, q: (i, j)))
    has_add = add is not None

    def body(*refs):
        if has_add:
            a_ref, b_ref, add_ref, o_ref, acc_ref = refs
        else:
            a_ref, b_ref, o_ref, acc_ref = refs
        q = pl.program_id(2)
        part = _bdot(a_ref[...], b_ref[...], dims)

        @pl.when(q == 0)
        def _():
            acc_ref[...] = part

        @pl.when(q > 0)
        def _():
            acc_ref[...] += part

        @pl.when(q == nk - 1)
        def _():
            r = acc_ref[...]
            if has_add:
                r = r + add_ref[...]
            o_ref[...] = r.astype(out_dtype)

    ins = [a, b] + ([add] if has_add else [])
    in_specs = [a_spec, b_spec] + ([o_spec] if has_add else [])
    return pl.pallas_call(
        body, name=name, grid=(ni, nj, nk) if rows_outer else (nj, ni, nk),
        in_specs=in_specs, out_specs=o_spec,
        out_shape=jax.ShapeDtypeStruct((m, n), out_dtype),
        scratch_shapes=[pltpu.VMEM((bm, bn), F32)],
        compiler_params=_params("parallel", "parallel", "arbitrary"),
    )(*ins)


def _rmsnorm(h, gains, name):
    m, d = h.shape
    bm = _pick(m, (640, 384, 128))
    ng = len(gains)

    def body(*refs):
        h_ref, g_refs, o_refs = refs[0], refs[1:1 + ng], refs[1 + ng:]
        x = h_ref[...]
        xhat = x * lax.rsqrt(jnp.mean(x * x, axis=-1, keepdims=True) + NORM_EPS)
        for g_ref, o_ref in zip(g_refs, o_refs):
            o_ref[...] = (xhat * g_ref[...]).astype(BF16)

    row = pl.BlockSpec((bm, d), lambda i: (i, 0))
    gain = pl.BlockSpec((1, d), lambda i: (0, 0))
    return pl.pallas_call(
        body, name=name, grid=(m // bm,),
        in_specs=[row] + [gain] * ng, out_specs=[row] * ng,
        out_shape=[jax.ShapeDtypeStruct((m, d), BF16)] * ng,
        compiler_params=_params("parallel"),
    )(h, *gains)


def _rmsnorm_bwd(h, gains, dns, dres, name):
    m, d = h.shape
    bm = _pick(m, (640, 384, 128))
    ng = len(gains)
    nsteps = m // bm

    def body(*refs):
        h_ref, dres_ref = refs[0], refs[1]
        g_refs = refs[2:2 + ng]
        dn_refs = refs[2 + ng:2 + 2 * ng]
        dh_ref = refs[2 + 2 * ng]
        dg_refs = refs[3 + 2 * ng:3 + 3 * ng]
        acc_refs = refs[3 + 3 * ng:]
        i = pl.program_id(0)
        x = h_ref[...]
        r = lax.rsqrt(jnp.mean(x * x, axis=-1, keepdims=True) + NORM_EPS)
        xhat = x * r
        dh = dres_ref[...]
        for g_ref, dn_ref, dg_ref, acc_ref in zip(g_refs, dn_refs, dg_refs, acc_refs):
            dn = dn_ref[...]
            dy = dn * g_ref[...]
            dh = dh + r * (dy - xhat * jnp.mean(dy * xhat, axis=-1, keepdims=True))
            part = jnp.sum((dn * xhat).reshape(bm // 8, 8, d), axis=0)

            @pl.when(i == 0)
            def _():
                acc_ref[...] = part

            @pl.when(i > 0)
            def _():
                acc_ref[...] += part

            @pl.when(i == nsteps - 1)
            def _():
                dg_ref[...] = jnp.sum(acc_ref[...], axis=0, keepdims=True)

        dh_ref[...] = dh

    row = pl.BlockSpec((bm, d), lambda i: (i, 0))
    gain = pl.BlockSpec((1, d), lambda i: (0, 0))
    outs = pl.pallas_call(
        body, name=name, grid=(nsteps,),
        in_specs=[row, row] + [gain] * ng + [row] * ng,
        out_specs=[row] + [gain] * ng,
        out_shape=[jax.ShapeDtypeStruct((m, d), F32)] + [jax.ShapeDtypeStruct((1, d), F32)] * ng,
        scratch_shapes=[pltpu.VMEM((8, d), F32)] * ng,
        compiler_params=_params("arbitrary"),
    )(h, dres, *gains, *dns)
    return outs[0], list(outs[1:])


def _loss_head(h, gain, target):
    m, d = h.shape
    nsteps = m // LANES

    def body(h_ref, g_ref, t_ref, loss_ref, dh_ref, dg_ref, lacc, gacc):
        i = pl.program_id(0)

        @pl.when(i == 0)
        def _():
            lacc[...] = jnp.zeros_like(lacc)
            gacc[...] = jnp.zeros_like(gacc)
            dh_ref[...] = jnp.zeros_like(dh_ref)

        @pl.when(i > 0)
        def _():
            x = h_ref[...]
            r = lax.rsqrt(jnp.mean(x * x, axis=-1, keepdims=True) + NORM_EPS)
            xhat = x * r
            g = g_ref[...]
            err = xhat * g - t_ref[...]
            lacc[...] += jnp.sum((err * err).reshape(LANES // 8, 8, d), axis=0)
            dn = err * (1.0 / d)
            dy = dn * g
            dh_ref[...] = r * (dy - xhat * jnp.mean(dy * xhat, axis=-1, keepdims=True))
            gacc[...] += jnp.sum((dn * xhat).reshape(LANES // 8, 8, d), axis=0)

        @pl.when(i == nsteps - 1)
        def _():
            tot = jnp.sum(jnp.sum(lacc[...], axis=1, keepdims=True), axis=0, keepdims=True)
            loss_ref[...] = jnp.broadcast_to(tot * (0.5 / d), (1, LANES))
            dg_ref[...] = jnp.sum(gacc[...], axis=0, keepdims=True)

    row = pl.BlockSpec((LANES, d), lambda i: (i, 0))
    trow = pl.BlockSpec((LANES, d), lambda i: (jnp.maximum(i - 1, 0), 0))
    gain_spec = pl.BlockSpec((1, d), lambda i: (0, 0))
    return pl.pallas_call(
        body, name="loss_head", grid=(nsteps,),
        in_specs=[row, gain_spec, trow],
        out_specs=[pl.BlockSpec((1, LANES), lambda i: (0, 0)), row, gain_spec],
        out_shape=[jax.ShapeDtypeStruct((1, LANES), F32), jax.ShapeDtypeStruct((m, d), F32),
                   jax.ShapeDtypeStruct((1, d), F32)],
        scratch_shapes=[pltpu.VMEM((8, d), F32), pltpu.VMEM((8, d), F32)],
        compiler_params=_params("arbitrary"),
    )(h, gain, target)


def _swiglu(gu, name):
    m, f2 = gu.shape
    f = f2 // 2

    def body(gu_ref, o_ref):
        g = gu_ref[:, :f].astype(F32)
        u = gu_ref[:, f:].astype(F32)
        o_ref[...] = (g * _sigmoid(g) * u).astype(BF16)

    return pl.pallas_call(
        body, name=name, grid=(m // LANES,),
        in_specs=[pl.BlockSpec((LANES, f2), lambda i: (i, 0))],
        out_specs=pl.BlockSpec((LANES, f), lambda i: (i, 0)),
        out_shape=jax.ShapeDtypeStruct((m, f), BF16),
        compiler_params=_params("parallel"),
    )(gu)


def _swiglu_bwd(gu, da, name):
    m, f2 = gu.shape
    f = f2 // 2

    def body(gu_ref, da_ref, o_ref):
        g = gu_ref[:, :f].astype(F32)
        u = gu_ref[:, f:].astype(F32)
        da_ = da_ref[...]
        s = _sigmoid(g)
        o_ref[:, :f] = (da_ * u * s * (1.0 + g * (1.0 - s))).astype(BF16)
        o_ref[:, f:] = (da_ * g * s).astype(BF16)

    return pl.pallas_call(
        body, name=name, grid=(m // LANES,),
        in_specs=[pl.BlockSpec((LANES, f2), lambda i: (i, 0)), pl.BlockSpec((LANES, f), lambda i: (i, 0))],
        out_specs=pl.BlockSpec((LANES, f2), lambda i: (i, 0)),
        out_shape=jax.ShapeDtypeStruct((m, f2), BF16),
        compiler_params=_params("parallel"),
    )(gu, da)


def _conv_taps(xa, w_ref):
    acc = xa * w_ref[CONV_WIDTH - 1:CONV_WIDTH, :]
    for i in range(CONV_WIDTH - 1):
        acc = acc + pltpu.roll(xa, CONV_WIDTH - 1 - i, 0) * w_ref[i:i + 1, :]
    return acc[8:]


def _gdn_conv(proj, conv_w, n_heads):
    m, w3 = proj.shape
    wd = w3 // 3
    nblk = m // LANES
    unroll = _pick(nblk - 1, (4, 2, 1))

    def body(x_ref, w_ref, o_ref):
        c = pl.program_id(0)
        is_qk = c < 2 * n_heads

        def block(b, xa=None):
            r0 = pl.multiple_of(b * LANES, LANES)
            if xa is None:
                xa = x_ref[pl.ds(r0 - 8, LANES + 8), :]
            cv = _conv_taps(xa, w_ref)
            s = cv * _sigmoid(cv)
            ss = jnp.sum(s * s, axis=-1, keepdims=True)
            yield
            o_ref[pl.ds(r0, LANES), :] = jnp.where(is_qk, s * lax.rsqrt(ss + NORM_EPS), s)

        x0 = jnp.concatenate([jnp.zeros((8, LANES), F32), x_ref[pl.ds(0, LANES), :]], axis=0)
        _lockstep([block(0, x0)])

        def step(j, carry):
            _lockstep([block(1 + unroll * j + u) for u in range(unroll)])
            return carry

        lax.fori_loop(0, (nblk - 1) // unroll, step, 0)

    return pl.pallas_call(
        body, name="gdn_conv", grid=(w3 // LANES,),
        in_specs=[pl.BlockSpec((m, LANES), lambda c: (0, c)), pl.BlockSpec((CONV_WIDTH, LANES), lambda c: (0, c))],
        out_specs=pl.BlockSpec((None, m, LANES), lambda c: (c // n_heads, 0, c % n_heads)),
        out_shape=jax.ShapeDtypeStruct((3, m, wd), F32),
        compiler_params=_params("parallel"),
    )(proj, conv_w)


def _gdn_conv_bwd(proj, conv_w, dact, n_heads):
    m, w3 = proj.shape
    nblk = m // LANES
    unroll = _pick(nblk - 1, (4, 2, 1))

    def body(x_ref, w_ref, dy_ref, dx_ref, dw_ref, dc_ref):
        c = pl.program_id(0)
        is_qk = c < 2 * n_heads
        dc_ref[pl.ds(m, 8), :] = jnp.zeros((8, LANES), F32)

        def block(b, out, xa=None):
            r0 = pl.multiple_of(b * LANES, LANES)
            if xa is None:
                xa = x_ref[pl.ds(r0 - 8, LANES + 8), :]
            cv = _conv_taps(xa, w_ref)
            sg = _sigmoid(cv)
            s = cv * sg
            dy = dy_ref[pl.ds(r0, LANES), :]
            ss = jnp.sum(s * s, axis=-1, keepdims=True)
            yield
            rn = lax.rsqrt(ss + NORM_EPS)
            yn = s * rn
            proj_len = jnp.sum(dy * yn, axis=-1, keepdims=True)
            yield
            ds = jnp.where(is_qk, rn * (dy - yn * proj_len), dy)
            dcv = ds * sg * (1.0 + cv * (1.0 - sg))
            dc_ref[pl.ds(r0, LANES), :] = dcv
            parts = []
            for i in range(CONV_WIDTH):
                xs = xa[8:] if i == CONV_WIDTH - 1 else pltpu.roll(xa, CONV_WIDTH - 1 - i, 0)[8:]
                parts.append(jnp.sum((dcv * xs).reshape(LANES // 8, 8, LANES), axis=0))
            out.append(parts)

        def add_parts(acc, outs):
            for parts in outs:
                acc = tuple(a + p for a, p in zip(acc, parts))
            return acc

        x0 = jnp.concatenate([jnp.zeros((8, LANES), F32), x_ref[pl.ds(0, LANES), :]], axis=0)
        first = []
        _lockstep([block(0, first, x0)])
        dwacc = tuple(first[0])

        def step1(j, acc):
            outs = []
            _lockstep([block(1 + unroll * j + u, outs) for u in range(unroll)])
            return add_parts(acc, outs)

        dwacc = lax.fori_loop(0, (nblk - 1) // unroll, step1, dwacc)
        for i in range(CONV_WIDTH):
            dw_ref[i:i + 1, :] = jnp.sum(dwacc[i], axis=0, keepdims=True)

        def step2(b, carry):
            r0 = pl.multiple_of(b * LANES, LANES)
            da = dc_ref[pl.ds(r0, LANES + 8), :]
            acc = da * w_ref[CONV_WIDTH - 1:CONV_WIDTH, :]
            for i in range(CONV_WIDTH - 1):
                sh = CONV_WIDTH - 1 - i
                acc = acc + pltpu.roll(da, LANES + 8 - sh, 0) * w_ref[i:i + 1, :]
            dx_ref[pl.ds(r0, LANES), :] = acc[:LANES]
            return carry

        lax.fori_loop(0, nblk, step2, 0)

    return pl.pallas_call(
        body, name="gdn_conv_bwd", grid=(w3 // LANES,),
        in_specs=[pl.BlockSpec((m, LANES), lambda c: (0, c)),
                  pl.BlockSpec((CONV_WIDTH, LANES), lambda c: (0, c)),
                  pl.BlockSpec((None, m, LANES), lambda c: (c // n_heads, 0, c % n_heads))],
        out_specs=[pl.BlockSpec((m, LANES), lambda c: (0, c)), pl.BlockSpec((CONV_WIDTH, LANES), lambda c: (0, c))],
        out_shape=[jax.ShapeDtypeStruct((m, w3), F32), jax.ShapeDtypeStruct((CONV_WIDTH, w3), F32)],
        scratch_shapes=[pltpu.VMEM((m + 8, LANES), F32)],
        compiler_params=_params("parallel"),
    )(proj, conv_w, dact)


def _softplus(x):
    return jnp.maximum(x, 0.0) + jnp.log(1.0 + jnp.exp(-jnp.abs(x)))


def _gdn_gates(ab, a_log, dt_bias, n_heads):
    m = ab.shape[0]
    bm = _pick(m, (640, 384, 128))

    def body(ab_ref, al_ref, dt_ref, g_ref, b_ref):
        i = pl.program_id(0)
        x = ab_ref[...]
        live = (_iota2((bm, LANES), 0) + i * bm) >= PAD_ROWS
        g = jnp.where(live, -jnp.exp(al_ref[...]) * _softplus(x + dt_ref[...]), 0.0)
        beta = jnp.where(live, _sigmoid(x), 0.0)
        for h in range(n_heads):
            g_ref[:, h * LANES:(h + 1) * LANES] = jnp.broadcast_to(g[:, h:h + 1], (bm, LANES))
            b_ref[:, h * LANES:(h + 1) * LANES] = jnp.broadcast_to(
                beta[:, n_heads + h:n_heads + h + 1], (bm, LANES))

    wide = pl.BlockSpec((bm, n_heads * LANES), lambda i: (i, 0))
    return pl.pallas_call(
        body, name="gdn_gates", grid=(m // bm,),
        in_specs=[pl.BlockSpec((bm, LANES), lambda i: (i, 0))] + [pl.BlockSpec((1, LANES), lambda i: (0, 0))] * 2,
        out_specs=[wide, wide],
        out_shape=[jax.ShapeDtypeStruct((m, n_heads * LANES), F32)] * 2,
        compiler_params=_params("parallel"),
    )(ab, a_log, dt_bias)


def _gdn_gates_bwd(ab, a_log, dt_bias, dg_b, dbeta_b, n_heads):
    m = ab.shape[0]
    bm = _pick(m, (640, 384, 128))
    nsteps = m // bm

    def body(ab_ref, al_ref, dt_ref, dg_ref, db_ref, dab_ref, dal_ref, ddt_ref, acc_al, acc_dt):
        i = pl.program_id(0)
        x = ab_ref[...]
        lane = _iota2((bm, LANES), 1)
        live = (_iota2((bm, LANES), 0) + i * bm) >= PAD_ROWS
        dg = jnp.zeros((bm, LANES), F32)
        dbeta = jnp.zeros((bm, LANES), F32)
        for h in range(n_heads):
            dg = jnp.where(lane == h, dg_ref[:, h * LANES:(h + 1) * LANES], dg)
            dbeta = jnp.where(lane == n_heads + h, db_ref[:, h * LANES:(h + 1) * LANES], dbeta)
        dg = jnp.where(live, dg, 0.0)
        dbeta = jnp.where(live, dbeta, 0.0)
        nea = -jnp.exp(al_ref[...])
        pre = x + dt_ref[...]
        d_pre = dg * nea * _sigmoid(pre)
        beta = _sigmoid(x)
        dab_ref[...] = d_pre + dbeta * beta * (1.0 - beta)
        p_al = jnp.sum((dg * nea * _softplus(pre)).reshape(bm // 8, 8, LANES), axis=0)
        p_dt = jnp.sum(d_pre.reshape(bm // 8, 8, LANES), axis=0)

        @pl.when(i == 0)
        def _():
            acc_al[...] = p_al
            acc_dt[...] = p_dt

        @pl.when(i > 0)
        def _():
            acc_al[...] += p_al
            acc_dt[...] += p_dt

        @pl.when(i == nsteps - 1)
        def _():
            dal_ref[...] = jnp.sum(acc_al[...], axis=0, keepdims=True)
            ddt_ref[...] = jnp.sum(acc_dt[...], axis=0, keepdims=True)

    wide = pl.BlockSpec((bm, n_heads * LANES), lambda i: (i, 0))
    one = pl.BlockSpec((1, LANES), lambda i: (0, 0))
    nar = pl.BlockSpec((bm, LANES), lambda i: (i, 0))
    return pl.pallas_call(
        body, name="gdn_gates_bwd", grid=(nsteps,),
        in_specs=[nar, one, one, wide, wide],
        out_specs=[nar, one, one],
        out_shape=[jax.ShapeDtypeStruct((m, LANES), F32), jax.ShapeDtypeStruct((1, LANES), F32),
                   jax.ShapeDtypeStruct((1, LANES), F32)],
        scratch_shapes=[pltpu.VMEM((8, LANES), F32)] * 2,
        compiler_params=_params("arbitrary"),
    )(ab, a_log, dt_bias, dg_b, dbeta_b)


def _chunk_masks():
    c = GDN_CHUNK
    ri, ci = _iota2((c, c), 0), _iota2((c, c), 1)
    return ri >= ci, ri > ci, ri == ci


def _chunk_decay(gbs):
    c = GDN_CHUNK
    incl, _, _ = _chunk_masks()
    inclf = incl.astype(F32)
    ones = jnp.ones((c, LANES), F32)
    g_cums = [_hdot(inclf, gb) for gb in gbs]
    g_rows = [_hdot(ones, g_cum, NT) * (1.0 / LANES) for g_cum in g_cums]
    dmats = [jnp.where(incl, jnp.exp(jnp.where(incl, g_cum[:, :c] - g_row, 0.0)), 0.0)
             for g_cum, g_row in zip(g_cums, g_rows)]
    return g_cums, dmats


def _unit_lower_inverse(mats):
    _, _, eye = _chunk_masks()
    xs = [eye.astype(F32) - a for a in mats]
    ps = list(mats)
    for _ in range(int(math.log2(GDN_CHUNK)) - 1):
        ps = [_hdot(p, p) for p in ps]
        xs = [x + _hdot(x, p) for x, p in zip(xs, ps)]
    return xs


def _gdn_prep(qkv, g_b, beta_b, n_heads):
    _, m, wd = qkv.shape
    c = GDN_CHUNK
    nc = m // c
    cb = _pick(nc, (10, 5, 3, 2))
    rows = cb * c
    qscale = GDN_HEAD_DIM ** -0.5

    def body(qkv_ref, g_ref, b_ref, ub_ref, wc_ref, qd_ref, kd_ref, gc_ref, p_ref, t_ref):
        _, strict, _ = _chunk_masks()
        sls = [pl.ds(s * c, c) for s in range(cb)]
        g_cums, dmats = _chunk_decay([g_ref[sl, :] for sl in sls])
        kks = [_bdot(qkv_ref[1, sl, :], qkv_ref[1, sl, :], NT) for sl in sls]
        ts = _unit_lower_inverse([jnp.where(strict, b_ref[sl, :][:, :c] * dmat * kk, 0.0)
                                  for sl, dmat, kk in zip(sls, dmats, kks)])
        for s, (sl, g_cum, dmat, t) in enumerate(zip(sls, g_cums, dmats, ts)):
            q = qkv_ref[0, sl, :] * qscale
            k = qkv_ref[1, sl, :]
            bb = b_ref[sl, :]
            gam = jnp.exp(g_cum)
            ub_ref[sl, :] = _hdot(t, bb * qkv_ref[2, sl, :])
            wc_ref[sl, :] = _hdot(t, bb * gam * k)
            p_ref[s] = _bdot(q, k, NT) * dmat
            qd_ref[sl, :] = q * gam
            kd_ref[sl, :] = k * jnp.exp(g_cum[c - 1:c, :] - g_cum)
            gc_ref[sl, :] = g_cum
            t_ref[s] = t

    blk = pl.BlockSpec((rows, LANES), lambda h, n: (n, h))
    sq = pl.BlockSpec((None, cb, c, c), lambda h, n: (h, n, 0, 0))
    big = jax.ShapeDtypeStruct((m, wd), F32)
    small = jax.ShapeDtypeStruct((n_heads, nc, c, c), F32)
    return pl.pallas_call(
        body, name="gdn_prep", grid=(n_heads, nc // cb),
        in_specs=[pl.BlockSpec((3, rows, LANES), lambda h, n: (0, n, h)), blk, blk],
        out_specs=[blk] * 5 + [sq, sq],
        out_shape=[big] * 5 + [small, small],
        compiler_params=_params("parallel", "parallel"),
    )(qkv, g_b, beta_b)


def _gdn_prep_bwd(qkv, g_b, beta_b, ub, wc, tinv, dub, dwc, dqd, dkd, dp, dgl, n_heads):
    _, m, wd = qkv.shape
    c = GDN_CHUNK
    nc = m // c
    cb = _pick(nc, (10, 5, 3, 2))
    rows = cb * c
    qscale = GDN_HEAD_DIM ** -0.5

    def body(qkv_ref, g_ref, b_ref, ub_ref, wc_ref, t_ref, dub_ref, dwc_ref, dqd_ref, dkd_ref, dp_ref, dgl_ref,
             dqkv_ref, dg_ref, db_ref):
        incl, strict, _ = _chunk_masks()
        ones_c = jnp.ones((c, LANES), F32)
        ones_l = jnp.ones((LANES, LANES), F32)
        last_row = _iota2((c, LANES), 0) == c - 1

        def chunk(s):
            sl = pl.ds(s * c, c)
            q = qkv_ref[0, sl, :] * qscale
            k = qkv_ref[1, sl, :]
            v = qkv_ref[2, sl, :]
            bb = b_ref[sl, :]
            bcol = bb[:, :c]
            t = t_ref[s]
            g_cum = _hdot(incl.astype(F32), g_ref[sl, :])
            kk = _bdot(k, k, NT)
            qk = _bdot(q, k, NT)
            drhs_u = _hdot(t, dub_ref[sl, :], TN)
            drhs_w = _hdot(t, dwc_ref[sl, :], TN)
            yield
            g_row = _hdot(ones_c, g_cum, NT) * (1.0 / LANES)
            da = -jnp.where(strict, _bdot(drhs_u, ub_ref[sl, :], NT) + _bdot(drhs_w, wc_ref[sl, :], NT), 0.0)
            dbeta = _hdot(drhs_u * v, ones_l)
            tk = _hdot(drhs_w * k, ones_l)
            dqd = dqd_ref[sl, :]
            dgam = _hdot(dqd * q, ones_l)
            yield
            dmat = jnp.where(incl, jnp.exp(jnp.where(incl, g_cum[:, :c] - g_row, 0.0)), 0.0)
            gam = jnp.exp(g_cum)
            edec = jnp.exp(g_cum[c - 1:c, :] - g_cum)
            dkd = dkd_ref[sl, :]
            skd = _hdot(dkd * k * edec, ones_l)
            dbeta = dbeta + _hdot(da * dmat * kk, ones_c)
            dkk = da * bcol * dmat
            dpm = jnp.where(incl, dp_ref[s], 0.0)
            dpd = dpm * dmat
            dk = _bdot(dkk, k) + _bdot(dkk, k, TN) + _bdot(dpd, q, TN)
            dq = _bdot(dpd, k)
            e = (da * bcol * kk + dpm * qk) * dmat
            yield
            dk = dk + bb * gam * drhs_w + dkd * edec
            dq = dq + dqd * gam
            dbeta = dbeta + tk * gam
            dgam = dgam + bb * tk
            dgc = dgam * gam - skd + _hdot(e, ones_c) - _hdot(e, ones_c, TN)
            tot = _hdot(jnp.ones((c, c), F32), skd)
            yield
            dgc = dgc + jnp.where(last_row, tot + dgl_ref[s, 0:1, :], 0.0)
            dg_ref[sl, :] = _hdot((_iota2((c, c), 0) <= _iota2((c, c), 1)).astype(F32), dgc)
            db_ref[sl, :] = dbeta
            dqkv_ref[0, sl, :] = dq * qscale
            dqkv_ref[1, sl, :] = dk
            dqkv_ref[2, sl, :] = bb * drhs_u

        _lockstep([chunk(s) for s in range(cb)])

    blk = pl.BlockSpec((rows, LANES), lambda h, n: (n, h))
    blk3 = pl.BlockSpec((3, rows, LANES), lambda h, n: (0, n, h))
    sq = pl.BlockSpec((None, cb, c, c), lambda h, n: (h, n, 0, 0))
    sgl = pl.BlockSpec((None, cb, 8, LANES), lambda h, n: (h, n, 0, 0))
    big = jax.ShapeDtypeStruct((m, wd), F32)
    return pl.pallas_call(
        body, name="gdn_prep_bwd", grid=(n_heads, nc // cb),
        in_specs=[blk3, blk, blk, blk, blk, sq, blk, blk, blk, blk, sq, sgl],
        out_specs=[blk3, blk, blk],
        out_shape=[jax.ShapeDtypeStruct((3, m, wd), F32), big, big],
        compiler_params=_params("parallel", "parallel"),
    )(qkv, g_b, beta_b, ub, wc, tinv, dub, dwc, dqd, dkd, dp, dgl)


def _gdn_scan(ub, wc, qd, kd, gc, p, n_heads):
    m, wd = ub.shape
    c = GDN_CHUNK
    nc = m // c
    dh = GDN_HEAD_DIM

    def body(ub_ref, wc_ref, qd_ref, kd_ref, gc_ref, p_ref, o_ref, ssave_ref, s_ref):
        n = pl.program_id(0)

        @pl.when(n == 0)
        def _():
            s_ref[...] = jnp.zeros_like(s_ref)

        def head(h):
            sl = slice(h * LANES, (h + 1) * LANES)
            s = s_ref[h]
            ssave_ref[h] = s
            ws = _bdot(wc_ref[:, sl], s)
            qs = _bdot(qd_ref[:, sl], s)
            yield
            u = ub_ref[:, sl] - ws
            pu = _bdot(p_ref[h], u)
            ku = _bdot(kd_ref[:, sl], u, TN)
            yield
            o_ref[:, sl] = qs + pu
            s_ref[h] = s * jnp.exp(gc_ref[c - 1:c, sl]) + ku

        _lockstep([head(h) for h in range(n_heads)])

    blk = pl.BlockSpec((c, wd), lambda n: (n, 0))
    return pl.pallas_call(
        body, name="gdn_scan", grid=(nc,),
        in_specs=[blk] * 5 + [pl.BlockSpec((n_heads, None, c, c), lambda n: (0, n, 0, 0))],
        out_specs=[blk, pl.BlockSpec((n_heads, None, dh, dh), lambda n: (0, n, 0, 0))],
        out_shape=[jax.ShapeDtypeStruct((m, wd), F32), jax.ShapeDtypeStruct((n_heads, nc, dh, dh), F32)],
        scratch_shapes=[pltpu.VMEM((n_heads, dh, dh), F32)],
        compiler_params=_params("arbitrary"),
    )(ub, wc, qd, kd, gc, p)


def _gdn_scan_bwd(do, ssave, ub, wc, qd, kd, gc, p, n_heads):
    m, wd = ub.shape
    c = GDN_CHUNK
    nc = m // c
    dh = GDN_HEAD_DIM

    def body(do_ref, ss_ref, ub_ref, wc_ref, qd_ref, kd_ref, gc_ref, p_ref,
             dub_ref, dwc_ref, dqd_ref, dkd_ref, dp_ref, dgl_ref, ds_ref):
        n = pl.program_id(0)

        @pl.when(n == 0)
        def _():
            ds_ref[...] = jnp.zeros_like(ds_ref)

        def head(h):
            sl = slice(h * LANES, (h + 1) * LANES)
            ds_next = ds_ref[h]
            s = ss_ref[h]
            do_ = do_ref[:, sl]
            wcv = wc_ref[:, sl]
            kdv = kd_ref[:, sl]
            ws = _bdot(wcv, s)
            du = _bdot(p_ref[h], do_, TN) + _bdot(kdv, ds_next)
            dqd_ref[:, sl] = _bdot(do_, s, NT)
            col = _hdot(jnp.ones((8, dh), F32), s * ds_next)
            qdo = _bdot(qd_ref[:, sl], do_, TN)
            yield
            u = ub_ref[:, sl] - ws
            egl = jnp.exp(gc_ref[c - 1:c, sl])
            dub_ref[:, sl] = du
            dwc_ref[:, sl] = -_bdot(du, s, NT)
            dp_ref[h] = _bdot(do_, u, NT)
            dkd_ref[:, sl] = _bdot(u, ds_next, NT)
            dgl_ref[h] = egl * _hdot(col, jnp.ones((LANES, LANES), F32))
            ds_ref[h] = qdo + egl * ds_next - _bdot(wcv, du, TN)

        _lockstep([head(h) for h in range(n_heads)])

    blk = pl.BlockSpec((c, wd), lambda n: (nc - 1 - n, 0))
    sq = pl.BlockSpec((n_heads, None, c, c), lambda n: (0, nc - 1 - n, 0, 0))
    big = jax.ShapeDtypeStruct((m, wd), F32)
    return pl.pallas_call(
        body, name="gdn_scan_bwd", grid=(nc,),
        in_specs=[blk, pl.BlockSpec((n_heads, None, dh, dh), lambda n: (0, nc - 1 - n, 0, 0))] + [blk] * 5 + [sq],
        out_specs=[blk] * 4 + [sq, pl.BlockSpec((n_heads, None, 8, LANES), lambda n: (0, nc - 1 - n, 0, 0))],
        out_shape=[big] * 4 + [jax.ShapeDtypeStruct((n_heads, nc, c, c), F32),
                               jax.ShapeDtypeStruct((n_heads, nc, 8, LANES), F32)],
        scratch_shapes=[pltpu.VMEM((n_heads, dh, dh), F32)],
        compiler_params=_params("arbitrary"),
    )(do, ssave, ub, wc, qd, kd, gc, p)


def _gdn_outnorm(o, gate, gain, n_heads):
    m, wd = o.shape
    bm = _pick(m, (640, 384, 128))

    def body(o_ref, gt_ref, gn_ref, y_ref):
        gn = gn_ref[...]
        for h in range(n_heads):
            sl = slice(h * LANES, (h + 1) * LANES)
            x = o_ref[:, sl]
            gt = gt_ref[:, sl]
            r = lax.rsqrt(jnp.mean(x * x, axis=-1, keepdims=True) + NORM_EPS)
            y_ref[:, sl] = (x * r * gn * gt * _sigmoid(gt)).astype(BF16)

    row = pl.BlockSpec((bm, wd), lambda i: (i, 0))
    return pl.pallas_call(
        body, name="gdn_outnorm", grid=(m // bm,),
        in_specs=[row, row, pl.BlockSpec((1, LANES), lambda i: (0, 0))], out_specs=row,
        out_shape=jax.ShapeDtypeStruct((m, wd), BF16),
        compiler_params=_params("parallel"),
    )(o, gate, gain)


def _gdn_outnorm_bwd(o, gate, gain, dy, n_heads):
    m, wd = o.shape
    bm = _pick(m, (640, 384, 128))
    nsteps = m // bm

    def body(o_ref, gt_ref, gn_ref, dy_ref, do_ref, dgt_ref, dgn_ref, acc_ref):
        i = pl.program_id(0)
        gn = gn_ref[...]
        part = jnp.zeros((8, LANES), F32)
        for h in range(n_heads):
            sl = slice(h * LANES, (h + 1) * LANES)
            x = o_ref[:, sl]
            gt = gt_ref[:, sl]
            d_out = dy_ref[:, sl]
            r = lax.rsqrt(jnp.mean(x * x, axis=-1, keepdims=True) + NORM_EPS)
            xhat = x * r
            sg = _sigmoid(gt)
            dz = d_out * gt * sg
            dgt_ref[:, sl] = d_out * xhat * gn * sg * (1.0 + gt * (1.0 - sg))
            dxh = dz * gn
            do_ref[:, sl] = r * (dxh - xhat * jnp.mean(dxh * xhat, axis=-1, keepdims=True))
            part = part + jnp.sum((dz * xhat).reshape(bm // 8, 8, LANES), axis=0)

        @pl.when(i == 0)
        def _():
            acc_ref[...] = part

        @pl.when(i > 0)
        def _():
            acc_ref[...] += part

        @pl.when(i == nsteps - 1)
        def _():
            dgn_ref[...] = jnp.sum(acc_ref[...], axis=0, keepdims=True)

    row = pl.BlockSpec((bm, wd), lambda i: (i, 0))
    one = pl.BlockSpec((1, LANES), lambda i: (0, 0))
    return pl.pallas_call(
        body, name="gdn_outnorm_bwd", grid=(nsteps,),
        in_specs=[row, row, one, row], out_specs=[row, row, one],
        out_shape=[jax.ShapeDtypeStruct((m, wd), F32)] * 2 + [jax.ShapeDtypeStruct((1, LANES), F32)],
        scratch_shapes=[pltpu.VMEM((8, LANES), F32)],
        compiler_params=_params("arbitrary"),
    )(o, gate, gain, dy)


def _sb_consts(kind):
    ri, ci = _iota2((LANES, LANES), 0), _iota2((LANES, LANES), 1)
    tri = {"suffix_excl": ri > ci, "prefix_incl": ri <= ci, "prefix_excl": ri < ci}[kind]
    return jnp.concatenate([tri.astype(BF16), jnp.ones((LANES, LANES), BF16)], axis=1)


def _sb_stack(a):
    first = _iota2((LANES, LANES), 1) < SB_HEAD_DIM
    return jnp.concatenate([jnp.where(first, a, 0.0), jnp.where(first, 0.0, a)], axis=0)


def _sb_unstack(a2):
    first = _iota2((LANES, LANES), 1) < SB_HEAD_DIM
    return jnp.where(first, a2[:LANES], a2[LANES:])


def _sb_visible(qi, kb, blocks=1):
    shape = (2 * LANES, blocks * LANES)
    kpos = _iota2(shape, 1) + kb * LANES
    qpos = (_iota2(shape, 0) & (LANES - 1)) + qi * LANES
    return (kpos < qpos) & (kpos >= PAD_ROWS)


def _sb_log_gates(z, vis):
    t = jnp.log2(1.0 + jnp.exp2(-jnp.abs(z)))
    lb = jnp.minimum(z, 0.0) - t
    lk = lb - z
    if vis is not None:
        lk = jnp.where(vis, lk, 0.0)
    return lb, lk


def _sb_attention(q, kv):
    m, wd = q.shape
    npair = wd // LANES
    nq = m // LANES
    scale = SB_HEAD_DIM ** -0.5

    def body(q_ref, k_ref, v_ref, o_ref, w_ref, b_ref):
        qi = pl.program_id(1)
        su = _sb_consts("suffix_excl")
        q2 = _sb_stack(q_ref[...] * (scale * math.log2(math.e))).astype(BF16)

        def tiles(kbs, carry, masked):
            run, acc = carry
            r0s = [pl.multiple_of(kb * LANES, LANES) for kb in kbs]
            zs = [lax.dot_general(q2, k_ref[pl.ds(r0, LANES), :], NT, preferred_element_type=F32) for r0 in r0s]
            parts = []
            for kb, r0, z in zip(kbs, r0s, zs):
                vis = _sb_visible(qi, kb) if masked else None
                lb, lk = _sb_log_gates(z, vis)
                parts.append((kb, r0, vis, lb, lk))
            parts = [(kb, r0, vis, lb, _split_dot(lk, su)) for kb, r0, vis, lb, lk in parts]
            for kb, r0, vis, lb, cs in parts:
                w = jnp.exp2(lb + cs[:, :LANES] + run)
                if vis is not None:
                    w = jnp.where(vis, w, 0.0)
                run = run + cs[:, LANES:]
                wb = w.astype(BF16)
                w_ref[kb] = wb
                b_ref[kb] = jnp.exp2(lb).astype(BF16)
                acc = acc + lax.dot_general(wb, v_ref[pl.ds(r0, LANES), :], NN, preferred_element_type=F32)
            return run, acc

        su2 = (_iota2((2 * LANES, 2 * LANES), 0) > _iota2((2 * LANES, 2 * LANES), 1)).astype(BF16)

        def pairs(his, carry, masked):
            run, acc = carry
            r0s = [pl.multiple_of((hi - 1) * LANES, LANES) for hi in his]
            zs = [lax.dot_general(q2, k_ref[pl.ds(r0, 2 * LANES), :], NT, preferred_element_type=F32) for r0 in r0s]
            masks = masked if isinstance(masked, list) else [masked] * len(his)
            viss = [_sb_visible(qi, hi - 1, 2) if mk else None for hi, mk in zip(his, masks)]
            gates = [_sb_log_gates(z, vis) for z, vis in zip(zs, viss)]
            css = [_split_dot(lk, su2) for _, lk in gates]
            for hi, r0, vis, (lb, lk), cs in zip(his, r0s, viss, gates, css):
                w = jnp.exp2(lb + cs + jnp.concatenate([run, run], axis=1))
                if vis is not None:
                    w = jnp.where(vis, w, 0.0)
                run = run + jnp.broadcast_to(cs[:, 0:1] + lk[:, 0:1], (2 * LANES, LANES))
                wb = w.astype(BF16)
                bb = jnp.exp2(lb).astype(BF16)
                w_ref[hi - 1], w_ref[hi] = wb[:, :LANES], wb[:, LANES:]
                b_ref[hi - 1], b_ref[hi] = bb[:, :LANES], bb[:, LANES:]
                acc = acc + lax.dot_general(wb, v_ref[pl.ds(r0, 2 * LANES), :], NN, preferred_element_type=F32)
            return run, acc

        def maybe(flag, fn, carry):
            return lax.fori_loop(0, flag, lambda j, c: fn(c), carry)

        zero = jnp.zeros((2 * LANES, LANES), F32)
        far = (qi >= SB_FAR).astype(jnp.int32)
        near = (qi >= 3).astype(jnp.int32) - far
        big = far + near
        carry = maybe(far, lambda c: pairs([qi, qi - 2, qi - 4, qi - 6], c, [True, False, False, False]), (zero, zero))
        carry = maybe(near, lambda c: pairs([qi], c, True), carry)
        carry = maybe(1 - big, lambda c: tiles([qi], c, True), carry)
        top, bottom = qi - 1 - big - 6 * far, 1 + big + 2 * far
        n_mid = jnp.maximum(top - bottom + 1, 0)
        n16 = n_mid // 16
        carry = lax.fori_loop(0, n16, lambda j, c: pairs([top - 16 * j - 2 * u for u in range(8)], c, False), carry)
        top = top - 16 * n16
        take8, take4, take2, take1 = (n_mid >> 3) & 1, (n_mid >> 2) & 1, (n_mid >> 1) & 1, n_mid & 1
        carry = maybe(take8, lambda c: pairs([top - 2 * u for u in range(4)], c, False), carry)
        top = top - 8 * take8
        carry = maybe(take4, lambda c: pairs([top, top - 2], c, False), carry)
        top = top - 4 * take4
        carry = maybe(take2, lambda c: pairs([top], c, False), carry)
        carry = maybe(take1, lambda c: tiles([bottom], c, False), carry)
        carry = maybe(far, lambda c: pairs([3, 1], c, [False, True]), carry)
        carry = maybe(near, lambda c: pairs([1], c, True), carry)
        carry = maybe((1 - big) * (qi >= 1).astype(jnp.int32), lambda c: tiles([0], c, True), carry)
        o_ref[...] = _sb_unstack(carry[1])

    qspec = pl.BlockSpec((LANES, LANES), lambda hp, qi: (qi, hp))
    saved = pl.BlockSpec((None, None, nq, 2 * LANES, LANES), lambda hp, qi: (hp, qi, 0, 0, 0))
    saved_shape = jax.ShapeDtypeStruct((npair, nq, nq, 2 * LANES, LANES), BF16)
    return pl.pallas_call(
        body, name="sb_attention", grid=(npair, nq),
        in_specs=[qspec, pl.BlockSpec((m, LANES), lambda hp, qi: (0, hp)),
                  pl.BlockSpec((m, LANES), lambda hp, qi: (0, npair + hp))],
        out_specs=[qspec, saved, saved],
        out_shape=[jax.ShapeDtypeStruct((m, wd), F32), saved_shape, saved_shape],
        compiler_params=_params("parallel", "arbitrary"),
    )(q, kv, kv)


def _sb_attention_bwd(q, kv, w_all, b_all, do):
    m, wd = q.shape
    npair = wd // LANES
    nq = m // LANES
    scale = SB_HEAD_DIM ** -0.5

    def body(q_ref, k_ref, v_ref, w_ref, b_ref, do_ref, dq_ref, dk_ref, dv_ref):
        qi = pl.program_id(1)

        @pl.when(qi == 0)
        def _():
            dk_ref[...] = jnp.zeros_like(dk_ref)
            dv_ref[...] = jnp.zeros_like(dv_ref)

        px = _sb_consts("prefix_excl")
        q2 = _sb_stack(q_ref[...] * scale).astype(BF16)
        do2 = _sb_stack(do_ref[...]).astype(BF16)

        def tiles(kbs, carry, masked):
            run_p, dq_acc = carry
            r0s = [pl.multiple_of(kb * LANES, LANES) for kb in kbs]
            das = [lax.dot_general(do2, v_ref[pl.ds(r0, LANES), :], NT, preferred_element_type=F32) for r0 in r0s]
            wbs = [w_ref[kb] for kb in kbs]
            for wb, r0 in zip(wbs, r0s):
                dv_ref[pl.ds(r0, LANES), :] += lax.dot_general(wb, do2, TN, preferred_element_type=F32)
            pws = [wb.astype(F32) * da for wb, da in zip(wbs, das)]
            pss = [_split_dot(pw, px, parts=SB_BWD_SUM_PARTS) for pw in pws]
            dzs = []
            for kb, pw, ps in zip(kbs, pws, pss):
                beta = b_ref[kb].astype(F32)
                dz = pw * (1.0 - beta) - beta * (run_p + ps[:, :LANES])
                if masked:
                    dz = jnp.where(_sb_visible(qi, kb), dz, 0.0)
                run_p = run_p + ps[:, LANES:]
                dzs.append(dz.astype(BF16))
            for r0, dz in zip(r0s, dzs):
                dq_acc = dq_acc + lax.dot_general(dz, k_ref[pl.ds(r0, LANES), :], NN, preferred_element_type=F32)
                dk_ref[pl.ds(r0, LANES), :] += lax.dot_general(dz, q2, TN, preferred_element_type=F32)
            return run_p, dq_acc

        px2 = (_iota2((2 * LANES, 2 * LANES), 0) < _iota2((2 * LANES, 2 * LANES), 1)).astype(BF16)

        def pairs(los, carry, masked):
            run_p, dq_acc = carry
            r0s = [pl.multiple_of(lo * LANES, LANES) for lo in los]
            das = [lax.dot_general(do2, v_ref[pl.ds(r0, 2 * LANES), :], NT, preferred_element_type=F32) for r0 in r0s]
            wbs = [jnp.concatenate([w_ref[lo], w_ref[lo + 1]], axis=1) for lo in los]
            for wb, r0 in zip(wbs, r0s):
                dv_ref[pl.ds(r0, 2 * LANES), :] += lax.dot_general(wb, do2, TN, preferred_element_type=F32)
            pws = [wb.astype(F32) * da for wb, da in zip(wbs, das)]
            pss = [_split_dot(pw, px2, parts=SB_BWD_SUM_PARTS) for pw in pws]
            masks = masked if isinstance(masked, list) else [masked] * len(los)
            dzs = []
            for lo, pw, ps, mk in zip(los, pws, pss, masks):
                beta = jnp.concatenate([b_ref[lo], b_ref[lo + 1]], axis=1).astype(F32)
                dz = pw * (1.0 - beta) - beta * (jnp.concatenate([run_p, run_p], axis=1) + ps)
                if mk:
                    dz = jnp.where(_sb_visible(qi, lo, 2), dz, 0.0)
                last = 2 * LANES - 1
                run_p = run_p + jnp.broadcast_to(ps[:, last:] + pw[:, last:], (2 * LANES, LANES))
                dzs.append(dz.astype(BF16))
            for r0, dz in zip(r0s, dzs):
                dq_acc = dq_acc + lax.dot_general(dz, k_ref[pl.ds(r0, 2 * LANES), :], NN, preferred_element_type=F32)
                dk_ref[pl.ds(r0, 2 * LANES), :] += lax.dot_general(dz, q2, TN, preferred_element_type=F32)
            return run_p, dq_acc

        def maybe(flag, fn, carry):
            return lax.fori_loop(0, flag, lambda j, c: fn(c), carry)

        zero = jnp.zeros((2 * LANES, LANES), F32)
        far = (qi >= SB_FAR).astype(jnp.int32)
        near = (qi >= 3).astype(jnp.int32) - far
        big = far + near
        carry = maybe(far, lambda c: pairs([0, 2], c, [True, False]), (zero, zero))
        carry = maybe(near, lambda c: pairs([0], c, True), carry)
        carry = maybe(1 - big, lambda c: tiles([0], c, True), carry)
        low, high = 1 + big + 2 * far, qi - 1 - big - 6 * far
        n_mid = jnp.maximum(high - low + 1, 0)
        n16 = n_mid // 16
        carry = lax.fori_loop(0, n16, lambda j, c: pairs([low + 16 * j + 2 * u for u in range(8)], c, False), carry)
        low = low + 16 * n16
        take8, take4, take2, take1 = (n_mid >> 3) & 1, (n_mid >> 2) & 1, (n_mid >> 1) & 1, n_mid & 1
        carry = maybe(take8, lambda c: pairs([low + 2 * u for u in range(4)], c, False), carry)
        low = low + 8 * take8
        carry = maybe(take4, lambda c: pairs([low, low + 2], c, False), carry)
        low = low + 4 * take4
        carry = maybe(take2, lambda c: pairs([low], c, False), carry)
        carry = maybe(take1, lambda c: tiles([high], c, False), carry)
        carry = maybe(far, lambda c: pairs([qi - 7, qi - 5, qi - 3, qi - 1], c, [False, False, False, True]), carry)
        carry = maybe(near, lambda c: pairs([qi - 1], c, True), carry)
        carry = maybe((1 - big) * (qi >= 1).astype(jnp.int32), lambda c: tiles([qi], c, True), carry)
        dq_ref[...] = _sb_unstack(carry[1]) * scale

    qspec = pl.BlockSpec((LANES, LANES), lambda hp, qi: (qi, hp))
    strip = pl.BlockSpec((m, LANES), lambda hp, qi: (0, hp))
    saved = pl.BlockSpec((None, None, nq, 2 * LANES, LANES), lambda hp, qi: (hp, qi, 0, 0, 0))
    big = jax.ShapeDtypeStruct((m, wd), F32)
    return pl.pallas_call(
        body, name="sb_attention_bwd", grid=(npair, nq),
        in_specs=[qspec, strip, pl.BlockSpec((m, LANES), lambda hp, qi: (0, npair + hp)), saved, saved, qspec],
        out_specs=[qspec, strip, strip],
        out_shape=[big, big, big],
        compiler_params=_params("parallel", "arbitrary"),
    )(q, kv, kv, w_all, b_all, do)


def _mesh_pos():
    return lax.axis_index("x"), lax.axis_index("y"), lax.axis_index("c")


def _exchange(srcs, name, scatter):
    n = len(srcs)
    rows = [s.shape[-2] for s in srcs]
    offs = [sum(rows[:i]) for i in range(n)]
    total = sum(rows)

    def body(*refs):
        src_refs, out_ref = refs[:n], refs[n]
        send_sems, recv_sems, local_sems = refs[n + 1:]
        x, y, c = _mesh_pos()
        me = 4 * x + 2 * y + c

        def piece(i, d):
            return src_refs[i].at[d] if scatter else src_refs[i]

        def window(slot, i):
            return out_ref.at[slot, pl.ds(offs[i], rows[i])]

        local = [pltpu.make_async_copy(piece(i, me), window(me, i), local_sems.at[i]) for i in range(n)]
        for cp in local:
            cp.start()
        peers = []
        for k in range(1, N_DEV):
            px = 1 - x if k & 4 else x
            py = 1 - y if k & 2 else y
            pc = 1 - c if k & 1 else c
            pid = 4 * px + 2 * py + pc
            for i in range(n):
                pltpu.make_async_remote_copy(
                    src_ref=piece(i, pid), dst_ref=window(me, i),
                    send_sem=send_sems.at[k - 1], recv_sem=recv_sems.at[k - 1],
                    device_id=(px, py, pc), device_id_type=pl.DeviceIdType.MESH).start()
            peers.append(pid)
        slabs = [pltpu.make_async_remote_copy(
            src_ref=out_ref.at[me], dst_ref=out_ref.at[peers[k - 1]],
            send_sem=send_sems.at[k - 1], recv_sem=recv_sems.at[k - 1],
            device_id=(x, y, c), device_id_type=pl.DeviceIdType.MESH) for k in range(1, N_DEV)]
        for cp in slabs:
            cp.wait_recv()
        for cp in slabs:
            cp.wait_send()
        for cp in local:
            cp.wait()

    return pl.pallas_call(
        body, name=name,
        in_specs=[pl.BlockSpec(memory_space=pl.ANY)] * n, out_specs=pl.BlockSpec(memory_space=pl.ANY),
        out_shape=jax.ShapeDtypeStruct((N_DEV, total, FLAT_COLS), srcs[0].dtype),
        scratch_shapes=[pltpu.SemaphoreType.DMA((N_DEV - 1,)), pltpu.SemaphoreType.DMA((N_DEV - 1,)),
                        pltpu.SemaphoreType.DMA((n,))],
        compiler_params=pltpu.CompilerParams(has_side_effects=True),
    )(*srcs)


def _all_gather_two_level(srcs, name):
    n = len(srcs)
    rows = [s.shape[-2] for s in srcs]
    offs = [sum(rows[:i]) for i in range(n)]
    total = sum(rows)

    def body(*refs):
        src_refs, out_ref = refs[:n], refs[n]
        send_sems, recv_sems, local_sems = refs[n + 1:]
        x, y, c = _mesh_pos()
        me = 4 * x + 2 * y + c
        sibling = (x, y, 1 - c)
        chips = [(1 - x, y), (x, 1 - y), (1 - x, 1 - y)]

        def slot_of(px, py, pc):
            return 4 * px + 2 * py + pc

        def window(slot, i):
            return out_ref.at[slot, pl.ds(offs[i], rows[i])]

        def slab(slot, k, to):
            return pltpu.make_async_remote_copy(
                src_ref=out_ref.at[slot], dst_ref=out_ref.at[slot],
                send_sem=send_sems.at[k], recv_sem=recv_sems.at[k],
                device_id=to, device_id_type=pl.DeviceIdType.MESH)

        local = [pltpu.make_async_copy(src_refs[i], window(me, i), local_sems.at[i]) for i in range(n)]
        for cp in local:
            cp.start()
        targets = [sibling] + [(cx, cy, c) for cx, cy in chips]
        for k, to in enumerate(targets):
            for i in range(n):
                pltpu.make_async_remote_copy(
                    src_ref=src_refs[i], dst_ref=window(me, i),
                    send_sem=send_sems.at[k], recv_sem=recv_sems.at[k],
                    device_id=to, device_id_type=pl.DeviceIdType.MESH).start()
        passed = []
        for j, (cx, cy) in enumerate(chips):
            slab(slot_of(cx, cy, c), 1 + j, (x, y, c)).wait_recv()
            cp = slab(slot_of(cx, cy, c), 4 + j, sibling)
            cp.start()
            passed.append(cp)
        slab(slot_of(x, y, 1 - c), 0, (x, y, c)).wait_recv()
        for j, (cx, cy) in enumerate(chips):
            slab(slot_of(cx, cy, 1 - c), 4 + j, (x, y, c)).wait_recv()
        for k in range(len(targets)):
            slab(me, k, (x, y, c)).wait_send()
        for cp in passed:
            cp.wait_send()
        for cp in local:
            cp.wait()

    return pl.pallas_call(
        body, name=name,
        in_specs=[pl.BlockSpec(memory_space=pl.ANY)] * n, out_specs=pl.BlockSpec(memory_space=pl.ANY),
        out_shape=jax.ShapeDtypeStruct((N_DEV, total, FLAT_COLS), srcs[0].dtype),
        scratch_shapes=[pltpu.SemaphoreType.DMA((N_DEV - 1,)), pltpu.SemaphoreType.DMA((N_DEV - 1,)),
                        pltpu.SemaphoreType.DMA((n,))],
        compiler_params=pltpu.CompilerParams(has_side_effects=True),
    )(*srcs)


def _adamw(parts, w, mom, var, name):
    rows = w.shape[0]
    br = FLAT_ROW_BLOCK
    c1 = 1.0 - ADAM_B1 ** ADAM_STEP
    c2 = 1.0 - ADAM_B2 ** ADAM_STEP

    def body(p_ref, w_ref, m_ref, v_ref, g_ref, d_ref, nm_ref, nv_ref):
        g = p_ref[0].astype(F32)
        for s in range(1, N_DEV):
            g = g + p_ref[s].astype(F32)
        m_new = ADAM_B1 * m_ref[...] + (1.0 - ADAM_B1) * g
        v_new = ADAM_B2 * v_ref[...] + (1.0 - ADAM_B2) * (g * g)
        m_hat = m_new / c1
        v_hat = v_new / c2
        g_ref[...] = g
        d_ref[...] = -ADAM_LR * (m_hat / (jnp.sqrt(v_hat) + ADAM_EPS) + ADAM_WD * w_ref[...])
        nm_ref[...] = m_new
        nv_ref[...] = v_new

    row = pl.BlockSpec((br, FLAT_COLS), lambda i: (i, 0))
    flat = jax.ShapeDtypeStruct((rows, FLAT_COLS), F32)
    return pl.pallas_call(
        body, name=name, grid=(rows // br,),
        in_specs=[pl.BlockSpec((N_DEV, br, FLAT_COLS), lambda i: (0, i, 0)), row, row, row],
        out_specs=[row] * 4, out_shape=[flat] * 4,
        compiler_params=_params("parallel"),
    )(parts, w, mom, var)


SMALL_SHARDED = (("meta_tokens", 1), ("gdn_norm_g", 1), ("gdn_conv_w", 2))
BIG_PIECES = (("gdn_w_in", 0, True), ("gdn_w_out", 0, False), ("w_kv", None, True), ("sb_w_q", 0, False),
              ("sb_w_o", 0, False), ("ffn_w_gate_up", 0, True), ("ffn_w_gate_up", 1, True),
              ("ffn_w_down", 0, False), ("ffn_w_down", 1, False))
REPLICATED = ("gdn_a_log", "gdn_dt_bias", "gdn_onorm_g", "kv_norm_g", "sb_norm_g", "ffn_norm_g", "final_norm_g")
WEIGHTS = ("meta_tokens", "gdn_norm_g", "gdn_w_in", "gdn_conv_w", "gdn_a_log", "gdn_dt_bias", "gdn_onorm_g",
           "gdn_w_out", "kv_norm_g", "w_kv", "sb_norm_g", "sb_w_q", "sb_w_o", "ffn_norm_g", "ffn_w_gate_up",
           "ffn_w_down", "final_norm_g")
F32_ROWS = 8
BF16_ROWS = 16


def _n_rows(shape, mult):
    return -(-math.prod(shape) // (mult * FLAT_COLS)) * mult


def _as_rows(a, nl, mult):
    lead = a.shape[:nl]
    size = math.prod(a.shape[nl:])
    nr = _n_rows(a.shape[nl:], mult)
    flat = a.reshape(lead + (size,))
    if nr * FLAT_COLS != size:
        flat = jnp.pad(flat, [(0, 0)] * nl + [(0, nr * FLAT_COLS - size)])
    return flat.reshape(lead + (nr, FLAT_COLS))


def _from_rows(flat, r0, shape, nl, mult):
    lead = flat.shape[:nl]
    nr = _n_rows(shape, mult)
    seg = lax.slice_in_dim(flat, r0, r0 + nr, axis=nl).reshape(lead + (nr * FLAT_COLS,))
    return lax.slice_in_dim(seg, 0, math.prod(shape), axis=nl).reshape(lead + tuple(shape)), r0 + nr


def _to_piece(shard, layer, transposed):
    mat = shard if layer is None else shard[layer]
    return mat.T if transposed else mat


def _from_pieces(pieces, name):
    mats = [p.T if tr else p for (n, layer, tr), p in zip(BIG_PIECES, pieces) if n == name]
    layers = [layer for n, layer, _ in BIG_PIECES if n == name]
    return mats[0] if layers[0] is None else jnp.stack(mats)


def _merge_shards(g, axis):
    t = jnp.moveaxis(g, 0, axis)
    shp = t.shape
    return t.reshape(shp[:axis] + (shp[axis] * shp[axis + 1],) + shp[axis + 2:])


def _split_shards(full, axis):
    shp = full.shape
    t = full.reshape(shp[:axis] + (N_DEV, shp[axis] // N_DEV) + shp[axis + 1:])
    return jnp.moveaxis(t, axis, 0)


def _pad_lanes(a, width=LANES):
    return jnp.pad(a, ((0, 0), (0, width - a.shape[1])))


def _local_step(x, target, w, mats):
    d = x.shape[1]
    gh = w["gdn_a_log"].shape[1]
    gw = gh * GDN_HEAD_DIM
    w_int = mats["gdn_w_in", 0]
    w_qkvt, w_gatet = w_int[:3 * gw], w_int[3 * gw:4 * gw]
    w_abt = jnp.pad(w_int[4 * gw:], ((0, LANES - 2 * gh), (0, 0)))
    conv_w = w["gdn_conv_w"][0]
    a_log, dt_bias = _pad_lanes(w["gdn_a_log"]), _pad_lanes(w["gdn_dt_bias"])
    w_out = mats["gdn_w_out", 0]
    w_kvt = mats["w_kv", None]
    sbw = w_kvt.shape[0] // 2
    w_kt, w_vt = w_kvt[:sbw], w_kvt[sbw:]
    w_q, w_o = mats["sb_w_q", 0], mats["sb_w_o", 0]
    w_gut = [mats["ffn_w_gate_up", l] for l in range(2)]
    w_dn = [mats["ffn_w_down", l] for l in range(2)]
    ffn_g = [w["ffn_norm_g"][l:l + 1] for l in range(2)]
    kv_g, fin_g = w["kv_norm_g"][None], w["final_norm_g"][None]

    h0 = jnp.concatenate([jnp.zeros((PAD_ROWS, d), F32), w["meta_tokens"], x], axis=0)

    (n1,) = _rmsnorm(h0, [w["gdn_norm_g"]], "norm_gdn")
    proj = _matmul(n1, w_qkvt, "nt", "mm_gdn_qkv")
    gate = _matmul(n1, w_gatet, "nt", "mm_gdn_gate")
    ab = _matmul(n1, w_abt, "nt", "mm_gdn_ab")
    qkv = _gdn_conv(proj, conv_w, gh)
    g_b, beta_b = _gdn_gates(ab, a_log, dt_bias, gh)
    ub, wc, qd, kd, gc, pmat, tinv = _gdn_prep(qkv, g_b, beta_b, gh)
    o_gdn, ssave = _gdn_scan(ub, wc, qd, kd, gc, pmat, gh)
    o2 = _gdn_outnorm(o_gdn, gate, w["gdn_onorm_g"], gh)
    h1 = _matmul(o2, w_out, "nn", "mm_gdn_out", add=h0)

    def ffn_fwd(h, l):
        (n,) = _rmsnorm(h, [ffn_g[l]], f"norm_ffn{l}")
        gu = _matmul(n, w_gut[l], "nt", f"mm_ffn{l}_gu", out_dtype=BF16)
        act = _swiglu(gu, f"swiglu{l}")
        return n, gu, act, _matmul(act, w_dn[l], "nn", f"mm_ffn{l}_down", add=h)

    n2, gu0, act0, h2 = ffn_fwd(h1, 0)

    nkv, n3 = _rmsnorm(h2, [kv_g, w["sb_norm_g"]], "norm_kv_sb")
    kv = _matmul(nkv, w_kvt, "nt", "mm_kv", out_dtype=BF16)
    q_sb = _matmul(n3, w_q, "nn", "mm_sb_q")
    o_sb, w_sb, beta_sb = _sb_attention(q_sb, kv)
    h3 = _matmul(o_sb, w_o, "nn", "mm_sb_o", add=h2)
    n4, gu1, act1, h4 = ffn_fwd(h3, 1)

    loss, dh4, d_fin = _loss_head(h4, fin_g, target)

    def ffn_bwd(dh, h, n, gu, act, l):
        d_act = _matmul(dh, w_dn[l], "nt", f"mm_ffn{l}_dact")
        dw_dn = _matmul(act, dh, "tn", f"mm_ffn{l}_dwdown")
        dgu = _swiglu_bwd(gu, d_act, f"swiglu_bwd{l}")
        dw_gu = _matmul(dgu, n, "tn", f"mm_ffn{l}_dwgu")
        dn = _matmul(dgu, w_gut[l], "nn", f"mm_ffn{l}_dn")
        dh_in, (dg,) = _rmsnorm_bwd(h, [ffn_g[l]], [dn], dh, f"norm_ffn{l}_bwd")
        return dh_in, dw_gu, dw_dn, dg

    dh3, dw_gu1, dw_dn1, dg_ffn1 = ffn_bwd(dh4, h3, n4, gu1, act1, 1)

    do_sb = _matmul(dh3, w_o, "nt", "mm_sb_do")
    dw_o = _matmul(o_sb, dh3, "tn", "mm_sb_dwo")
    dq_sb, dk_sb, dv_sb = _sb_attention_bwd(q_sb, kv, w_sb, beta_sb, do_sb)
    dw_q = _matmul(n3, dq_sb, "tn", "mm_sb_dwq")
    dn3 = _matmul(dq_sb, w_q, "nt", "mm_sb_dn")
    dw_k = _matmul(dk_sb, nkv, "tn", "mm_dwk")
    dw_v = _matmul(dv_sb, nkv, "tn", "mm_dwv")
    dnkv = _matmul(dk_sb, w_kt, "nn", "mm_dnk")
    dnkv = _matmul(dv_sb, w_vt, "nn", "mm_dnv", add=dnkv)
    dh2, (dg_kv, dg_sb) = _rmsnorm_bwd(h2, [kv_g, w["sb_norm_g"]], [dnkv, dn3], dh3, "norm_kv_sb_bwd")

    dh1, dw_gu0, dw_dn0, dg_ffn0 = ffn_bwd(dh2, h1, n2, gu0, act0, 0)

    do2 = _matmul(dh1, w_out, "nt", "mm_gdn_do")
    dw_out = _matmul(o2, dh1, "tn", "mm_gdn_dwout")
    do_gdn, dgate, d_onorm = _gdn_outnorm_bwd(o_gdn, gate, w["gdn_onorm_g"], do2, gh)
    dub, dwc, dqd, dkd, dpm, dgl = _gdn_scan_bwd(do_gdn, ssave, ub, wc, qd, kd, gc, pmat, gh)
    dqkv, dg_b, dbeta_b = _gdn_prep_bwd(qkv, g_b, beta_b, ub, wc, tinv, dub, dwc, dqd, dkd, dpm, dgl, gh)
    dab, d_alog, d_dtb = _gdn_gates_bwd(ab, a_log, dt_bias, dg_b, dbeta_b, gh)
    dproj, d_conv = _gdn_conv_bwd(proj, conv_w, dqkv, gh)
    dw_qkv = _matmul(dproj, n1, "tn", "mm_gdn_dwqkv")
    dw_gate = _matmul(dgate, n1, "tn", "mm_gdn_dwgate")
    dw_ab = _matmul(dab, n1, "tn", "mm_gdn_dwab")
    dn1 = _matmul(dproj, w_qkvt, "nn", "mm_gdn_dn_qkv")
    dn1 = _matmul(dgate, w_gatet, "nn", "mm_gdn_dn_gate", add=dn1)
    dn1 = _matmul(dab, w_abt, "nn", "mm_gdn_dn_ab", add=dn1)
    dh0, (dg_gdn,) = _rmsnorm_bwd(h0, [w["gdn_norm_g"]], [dn1], dh1, "norm_gdn_bwd")

    grads = {
        "meta_tokens": dh0[PAD_ROWS:LANES],
        "gdn_norm_g": dg_gdn,
        "gdn_conv_w": d_conv[None],
        "gdn_a_log": d_alog[:, :gh],
        "gdn_dt_bias": d_dtb[:, :gh],
        "gdn_onorm_g": d_onorm,
        "kv_norm_g": dg_kv[0],
        "sb_norm_g": dg_sb,
        "ffn_norm_g": jnp.concatenate([dg_ffn0, dg_ffn1], axis=0),
        "final_norm_g": d_fin[0],
    }
    gmats = {
        ("gdn_w_in", 0): jnp.concatenate([dw_qkv, dw_gate, dw_ab[:2 * gh]], axis=0),
        ("gdn_w_out", 0): dw_out,
        ("w_kv", None): jnp.concatenate([dw_k, dw_v], axis=0),
        ("sb_w_q", 0): dw_q,
        ("sb_w_o", 0): dw_o,
        ("ffn_w_gate_up", 0): dw_gu0,
        ("ffn_w_gate_up", 1): dw_gu1,
        ("ffn_w_down", 0): dw_dn0,
        ("ffn_w_down", 1): dw_dn1,
    }
    return loss, dh0[LANES:], grads, gmats


def kernel(x, meta_tokens, gdn_norm_g, gdn_w_in, gdn_conv_w, gdn_a_log, gdn_dt_bias, gdn_onorm_g, gdn_w_out, kv_norm_g, w_kv, sb_norm_g, sb_w_q, sb_w_o, ffn_norm_g, ffn_w_gate_up, ffn_w_down, final_norm_g, loss_target, m_meta_tokens, m_gdn_norm_g, m_gdn_w_in, m_gdn_conv_w, m_gdn_a_log, m_gdn_dt_bias, m_gdn_onorm_g, m_gdn_w_out, m_kv_norm_g, m_w_kv, m_sb_norm_g, m_sb_w_q, m_sb_w_o, m_ffn_norm_g, m_ffn_w_gate_up, m_ffn_w_down, m_final_norm_g, v_meta_tokens, v_gdn_norm_g, v_gdn_w_in, v_gdn_conv_w, v_gdn_a_log, v_gdn_dt_bias, v_gdn_onorm_g, v_gdn_w_out, v_kv_norm_g, v_w_kv, v_sb_norm_g, v_sb_w_q, v_sb_w_o, v_ffn_norm_g, v_ffn_w_gate_up, v_ffn_w_down, v_final_norm_g):
    given = dict(locals())
    small = [n for n, _ in SMALL_SHARDED]
    big_names = list(dict.fromkeys(n for n, _, _ in BIG_PIECES))

    def local_pieces(prefix):
        return [_to_piece(given[prefix + n], layer, tr) for n, layer, tr in BIG_PIECES]

    piece_shapes = [p.shape for p in local_pieces("")]

    g_small = _exchange([jnp.concatenate([_as_rows(given[n], 0, F32_ROWS) for n in small], axis=0)],
                        "all_gather_small", scatter=False)
    g_big = _all_gather_two_level([_as_rows(p.astype(BF16), 0, BF16_ROWS) for p in local_pieces("")],
                                  "all_gather_weights")
    full = {n: given[n] for n in REPLICATED}
    r = 0
    for n, ax in SMALL_SHARDED:
        shards, r = _from_rows(g_small, r, given[n].shape, 1, F32_ROWS)
        full[n] = _merge_shards(shards, ax)
    mats, r = {}, 0
    for (n, layer, _), shp in zip(BIG_PIECES, piece_shapes):
        rows8, r = _from_rows(g_big, r, shp, 1, BF16_ROWS)
        mats[n, layer] = rows8.reshape((N_DEV * shp[0],) + shp[1:])

    loss_b, grad_x, grads, gmats = _local_step(x[0], loss_target[0], full, mats)

    order = small + list(REPLICATED)
    fill_small = (-sum(_n_rows(given[n].shape, F32_ROWS) for n in order)) % FLAT_ROW_BLOCK
    fill_big = (-sum(_n_rows(shp, BF16_ROWS) for shp in piece_shapes)) % FLAT_ROW_BLOCK
    small_send = [_as_rows(_split_shards(grads[n], ax), 1, F32_ROWS) for n, ax in SMALL_SHARDED]
    small_send += [_as_rows(jnp.broadcast_to(grads[n][None], (N_DEV,) + grads[n].shape), 1, F32_ROWS)
                   for n in REPLICATED]
    if fill_small:
        small_send.append(jnp.zeros((N_DEV, fill_small, FLAT_COLS), F32))
    recv_small = _exchange([jnp.concatenate(small_send, axis=1)], "scatter_small_gradients", scatter=True)
    big_send = [_as_rows(gmats[n, layer].reshape((N_DEV,) + shp), 1, BF16_ROWS)
                for (n, layer, _), shp in zip(BIG_PIECES, piece_shapes)]
    if fill_big:
        big_send.append(jnp.zeros((N_DEV, fill_big, FLAT_COLS), BF16))
    recv_big = _exchange(big_send, "scatter_gradients", scatter=True)

    def flat_small(prefix):
        segs = [_as_rows(given[prefix + n], 0, F32_ROWS) for n in order]
        return jnp.concatenate(segs + ([jnp.zeros((fill_small, FLAT_COLS), F32)] if fill_small else []), axis=0)

    def flat_big(prefix):
        segs = [_as_rows(p, 0, BF16_ROWS) for p in local_pieces(prefix)]
        return jnp.concatenate(segs + ([jnp.zeros((fill_big, FLAT_COLS), F32)] if fill_big else []), axis=0)

    upd_small = _adamw(recv_small, flat_small(""), flat_small("m_"), flat_small("v_"), "adamw_small")
    upd_big = _adamw(recv_big, flat_big(""), flat_big("m_"), flat_big("v_"), "adamw")
    outs = {}
    for tag, f_small, f_big in zip(("grad", "delta", "new_m", "new_v"), upd_small, upd_big):
        outs[tag], r = {}, 0
        for n in order:
            outs[tag][n], r = _from_rows(f_small, r, given[n].shape, 0, F32_ROWS)
        pieces, r = [], 0
        for shp in piece_shapes:
            p, r = _from_rows(f_big, r, shp, 0, BF16_ROWS)
            pieces.append(p)
        for n in big_names:
            outs[tag][n] = _from_pieces(pieces, n)

    loss = lax.psum(loss_b[0, 0], ("x", "y", "c"))
    result = [loss, grad_x[None]]
    for tag in ("grad", "delta", "new_m", "new_v"):
        result += [outs[tag][n] for n in WEIGHTS]
    return tuple(result)
```

```python
import functools
import math

import jax
import jax.numpy as jnp
from jax import lax
from jax.experimental import pallas as pl
from jax.experimental.pallas import tpu as pltpu

F32 = jnp.float32
BF16 = jnp.bfloat16

N_DEV = 8
LANES = 128
N_META = 16
PAD_ROWS = LANES - N_META
GDN_CHUNK = 64
GDN_HEAD_DIM = 128
SB_HEAD_DIM = 64
SB_FAR = 11
SB_BWD_SUM_PARTS = 1
CONV_WIDTH = 4
NORM_EPS = 1e-6
FLAT_COLS = 1024
FLAT_ROW_BLOCK = 64
VMEM_LIMIT = 56 * 1024 * 1024

ADAM_LR = 0.001
ADAM_B1 = 0.9
ADAM_B2 = 0.999
ADAM_EPS = 1e-08
ADAM_WD = 0.01
ADAM_STEP = 10

NN = (((1,), (0,)), ((), ()))
NT = (((1,), (1,)), ((), ()))
TN = (((0,), (0,)), ((), ()))


def _params(*sem):
    return pltpu.CompilerParams(dimension_semantics=sem, vmem_limit_bytes=VMEM_LIMIT)


def _pick(n, cands):
    for c in cands:
        if n % c == 0:
            return c
    return n


def _bdot(a, b, dims=NN):
    return lax.dot_general(a.astype(BF16), b.astype(BF16), dims, preferred_element_type=F32)


def _hdot(a, b, dims=NN):
    return lax.dot_general(a, b, dims, preferred_element_type=F32, precision=lax.Precision.HIGH)


def _split_dot(a, m, parts=2):
    out = None
    for _ in range(parts):
        piece = a.astype(BF16)
        a = a - piece.astype(F32)
        term = lax.dot_general(piece, m, NN, preferred_element_type=F32)
        out = term if out is None else out + term
    return out


def _lockstep(gens):
    while gens:
        alive = []
        for g in gens:
            try:
                next(g)
                alive.append(g)
            except StopIteration:
                pass
        gens = alive


def _sigmoid(x):
    return 1.0 / (1.0 + jnp.exp(-x))


def _iota2(shape, axis):
    return lax.broadcasted_iota(jnp.int32, shape, axis)


def _matmul(a, b, mode, name, out_dtype=None, add=None):
    if out_dtype is None:
        out_dtype = BF16 if mode == "tn" else F32
    if mode == "nn":
        (m, k), n = a.shape, b.shape[1]
    elif mode == "nt":
        (m, k), n = a.shape, b.shape[0]
    else:
        (k, m), n = a.shape, b.shape[1]
    row_c = (640, 512, 384, 256, 128)
    col_c = (1024, 1408, 768, 512, 384, 256, 128)
    if mode == "tn":
        bm, bn, bk = _pick(m, col_c), _pick(n, col_c), _pick(k, (1664,) + row_c)
    else:
        bm, bn, bk = _pick(m, row_c), _pick(n, col_c), _pick(k, (1024, 1408, 768, 512, 256, 128))
    nk = k // bk
    ni, nj = m // bm, n // bn
    dims = {"nn": NN, "nt": NT, "tn": TN}[mode]
    a_bytes, b_bytes = a.size * a.dtype.itemsize, b.size * b.dtype.itemsize
    rows_outer = (a_bytes * (1 if nk == 1 else nj) + b_bytes * ni
                  <= b_bytes * (1 if nk == 1 else ni) + a_bytes * nj)

    def at(f):
        return (lambda g0, g1, q: f(g0, g1, q)) if rows_outer else (lambda g0, g1, q: f(g1, g0, q))

    a_spec = {"nn": pl.BlockSpec((bm, bk), at(lambda i, j, q: (i, q))),
              "nt": pl.BlockSpec((bm, bk), at(lambda i, j, q: (i, q))),
              "tn": pl.BlockSpec((bk, bm), at(lambda i, j, q: (q, i)))}[mode]
    b_spec = {"nn": pl.BlockSpec((bk, bn), at(lambda i, j, q: (q, j))),
              "nt": pl.BlockSpec((bn, bk), at(lambda i, j, q: (j, q))),
              "tn": pl.BlockSpec((bk, bn), at(lambda i, j, q: (q, j)))}[mode]
    o_spec = pl.BlockSpec((bm, bn), at(lambda i, j, q: (i, j)))
    has_add = add is not None

    def body(*refs):
        if has_add:
            a_ref, b_ref, add_ref, o_ref, acc_ref = refs
        else:
            a_ref, b_ref, o_ref, acc_ref = refs
        q = pl.program_id(2)
        part = _bdot(a_ref[...], b_ref[...], dims)

        @pl.when(q == 0)
        def _():
            acc_ref[...] = part

        @pl.when(q > 0)
        def _():
            acc_ref[...] += part

        @pl.when(q == nk - 1)
        def _():
            r = acc_ref[...]
            if has_add:
                r = r + add_ref[...]
            o_ref[...] = r.astype(out_dtype)

    ins = [a, b] + ([add] if has_add else [])
    in_specs = [a_spec, b_spec] + ([o_spec] if has_add else [])
    return pl.pallas_call(
        body, name=name, grid=(ni, nj, nk) if rows_outer else (nj, ni, nk),
        in_specs=in_specs, out_specs=o_spec,
        out_shape=jax.ShapeDtypeStruct((m, n), out_dtype),
        scratch_shapes=[pltpu.VMEM((bm, bn), F32)],
        compiler_params=_params("parallel", "parallel", "arbitrary"),
    )(*ins)


def _rmsnorm(h, gains, name):
    m, d = h.shape
    bm = _pick(m, (640, 384, 128))
    ng = len(gains)

    def body(*refs):
        h_ref, g_refs, o_refs = refs[0], refs[1:1 + ng], refs[1 + ng:]
        x = h_ref[...]
        xhat = x * lax.rsqrt(jnp.mean(x * x, axis=-1, keepdims=True) + NORM_EPS)
        for g_ref, o_ref in zip(g_refs, o_refs):
            o_ref[...] = (xhat * g_ref[...]).astype(BF16)

    row = pl.BlockSpec((bm, d), lambda i: (i, 0))
    gain = pl.BlockSpec((1, d), lambda i: (0, 0))
    return pl.pallas_call(
        body, name=name, grid=(m // bm,),
        in_specs=[row] + [gain] * ng, out_specs=[row] * ng,
        out_shape=[jax.ShapeDtypeStruct((m, d), BF16)] * ng,
        compiler_params=_params("parallel"),
    )(h, *gains)


def _rmsnorm_bwd(h, gains, dns, dres, name):
    m, d = h.shape
    bm = _pick(m, (640, 384, 128))
    ng = len(gains)
    nsteps = m // bm

    def body(*refs):
        h_ref, dres_ref = refs[0], refs[1]
        g_refs = refs[2:2 + ng]
        dn_refs = refs[2 + ng:2 + 2 * ng]
        dh_ref = refs[2 + 2 * ng]
        dg_refs = refs[3 + 2 * ng:3 + 3 * ng]
        acc_refs = refs[3 + 3 * ng:]
        i = pl.program_id(0)
        x = h_ref[...]
        r = lax.rsqrt(jnp.mean(x * x, axis=-1, keepdims=True) + NORM_EPS)
        xhat = x * r
        dh = dres_ref[...]
        for g_ref, dn_ref, dg_ref, acc_ref in zip(g_refs, dn_refs, dg_refs, acc_refs):
            dn = dn_ref[...]
            dy = dn * g_ref[...]
            dh = dh + r * (dy - xhat * jnp.mean(dy * xhat, axis=-1, keepdims=True))
            part = jnp.sum((dn * xhat).reshape(bm // 8, 8, d), axis=0)

            @pl.when(i == 0)
            def _():
                acc_ref[...] = part

            @pl.when(i > 0)
            def _():
                acc_ref[...] += part

            @pl.when(i == nsteps - 1)
            def _():
                dg_ref[...] = jnp.sum(acc_ref[...], axis=0, keepdims=True)

        dh_ref[...] = dh

    row = pl.BlockSpec((bm, d), lambda i: (i, 0))
    gain = pl.BlockSpec((1, d), lambda i: (0, 0))
    outs = pl.pallas_call(
        body, name=name, grid=(nsteps,),
        in_specs=[row, row] + [gain] * ng + [row] * ng,
        out_specs=[row] + [gain] * ng,
        out_shape=[jax.ShapeDtypeStruct((m, d), F32)] + [jax.ShapeDtypeStruct((1, d), F32)] * ng,
        scratch_shapes=[pltpu.VMEM((8, d), F32)] * ng,
        compiler_params=_params("arbitrary"),
    )(h, dres, *gains, *dns)
    return outs[0], list(outs[1:])


def _loss_head(h, gain, target):
    m, d = h.shape
    nsteps = m // LANES

    def body(h_ref, g_ref, t_ref, loss_ref, dh_ref, dg_ref, lacc, gacc):
        i = pl.program_id(0)

        @pl.when(i == 0)
        def _():
            lacc[...] = jnp.zeros_like(lacc)
            gacc[...] = jnp.zeros_like(gacc)
            dh_ref[...] = jnp.zeros_like(dh_ref)

        @pl.when(i > 0)
        def _():
            x = h_ref[...]
            r = lax.rsqrt(jnp.mean(x * x, axis=-1, keepdims=True) + NORM_EPS)
            xhat = x * r
            g = g_ref[...]
            err = xhat * g - t_ref[...]
            lacc[...] += jnp.sum((err * err).reshape(LANES // 8, 8, d), axis=0)
            dn = err * (1.0 / d)
            dy = dn * g
            dh_ref[...] = r * (dy - xhat * jnp.mean(dy * xhat, axis=-1, keepdims=True))
            gacc[...] += jnp.sum((dn * xhat).reshape(LANES // 8, 8, d), axis=0)

        @pl.when(i == nsteps - 1)
        def _():
            tot = jnp.sum(jnp.sum(lacc[...], axis=1, keepdims=True), axis=0, keepdims=True)
            loss_ref[...] = jnp.broadcast_to(tot * (0.5 / d), (1, LANES))
            dg_ref[...] = jnp.sum(gacc[...], axis=0, keepdims=True)

    row = pl.BlockSpec((LANES, d), lambda i: (i, 0))
    trow = pl.BlockSpec((LANES, d), lambda i: (jnp.maximum(i - 1, 0), 0))
    gain_spec = pl.BlockSpec((1, d), lambda i: (0, 0))
    return pl.pallas_call(
        body, name="loss_head", grid=(nsteps,),
        in_specs=[row, gain_spec, trow],
        out_specs=[pl.BlockSpec((1, LANES), lambda i: (0, 0)), row, gain_spec],
        out_shape=[jax.ShapeDtypeStruct((1, LANES), F32), jax.ShapeDtypeStruct((m, d), F32),
                   jax.ShapeDtypeStruct((1, d), F32)],
        scratch_shapes=[pltpu.VMEM((8, d), F32), pltpu.VMEM((8, d), F32)],
        compiler_params=_params("arbitrary"),
    )(h, gain, target)


def _swiglu(gu, name):
    m, f2 = gu.shape
    f = f2 // 2

    def body(gu_ref, o_ref):
        g = gu_ref[:, :f].astype(F32)
        u = gu_ref[:, f:].astype(F32)
        o_ref[...] = (g * _sigmoid(g) * u).astype(BF16)

    return pl.pallas_call(
        body, name=name, grid=(m // LANES,),
        in_specs=[pl.BlockSpec((LANES, f2), lambda i: (i, 0))],
        out_specs=pl.BlockSpec((LANES, f), lambda i: (i, 0)),
        out_shape=jax.ShapeDtypeStruct((m, f), BF16),
        compiler_params=_params("parallel"),
    )(gu)


def _swiglu_bwd(gu, da, name):
    m, f2 = gu.shape
    f = f2 // 2

    def body(gu_ref, da_ref, o_ref):
        g = gu_ref[:, :f].astype(F32)
        u = gu_ref[:, f:].astype(F32)
        da_ = da_ref[...]
        s = _sigmoid(g)
        o_ref[:, :f] = (da_ * u * s * (1.0 + g * (1.0 - s))).astype(BF16)
        o_ref[:, f:] = (da_ * g * s).astype(BF16)

    return pl.pallas_call(
        body, name=name, grid=(m // LANES,),
        in_specs=[pl.BlockSpec((LANES, f2), lambda i: (i, 0)), pl.BlockSpec((LANES, f), lambda i: (i, 0))],
        out_specs=pl.BlockSpec((LANES, f2), lambda i: (i, 0)),
        out_shape=jax.ShapeDtypeStruct((m, f2), BF16),
        compiler_params=_params("parallel"),
    )(gu, da)


def _conv_taps(xa, w_ref):
    acc = xa * w_ref[CONV_WIDTH - 1:CONV_WIDTH, :]
    for i in range(CONV_WIDTH - 1):
        acc = acc + pltpu.roll(xa, CONV_WIDTH - 1 - i, 0) * w_ref[i:i + 1, :]
    return acc[8:]


def _gdn_conv(proj, conv_w, n_heads):
    m, w3 = proj.shape
    wd = w3 // 3
    nblk = m // LANES
    unroll = _pick(nblk - 1, (4, 2, 1))

    def body(x_ref, w_ref, o_ref):
        c = pl.program_id(0)
        is_qk = c < 2 * n_heads

        def block(b, xa=None):
            r0 = pl.multiple_of(b * LANES, LANES)
            if xa is None:
                xa = x_ref[pl.ds(r0 - 8, LANES + 8), :]
            cv = _conv_taps(xa, w_ref)
            s = cv * _sigmoid(cv)
            ss = jnp.sum(s * s, axis=-1, keepdims=True)
            yield
            o_ref[pl.ds(r0, LANES), :] = jnp.where(is_qk, s * lax.rsqrt(ss + NORM_EPS), s)

        x0 = jnp.concatenate([jnp.zeros((8, LANES), F32), x_ref[pl.ds(0, LANES), :]], axis=0)
        _lockstep([block(0, x0)])

        def step(j, carry):
            _lockstep([block(1 + unroll * j + u) for u in range(unroll)])
            return carry

        lax.fori_loop(0, (nblk - 1) // unroll, step, 0)

    return pl.pallas_call(
        body, name="gdn_conv", grid=(w3 // LANES,),
        in_specs=[pl.BlockSpec((m, LANES), lambda c: (0, c)), pl.BlockSpec((CONV_WIDTH, LANES), lambda c: (0, c))],
        out_specs=pl.BlockSpec((None, m, LANES), lambda c: (c // n_heads, 0, c % n_heads)),
        out_shape=jax.ShapeDtypeStruct((3, m, wd), F32),
        compiler_params=_params("parallel"),
    )(proj, conv_w)


def _gdn_conv_bwd(proj, conv_w, dact, n_heads):
    m, w3 = proj.shape
    nblk = m // LANES
    unroll = _pick(nblk - 1, (4, 2, 1))

    def body(x_ref, w_ref, dy_ref, dx_ref, dw_ref, dc_ref):
        c = pl.program_id(0)
        is_qk = c < 2 * n_heads
        dc_ref[pl.ds(m, 8), :] = jnp.zeros((8, LANES), F32)

        def block(b, out, xa=None):
            r0 = pl.multiple_of(b * LANES, LANES)
            if xa is None:
                xa = x_ref[pl.ds(r0 - 8, LANES + 8), :]
            cv = _conv_taps(xa, w_ref)
            sg = _sigmoid(cv)
            s = cv * sg
            dy = dy_ref[pl.ds(r0, LANES), :]
            ss = jnp.sum(s * s, axis=-1, keepdims=True)
            yield
            rn = lax.rsqrt(ss + NORM_EPS)
            yn = s * rn
            proj_len = jnp.sum(dy * yn, axis=-1, keepdims=True)
            yield
            ds = jnp.where(is_qk, rn * (dy - yn * proj_len), dy)
            dcv = ds * sg * (1.0 + cv * (1.0 - sg))
            dc_ref[pl.ds(r0, LANES), :] = dcv
            parts = []
            for i in range(CONV_WIDTH):
                xs = xa[8:] if i == CONV_WIDTH - 1 else pltpu.roll(xa, CONV_WIDTH - 1 - i, 0)[8:]
                parts.append(jnp.sum((dcv * xs).reshape(LANES // 8, 8, LANES), axis=0))
            out.append(parts)

        def add_parts(acc, outs):
            for parts in outs:
                acc = tuple(a + p for a, p in zip(acc, parts))
            return acc

        x0 = jnp.concatenate([jnp.zeros((8, LANES), F32), x_ref[pl.ds(0, LANES), :]], axis=0)
        first = []
        _lockstep([block(0, first, x0)])
        dwacc = tuple(first[0])

        def step1(j, acc):
            outs = []
            _lockstep([block(1 + unroll * j + u, outs) for u in range(unroll)])
            return add_parts(acc, outs)

        dwacc = lax.fori_loop(0, (nblk - 1) // unroll, step1, dwacc)
        for i in range(CONV_WIDTH):
            dw_ref[i:i + 1, :] = jnp.sum(dwacc[i], axis=0, keepdims=True)

        def step2(b, carry):
            r0 = pl.multiple_of(b * LANES, LANES)
            da = dc_ref[pl.ds(r0, LANES + 8), :]
            acc = da * w_ref[CONV_WIDTH - 1:CONV_WIDTH, :]
            for i in range(CONV_WIDTH - 1):
                sh = CONV_WIDTH - 1 - i
                acc = acc + pltpu.roll(da, LANES + 8 - sh, 0) * w_ref[i:i + 1, :]
            dx_ref[pl.ds(r0, LANES), :] = acc[:LANES]
            return carry

        lax.fori_loop(0, nblk, step2, 0)

    return pl.pallas_call(
        body, name="gdn_conv_bwd", grid=(w3 // LANES,),
        in_specs=[pl.BlockSpec((m, LANES), lambda c: (0, c)),
                  pl.BlockSpec((CONV_WIDTH, LANES), lambda c: (0, c)),
                  pl.BlockSpec((None, m, LANES), lambda c: (c // n_heads, 0, c % n_heads))],
        out_specs=[pl.BlockSpec((m, LANES), lambda c: (0, c)), pl.BlockSpec((CONV_WIDTH, LANES), lambda c: (0, c))],
        out_shape=[jax.ShapeDtypeStruct((m, w3), F32), jax.ShapeDtypeStruct((CONV_WIDTH, w3), F32)],
        scratch_shapes=[pltpu.VMEM((m + 8, LANES), F32)],
        compiler_params=_params("parallel"),
    )(proj, conv_w, dact)


def _softplus(x):
    return jnp.maximum(x, 0.0) + jnp.log(1.0 + jnp.exp(-jnp.abs(x)))


def _gdn_gates(ab, a_log, dt_bias, n_heads):
    m = ab.shape[0]
    bm = _pick(m, (640, 384, 128))

    def body(ab_ref, al_ref, dt_ref, g_ref, b_ref):
        i = pl.program_id(0)
        x = ab_ref[...]
        live = (_iota2((bm, LANES), 0) + i * bm) >= PAD_ROWS
        g = jnp.where(live, -jnp.exp(al_ref[...]) * _softplus(x + dt_ref[...]), 0.0)
        beta = jnp.where(live, _sigmoid(x), 0.0)
        for h in range(n_heads):
            g_ref[:, h * LANES:(h + 1) * LANES] = jnp.broadcast_to(g[:, h:h + 1], (bm, LANES))
            b_ref[:, h * LANES:(h + 1) * LANES] = jnp.broadcast_to(
                beta[:, n_heads + h:n_heads + h + 1], (bm, LANES))

    wide = pl.BlockSpec((bm, n_heads * LANES), lambda i: (i, 0))
    return pl.pallas_call(
        body, name="gdn_gates", grid=(m // bm,),
        in_specs=[pl.BlockSpec((bm, LANES), lambda i: (i, 0))] + [pl.BlockSpec((1, LANES), lambda i: (0, 0))] * 2,
        out_specs=[wide, wide],
        out_shape=[jax.ShapeDtypeStruct((m, n_heads * LANES), F32)] * 2,
        compiler_params=_params("parallel"),
    )(ab, a_log, dt_bias)


def _gdn_gates_bwd(ab, a_log, dt_bias, dg_b, dbeta_b, n_heads):
    m = ab.shape[0]
    bm = _pick(m, (640, 384, 128))
    nsteps = m // bm

    def body(ab_ref, al_ref, dt_ref, dg_ref, db_ref, dab_ref, dal_ref, ddt_ref, acc_al, acc_dt):
        i = pl.program_id(0)
        x = ab_ref[...]
        lane = _iota2((bm, LANES), 1)
        live = (_iota2((bm, LANES), 0) + i * bm) >= PAD_ROWS
        dg = jnp.zeros((bm, LANES), F32)
        dbeta = jnp.zeros((bm, LANES), F32)
        for h in range(n_heads):
            dg = jnp.where(lane == h, dg_ref[:, h * LANES:(h + 1) * LANES], dg)
            dbeta = jnp.where(lane == n_heads + h, db_ref[:, h * LANES:(h + 1) * LANES], dbeta)
        dg = jnp.where(live, dg, 0.0)
        dbeta = jnp.where(live, dbeta, 0.0)
        nea = -jnp.exp(al_ref[...])
        pre = x + dt_ref[...]
        d_pre = dg * nea * _sigmoid(pre)
        beta = _sigmoid(x)
        dab_ref[...] = d_pre + dbeta * beta * (1.0 - beta)
        p_al = jnp.sum((dg * nea * _softplus(pre)).reshape(bm // 8, 8, LANES), axis=0)
        p_dt = jnp.sum(d_pre.reshape(bm // 8, 8, LANES), axis=0)

        @pl.when(i == 0)
        def _():
            acc_al[...] = p_al
            acc_dt[...] = p_dt

        @pl.when(i > 0)
        def _():
            acc_al[...] += p_al
            acc_dt[...] += p_dt

        @pl.when(i == nsteps - 1)
        def _():
            dal_ref[...] = jnp.sum(acc_al[...], axis=0, keepdims=True)
            ddt_ref[...] = jnp.sum(acc_dt[...], axis=0, keepdims=True)

    wide = pl.BlockSpec((bm, n_heads * LANES), lambda i: (i, 0))
    one = pl.BlockSpec((1, LANES), lambda i: (0, 0))
    nar = pl.BlockSpec((bm, LANES), lambda i: (i, 0))
    return pl.pallas_call(
        body, name="gdn_gates_bwd", grid=(nsteps,),
        in_specs=[nar, one, one, wide, wide],
        out_specs=[nar, one, one],
        out_shape=[jax.ShapeDtypeStruct((m, LANES), F32), jax.ShapeDtypeStruct((1, LANES), F32),
                   jax.ShapeDtypeStruct((1, LANES), F32)],
        scratch_shapes=[pltpu.VMEM((8, LANES), F32)] * 2,
        compiler_params=_params("arbitrary"),
    )(ab, a_log, dt_bias, dg_b, dbeta_b)


def _chunk_masks():
    c = GDN_CHUNK
    ri, ci = _iota2((c, c), 0), _iota2((c, c), 1)
    return ri >= ci, ri > ci, ri == ci


def _chunk_decay(gbs):
    c = GDN_CHUNK
    incl, _, _ = _chunk_masks()
    inclf = incl.astype(F32)
    ones = jnp.ones((c, LANES), F32)
    g_cums = [_hdot(inclf, gb) for gb in gbs]
    g_rows = [_hdot(ones, g_cum, NT) * (1.0 / LANES) for g_cum in g_cums]
    dmats = [jnp.where(incl, jnp.exp(jnp.where(incl, g_cum[:, :c] - g_row, 0.0)), 0.0)
             for g_cum, g_row in zip(g_cums, g_rows)]
    return g_cums, dmats


def _unit_lower_inverse(mats):
    _, _, eye = _chunk_masks()
    xs = [eye.astype(F32) - a for a in mats]
    ps = list(mats)
    for _ in range(int(math.log2(GDN_CHUNK)) - 1):
        ps = [_hdot(p, p) for p in ps]
        xs = [x + _hdot(x, p) for x, p in zip(xs, ps)]
    return xs


def _gdn_prep(qkv, g_b, beta_b, n_heads):
    _, m, wd = qkv.shape
    c = GDN_CHUNK
    nc = m // c
    cb = _pick(nc, (10, 5, 3, 2))
    rows = cb * c
    qscale = GDN_HEAD_DIM ** -0.5

    def body(qkv_ref, g_ref, b_ref, ub_ref, wc_ref, qd_ref, kd_ref, gc_ref, p_ref, t_ref):
        _, strict, _ = _chunk_masks()
        sls = [pl.ds(s * c, c) for s in range(cb)]
        g_cums, dmats = _chunk_decay([g_ref[sl, :] for sl in sls])
        kks = [_bdot(qkv_ref[1, sl, :], qkv_ref[1, sl, :], NT) for sl in sls]
        ts = _unit_lower_inverse([jnp.where(strict, b_ref[sl, :][:, :c] * dmat * kk, 0.0)
                                  for sl, dmat, kk in zip(sls, dmats, kks)])
        for s, (sl, g_cum, dmat, t) in enumerate(zip(sls, g_cums, dmats, ts)):
            q = qkv_ref[0, sl, :] * qscale
            k = qkv_ref[1, sl, :]
            bb = b_ref[sl, :]
            gam = jnp.exp(g_cum)
            ub_ref[sl, :] = _hdot(t, bb * qkv_ref[2, sl, :])
            wc_ref[sl, :] = _hdot(t, bb * gam * k)
            p_ref[s] = _bdot(q, k, NT) * dmat
            qd_ref[sl, :] = q * gam
            kd_ref[sl, :] = k * jnp.exp(g_cum[c - 1:c, :] - g_cum)
            gc_ref[sl, :] = g_cum
            t_ref[s] = t

    blk = pl.BlockSpec((rows, LANES), lambda h, n: (n, h))
    sq = pl.BlockSpec((None, cb, c, c), lambda h, n: (h, n, 0, 0))
    big = jax.ShapeDtypeStruct((m, wd), F32)
    small = jax.ShapeDtypeStruct((n_heads, nc, c, c), F32)
    return pl.pallas_call(
        body, name="gdn_prep", grid=(n_heads, nc // cb),
        in_specs=[pl.BlockSpec((3, rows, LANES), lambda h, n: (0, n, h)), blk, blk],
        out_specs=[blk] * 5 + [sq, sq],
        out_shape=[big] * 5 + [small, small],
        compiler_params=_params("parallel", "parallel"),
    )(qkv, g_b, beta_b)


def _gdn_prep_bwd(qkv, g_b, beta_b, ub, wc, tinv, dub, dwc, dqd, dkd, dp, dgl, n_heads):
    _, m, wd = qkv.shape
    c = GDN_CHUNK
    nc = m // c
    cb = _pick(nc, (10, 5, 3, 2))
    rows = cb * c
    qscale = GDN_HEAD_DIM ** -0.5

    def body(qkv_ref, g_ref, b_ref, ub_ref, wc_ref, t_ref, dub_ref, dwc_ref, dqd_ref, dkd_ref, dp_ref, dgl_ref,
             dqkv_ref, dg_ref, db_ref):
        incl, strict, _ = _chunk_masks()
        ones_c = jnp.ones((c, LANES), F32)
        ones_l = jnp.ones((LANES, LANES), F32)
        last_row = _iota2((c, LANES), 0) == c - 1

        def chunk(s):
            sl = pl.ds(s * c, c)
            q = qkv_ref[0, sl, :] * qscale
            k = qkv_ref[1, sl, :]
            v = qkv_ref[2, sl, :]
            bb = b_ref[sl, :]
            bcol = bb[:, :c]
            t = t_ref[s]
            g_cum = _hdot(incl.astype(F32), g_ref[sl, :])
            kk = _bdot(k, k, NT)
            qk = _bdot(q, k, NT)
            drhs_u = _hdot(t, dub_ref[sl, :], TN)
            drhs_w = _hdot(t, dwc_ref[sl, :], TN)
            yield
            g_row = _hdot(ones_c, g_cum, NT) * (1.0 / LANES)
            da = -jnp.where(strict, _bdot(drhs_u, ub_ref[sl, :], NT) + _bdot(drhs_w, wc_ref[sl, :], NT), 0.0)
            dbeta = _hdot(drhs_u * v, ones_l)
            tk = _hdot(drhs_w * k, ones_l)
            dqd = dqd_ref[sl, :]
            dgam = _hdot(dqd * q, ones_l)
            yield
            dmat = jnp.where(incl, jnp.exp(jnp.where(incl, g_cum[:, :c] - g_row, 0.0)), 0.0)
            gam = jnp.exp(g_cum)
            edec = jnp.exp(g_cum[c - 1:c, :] - g_cum)
            dkd = dkd_ref[sl, :]
            skd = _hdot(dkd * k * edec, ones_l)
            dbeta = dbeta + _hdot(da * dmat * kk, ones_c)
            dkk = da * bcol * dmat
            dpm = jnp.where(incl, dp_ref[s], 0.0)
            dpd = dpm * dmat
            dk = _bdot(dkk, k) + _bdot(dkk, k, TN) + _bdot(dpd, q, TN)
            dq = _bdot(dpd, k)
            e = (da * bcol * kk + dpm * qk) * dmat
            yield
            dk = dk + bb * gam * drhs_w + dkd * edec
            dq = dq + dqd * gam
            dbeta = dbeta + tk * gam
            dgam = dgam + bb * tk
            dgc = dgam * gam - skd + _hdot(e, ones_c) - _hdot(e, ones_c, TN)
            tot = _hdot(jnp.ones((c, c), F32), skd)
            yield
            dgc = dgc + jnp.where(last_row, tot + dgl_ref[s, 0:1, :], 0.0)
            dg_ref[sl, :] = _hdot((_iota2((c, c), 0) <= _iota2((c, c), 1)).astype(F32), dgc)
            db_ref[sl, :] = dbeta
            dqkv_ref[0, sl, :] = dq * qscale
            dqkv_ref[1, sl, :] = dk
            dqkv_ref[2, sl, :] = bb * drhs_u

        _lockstep([chunk(s) for s in range(cb)])

    blk = pl.BlockSpec((rows, LANES), lambda h, n: (n, h))
    blk3 = pl.BlockSpec((3, rows, LANES), lambda h, n: (0, n, h))
    sq = pl.BlockSpec((None, cb, c, c), lambda h, n: (h, n, 0, 0))
    sgl = pl.BlockSpec((None, cb, 8, LANES), lambda h, n: (h, n, 0, 0))
    big = jax.ShapeDtypeStruct((m, wd), F32)
    return pl.pallas_call(
        body, name="gdn_prep_bwd", grid=(n_heads, nc // cb),
        in_specs=[blk3, blk, blk, blk, blk, sq, blk, blk, blk, blk, sq, sgl],
        out_specs=[blk3, blk, blk],
        out_shape=[jax.ShapeDtypeStruct((3, m, wd), F32), big, big],
        compiler_params=_params("parallel", "parallel"),
    )(qkv, g_b, beta_b, ub, wc, tinv, dub, dwc, dqd, dkd, dp, dgl)


def _gdn_scan(ub, wc, qd, kd, gc, p, n_heads):
    m, wd = ub.shape
    c = GDN_CHUNK
    nc = m // c
    dh = GDN_HEAD_DIM

    def body(ub_ref, wc_ref, qd_ref, kd_ref, gc_ref, p_ref, o_ref, ssave_ref, s_ref):
        n = pl.program_id(0)

        @pl.when(n == 0)
        def _():
            s_ref[...] = jnp.zeros_like(s_ref)

        def head(h):
            sl = slice(h * LANES, (h + 1) * LANES)
            s = s_ref[h]
            ssave_ref[h] = s
            ws = _bdot(wc_ref[:, sl], s)
            qs = _bdot(qd_ref[:, sl], s)
            yield
            u = ub_ref[:, sl] - ws
            pu = _bdot(p_ref[h], u)
            ku = _bdot(kd_ref[:, sl], u, TN)
            yield
            o_ref[:, sl] = qs + pu
            s_ref[h] = s * jnp.exp(gc_ref[c - 1:c, sl]) + ku

        _lockstep([head(h) for h in range(n_heads)])

    blk = pl.BlockSpec((c, wd), lambda n: (n, 0))
    return pl.pallas_call(
        body, name="gdn_scan", grid=(nc,),
        in_specs=[blk] * 5 + [pl.BlockSpec((n_heads, None, c, c), lambda n: (0, n, 0, 0))],
        out_specs=[blk, pl.BlockSpec((n_heads, None, dh, dh), lambda n: (0, n, 0, 0))],
        out_shape=[jax.ShapeDtypeStruct((m, wd), F32), jax.ShapeDtypeStruct((n_heads, nc, dh, dh), F32)],
        scratch_shapes=[pltpu.VMEM((n_heads, dh, dh), F32)],
        compiler_params=_params("arbitrary"),
    )(ub, wc, qd, kd, gc, p)


def _gdn_scan_bwd(do, ssave, ub, wc, qd, kd, gc, p, n_heads):
    m, wd = ub.shape
    c = GDN_CHUNK
    nc = m // c
    dh = GDN_HEAD_DIM

    def body(do_ref, ss_ref, ub_ref, wc_ref, qd_ref, kd_ref, gc_ref, p_ref,
             dub_ref, dwc_ref, dqd_ref, dkd_ref, dp_ref, dgl_ref, ds_ref):
        n = pl.program_id(0)

        @pl.when(n == 0)
        def _():
            ds_ref[...] = jnp.zeros_like(ds_ref)

        def head(h):
            sl = slice(h * LANES, (h + 1) * LANES)
            ds_next = ds_ref[h]
            s = ss_ref[h]
            do_ = do_ref[:, sl]
            wcv = wc_ref[:, sl]
            kdv = kd_ref[:, sl]
            ws = _bdot(wcv, s)
            du = _bdot(p_ref[h], do_, TN) + _bdot(kdv, ds_next)
            dqd_ref[:, sl] = _bdot(do_, s, NT)
            col = _hdot(jnp.ones((8, dh), F32), s * ds_next)
            qdo = _bdot(qd_ref[:, sl], do_, TN)
            yield
            u = ub_ref[:, sl] - ws
            egl = jnp.exp(gc_ref[c - 1:c, sl])
            dub_ref[:, sl] = du
            dwc_ref[:, sl] = -_bdot(du, s, NT)
            dp_ref[h] = _bdot(do_, u, NT)
            dkd_ref[:, sl] = _bdot(u, ds_next, NT)
            dgl_ref[h] = egl * _hdot(col, jnp.ones((LANES, LANES), F32))
            ds_ref[h] = qdo + egl * ds_next - _bdot(wcv, du, TN)

        _lockstep([head(h) for h in range(n_heads)])

    blk = pl.BlockSpec((c, wd), lambda n: (nc - 1 - n, 0))
    sq = pl.BlockSpec((n_heads, None, c, c), lambda n: (0, nc - 1 - n, 0, 0))
    big = jax.ShapeDtypeStruct((m, wd), F32)
    return pl.pallas_call(
        body, name="gdn_scan_bwd", grid=(nc,),
        in_specs=[blk, pl.BlockSpec((n_heads, None, dh, dh), lambda n: (0, nc - 1 - n, 0, 0))] + [blk] * 5 + [sq],
        out_specs=[blk] * 4 + [sq, pl.BlockSpec((n_heads, None, 8, LANES), lambda n: (0, nc - 1 - n, 0, 0))],
        out_shape=[big] * 4 + [jax.ShapeDtypeStruct((n_heads, nc, c, c), F32),
                               jax.ShapeDtypeStruct((n_heads, nc, 8, LANES), F32)],
        scratch_shapes=[pltpu.VMEM((n_heads, dh, dh), F32)],
        compiler_params=_params("arbitrary"),
    )(do, ssave, ub, wc, qd, kd, gc, p)


def _gdn_outnorm(o, gate, gain, n_heads):
    m, wd = o.shape
    bm = _pick(m, (640, 384, 128))

    def body(o_ref, gt_ref, gn_ref, y_ref):
        gn = gn_ref[...]
        for h in range(n_heads):
            sl = slice(h * LANES, (h + 1) * LANES)
            x = o_ref[:, sl]
            gt = gt_ref[:, sl]
            r = lax.rsqrt(jnp.mean(x * x, axis=-1, keepdims=True) + NORM_EPS)
            y_ref[:, sl] = (x * r * gn * gt * _sigmoid(gt)).astype(BF16)

    row = pl.BlockSpec((bm, wd), lambda i: (i, 0))
    return pl.pallas_call(
        body, name="gdn_outnorm", grid=(m // bm,),
        in_specs=[row, row, pl.BlockSpec((1, LANES), lambda i: (0, 0))], out_specs=row,
        out_shape=jax.ShapeDtypeStruct((m, wd), BF16),
        compiler_params=_params("parallel"),
    )(o, gate, gain)


def _gdn_outnorm_bwd(o, gate, gain, dy, n_heads):
    m, wd = o.shape
    bm = _pick(m, (640, 384, 128))
    nsteps = m // bm

    def body(o_ref, gt_ref, gn_ref, dy_ref, do_ref, dgt_ref, dgn_ref, acc_ref):
        i = pl.program_id(0)
        gn = gn_ref[...]
        part = jnp.zeros((8, LANES), F32)
        for h in range(n_heads):
            sl = slice(h * LANES, (h + 1) * LANES)
            x = o_ref[:, sl]
            gt = gt_ref[:, sl]
            d_out = dy_ref[:, sl]
            r = lax.rsqrt(jnp.mean(x * x, axis=-1, keepdims=True) + NORM_EPS)
            xhat = x * r
            sg = _sigmoid(gt)
            dz = d_out * gt * sg
            dgt_ref[:, sl] = d_out * xhat * gn * sg * (1.0 + gt * (1.0 - sg))
            dxh = dz * gn
            do_ref[:, sl] = r * (dxh - xhat * jnp.mean(dxh * xhat, axis=-1, keepdims=True))
            part = part + jnp.sum((dz * xhat).reshape(bm // 8, 8, LANES), axis=0)

        @pl.when(i == 0)
        def _():
            acc_ref[...] = part

        @pl.when(i > 0)
        def _():
            acc_ref[...] += part

        @pl.when(i == nsteps - 1)
        def _():
            dgn_ref[...] = jnp.sum(acc_ref[...], axis=0, keepdims=True)

    row = pl.BlockSpec((bm, wd), lambda i: (i, 0))
    one = pl.BlockSpec((1, LANES), lambda i: (0, 0))
    return pl.pallas_call(
        body, name="gdn_outnorm_bwd", grid=(nsteps,),
        in_specs=[row, row, one, row], out_specs=[row, row, one],
        out_shape=[jax.ShapeDtypeStruct((m, wd), F32)] * 2 + [jax.ShapeDtypeStruct((1, LANES), F32)],
        scratch_shapes=[pltpu.VMEM((8, LANES), F32)],
        compiler_params=_params("arbitrary"),
    )(o, gate, gain, dy)


def _sb_consts(kind):
    ri, ci = _iota2((LANES, LANES), 0), _iota2((LANES, LANES), 1)
    tri = {"suffix_excl": ri > ci, "prefix_incl": ri <= ci, "prefix_excl": ri < ci}[kind]
    return jnp.concatenate([tri.astype(BF16), jnp.ones((LANES, LANES), BF16)], axis=1)


def _sb_stack(a):
    first = _iota2((LANES, LANES), 1) < SB_HEAD_DIM
    return jnp.concatenate([jnp.where(first, a, 0.0), jnp.where(first, 0.0, a)], axis=0)


def _sb_unstack(a2):
    first = _iota2((LANES, LANES), 1) < SB_HEAD_DIM
    return jnp.where(first, a2[:LANES], a2[LANES:])


def _sb_visible(qi, kb, blocks=1):
    shape = (2 * LANES, blocks * LANES)
    kpos = _iota2(shape, 1) + kb * LANES
    qpos = (_iota2(shape, 0) & (LANES - 1)) + qi * LANES
    return (kpos < qpos) & (kpos >= PAD_ROWS)


def _sb_log_gates(z, vis):
    t = jnp.log2(1.0 + jnp.exp2(-jnp.abs(z)))
    lb = jnp.minimum(z, 0.0) - t
    lk = lb - z
    if vis is not None:
        lk = jnp.where(vis, lk, 0.0)
    return lb, lk


def _sb_attention(q, kv):
    m, wd = q.shape
    npair = wd // LANES
    nq = m // LANES
    scale = SB_HEAD_DIM ** -0.5

    def body(q_ref, k_ref, v_ref, o_ref, w_ref, b_ref):
        qi = pl.program_id(1)
        su = _sb_consts("suffix_excl")
        q2 = _sb_stack(q_ref[...] * (scale * math.log2(math.e))).astype(BF16)

        def tiles(kbs, carry, masked):
            run, acc = carry
            r0s = [pl.multiple_of(kb * LANES, LANES) for kb in kbs]
            zs = [lax.dot_general(q2, k_ref[pl.ds(r0, LANES), :], NT, preferred_element_type=F32) for r0 in r0s]
            parts = []
            for kb, r0, z in zip(kbs, r0s, zs):
                vis = _sb_visible(qi, kb) if masked else None
                lb, lk = _sb_log_gates(z, vis)
                parts.append((kb, r0, vis, lb, lk))
            parts = [(kb, r0, vis, lb, _split_dot(lk, su)) for kb, r0, vis, lb, lk in parts]
            for kb, r0, vis, lb, cs in parts:
                w = jnp.exp2(lb + cs[:, :LANES] + run)
                if vis is not None:
                    w = jnp.where(vis, w, 0.0)
                run = run + cs[:, LANES:]
                wb = w.astype(BF16)
                w_ref[kb] = wb
                b_ref[kb] = jnp.exp2(lb).astype(BF16)
                acc = acc + lax.dot_general(wb, v_ref[pl.ds(r0, LANES), :], NN, preferred_element_type=F32)
            return run, acc

        su2 = (_iota2((2 * LANES, 2 * LANES), 0) > _iota2((2 * LANES, 2 * LANES), 1)).astype(BF16)

        def pairs(his, carry, masked):
            run, acc = carry
            r0s = [pl.multiple_of((hi - 1) * LANES, LANES) for hi in his]
            zs = [lax.dot_general(q2, k_ref[pl.ds(r0, 2 * LANES), :], NT, preferred_element_type=F32) for r0 in r0s]
            masks = masked if isinstance(masked, list) else [masked] * len(his)
            viss = [_sb_visible(qi, hi - 1, 2) if mk else None for hi, mk in zip(his, masks)]
            gates = [_sb_log_gates(z, vis) for z, vis in zip(zs, viss)]
            css = [_split_dot(lk, su2) for _, lk in gates]
            for hi, r0, vis, (lb, lk), cs in zip(his, r0s, viss, gates, css):
                w = jnp.exp2(lb + cs + jnp.concatenate([run, run], axis=1))
                if vis is not None:
                    w = jnp.where(vis, w, 0.0)
                run = run + jnp.broadcast_to(cs[:, 0:1] + lk[:, 0:1], (2 * LANES, LANES))
                wb = w.astype(BF16)
                bb = jnp.exp2(lb).astype(BF16)
                w_ref[hi - 1], w_ref[hi] = wb[:, :LANES], wb[:, LANES:]
                b_ref[hi - 1], b_ref[hi] = bb[:, :LANES], bb[:, LANES:]
                acc = acc + lax.dot_general(wb, v_ref[pl.ds(r0, 2 * LANES), :], NN, preferred_element_type=F32)
            return run, acc

        def maybe(flag, fn, carry):
            return lax.fori_loop(0, flag, lambda j, c: fn(c), carry)

        zero = jnp.zeros((2 * LANES, LANES), F32)
        far = (qi >= SB_FAR).astype(jnp.int32)
        near = (qi >= 3).astype(jnp.int32) - far
        big = far + near
        carry = maybe(far, lambda c: pairs([qi, qi - 2, qi - 4, qi - 6], c, [True, False, False, False]), (zero, zero))
        carry = maybe(near, lambda c: pairs([qi], c, True), carry)
        carry = maybe(1 - big, lambda c: tiles([qi], c, True), carry)
        top, bottom = qi - 1 - big - 6 * far, 1 + big + 2 * far
        n_mid = jnp.maximum(top - bottom + 1, 0)
        n16 = n_mid // 16
        carry = lax.fori_loop(0, n16, lambda j, c: pairs([top - 16 * j - 2 * u for u in range(8)], c, False), carry)
        top = top - 16 * n16
        take8, take4, take2, take1 = (n_mid >> 3) & 1, (n_mid >> 2) & 1, (n_mid >> 1) & 1, n_mid & 1
        carry = maybe(take8, lambda c: pairs([top - 2 * u for u in range(4)], c, False), carry)
        top = top - 8 * take8
        carry = maybe(take4, lambda c: pairs([top, top - 2], c, False), carry)
        top = top - 4 * take4
        carry = maybe(take2, lambda c: pairs([top], c, False), carry)
        carry = maybe(take1, lambda c: tiles([bottom], c, False), carry)
        carry = maybe(far, lambda c: pairs([3, 1], c, [False, True]), carry)
        carry = maybe(near, lambda c: pairs([1], c, True), carry)
        carry = maybe((1 - big) * (qi >= 1).astype(jnp.int32), lambda c: tiles([0], c, True), carry)
        o_ref[...] = _sb_unstack(carry[1])

    qspec = pl.BlockSpec((LANES, LANES), lambda hp, qi: (qi, hp))
    saved = pl.BlockSpec((None, None, nq, 2 * LANES, LANES), lambda hp, qi: (hp, qi, 0, 0, 0))
    saved_shape = jax.ShapeDtypeStruct((npair, nq, nq, 2 * LANES, LANES), BF16)
    return pl.pallas_call(
        body, name="sb_attention", grid=(npair, nq),
        in_specs=[qspec, pl.BlockSpec((m, LANES), lambda hp, qi: (0, hp)),
                  pl.BlockSpec((m, LANES), lambda hp, qi: (0, npair + hp))],
        out_specs=[qspec, saved, saved],
        out_shape=[jax.ShapeDtypeStruct((m, wd), F32), saved_shape, saved_shape],
        compiler_params=_params("parallel", "arbitrary"),
    )(q, kv, kv)


def _sb_attention_bwd(q, kv, w_all, b_all, do):
    m, wd = q.shape
    npair = wd // LANES
    nq = m // LANES
    scale = SB_HEAD_DIM ** -0.5

    def body(q_ref, k_ref, v_ref, w_ref, b_ref, do_ref, dq_ref, dk_ref, dv_ref):
        qi = pl.program_id(1)

        @pl.when(qi == 0)
        def _():
            dk_ref[...] = jnp.zeros_like(dk_ref)
            dv_ref[...] = jnp.zeros_like(dv_ref)

        px = _sb_consts("prefix_excl")
        q2 = _sb_stack(q_ref[...] * scale).astype(BF16)
        do2 = _sb_stack(do_ref[...]).astype(BF16)

        def tiles(kbs, carry, masked):
            run_p, dq_acc = carry
            r0s = [pl.multiple_of(kb * LANES, LANES) for kb in kbs]
            das = [lax.dot_general(do2, v_ref[pl.ds(r0, LANES), :], NT, preferred_element_type=F32) for r0 in r0s]
            wbs = [w_ref[kb] for kb in kbs]
            for wb, r0 in zip(wbs, r0s):
                dv_ref[pl.ds(r0, LANES), :] += lax.dot_general(wb, do2, TN, preferred_element_type=F32)
            pws = [wb.astype(F32) * da for wb, da in zip(wbs, das)]
            pss = [_split_dot(pw, px, parts=SB_BWD_SUM_PARTS) for pw in pws]
            dzs = []
            for kb, pw, ps in zip(kbs, pws, pss):
                beta = b_ref[kb].astype(F32)
                dz = pw * (1.0 - beta) - beta * (run_p + ps[:, :LANES])
                if masked:
                    dz = jnp.where(_sb_visible(qi, kb), dz, 0.0)
                run_p = run_p + ps[:, LANES:]
                dzs.append(dz.astype(BF16))
            for r0, dz in zip(r0s, dzs):
                dq_acc = dq_acc + lax.dot_general(dz, k_ref[pl.ds(r0, LANES), :], NN, preferred_element_type=F32)
                dk_ref[pl.ds(r0, LANES), :] += lax.dot_general(dz, q2, TN, preferred_element_type=F32)
            return run_p, dq_acc

        px2 = (_iota2((2 * LANES, 2 * LANES), 0) < _iota2((2 * LANES, 2 * LANES), 1)).astype(BF16)

        def pairs(los, carry, masked):
            run_p, dq_acc = carry
            r0s = [pl.multiple_of(lo * LANES, LANES) for lo in los]
            das = [lax.dot_general(do2, v_ref[pl.ds(r0, 2 * LANES), :], NT, preferred_element_type=F32) for r0 in r0s]
            wbs = [jnp.concatenate([w_ref[lo], w_ref[lo + 1]], axis=1) for lo in los]
            for wb, r0 in zip(wbs, r0s):
                dv_ref[pl.ds(r0, 2 * LANES), :] += lax.dot_general(wb, do2, TN, preferred_element_type=F32)
            pws = [wb.astype(F32) * da for wb, da in zip(wbs, das)]
            pss = [_split_dot(pw, px2, parts=SB_BWD_SUM_PARTS) for pw in pws]
            masks = masked if isinstance(masked, list) else [masked] * len(los)
            dzs = []
            for lo, pw, ps, mk in zip(los, pws, pss, masks):
                beta = jnp.concatenate([b_ref[lo], b_ref[lo + 1]], axis=1).astype(F32)
                dz = pw * (1.0 - beta) - beta * (jnp.concatenate([run_p, run_p], axis=1) + ps)
                if mk:
                    dz = jnp.where(_sb_visible(qi, lo, 2), dz, 0.0)
                last = 2 * LANES - 1
                run_p = run_p + jnp.broadcast_to(ps[:, last:] + pw[:, last:], (2 * LANES, LANES))
                dzs.append(dz.astype(BF16))
            for r0, dz in zip(r0s, dzs):
                dq_acc = dq_acc + lax.dot_general(dz, k_ref[pl.ds(r0, 2 * LANES), :], NN, preferred_element_type=F32)
                dk_ref[pl.ds(r0, 2 * LANES), :] += lax.dot_general(dz, q2, TN, preferred_element_type=F32)
            return run_p, dq_acc

        def maybe(flag, fn, carry):
            return lax.fori_loop(0, flag, lambda j, c: fn(c), carry)

        zero = jnp.zeros((2 * LANES, LANES), F32)
        far = (qi >= SB_FAR).astype(jnp.int32)
        near = (qi >= 3).astype(jnp.int32) - far
        big = far + near
        carry = maybe(far, lambda c: pairs([0, 2], c, [True, False]), (zero, zero))
        carry = maybe(near, lambda c: pairs([0], c, True), carry)
        carry = maybe(1 - big, lambda c: tiles([0], c, True), carry)
        low, high = 1 + big + 2 * far, qi - 1 - big - 6 * far
        n_mid = jnp.maximum(high - low + 1, 0)
        n16 = n_mid // 16
        carry = lax.fori_loop(0, n16, lambda j, c: pairs([low + 16 * j + 2 * u for u in range(8)], c, False), carry)
        low = low + 16 * n16
        take8, take4, take2, take1 = (n_mid >> 3) & 1, (n_mid >> 2) & 1, (n_mid >> 1) & 1, n_mid & 1
        carry = maybe(take8, lambda c: pairs([low + 2 * u for u in range(4)], c, False), carry)
        low = low + 8 * take8
        carry = maybe(take4, lambda c: pairs([low, low + 2], c, False), carry)
        low = low + 4 * take4
        carry = maybe(take2, lambda c: pairs([low], c, False), carry)
        carry = maybe(take1, lambda c: tiles([high], c, False), carry)
        carry = maybe(far, lambda c: pairs([qi - 7, qi - 5, qi - 3, qi - 1], c, [False, False, False, True]), carry)
        carry = maybe(near, lambda c: pairs([qi - 1], c, True), carry)
        carry = maybe((1 - big) * (qi >= 1).astype(jnp.int32), lambda c: tiles([qi], c, True), carry)
        dq_ref[...] = _sb_unstack(carry[1]) * scale

    qspec = pl.BlockSpec((LANES, LANES), lambda hp, qi: (qi, hp))
    strip = pl.BlockSpec((m, LANES), lambda hp, qi: (0, hp))
    saved = pl.BlockSpec((None, None, nq, 2 * LANES, LANES), lambda hp, qi: (hp, qi, 0, 0, 0))
    big = jax.ShapeDtypeStruct((m, wd), F32)
    return pl.pallas_call(
        body, name="sb_attention_bwd", grid=(npair, nq),
        in_specs=[qspec, strip, pl.BlockSpec((m, LANES), lambda hp, qi: (0, npair + hp)), saved, saved, qspec],
        out_specs=[qspec, strip, strip],
        out_shape=[big, big, big],
        compiler_params=_params("parallel", "arbitrary"),
    )(q, kv, kv, w_all, b_all, do)


def _mesh_pos():
    return lax.axis_index("x"), lax.axis_index("y"), lax.axis_index("c")


def _exchange(srcs, name, scatter):
    n = len(srcs)
    rows = [s.shape[-2] for s in srcs]
    offs = [sum(rows[:i]) for i in range(n)]
    total = sum(rows)

    def body(*refs):
        src_refs, out_ref = refs[:n], refs[n]
        send_sems, recv_sems, local_sems = refs[n + 1:]
        x, y, c = _mesh_pos()
        me = 4 * x + 2 * y + c

        def piece(i, d):
            return src_refs[i].at[d] if scatter else src_refs[i]

        def window(slot, i):
            return out_ref.at[slot, pl.ds(offs[i], rows[i])]

        local = [pltpu.make_async_copy(piece(i, me), window(me, i), local_sems.at[i]) for i in range(n)]
        for cp in local:
            cp.start()
        peers = []
        for k in range(1, N_DEV):
            px = 1 - x if k & 4 else x
            py = 1 - y if k & 2 else y
            pc = 1 - c if k & 1 else c
            pid = 4 * px + 2 * py + pc
            for i in range(n):
                pltpu.make_async_remote_copy(
                    src_ref=piece(i, pid), dst_ref=window(me, i),
                    send_sem=send_sems.at[k - 1], recv_sem=recv_sems.at[k - 1],
                    device_id=(px, py, pc), device_id_type=pl.DeviceIdType.MESH).start()
            peers.append(pid)
        slabs = [pltpu.make_async_remote_copy(
            src_ref=out_ref.at[me], dst_ref=out_ref.at[peers[k - 1]],
            send_sem=send_sems.at[k - 1], recv_sem=recv_sems.at[k - 1],
            device_id=(x, y, c), device_id_type=pl.DeviceIdType.MESH) for k in range(1, N_DEV)]
        for cp in slabs:
            cp.wait_recv()
        for cp in slabs:
            cp.wait_send()
        for cp in local:
            cp.wait()

    return pl.pallas_call(
        body, name=name,
        in_specs=[pl.BlockSpec(memory_space=pl.ANY)] * n, out_specs=pl.BlockSpec(memory_space=pl.ANY),
        out_shape=jax.ShapeDtypeStruct((N_DEV, total, FLAT_COLS), srcs[0].dtype),
        scratch_shapes=[pltpu.SemaphoreType.DMA((N_DEV - 1,)), pltpu.SemaphoreType.DMA((N_DEV - 1,)),
                        pltpu.SemaphoreType.DMA((n,))],
        compiler_params=pltpu.CompilerParams(has_side_effects=True),
    )(*srcs)


def _all_gather_two_level(srcs, name):
    n = len(srcs)
    rows = [s.shape[-2] for s in srcs]
    offs = [sum(rows[:i]) for i in range(n)]
    total = sum(rows)

    def body(*refs):
        src_refs, out_ref = refs[:n], refs[n]
        send_sems, recv_sems, local_sems = refs[n + 1:]
        x, y, c = _mesh_pos()
        me = 4 * x + 2 * y + c
        sibling = (x, y, 1 - c)
        chips = [(1 - x, y), (x, 1 - y), (1 - x, 1 - y)]

        def slot_of(px, py, pc):
            return 4 * px + 2 * py + pc

        def window(slot, i):
            return out_ref.at[slot, pl.ds(offs[i], rows[i])]

        def slab(slot, k, to):
            return pltpu.make_async_remote_copy(
                src_ref=out_ref.at[slot], dst_ref=out_ref.at[slot],
                send_sem=send_sems.at[k], recv_sem=recv_sems.at[k],
                device_id=to, device_id_type=pl.DeviceIdType.MESH)

        local = [pltpu.make_async_copy(src_refs[i], window(me, i), local_sems.at[i]) for i in range(n)]
        for cp in local:
            cp.start()
        targets = [sibling] + [(cx, cy, c) for cx, cy in chips]
        for k, to in enumerate(targets):
            for i in range(n):
                pltpu.make_async_remote_copy(
                    src_ref=src_refs[i], dst_ref=window(me, i),
                    send_sem=send_sems.at[k], recv_sem=recv_sems.at[k],
                    device_id=to, device_id_type=pl.DeviceIdType.MESH).start()
        passed = []
        for j, (cx, cy) in enumerate(chips):
            slab(slot_of(cx, cy, c), 1 + j, (x, y, c)).wait_recv()
            cp = slab(slot_of(cx, cy, c), 4 + j, sibling)
            cp.start()
            passed.append(cp)
        slab(slot_of(x, y, 1 - c), 0, (x, y, c)).wait_recv()
        for j, (cx, cy) in enumerate(chips):
            slab(slot_of(cx, cy, 1 - c), 4 + j, (x, y, c)).wait_recv()
        for k in range(len(targets)):
            slab(me, k, (x, y, c)).wait_send()
        for cp in passed:
            cp.wait_send()
        for cp in local:
            cp.wait()

    return pl.pallas_call(
        body, name=name,
        in_specs=[pl.BlockSpec(memory_space=pl.ANY)] * n, out_specs=pl.BlockSpec(memory_space=pl.ANY),
        out_shape=jax.ShapeDtypeStruct((N_DEV, total, FLAT_COLS), srcs[0].dtype),
        scratch_shapes=[pltpu.SemaphoreType.DMA((N_DEV - 1,)), pltpu.SemaphoreType.DMA((N_DEV - 1,)),
                        pltpu.SemaphoreType.DMA((n,))],
        compiler_params=pltpu.CompilerParams(has_side_effects=True),
    )(*srcs)


def _adamw(parts, w, mom, var, name):
    rows = w.shape[0]
    br = FLAT_ROW_BLOCK
    c1 = 1.0 - ADAM_B1 ** ADAM_STEP
    c2 = 1.0 - ADAM_B2 ** ADAM_STEP

    def body(p_ref, w_ref, m_ref, v_ref, g_ref, d_ref, nm_ref, nv_ref):
        g = p_ref[0].astype(F32)
        for s in range(1, N_DEV):
            g = g + p_ref[s].astype(F32)
        m_new = ADAM_B1 * m_ref[...] + (1.0 - ADAM_B1) * g
        v_new = ADAM_B2 * v_ref[...] + (1.0 - ADAM_B2) * (g * g)
        m_hat = m_new / c1
        v_hat = v_new / c2
        g_ref[...] = g
        d_ref[...] = -ADAM_LR * (m_hat / (jnp.sqrt(v_hat) + ADAM_EPS) + ADAM_WD * w_ref[...])
        nm_ref[...] = m_new
        nv_ref[...] = v_new

    row = pl.BlockSpec((br, FLAT_COLS), lambda i: (i, 0))
    flat = jax.ShapeDtypeStruct((rows, FLAT_COLS), F32)
    return pl.pallas_call(
        body, name=name, grid=(rows // br,),
        in_specs=[pl.BlockSpec((N_DEV, br, FLAT_COLS), lambda i: (0, i, 0)), row, row, row],
        out_specs=[row] * 4, out_shape=[flat] * 4,
        compiler_params=_params("parallel"),
    )(parts, w, mom, var)


SMALL_SHARDED = (("meta_tokens", 1), ("gdn_norm_g", 1), ("gdn_conv_w", 2))
BIG_PIECES = (("gdn_w_in", 0, True), ("gdn_w_out", 0, False), ("w_kv", None, True), ("sb_w_q", 0, False),
              ("sb_w_o", 0, False), ("ffn_w_gate_up", 0, True), ("ffn_w_gate_up", 1, True),
              ("ffn_w_down", 0, False), ("ffn_w_down", 1, False))
REPLICATED = ("gdn_a_log", "gdn_dt_bias", "gdn_onorm_g", "kv_norm_g", "sb_norm_g", "ffn_norm_g", "final_norm_g")
WEIGHTS = ("meta_tokens", "gdn_norm_g", "gdn_w_in", "gdn_conv_w", "gdn_a_log", "gdn_dt_bias", "gdn_onorm_g",
           "gdn_w_out", "kv_norm_g", "w_kv", "sb_norm_g", "sb_w_q", "sb_w_o", "ffn_norm_g", "ffn_w_gate_up",
           "ffn_w_down", "final_norm_g")
F32_ROWS = 8
BF16_ROWS = 16


def _n_rows(shape, mult):
    return -(-math.prod(shape) // (mult * FLAT_COLS)) * mult


def _as_rows(a, nl, mult):
    lead = a.shape[:nl]
    size = math.prod(a.shape[nl:])
    nr = _n_rows(a.shape[nl:], mult)
    flat = a.reshape(lead + (size,))
    if nr * FLAT_COLS != size:
        flat = jnp.pad(flat, [(0, 0)] * nl + [(0, nr * FLAT_COLS - size)])
    return flat.reshape(lead + (nr, FLAT_COLS))


def _from_rows(flat, r0, shape, nl, mult):
    lead = flat.shape[:nl]
    nr = _n_rows(shape, mult)
    seg = lax.slice_in_dim(flat, r0, r0 + nr, axis=nl).reshape(lead + (nr * FLAT_COLS,))
    return lax.slice_in_dim(seg, 0, math.prod(shape), axis=nl).reshape(lead + tuple(shape)), r0 + nr


def _to_piece(shard, layer, transposed):
    mat = shard if layer is None else shard[layer]
    return mat.T if transposed else mat


def _from_pieces(pieces, name):
    mats = [p.T if tr else p for (n, layer, tr), p in zip(BIG_PIECES, pieces) if n == name]
    layers = [layer for n, layer, _ in BIG_PIECES if n == name]
    return mats[0] if layers[0] is None else jnp.stack(mats)


def _merge_shards(g, axis):
    t = jnp.moveaxis(g, 0, axis)
    shp = t.shape
    return t.reshape(shp[:axis] + (shp[axis] * shp[axis + 1],) + shp[axis + 2:])


def _split_shards(full, axis):
    shp = full.shape
    t = full.reshape(shp[:axis] + (N_DEV, shp[axis] // N_DEV) + shp[axis + 1:])
    return jnp.moveaxis(t, axis, 0)


def _pad_lanes(a, width=LANES):
    return jnp.pad(a, ((0, 0), (0, width - a.shape[1])))


def _local_step(x, target, w, mats):
    d = x.shape[1]
    gh = w["gdn_a_log"].shape[1]
    gw = gh * GDN_HEAD_DIM
    w_int = mats["gdn_w_in", 0]
    w_qkvt, w_gatet = w_int[:3 * gw], w_int[3 * gw:4 * gw]
    w_abt = jnp.pad(w_int[4 * gw:], ((0, LANES - 2 * gh), (0, 0)))
    conv_w = w["gdn_conv_w"][0]
    a_log, dt_bias = _pad_lanes(w["gdn_a_log"]), _pad_lanes(w["gdn_dt_bias"])
    w_out = mats["gdn_w_out", 0]
    w_kvt = mats["w_kv", None]
    sbw = w_kvt.shape[0] // 2
    w_kt, w_vt = w_kvt[:sbw], w_kvt[sbw:]
    w_q, w_o = mats["sb_w_q", 0], mats["sb_w_o", 0]
    w_gut = [mats["ffn_w_gate_up", l] for l in range(2)]
    w_dn = [mats["ffn_w_down", l] for l in range(2)]
    ffn_g = [w["ffn_norm_g"][l:l + 1] for l in range(2)]
    kv_g, fin_g = w["kv_norm_g"][None], w["final_norm_g"][None]

    h0 = jnp.concatenate([jnp.zeros((PAD_ROWS, d), F32), w["meta_tokens"], x], axis=0)

    (n1,) = _rmsnorm(h0, [w["gdn_norm_g"]], "norm_gdn")
    proj = _matmul(n1, w_qkvt, "nt", "mm_gdn_qkv")
    gate = _matmul(n1, w_gatet, "nt", "mm_gdn_gate")
    ab = _matmul(n1, w_abt, "nt", "mm_gdn_ab")
    qkv = _gdn_conv(proj, conv_w, gh)
    g_b, beta_b = _gdn_gates(ab, a_log, dt_bias, gh)
    ub, wc, qd, kd, gc, pmat, tinv = _gdn_prep(qkv, g_b, beta_b, gh)
    o_gdn, ssave = _gdn_scan(ub, wc, qd, kd, gc, pmat, gh)
    o2 = _gdn_outnorm(o_gdn, gate, w["gdn_onorm_g"], gh)
    h1 = _matmul(o2, w_out, "nn", "mm_gdn_out", add=h0)

    def ffn_fwd(h, l):
        (n,) = _rmsnorm(h, [ffn_g[l]], f"norm_ffn{l}")
        gu = _matmul(n, w_gut[l], "nt", f"mm_ffn{l}_gu", out_dtype=BF16)
        act = _swiglu(gu, f"swiglu{l}")
        return n, gu, act, _matmul(act, w_dn[l], "nn", f"mm_ffn{l}_down", add=h)

    n2, gu0, act0, h2 = ffn_fwd(h1, 0)

    nkv, n3 = _rmsnorm(h2, [kv_g, w["sb_norm_g"]], "norm_kv_sb")
    kv = _matmul(nkv, w_kvt, "nt", "mm_kv", out_dtype=BF16)
    q_sb = _matmul(n3, w_q, "nn", "mm_sb_q")
    o_sb, w_sb, beta_sb = _sb_attention(q_sb, kv)
    h3 = _matmul(o_sb, w_o, "nn", "mm_sb_o", add=h2)
    n4, gu1, act1, h4 = ffn_fwd(h3, 1)

    loss, dh4, d_fin = _loss_head(h4, fin_g, target)

    def ffn_bwd(dh, h, n, gu, act, l):
        d_act = _matmul(dh, w_dn[l], "nt", f"mm_ffn{l}_dact")
        dw_dn = _matmul(act, dh, "tn", f"mm_ffn{l}_dwdown")
        dgu = _swiglu_bwd(gu, d_act, f"swiglu_bwd{l}")
        dw_gu = _matmul(dgu, n, "tn", f"mm_ffn{l}_dwgu")
        dn = _matmul(dgu, w_gut[l], "nn", f"mm_ffn{l}_dn")
        dh_in, (dg,) = _rmsnorm_bwd(h, [ffn_g[l]], [dn], dh, f"norm_ffn{l}_bwd")
        return dh_in, dw_gu, dw_dn, dg

    dh3, dw_gu1, dw_dn1, dg_ffn1 = ffn_bwd(dh4, h3, n4, gu1, act1, 1)

    do_sb = _matmul(dh3, w_o, "nt", "mm_sb_do")
    dw_o = _matmul(o_sb, dh3, "tn", "mm_sb_dwo")
    dq_sb, dk_sb, dv_sb = _sb_attention_bwd(q_sb, kv, w_sb, beta_sb, do_sb)
    dw_q = _matmul(n3, dq_sb, "tn", "mm_sb_dwq")
    dn3 = _matmul(dq_sb, w_q, "nt", "mm_sb_dn")
    dw_k = _matmul(dk_sb, nkv, "tn", "mm_dwk")
    dw_v = _matmul(dv_sb, nkv, "tn", "mm_dwv")
    dnkv = _matmul(dk_sb, w_kt, "nn", "mm_dnk")
    dnkv = _matmul(dv_sb, w_vt, "nn", "mm_dnv", add=dnkv)
    dh2, (dg_kv, dg_sb) = _rmsnorm_bwd(h2, [kv_g, w["sb_norm_g"]], [dnkv, dn3], dh3, "norm_kv_sb_bwd")

    dh1, dw_gu0, dw_dn0, dg_ffn0 = ffn_bwd(dh2, h1, n2, gu0, act0, 0)

    do2 = _matmul(dh1, w_out, "nt", "mm_gdn_do")
    dw_out = _matmul(o2, dh1, "tn", "mm_gdn_dwout")
    do_gdn, dgate, d_onorm = _gdn_outnorm_bwd(o_gdn, gate, w["gdn_onorm_g"], do2, gh)
    dub, dwc, dqd, dkd, dpm, dgl = _gdn_scan_bwd(do_gdn, ssave, ub, wc, qd, kd, gc, pmat, gh)
    dqkv, dg_b, dbeta_b = _gdn_prep_bwd(qkv, g_b, beta_b, ub, wc, tinv, dub, dwc, dqd, dkd, dpm, dgl, gh)
    dab, d_alog, d_dtb = _gdn_gates_bwd(ab, a_log, dt_bias, dg_b, dbeta_b, gh)
    dproj, d_conv = _gdn_conv_bwd(proj, conv_w, dqkv, gh)
    dw_qkv = _matmul(dproj, n1, "tn", "mm_gdn_dwqkv")
    dw_gate = _matmul(dgate, n1, "tn", "mm_gdn_dwgate")
    dw_ab = _matmul(dab, n1, "tn", "mm_gdn_dwab")
    dn1 = _matmul(dproj, w_qkvt, "nn", "mm_gdn_dn_qkv")
    dn1 = _matmul(dgate, w_gatet, "nn", "mm_gdn_dn_gate", add=dn1)
    dn1 = _matmul(dab, w_abt, "nn", "mm_gdn_dn_ab", add=dn1)
    dh0, (dg_gdn,) = _rmsnorm_bwd(h0, [w["gdn_norm_g"]], [dn1], dh1, "norm_gdn_bwd")

    grads = {
        "meta_tokens": dh0[PAD_ROWS:LANES],
        "gdn_norm_g": dg_gdn,
        "gdn_conv_w": d_conv[None],
        "gdn_a_log": d_alog[:, :gh],
        "gdn_dt_bias": d_dtb[:, :gh],
        "gdn_onorm_g": d_onorm,
        "kv_norm_g": dg_kv[0],
        "sb_norm_g": dg_sb,
        "ffn_norm_g": jnp.concatenate([dg_ffn0, dg_ffn1], axis=0),
        "final_norm_g": d_fin[0],
    }
    gmats = {
        ("gdn_w_in", 0): jnp.concatenate([dw_qkv, dw_gate, dw_ab[:2 * gh]], axis=0),
        ("gdn_w_out", 0): dw_out,
        ("w_kv", None): jnp.concatenate([dw_k, dw_v], axis=0),
        ("sb_w_q", 0): dw_q,
        ("sb_w_o", 0): dw_o,
        ("ffn_w_gate_up", 0): dw_gu0,
        ("ffn_w_gate_up", 1): dw_gu1,
        ("ffn_w_down", 0): dw_dn0,
        ("ffn_w_down", 1): dw_dn1,
    }
    return loss, dh0[LANES:], grads, gmats


def kernel(x, meta_tokens, gdn_norm_g, gdn_w_in, gdn_conv_w, gdn_a_log, gdn_dt_bias, gdn_onorm_g, gdn_w_out, kv_norm_g, w_kv, sb_norm_g, sb_w_q, sb_w_o, ffn_norm_g, ffn_w_gate_up, ffn_w_down, final_norm_g, loss_target, m_meta_tokens, m_gdn_norm_g, m_gdn_w_in, m_gdn_conv_w, m_gdn_a_log, m_gdn_dt_bias, m_gdn_onorm_g, m_gdn_w_out, m_kv_norm_g, m_w_kv, m_sb_norm_g, m_sb_w_q, m_sb_w_o, m_ffn_norm_g, m_ffn_w_gate_up, m_ffn_w_down, m_final_norm_g, v_meta_tokens, v_gdn_norm_g, v_gdn_w_in, v_gdn_conv_w, v_gdn_a_log, v_gdn_dt_bias, v_gdn_onorm_g, v_gdn_w_out, v_kv_norm_g, v_w_kv, v_sb_norm_g, v_sb_w_q, v_sb_w_o, v_ffn_norm_g, v_ffn_w_gate_up, v_ffn_w_down, v_final_norm_g):
    given = dict(locals())
    small = [n for n, _ in SMALL_SHARDED]
    big_names = list(dict.fromkeys(n for n, _, _ in BIG_PIECES))

    def local_pieces(prefix):
        return [_to_piece(given[prefix + n], layer, tr) for n, layer, tr in BIG_PIECES]

    piece_shapes = [p.shape for p in local_pieces("")]

    g_small = _exchange([jnp.concatenate([_as_rows(given[n], 0, F32_ROWS) for n in small], axis=0)],
                        "all_gather_small", scatter=False)
    g_big = _all_gather_two_level([_as_rows(p.astype(BF16), 0, BF16_ROWS) for p in local_pieces("")],
                                  "all_gather_weights")
    full = {n: given[n] for n in REPLICATED}
    r = 0
    for n, ax in SMALL_SHARDED:
        shards, r = _from_rows(g_small, r, given[n].shape, 1, F32_ROWS)
        full[n] = _merge_shards(shards, ax)
    mats, r = {}, 0
    for (n, layer, _), shp in zip(BIG_PIECES, piece_shapes):
        rows8, r = _from_rows(g_big, r, shp, 1, BF16_ROWS)
        mats[n, layer] = rows8.reshape((N_DEV * shp[0],) + shp[1:])

    loss_b, grad_x, grads, gmats = _local_step(x[0], loss_target[0], full, mats)

    order = small + list(REPLICATED)
    fill_small = (-sum(_n_rows(given[n].shape, F32_ROWS) for n in order)) % FLAT_ROW_BLOCK
    fill_big = (-sum(_n_rows(shp, BF16_ROWS) for shp in piece_shapes)) % FLAT_ROW_BLOCK
    small_send = [_as_rows(_split_shards(grads[n], ax), 1, F32_ROWS) for n, ax in SMALL_SHARDED]
    small_send += [_as_rows(jnp.broadcast_to(grads[n][None], (N_DEV,) + grads[n].shape), 1, F32_ROWS)
                   for n in REPLICATED]
    if fill_small:
        small_send.append(jnp.zeros((N_DEV, fill_small, FLAT_COLS), F32))
    recv_small = _exchange([jnp.concatenate(small_send, axis=1)], "scatter_small_gradients", scatter=True)
    big_send = [_as_rows(gmats[n, layer].reshape((N_DEV,) + shp), 1, BF16_ROWS)
                for (n, layer, _), shp in zip(BIG_PIECES, piece_shapes)]
    if fill_big:
        big_send.append(jnp.zeros((N_DEV, fill_big, FLAT_COLS), BF16))
    recv_big = _exchange(big_send, "scatter_gradients", scatter=True)

    def flat_small(prefix):
        segs = [_as_rows(given[prefix + n], 0, F32_ROWS) for n in order]
        return jnp.concatenate(segs + ([jnp.zeros((fill_small, FLAT_COLS), F32)] if fill_small else []), axis=0)

    def flat_big(prefix):
        segs = [_as_rows(p, 0, BF16_ROWS) for p in local_pieces(prefix)]
        return jnp.concatenate(segs + ([jnp.zeros((fill_big, FLAT_COLS), F32)] if fill_big else []), axis=0)

    upd_small = _adamw(recv_small, flat_small(""), flat_small("m_"), flat_small("v_"), "adamw_small")
    upd_big = _adamw(recv_big, flat_big(""), flat_big("m_"), flat_big("v_"), "adamw")
    outs = {}
    for tag, f_small, f_big in zip(("grad", "delta", "new_m", "new_v"), upd_small, upd_big):
        outs[tag], r = {}, 0
        for n in order:
            outs[tag][n], r = _from_rows(f_small, r, given[n].shape, 0, F32_ROWS)
        pieces, r = [], 0
        for shp in piece_shapes:
            p, r = _from_rows(f_big, r, shp, 0, BF16_ROWS)
            pieces.append(p)
        for n in big_names:
            outs[tag][n] = _from_pieces(pieces, n)

    loss = lax.psum(loss_b[0, 0], ("x", "y", "c"))
    result = [loss, grad_x[None]]
    for tag in ("grad", "delta", "new_m", "new_v"):
        result += [outs[tag][n] for n in WEIGHTS]
    return tuple(result)
```
